```python
import jax, jax.numpy as jnp
from jax import lax
import numpy as np

D_MODEL = 1024
BATCH = 32
SEQ = 256
DEPTH = 4
DEC_BATCH = 2
DEC_SEQ = 1024
PAST_LEN = 512

GRID_W = 64
CONV_DIM = 256
CONV_WIDTH = 3
NA_HEADS = 4
NA_HEAD_DIM = 64
NA_DIM = NA_HEADS * NA_HEAD_DIM
NA_WIN_ROWS = 8
NA_WIN_COLS = 16
ML_HEADS = 4
ML_HEAD_DIM = 128
ML_DIM = ML_HEADS * ML_HEAD_DIM
ML_CHUNK = 64
N_DIRS = 2
N_GROUPS = 4
EXPERTS_PER_GROUP = 4
N_EXPERTS = N_GROUPS * EXPERTS_PER_GROUP
TOP_K = 2
EXPERT_FF = 512
ROPE_BASE = 10000.0
ATTN_BLOCK = 128
EPS = 1e-6
N_MOD = 6
SPLIT_POINTS = (CONV_DIM, 2 * CONV_DIM, 3 * CONV_DIM,
                3 * CONV_DIM + NA_DIM, 3 * CONV_DIM + 2 * NA_DIM, 3 * CONV_DIM + 3 * NA_DIM,
                3 * CONV_DIM + 3 * NA_DIM + ML_DIM, 3 * CONV_DIM + 3 * NA_DIM + 2 * ML_DIM,
                3 * CONV_DIM + 3 * NA_DIM + 3 * ML_DIM, 3 * CONV_DIM + 3 * NA_DIM + 4 * ML_DIM)
IN_COLS = 3 * CONV_DIM + 3 * NA_DIM + 4 * ML_DIM + 2 * N_DIRS * ML_HEADS
MIX_DIM = CONV_DIM + NA_DIM + ML_DIM

kernel_name = 'hybrid_diffusion_prefix_step'

F32 = jnp.float32


def rmsnorm(x, g):
    xf = x.astype(F32)
    y = xf * lax.rsqrt(jnp.mean(xf * xf, axis=-1, keepdims=True) + EPS)
    return (y * g.astype(F32)).astype(x.dtype)


def adaln_params(cvec, w_mod, b_mod):
    m = jax.nn.silu(cvec) @ w_mod + b_mod
    m = jnp.moveaxis(m.reshape(*cvec.shape[:-1], N_MOD, D_MODEL), -2, 0)
    return m[..., None, :]


def project(h, w_in, ml_gate_b):
    B, T, _ = h.shape
    z = h @ w_in
    cb, cc, ch, nq, nk, nv, mq, mk, mv, mo, mg = jnp.split(z, SPLIT_POINTS, axis=-1)
    na = tuple(a.reshape(B, T, NA_HEADS, NA_HEAD_DIM) for a in (nq, nk, nv))
    ml = tuple(a.reshape(B, T, ML_HEADS, ML_HEAD_DIM) for a in (mq, mk, mv))
    gates = mg.astype(F32).reshape(B, T, 2, N_DIRS, ML_HEADS) + ml_gate_b.astype(F32)
    return (cb, cc, ch), na, ml, mo, gates[:, :, 0], gates[:, :, 1]


def short_conv(u, w):
    return lax.conv_general_dilated(u, w.astype(u.dtype)[:, None, :], window_strides=(1,),
                                    padding=((CONV_WIDTH // 2, CONV_WIDTH // 2),),
                                    dimension_numbers=('NWC', 'WIO', 'NWC'),
                                    feature_group_count=u.shape[-1])


def axial_rope(x):
    B, T, H, Dh = x.shape
    t = jnp.arange(T)
    pos = jnp.stack([t // GRID_W, t % GRID_W], axis=-1).astype(F32)
    nf = Dh // 4
    inv = ROPE_BASE ** (-jnp.arange(nf, dtype=F32) / nf)
    ang = pos[:, :, None] * inv
    cos = jnp.cos(ang)[None, :, None].astype(x.dtype)
    sin = jnp.sin(ang)[None, :, None].astype(x.dtype)
    xr = x.reshape(B, T, H, 2, 2, nf)
    x1, x2 = xr[..., 0, :], xr[..., 1, :]
    out = jnp.stack([x1 * cos - x2 * sin, x1 * sin + x2 * cos], axis=-2)
    return out.reshape(B, T, H, Dh)


def context_attention(q, k, v):
    B, L, H, Dh = q.shape
    nb = L // ATTN_BLOCK
    qb = q.reshape(B, nb, ATTN_BLOCK, H, Dh).transpose(1, 0, 2, 3, 4)
    scale = Dh ** -0.5

    def block(qi):
        s = jnp.einsum('bqhd,bkhd->bhqk', qi, k).astype(F32) * scale
        p = jax.nn.softmax(s, axis=-1).astype(v.dtype)
        return jnp.einsum('bhqk,bkhd->bqhd', p, v)

    o = lax.map(block, qb)
    return o.transpose(1, 0, 2, 3, 4).reshape(B, L, H, Dh)


def neighbourhood_attention(q, k, v, k_ctx, v_ctx, rpb):
    B, T, H, Dh = q.shape
    rows = T // GRID_W
    kr = min(NA_WIN_ROWS, rows)
    kc = NA_WIN_COLS
    scale = Dh ** -0.5
    kg = k.reshape(B, rows, GRID_W, H, Dh)
    vg = v.reshape(B, rows, GRID_W, H, Dh)
    qg = q.reshape(B, rows, GRID_W, H, Dh).transpose(1, 0, 2, 3, 4)
    cols = jnp.arange(GRID_W)
    col_start = jnp.clip(cols - kc // 2, 0, GRID_W - kc)
    col_mask = (cols[None, :] >= col_start[:, None]) & (cols[None, :] < col_start[:, None] + kc)
    col_idx = jnp.clip(cols[None, :] - cols[:, None] + NA_WIN_COLS - 1, 0, 2 * NA_WIN_COLS - 2)
    r_idx = jnp.arange(rows)
    row_start = jnp.clip(r_idx - kr // 2, 0, rows - kr)

    def row_block(args):
        r, start, q_r = args
        k_blk = lax.dynamic_slice_in_dim(kg, start, kr, axis=1)
        v_blk = lax.dynamic_slice_in_dim(vg, start, kr, axis=1)
        row_off = start + jnp.arange(kr) - r + NA_WIN_ROWS - 1
        bias = rpb[:, row_off[None, :, None], col_idx[:, None, :]].astype(F32)
        s_loc = jnp.einsum('bqhd,bkwhd->bhqkw', q_r, k_blk).astype(F32) * scale + bias[None]
        s_loc = jnp.where(col_mask[:, None, :], s_loc, -jnp.inf).reshape(B, H, GRID_W, kr * GRID_W)
        s_ctx = jnp.einsum('bqhd,bchd->bhqc', q_r, k_ctx).astype(F32) * scale
        p = jax.nn.softmax(jnp.concatenate([s_loc, s_ctx], axis=-1), axis=-1).astype(v.dtype)
        p_loc = p[..., :kr * GRID_W].reshape(B, H, GRID_W, kr, GRID_W)
        p_ctx = p[..., kr * GRID_W:]
        return (jnp.einsum('bhqkw,bkwhd->bqhd', p_loc, v_blk)
                + jnp.einsum('bhqc,bchd->bqhd', p_ctx, v_ctx))

    o = lax.map(row_block, (r_idx, row_start, qg))
    return o.transpose(1, 0, 2, 3, 4).reshape(B, T, H, Dh)


def mlstm_chunkwise(q, k, v, log_i, log_f, C0, n0, m0):
    B, H, T, Dh = q.shape
    L = ML_CHUNK
    nc = T // L

    def to_chunks(a):
        return jnp.moveaxis(a.reshape(B, H, nc, L, *a.shape[3:]), 2, 0)

    causal = jnp.tril(jnp.ones((L, L), dtype=bool))

    def step(carry, xs):
        C, n, m = carry
        qc, kc, vc, li, lf = xs
        b = jnp.cumsum(lf, axis=-1)
        d = jnp.where(causal, b[..., :, None] - b[..., None, :] + li[..., None, :], -jnp.inf)
        inter = b + m[..., None]
        m_t = jnp.maximum(inter, d.max(axis=-1))
        w = jnp.exp(d - m_t[..., None])
        a = jnp.exp(inter - m_t)
        s = jnp.einsum('bhtd,bhsd->bhts', qc, kc) * w
        num = a[..., None] * jnp.einsum('bhtd,bhde->bhte', qc, C) + jnp.einsum('bhts,bhse->bhte', s, vc)
        den = a * jnp.einsum('bhtd,bhd->bht', qc, n) + s.sum(axis=-1)
        h = num / jnp.maximum(jnp.abs(den), jnp.exp(-m_t))[..., None]
        b_last = b[..., -1]
        g = b_last[..., None] - b + li
        m_new = jnp.maximum(b_last + m, g.max(axis=-1))
        a_s = jnp.exp(b_last + m - m_new)
        w_s = jnp.exp(g - m_new[..., None])
        C_new = a_s[..., None, None] * C + jnp.einsum('bhs,bhsd,bhse->bhde', w_s, kc, vc)
        n_new = a_s[..., None] * n + jnp.einsum('bhs,bhsd->bhd', w_s, kc)
        return (C_new, n_new, m_new), h

    (C, n, m), hs = lax.scan(step, (C0, n0, m0),
                             (to_chunks(q), to_chunks(k), to_chunks(v), to_chunks(log_i), to_chunks(log_f)))
    return jnp.moveaxis(hs, 0, 2).reshape(B, H, T, Dh), C, n, m


def mlstm_bidir(q, k, v, i_pre, f_pre, C0, n0, m0):
    out_dtype = q.dtype
    qf = jnp.moveaxis(q, 2, 1).astype(F32)
    kf = jnp.moveaxis(k, 2, 1).astype(F32) * (ML_HEAD_DIM ** -0.5)
    vf = jnp.moveaxis(v, 2, 1).astype(F32)
    log_i = jnp.moveaxis(i_pre, 1, -1)
    log_f = jax.nn.log_sigmoid(jnp.moveaxis(f_pre, 1, -1))
    C0, n0, m0 = C0.astype(F32), n0.astype(F32), m0.astype(F32)
    flip = lambda a: jnp.flip(a, axis=2)
    h_f, C_f, n_f, m_f = mlstm_chunkwise(qf, kf, vf, log_i[:, 0], log_f[:, 0], C0[:, 0], n0[:, 0], m0[:, 0])
    h_b, C_b, n_b, m_b = mlstm_chunkwise(flip(qf), flip(kf), flip(vf), flip(log_i[:, 1]), flip(log_f[:, 1]),
                                         C0[:, 1], n0[:, 1], m0[:, 1])
    h = jnp.moveaxis(h_f + flip(h_b), 1, 2)
    C = jnp.stack([C_f, C_b], axis=1).astype(out_dtype)
    n = jnp.stack([n_f, n_b], axis=1).astype(out_dtype)
    m = jnp.stack([m_f, m_b], axis=1).astype(out_dtype)
    return h, C, n, m


def mlstm_readout(h, mo, g):
    B, T = h.shape[:2]
    hf = h.astype(F32)
    hn = hf * lax.rsqrt(jnp.mean(hf * hf, axis=-1, keepdims=True) + EPS)
    return (hn.reshape(B, T, ML_DIM) * g.astype(F32) * jax.nn.sigmoid(mo.astype(F32))).astype(mo.dtype)


def merge(conv_o, na_o, ml_o, w_out):
    B, T, _ = conv_o.shape
    return jnp.concatenate([conv_o, na_o.reshape(B, T, NA_DIM), ml_o], axis=-1) @ w_out


def hier_moe(h, w_rg, b_rg, w_re, b_re, w_gate, w_up, w_down):
    shp = h.shape
    x = h.reshape(-1, D_MODEL)
    N = x.shape[0]
    tok = jnp.arange(N)
    gl = (x @ w_rg).astype(F32) + b_rg.astype(F32)
    gp = jax.nn.softmax(gl, axis=-1)
    g_top = jnp.argmax(gl, axis=-1)
    el = ((x @ w_re).astype(F32) + b_re.astype(F32)).reshape(N, N_GROUPS, EXPERTS_PER_GROUP)
    top_v, top_i = lax.top_k(el[tok, g_top], TOP_K)
    w = jax.nn.softmax(top_v, axis=-1) * gp[tok, g_top][:, None]
    eid = g_top[:, None] * EXPERTS_PER_GROUP + top_i
    gates = jnp.sum(jax.nn.one_hot(eid, N_EXPERTS, dtype=F32) * w[..., None], axis=1)
    a = jax.nn.silu(jnp.einsum('nd,edf->nef', x, w_gate)) * jnp.einsum('nd,edf->nef', x, w_up)
    y = jnp.einsum('nef,efd->nd', a * gates[..., None].astype(a.dtype), w_down)
    return y.reshape(shp)


def context_layer(x, mods, ln1_g, w_in, conv_w, ml_gate_b, ml_norm_g, w_out, ln2_g,
                  w_rg, b_rg, w_re, b_re, w_gate, w_up, w_down):
    sh1, sc1, g1, sh2, sc2, g2 = mods
    B = x.shape[0]
    h = rmsnorm(x, ln1_g) * (1 + sc1) + sh1
    (cb, cc, ch), (nq, nk, nv), (mq, mk, mv), mo, ig, fg = project(h, w_in, ml_gate_b)
    conv_o = cb * short_conv(cc * ch, conv_w)
    na_o = context_attention(nq, nk, nv)
    C0 = jnp.zeros((B, N_DIRS, ML_HEADS, ML_HEAD_DIM, ML_HEAD_DIM), F32)
    n0 = jnp.zeros((B, N_DIRS, ML_HEADS, ML_HEAD_DIM), F32)
    m0 = jnp.zeros((B, N_DIRS, ML_HEADS), F32)
    ml_h, C, n, m = mlstm_bidir(mq, mk, mv, ig, fg, C0, n0, m0)
    ml_o = mlstm_readout(ml_h, mo, ml_norm_g)
    x = x + g1 * merge(conv_o, na_o, ml_o, w_out)
    h2 = rmsnorm(x, ln2_g) * (1 + sc2) + sh2
    x = x + g2 * hier_moe(h2, w_rg, b_rg, w_re, b_re, w_gate, w_up, w_down)
    return x, nk, nv, C, n, m


def latent_layer(x, mods, k_ctx, v_ctx, C0, n0, m0, ln1_g, w_in, conv_w, na_rpb, ml_gate_b, ml_norm_g,
                 w_out, ln2_g, w_rg, b_rg, w_re, b_re, w_gate, w_up, w_down):
    sh1, sc1, g1, sh2, sc2, g2 = mods
    h = rmsnorm(x, ln1_g) * (1 + sc1) + sh1
    (cb, cc, ch), (nq, nk, nv), (mq, mk, mv), mo, ig, fg = project(h, w_in, ml_gate_b)
    conv_o = cb * short_conv(cc * ch, conv_w)
    na_o = neighbourhood_attention(nq, nk, nv, k_ctx, v_ctx, na_rpb)
    ml_h, _, _, _ = mlstm_bidir(axial_rope(mq), axial_rope(mk), mv, ig, fg, C0, n0, m0)
    ml_o = mlstm_readout(ml_h, mo, ml_norm_g)
    x = x + g1 * merge(conv_o, na_o, ml_o, w_out)
    h2 = rmsnorm(x, ln2_g) * (1 + sc2) + sh2
    x = x + g2 * hier_moe(h2, w_rg, b_rg, w_re, b_re, w_gate, w_up, w_down)
    return x


def setup_inputs(seed: int = 0) -> dict:
    key = jax.random.key(seed)
    ks = jax.random.split(key, 32)
    nrm = lambda k, shape, s: jax.random.normal(k, shape, F32) * s
    x_prompt = nrm(ks[0], (BATCH, SEQ, D_MODEL), 1.0)
    x_sample = nrm(ks[1], (DEC_BATCH, DEC_SEQ, D_MODEL), 1.0)
    cache_k = nrm(ks[2], (DEC_BATCH, DEPTH, PAST_LEN, NA_HEADS, NA_HEAD_DIM), 1.0)
    cache_v = nrm(ks[3], (DEC_BATCH, DEPTH, PAST_LEN, NA_HEADS, NA_HEAD_DIM), 1.0)
    state_C = nrm(ks[4], (DEC_BATCH, DEPTH, N_DIRS, ML_HEADS, ML_HEAD_DIM, ML_HEAD_DIM), 0.1)
    state_n = nrm(ks[5], (DEC_BATCH, DEPTH, N_DIRS, ML_HEADS, ML_HEAD_DIM), 0.1)
    state_m = nrm(ks[6], (DEC_BATCH, DEPTH, N_DIRS, ML_HEADS), 0.5)
    c = nrm(ks[7], (DEC_BATCH, D_MODEL), 1.0)
    c_ctx = nrm(ks[8], (D_MODEL,), 1.0)
    w_mod = nrm(ks[9], (DEPTH, D_MODEL, N_MOD * D_MODEL), 0.5 * D_MODEL ** -0.5)
    b_mod = nrm(ks[10], (DEPTH, N_MOD * D_MODEL), 0.02)
    ln1_g = 1.0 + nrm(ks[11], (DEPTH, D_MODEL), 0.02)
    w_in = nrm(ks[12], (DEPTH, D_MODEL, IN_COLS), D_MODEL ** -0.5)
    conv_w = nrm(ks[13], (DEPTH, CONV_WIDTH, CONV_DIM), CONV_WIDTH ** -0.5)
    na_rpb = nrm(ks[14], (DEPTH, NA_HEADS, 2 * NA_WIN_ROWS - 1, 2 * NA_WIN_COLS - 1), 0.02)
    ml_gate_b = jnp.concatenate([nrm(ks[15], (DEPTH, 1, N_DIRS, ML_HEADS), 0.1),
                                 3.0 + nrm(ks[16], (DEPTH, 1, N_DIRS, ML_HEADS), 0.1)], axis=1)
    ml_norm_g = 1.0 + nrm(ks[17], (DEPTH, ML_DIM), 0.02)
    w_out = nrm(ks[18], (DEPTH, MIX_DIM, D_MODEL), MIX_DIM ** -0.5)
    ln2_g = 1.0 + nrm(ks[19], (DEPTH, D_MODEL), 0.02)
    w_rg = nrm(ks[20], (DEPTH, D_MODEL, N_GROUPS), D_MODEL ** -0.5)
    b_rg = nrm(ks[21], (DEPTH, N_GROUPS), 0.01)
    w_re = nrm(ks[22], (DEPTH, D_MODEL, N_EXPERTS), D_MODEL ** -0.5)
    b_re = nrm(ks[23], (DEPTH, N_EXPERTS), 0.01)
    w_gate = nrm(ks[24], (DEPTH, N_EXPERTS, D_MODEL, EXPERT_FF), D_MODEL ** -0.5)
    w_up = nrm(ks[25], (DEPTH, N_EXPERTS, D_MODEL, EXPERT_FF), D_MODEL ** -0.5)
    w_down = nrm(ks[26], (DEPTH, N_EXPERTS, EXPERT_FF, D_MODEL), EXPERT_FF ** -0.5)
    final_g = 1.0 + nrm(ks[27], (D_MODEL,), 0.02)
    return {'x_prompt': x_prompt, 'x_sample': x_sample, 'cache_k': cache_k, 'cache_v': cache_v,
            'state_C': state_C, 'state_n': state_n, 'state_m': state_m, 'c': c, 'c_ctx': c_ctx,
            'w_mod': w_mod, 'b_mod': b_mod, 'ln1_g': ln1_g, 'w_in': w_in, 'conv_w': conv_w,
            'na_rpb': na_rpb, 'ml_gate_b': ml_gate_b, 'ml_norm_g': ml_norm_g, 'w_out': w_out,
            'ln2_g': ln2_g, 'w_rg': w_rg, 'b_rg': b_rg, 'w_re': w_re, 'b_re': b_re,
            'w_gate': w_gate, 'w_up': w_up, 'w_down': w_down, 'final_g': final_g}


def reference(x_prompt, x_sample, cache_k, cache_v, state_C, state_n, state_m, c, c_ctx,
              w_mod, b_mod, ln1_g, w_in, conv_w, na_rpb, ml_gate_b, ml_norm_g, w_out,
              ln2_g, w_rg, b_rg, w_re, b_re, w_gate, w_up, w_down, final_g):
    xp = x_prompt
    ks, vs, Cs, ns, ms = [], [], [], [], []
    for l in range(DEPTH):
        mods = adaln_params(c_ctx, w_mod[l], b_mod[l])
        xp, k_l, v_l, C_l, n_l, m_l = context_layer(
            xp, mods, ln1_g[l], w_in[l], conv_w[l], ml_gate_b[l], ml_norm_g[l], w_out[l], ln2_g[l],
            w_rg[l], b_rg[l], w_re[l], b_re[l], w_gate[l], w_up[l], w_down[l])
        ks.append(k_l)
        vs.append(v_l)
        Cs.append(C_l)
        ns.append(n_l)
        ms.append(m_l)
    xs = x_sample
    for l in range(DEPTH):
        mods = adaln_params(c, w_mod[l], b_mod[l])
        xs = latent_layer(
            xs, mods, cache_k[:, l], cache_v[:, l], state_C[:, l], state_n[:, l], state_m[:, l],
            ln1_g[l], w_in[l], conv_w[l], na_rpb[l], ml_gate_b[l], ml_norm_g[l], w_out[l], ln2_g[l],
            w_rg[l], b_rg[l], w_re[l], b_re[l], w_gate[l], w_up[l], w_down[l])
    y_prompt = rmsnorm(xp, final_g)
    y_sample = rmsnorm(xs, final_g)
    new_cache_k = jnp.stack(ks, axis=1)
    new_cache_v = jnp.stack(vs, axis=1)
    new_state_C = jnp.stack(Cs, axis=1)
    new_state_n = jnp.stack(ns, axis=1)
    new_state_m = jnp.stack(ms, axis=1)
    return (y_prompt, y_sample, new_cache_k, new_cache_v, new_state_C, new_state_n, new_state_m)
```

```python
import functools

import numpy as np
import jax
import jax.numpy as jnp
from jax import lax
from jax.experimental import pallas as pl
from jax.experimental.pallas import tpu as pltpu

F32 = jnp.float32
BF16 = jnp.bfloat16

D_MODEL = 1024
SEQ = 256
DEPTH = 4
DEC_SEQ = 1024
PAST_LEN = 512
GRID_W = 64
CONV_DIM = 256
NA_HEADS = 4
NA_HEAD_DIM = 64
NA_DIM = NA_HEADS * NA_HEAD_DIM
NA_WIN_ROWS = 8
NA_WIN_COLS = 16
ML_HEADS = 4
ML_HEAD_DIM = 128
ML_DIM = ML_HEADS * ML_HEAD_DIM
N_DIRS = 2
N_GROUPS = 4
EXPERTS_PER_GROUP = 4
N_EXPERTS = N_GROUPS * EXPERTS_PER_GROUP
EXPERT_FF = 512
ROPE_BASE = 10000.0
EPS = 1e-6
N_MOD = 6

LANES = 128
N_GATE = N_DIRS * ML_HEADS
C_CONV = 0
C_NA = 3 * CONV_DIM
C_ML = C_NA + 3 * NA_DIM
C_IG = C_ML + 4 * ML_DIM
C_FG = C_IG + LANES
W_COLS = C_FG + LANES
CUMSUM_BLOCK = 256
MODS_ROWS = 8
VMEM_LIMIT = 56 * 1024 * 1024

NEG_INF = float("-inf")
HIGHEST = lax.Precision.HIGHEST


def _dot(a, b):
    return jnp.dot(a, b, preferred_element_type=F32)


def _dot_nt(a, b):
    return lax.dot_general(a, b, (((1,), (1,)), ((), ())), preferred_element_type=F32)


def _rms_mod(x, g, sc, sh):
    y = x * lax.rsqrt(jnp.mean(x * x, axis=-1, keepdims=True) + EPS)
    return (y * g) * (1.0 + sc) + sh


def _mod_rows(mods_ref, row, first):
    return [mods_ref[pl.ds(row, 1), pl.ds((first + j) * D_MODEL, D_MODEL)] for j in range(3)]


def _mods_kernel(cv_ref, w_ref, b_ref, o_ref):
    cv = cv_ref[...]
    s = cv * jax.nn.sigmoid(cv)
    o_ref[...] = _dot(s.astype(BF16), w_ref[...].astype(BF16)) + b_ref[...]


def _mods_call(cvecs, w_mod, b_mod):
    tn = 1536
    n = N_MOD * D_MODEL
    return pl.pallas_call(
        _mods_kernel,
        grid=(DEPTH, n // tn),
        in_specs=[pl.BlockSpec((MODS_ROWS, D_MODEL), lambda l, j: (0, 0)),
                  pl.BlockSpec((None, D_MODEL, tn), lambda l, j: (l, 0, j)),
                  pl.BlockSpec((None, 1, tn), lambda l, j: (l, 0, j))],
        out_specs=pl.BlockSpec((None, MODS_ROWS, tn), lambda l, j: (l, 0, j)),
        out_shape=jax.ShapeDtypeStruct((DEPTH, MODS_ROWS, n), F32),
        compiler_params=pltpu.CompilerParams(vmem_limit_bytes=VMEM_LIMIT),
        name="adaln_mods",
    )(cvecs, w_mod, b_mod.reshape(DEPTH, 1, n))


def _short_conv(zc, cw):
    T = zc.shape[0]
    cb = zc[:, 0:CONV_DIM]
    u = zc[:, CONV_DIM:2 * CONV_DIM] * zc[:, 2 * CONV_DIM:3 * CONV_DIM]
    t = lax.broadcasted_iota(jnp.int32, u.shape, 0)
    u_prev = jnp.where(t == 0, 0.0, pltpu.roll(u, 1, axis=0))
    u_next = jnp.where(t == T - 1, 0.0, pltpu.roll(u, T - 1, axis=0))
    return cb * (cw[0:1, :] * u_prev + cw[1:2, :] * u + cw[2:3, :] * u_next)


def _softmax_attention(nq, nk, nv):
    outs = []
    for h in range(NA_HEADS):
        sl = slice(h * NA_HEAD_DIM, (h + 1) * NA_HEAD_DIM)
        q = (nq[:, sl] * NA_HEAD_DIM ** -0.5).astype(BF16)
        s = _dot_nt(q, nk[:, sl].astype(BF16))
        p = jnp.exp(s - jnp.max(s, axis=-1, keepdims=True))
        o = _dot(p.astype(BF16), nv[:, sl].astype(BF16)) / jnp.sum(p, axis=-1, keepdims=True)
        outs.append(o)
    return jnp.concatenate(outs, axis=-1)


def _log_sigmoid(x):
    return jnp.minimum(x, 0.0) - jnp.log(1.0 + jnp.exp(-jnp.abs(x)))


def _gate_terms(zi, zf):
    T = zi.shape[0]
    lf = _log_sigmoid(zf)
    blk = min(T, CUMSUM_BLOCK)
    r = lax.broadcasted_iota(jnp.int32, (blk, blk), 0)
    c = lax.broadcasted_iota(jnp.int32, (blk, blk), 1)
    tril = jnp.where(c <= r, 1.0, 0.0).astype(F32)
    parts, carry = [], None
    for r0 in range(0, T, blk):
        b = jnp.dot(tril, lf[r0:r0 + blk], precision=HIGHEST, preferred_element_type=F32)
        if carry is not None:
            b = b + carry
        carry = b[blk - 1:blk, :]
        parts.append(b)
    b_fwd = parts[0] if len(parts) == 1 else jnp.concatenate(parts, axis=0)
    b_bwd = (b_fwd[T - 1:T, :] - b_fwd) + lf
    lane = lax.broadcasted_iota(jnp.int32, zi.shape, 1)
    b_col = jnp.where(lane < ML_HEADS, b_fwd, b_bwd)
    row_t = jnp.transpose(zi - b_col)
    return b_col, row_t


def _mlstm_outputs(q, k, v, b_col, li_row, m0, c0, n0, backward, q_block):
    T = q.shape[0]
    qb, kb, vb = q.astype(BF16), k.astype(BF16), v.astype(BF16)
    c0b = None if c0 is None else c0.astype(BF16)
    outs = []
    for r0 in range(0, T, q_block):
        ks, ke = (r0, T) if backward else (0, r0 + q_block)
        bq = b_col[r0:r0 + q_block]
        d = bq + li_row[:, ks:ke]
        t_idx = r0 + lax.broadcasted_iota(jnp.int32, d.shape, 0)
        s_idx = ks + lax.broadcasted_iota(jnp.int32, d.shape, 1)
        d = jnp.where((s_idx >= t_idx) if backward else (s_idx <= t_idx), d, NEG_INF)
        inter = bq + m0
        m_t = jnp.maximum(jnp.max(d, axis=-1, keepdims=True), inter)
        w = jnp.exp(d - m_t)
        s = _dot_nt(qb[r0:r0 + q_block], kb[ks:ke]) * w
        num = _dot(s.astype(BF16), vb[ks:ke])
        den = jnp.sum(s, axis=-1, keepdims=True)
        if c0 is not None:
            a = jnp.exp(inter - m_t)
            num = num + a * _dot(qb[r0:r0 + q_block], c0b)
            den = den + a * jnp.sum(q[r0:r0 + q_block] * n0, axis=-1, keepdims=True)
        outs.append(num / jnp.maximum(jnp.abs(den), jnp.exp(-m_t)))
    return outs[0] if len(outs) == 1 else jnp.concatenate(outs, axis=0)


def _mlstm_state(k, v, b_col, li_col, backward):
    T = k.shape[0]
    b_tot = b_col[0:1] if backward else b_col[T - 1:T]
    g = (b_tot - b_col) + li_col
    m_new = jnp.maximum(b_tot, jnp.max(g, axis=0, keepdims=True))
    kw = k * jnp.exp(g - m_new)
    c_new = _dot(jnp.transpose(kw).astype(BF16), v.astype(BF16))
    n_new = jnp.sum(kw, axis=0, keepdims=True)
    return c_new, n_new, m_new


def _head_readout(h, mo, g):
    hn = h * lax.rsqrt(jnp.mean(h * h, axis=-1, keepdims=True) + EPS)
    return hn * g * jax.nn.sigmoid(mo)


def _ctx_mixer_kernel(x_ref, mods_ref, ln_ref, win_ref, convw_ref, gb_ref, mlg_ref, wout_ref, *rest):
    xo_ref, ko_ref, vo_ref, co_ref, no_ref, mo_ref = rest[-6:]
    T = SEQ
    sh1, sc1, g1 = _mod_rows(mods_ref, 0, 0)
    x = x_ref[...]
    h = _rms_mod(x, ln_ref[...], sc1, sh1).astype(BF16)

    conv_o = _short_conv(_dot(h, win_ref[:, C_CONV:C_NA]), convw_ref[...])

    zn = _dot(h, win_ref[:, C_NA:C_ML])
    nk, nv = zn[:, NA_DIM:2 * NA_DIM], zn[:, 2 * NA_DIM:3 * NA_DIM]
    ko_ref[...] = nk
    vo_ref[...] = nv
    na_o = _softmax_attention(zn[:, 0:NA_DIM], nk, nv)

    zg = _dot(h, win_ref[:, C_IG:W_COLS])
    zi = zg[:, 0:LANES] + gb_ref[:, 0:LANES]
    zf = zg[:, LANES:2 * LANES] + gb_ref[:, LANES:2 * LANES]
    b_col, row_t = _gate_terms(zi, zf)
    m0 = jnp.zeros((1, 1), F32)

    zm = _dot(h, win_ref[:, C_ML:C_IG])
    ml_parts = []
    for hh in range(ML_HEADS):
        zq, zk, zv, zo = (zm[:, p * ML_DIM + hh * ML_HEAD_DIM:p * ML_DIM + (hh + 1) * ML_HEAD_DIM]
                          for p in range(4))
        zk = zk * ML_HEAD_DIM ** -0.5
        hsum = None
        for d in range(N_DIRS):
            j = d * ML_HEADS + hh
            bc = b_col[:, j:j + 1]
            ho = _mlstm_outputs(zq, zk, zv, bc, row_t[j:j + 1, :], m0, None, None, d == 1, T)
            hsum = ho if hsum is None else hsum + ho
            c_new, n_new, m_new = _mlstm_state(zk, zv, bc, zi[:, j:j + 1], d == 1)
            co_ref[d, hh] = c_new
            no_ref[j:j + 1, :] = n_new
            mo_ref[j:j + 1, :] = jnp.broadcast_to(m_new, (1, LANES))
        ml_parts.append(_head_readout(hsum, zo, mlg_ref[:, hh * ML_HEAD_DIM:(hh + 1) * ML_HEAD_DIM]))

    mix = jnp.concatenate([conv_o, na_o] + ml_parts, axis=-1).astype(BF16)
    xo_ref[...] = x + g1 * _dot(mix, wout_ref[...])


def _ctx_mixer_call(layer, x, mods, ln1, w_in, conv_w, gate_b, ml_g, w_out, prev):
    B = x.shape[0]
    T = SEQ
    state_shapes = [
        jax.ShapeDtypeStruct((B, DEPTH, T, NA_DIM), F32),
        jax.ShapeDtypeStruct((B, DEPTH, T, NA_DIM), F32),
        jax.ShapeDtypeStruct((B, DEPTH, N_DIRS, ML_HEADS, ML_HEAD_DIM, ML_HEAD_DIM), F32),
        jax.ShapeDtypeStruct((B, DEPTH, N_GATE, ML_HEAD_DIM), F32),
        jax.ShapeDtypeStruct((B, DEPTH, N_GATE, LANES), F32),
    ]
    in_specs = [
        pl.BlockSpec((None, T, D_MODEL), lambda b: (b, 0, 0)),
        pl.BlockSpec((None, MODS_ROWS, N_MOD * D_MODEL), lambda b: (layer, 0, 0)),
        pl.BlockSpec((None, 1, D_MODEL), lambda b: (layer, 0, 0)),
        pl.BlockSpec((None, D_MODEL, W_COLS), lambda b: (layer, 0, 0)),
        pl.BlockSpec((None, 3, CONV_DIM), lambda b: (layer, 0, 0)),
        pl.BlockSpec((None, 1, 2 * LANES), lambda b: (layer, 0, 0)),
        pl.BlockSpec((None, 1, ML_DIM), lambda b: (layer, 0, 0)),
        pl.BlockSpec((None, D_MODEL, D_MODEL), lambda b: (layer, 0, 0)),
    ]
    args = [x, mods, ln1, w_in, conv_w, gate_b, ml_g, w_out]
    aliases = {}
    if prev is not None:
        in_specs += [pl.BlockSpec(memory_space=pl.ANY)] * len(prev)
        aliases = {len(args) + i: 1 + i for i in range(len(prev))}
        args += list(prev)
    out_specs = [
        pl.BlockSpec((None, T, D_MODEL), lambda b: (b, 0, 0)),
        pl.BlockSpec((None, None, T, NA_DIM), lambda b: (b, layer, 0, 0)),
        pl.BlockSpec((None, None, T, NA_DIM), lambda b: (b, layer, 0, 0)),
        pl.BlockSpec((None, None, N_DIRS, ML_HEADS, ML_HEAD_DIM, ML_HEAD_DIM),
                     lambda b: (b, layer, 0, 0, 0, 0)),
        pl.BlockSpec((None, None, N_GATE, ML_HEAD_DIM), lambda b: (b, layer, 0, 0)),
        pl.BlockSpec((None, None, N_GATE, LANES), lambda b: (b, layer, 0, 0)),
    ]
    outs = pl.pallas_call(
        _ctx_mixer_kernel,
        grid=(B,),
        in_specs=in_specs,
        out_specs=out_specs,
        out_shape=[jax.ShapeDtypeStruct(x.shape, F32)] + state_shapes,
        input_output_aliases=aliases,
        compiler_params=pltpu.CompilerParams(vmem_limit_bytes=VMEM_LIMIT),
        name="ctx_mixer",
    )(*args)
    return outs[0], tuple(outs[1:])


def _route(lg):
    lane = lax.broadcasted_iota(jnp.int32, lg.shape, 1)
    lane_f = lane.astype(F32)
    big = float(LANES)
    is_g = (lane >= N_EXPERTS) & (lane < N_EXPERTS + N_GROUPS)
    glm = jnp.where(is_g, lg, NEG_INF)
    gmax = jnp.max(glm, axis=-1, keepdims=True)
    g_top = jnp.min(jnp.where(glm == gmax, lane_f, big), axis=-1, keepdims=True) - N_EXPERTS
    gp = 1.0 / jnp.sum(jnp.where(is_g, jnp.exp(lg - gmax), 0.0), axis=-1, keepdims=True)
    grp = (lane >> (EXPERTS_PER_GROUP.bit_length() - 1)).astype(F32)
    in_grp = (lane < N_EXPERTS) & (grp == g_top)
    e1 = jnp.where(in_grp, lg, NEG_INF)
    v1 = jnp.max(e1, axis=-1, keepdims=True)
    i1 = jnp.min(jnp.where(e1 == v1, lane_f, big), axis=-1, keepdims=True)
    e2 = jnp.where(in_grp & (lane_f != i1), lg, NEG_INF)
    v2 = jnp.max(e2, axis=-1, keepdims=True)
    i2 = jnp.min(jnp.where(e2 == v2, lane_f, big), axis=-1, keepdims=True)
    t = jnp.exp(v2 - v1)
    w1 = 1.0 / (1.0 + t)
    w2 = t / (1.0 + t)
    return gp * (jnp.where(lane_f == i1, w1, 0.0) + jnp.where(lane_f == i2, w2, 0.0))


def _moe_kernel(x_ref, mods_ref, ln_ref, wr_ref, br_ref, wg_ref, wu_ref, wd_ref, fg_ref, o_ref,
                h2_scr, gates_scr, acc_scr, *, per_block_row, final):
    e = pl.program_id(1)
    row = (1 + pl.program_id(0)) if per_block_row else 0
    sh2, sc2, g2 = _mod_rows(mods_ref, row, 3)

    @pl.when(e == 0)
    def _():
        h2 = _rms_mod(x_ref[...], ln_ref[...], sc2, sh2)
        h2_scr[...] = h2.astype(BF16)
        lg = jnp.dot(h2, wr_ref[...], precision=HIGHEST, preferred_element_type=F32) + br_ref[...]
        gates_scr[...] = _route(lg)
        acc_scr[...] = jnp.zeros_like(acc_scr)

    h2 = h2_scr[...]
    gates = gates_scr[...]
    lane = lax.broadcasted_iota(jnp.int32, gates.shape, 1)
    ge = jnp.sum(jnp.where(lane == e, gates, 0.0), axis=-1, keepdims=True)
    a = jax.nn.silu(_dot(h2, wg_ref[...].astype(BF16))) * _dot(h2, wu_ref[...].astype(BF16))
    acc_scr[...] += _dot((a * ge).astype(BF16), wd_ref[...].astype(BF16))

    @pl.when(e == N_EXPERTS - 1)
    def _():
        y = x_ref[...] + g2 * acc_scr[...]
        if final:
            y = y * lax.rsqrt(jnp.mean(y * y, axis=-1, keepdims=True) + EPS) * fg_ref[...]
        o_ref[...] = y


def _moe_call(layer, x2d, mods, ln2, w_router, b_router, w_gate, w_up, w_down, final_g, *,
              per_block_row, final):
    n = x2d.shape[0]
    tb = 1024
    kern = functools.partial(_moe_kernel, per_block_row=per_block_row, final=final)
    return pl.pallas_call(
        kern,
        grid=(n // tb, N_EXPERTS),
        in_specs=[
            pl.BlockSpec((tb, D_MODEL), lambda i, e: (i, 0)),
            pl.BlockSpec((None, MODS_ROWS, N_MOD * D_MODEL), lambda i, e: (layer, 0, 0)),
            pl.BlockSpec((None, 1, D_MODEL), lambda i, e: (layer, 0, 0)),
            pl.BlockSpec((None, D_MODEL, LANES), lambda i, e: (layer, 0, 0)),
            pl.BlockSpec((None, 1, LANES), lambda i, e: (layer, 0, 0)),
            pl.BlockSpec((None, None, D_MODEL, EXPERT_FF), lambda i, e: (layer, e, 0, 0)),
            pl.BlockSpec((None, None, D_MODEL, EXPERT_FF), lambda i, e: (layer, e, 0, 0)),
            pl.BlockSpec((None, None, EXPERT_FF, D_MODEL), lambda i, e: (layer, e, 0, 0)),
            pl.BlockSpec((1, D_MODEL), lambda i, e: (0, 0)),
        ],
        out_specs=pl.BlockSpec((tb, D_MODEL), lambda i, e: (i, 0)),
        out_shape=jax.ShapeDtypeStruct(x2d.shape, F32),
        scratch_shapes=[pltpu.VMEM((tb, D_MODEL), BF16),
                        pltpu.VMEM((tb, LANES), F32),
                        pltpu.VMEM((tb, D_MODEL), F32)],
        compiler_params=pltpu.CompilerParams(vmem_limit_bytes=VMEM_LIMIT),
        name="moe",
    )(x2d, mods, ln2, w_router, b_router, w_gate, w_up, w_down, final_g)


LAT_PROJ_ROWS = 512


def _lat_proj_kernel(x_ref, mods_ref, ln_ref, win_ref, gb_ref, zc_ref, zn_ref, zm_ref, zg_ref):
    row = 1 + pl.program_id(0) // (DEC_SEQ // LAT_PROJ_ROWS)
    sh1, sc1, _ = _mod_rows(mods_ref, row, 0)
    h = _rms_mod(x_ref[...], ln_ref[...], sc1, sh1).astype(BF16)
    zc_ref[...] = _dot(h, win_ref[:, C_CONV:C_NA])
    zn_ref[...] = _dot(h, win_ref[:, C_NA:C_ML])
    zm_ref[...] = _dot(h, win_ref[:, C_ML:C_IG])
    zg_ref[...] = _dot(h, win_ref[:, C_IG:W_COLS]) + gb_ref[...]


def _lat_proj_call(layer, x2d, mods, ln1, w_in, gate_b):
    n = x2d.shape[0]
    tb = LAT_PROJ_ROWS
    widths = (C_NA - C_CONV, C_ML - C_NA, C_IG - C_ML, W_COLS - C_IG)
    return pl.pallas_call(
        _lat_proj_kernel,
        grid=(n // tb,),
        in_specs=[
            pl.BlockSpec((tb, D_MODEL), lambda i: (i, 0)),
            pl.BlockSpec((None, MODS_ROWS, N_MOD * D_MODEL), lambda i: (layer, 0, 0)),
            pl.BlockSpec((None, 1, D_MODEL), lambda i: (layer, 0, 0)),
            pl.BlockSpec((None, D_MODEL, W_COLS), lambda i: (layer, 0, 0)),
            pl.BlockSpec((None, 1, 2 * LANES), lambda i: (layer, 0, 0)),
        ],
        out_specs=[pl.BlockSpec((tb, w), lambda i: (i, 0)) for w in widths],
        out_shape=[jax.ShapeDtypeStruct((n, w), F32) for w in widths],
        compiler_params=pltpu.CompilerParams(vmem_limit_bytes=VMEM_LIMIT),
        name="lat_proj",
    )(x2d, mods, ln1, w_in, gate_b)


def _na_row_start(r):
    rows = DEC_SEQ // GRID_W
    return min(max(r - NA_WIN_ROWS // 2, 0), rows - NA_WIN_ROWS)


def _lat_local_kernel(zc_ref, zn_ref, kc_ref, vc_ref, band_ref, convw_ref, o_ref):
    o_ref[:, 0:CONV_DIM] = _short_conv(zc_ref[...], convw_ref[...])
    rows = DEC_SEQ // GRID_W
    blk = NA_WIN_ROWS * GRID_W
    q_all = (zn_ref[:, 0:NA_DIM] * NA_HEAD_DIM ** -0.5).astype(BF16)
    k_all = zn_ref[:, NA_DIM:2 * NA_DIM].astype(BF16)
    v_all = zn_ref[:, 2 * NA_DIM:3 * NA_DIM].astype(BF16)
    kc_all = kc_ref[...].astype(BF16)
    vc_all = vc_ref[...].astype(BF16)
    heads = []
    for h in range(NA_HEADS):
        sl = slice(h * NA_HEAD_DIM, (h + 1) * NA_HEAD_DIM)
        q, k, v, kc, vc = q_all[:, sl], k_all[:, sl], v_all[:, sl], kc_all[:, sl], vc_all[:, sl]
        band = band_ref[h]
        outs = []
        for r in range(rows):
            start = _na_row_start(r)
            off = (start - r + NA_WIN_ROWS - 1) * GRID_W
            q_r = q[r * GRID_W:(r + 1) * GRID_W]
            s_loc = _dot_nt(q_r, k[start * GRID_W:start * GRID_W + blk]) + band[:, off:off + blk]
            s_ctx = _dot_nt(q_r, kc)
            m = jnp.maximum(jnp.max(s_loc, axis=-1, keepdims=True), jnp.max(s_ctx, axis=-1, keepdims=True))
            p_loc = jnp.exp(s_loc - m)
            p_ctx = jnp.exp(s_ctx - m)
            den = jnp.sum(p_loc, axis=-1, keepdims=True) + jnp.sum(p_ctx, axis=-1, keepdims=True)
            o = _dot(p_loc.astype(BF16), v[start * GRID_W:start * GRID_W + blk]) + _dot(p_ctx.astype(BF16), vc)
            outs.append(o / den)
        heads.append(jnp.concatenate(outs, axis=0))
    o_ref[:, CONV_DIM:CONV_DIM + NA_DIM] = jnp.concatenate(heads, axis=-1)


def _lat_local_call(layer, zc, zn, cache_k, cache_v, band, conv_w):
    nb = zc.shape[0] // DEC_SEQ
    T = DEC_SEQ
    return pl.pallas_call(
        _lat_local_kernel,
        grid=(nb,),
        in_specs=[
            pl.BlockSpec((T, 3 * CONV_DIM), lambda b: (b, 0)),
            pl.BlockSpec((T, 3 * NA_DIM), lambda b: (b, 0)),
            pl.BlockSpec((None, None, PAST_LEN, NA_DIM), lambda b: (b, layer, 0, 0)),
            pl.BlockSpec((None, None, PAST_LEN, NA_DIM), lambda b: (b, layer, 0, 0)),
            pl.BlockSpec((None, NA_HEADS, GRID_W, band.shape[-1]), lambda b: (layer, 0, 0, 0)),
            pl.BlockSpec((None, 3, CONV_DIM), lambda b: (layer, 0, 0)),
        ],
        out_specs=pl.BlockSpec((T, CONV_DIM + NA_DIM), lambda b: (b, 0)),
        out_shape=jax.ShapeDtypeStruct((zc.shape[0], CONV_DIM + NA_DIM), F32),
        compiler_params=pltpu.CompilerParams(vmem_limit_bytes=VMEM_LIMIT),
        name="lat_conv_na",
    )(zc, zn, cache_k, cache_v, band, conv_w)


def _rope(x, cos, sin_signed):
    w = x.shape[-1]
    half = NA_HEAD_DIM // 2
    lane = lax.broadcasted_iota(jnp.int32, x.shape, 1)
    partner = jnp.where(lane % (2 * half) < half, pltpu.roll(x, w - half, axis=1), pltpu.roll(x, half, axis=1))
    return x * cos + partner * sin_signed


def _lat_mlstm_kernel(q_ref, k_ref, v_ref, o_ref, zg_ref, c0_ref, n0_ref, m0_ref, cos_ref, sin_ref,
                      mlg_ref, out_ref, bcol_scr, rowt_scr):
    hh = pl.program_id(1)

    @pl.when(hh == 0)
    def _():
        b_all, r_all = _gate_terms(zg_ref[:, 0:LANES], zg_ref[:, LANES:2 * LANES])
        bcol_scr[...] = b_all
        rowt_scr[...] = r_all

    b_col = bcol_scr[...]
    cos, sin_signed = cos_ref[...], sin_ref[...]
    q = _rope(q_ref[...], cos, sin_signed)
    k = _rope(k_ref[...], cos, sin_signed) * ML_HEAD_DIM ** -0.5
    v = v_ref[...]
    lane = lax.broadcasted_iota(jnp.int32, (DEC_SEQ, LANES), 1)
    hsum = None
    for d in range(N_DIRS):
        j = d * ML_HEADS + hh
        bc = jnp.sum(jnp.where(lane == j, b_col, 0.0), axis=-1, keepdims=True)
        lr = rowt_scr[pl.ds(j, 1), :]
        m0 = m0_ref[d, :, 0:1]
        ho = _mlstm_outputs(q, k, v, bc, lr, m0, c0_ref[d], n0_ref[d], d == 1, 256)
        hsum = ho if hsum is None else hsum + ho
    out_ref[...] = _head_readout(hsum, o_ref[...], mlg_ref[...])


def _lat_mlstm_call(layer, zm, zg, state_c, state_n, state_m, cos, sin_signed, ml_g):
    nb = zm.shape[0] // DEC_SEQ
    T = DEC_SEQ
    hd = ML_HEAD_DIM

    def col(part):
        return pl.BlockSpec((T, hd), lambda b, h: (b, part * ML_HEADS + h))

    return pl.pallas_call(
        _lat_mlstm_kernel,
        grid=(nb, ML_HEADS),
        in_specs=[
            col(0), col(1), col(2), col(3),
            pl.BlockSpec((T, 2 * LANES), lambda b, h: (b, 0)),
            pl.BlockSpec((None, None, N_DIRS, None, hd, hd), lambda b, h: (b, layer, 0, h, 0, 0)),
            pl.BlockSpec((None, None, N_DIRS, None, 1, hd), lambda b, h: (b, layer, 0, h, 0, 0)),
            pl.BlockSpec((None, None, N_DIRS, None, 1, LANES), lambda b, h: (b, layer, 0, h, 0, 0)),
            pl.BlockSpec((T, hd), lambda b, h: (0, 0)),
            pl.BlockSpec((T, hd), lambda b, h: (0, 0)),
            pl.BlockSpec((None, 1, hd), lambda b, h: (layer, 0, h)),
        ],
        out_specs=pl.BlockSpec((T, hd), lambda b, h: (b, h)),
        out_shape=jax.ShapeDtypeStruct((zm.shape[0], ML_DIM), F32),
        scratch_shapes=[pltpu.VMEM((T, LANES), F32), pltpu.VMEM((LANES, T), F32)],
        compiler_params=pltpu.CompilerParams(vmem_limit_bytes=VMEM_LIMIT),
        name="lat_mlstm",
    )(zm, zm, zm, zm, zg, state_c, state_n, state_m, cos, sin_signed, ml_g)


def _lat_merge_kernel(x_ref, loc_ref, ml_ref, mods_ref, wout_ref, o_ref):
    row = 1 + pl.program_id(0) // (DEC_SEQ // LAT_PROJ_ROWS)
    g1 = _mod_rows(mods_ref, row, 0)[2]
    split = CONV_DIM + NA_DIM
    y = _dot(loc_ref[...].astype(BF16), wout_ref[0:split, :]) + _dot(ml_ref[...].astype(BF16), wout_ref[split:, :])
    o_ref[...] = x_ref[...] + g1 * y


def _lat_merge_call(layer, x2d, loc, ml, mods, w_out):
    n = x2d.shape[0]
    tb = LAT_PROJ_ROWS
    return pl.pallas_call(
        _lat_merge_kernel,
        grid=(n // tb,),
        in_specs=[
            pl.BlockSpec((tb, D_MODEL), lambda i: (i, 0)),
            pl.BlockSpec((tb, CONV_DIM + NA_DIM), lambda i: (i, 0)),
            pl.BlockSpec((tb, ML_DIM), lambda i: (i, 0)),
            pl.BlockSpec((None, MODS_ROWS, N_MOD * D_MODEL), lambda i: (layer, 0, 0)),
            pl.BlockSpec((None, D_MODEL, D_MODEL), lambda i: (layer, 0, 0)),
        ],
        out_specs=pl.BlockSpec((tb, D_MODEL), lambda i: (i, 0)),
        out_shape=jax.ShapeDtypeStruct(x2d.shape, F32),
        compiler_params=pltpu.CompilerParams(vmem_limit_bytes=VMEM_LIMIT),
        name="lat_merge",
    )(x2d, loc, ml, mods, w_out)


def _pad_lanes(a, width):
    return jnp.pad(a, [(0, 0)] * (a.ndim - 1) + [(0, width - a.shape[-1])])


def _pack_w_in(w_in):
    main = w_in[..., :C_IG]
    gates = w_in[..., C_IG:]
    packed = jnp.concatenate([main, _pad_lanes(gates[..., :N_GATE], LANES),
                              _pad_lanes(gates[..., N_GATE:], LANES)], axis=-1)
    return packed.astype(BF16)


def _pack_gate_bias(ml_gate_b):
    gb = ml_gate_b.reshape(DEPTH, 2, N_GATE).astype(F32)
    return jnp.concatenate([_pad_lanes(gb[:, 0], LANES), _pad_lanes(gb[:, 1], LANES)], axis=-1)[:, None, :]


def _rpb_band(na_rpb):
    cols = np.arange(GRID_W)
    col_idx = np.clip(cols[None, :] - cols[:, None] + NA_WIN_COLS - 1, 0, 2 * NA_WIN_COLS - 2)
    col_start = np.clip(cols - NA_WIN_COLS // 2, 0, GRID_W - NA_WIN_COLS)
    col_mask = (cols[None, :] >= col_start[:, None]) & (cols[None, :] < col_start[:, None] + NA_WIN_COLS)
    t = na_rpb.astype(F32)[:, :, :, col_idx]
    t = jnp.where(col_mask[None, None, None], t, NEG_INF)
    t = jnp.transpose(t, (0, 1, 3, 2, 4))
    t = t.reshape(DEPTH, NA_HEADS, GRID_W, (2 * NA_WIN_ROWS - 1) * GRID_W)
    return _pad_lanes(t, 2 * NA_WIN_ROWS * GRID_W)


def _rope_tables():
    t = np.arange(DEC_SEQ)
    pos = np.stack([t // GRID_W, t % GRID_W], axis=-1).astype(np.float32)
    nf = ML_HEAD_DIM // 4
    inv = jnp.asarray(ROPE_BASE, F32) ** (-jnp.arange(nf, dtype=F32) / nf)
    ang = jnp.asarray(pos)[:, :, None] * inv
    cos = jnp.cos(ang)
    sin = jnp.sin(ang)
    cos_t = jnp.concatenate([cos, cos], axis=-1).reshape(DEC_SEQ, ML_HEAD_DIM)
    sin_t = jnp.concatenate([-sin, sin], axis=-1).reshape(DEC_SEQ, ML_HEAD_DIM)
    return cos_t, sin_t


def kernel(x_prompt, x_sample, cache_k, cache_v, state_C, state_n, state_m, c, c_ctx, w_mod, b_mod,
           ln1_g, w_in, conv_w, na_rpb, ml_gate_b, ml_norm_g, w_out, ln2_g, w_rg, b_rg, w_re, b_re,
           w_gate, w_up, w_down, final_g):
    nb_ctx = x_prompt.shape[0]
    nb_lat = x_sample.shape[0]
    assert 1 + nb_lat <= MODS_ROWS

    cvecs = jnp.concatenate([c_ctx[None, :], c,
                             jnp.zeros((MODS_ROWS - 1 - nb_lat, D_MODEL), F32)], axis=0)
    mods = _mods_call(cvecs, w_mod, b_mod)

    w_in_p = _pack_w_in(w_in)
    w_out_b = w_out.astype(BF16)
    gate_b = _pack_gate_bias(ml_gate_b)
    ln1 = ln1_g.reshape(DEPTH, 1, D_MODEL)
    ln2 = ln2_g.reshape(DEPTH, 1, D_MODEL)
    ml_g = ml_norm_g.reshape(DEPTH, 1, ML_DIM)
    w_router = _pad_lanes(jnp.concatenate([w_re, w_rg], axis=-1), LANES)
    b_router = _pad_lanes(jnp.concatenate([b_re, b_rg], axis=-1), LANES)[:, None, :]
    fg = final_g.reshape(1, D_MODEL)
    band = _rpb_band(na_rpb)
    cos_t, sin_t = _rope_tables()
    ck = cache_k.reshape(nb_lat, DEPTH, PAST_LEN, NA_DIM)
    cv = cache_v.reshape(nb_lat, DEPTH, PAST_LEN, NA_DIM)
    st_n = state_n.reshape(nb_lat, DEPTH, N_DIRS, ML_HEADS, 1, ML_HEAD_DIM)
    st_m = jnp.broadcast_to(state_m[..., None, None], (nb_lat, DEPTH, N_DIRS, ML_HEADS, 1, LANES))

    xp = x_prompt
    xs = x_sample.reshape(nb_lat * DEC_SEQ, D_MODEL)
    states = None
    for l in range(DEPTH):
        last = l == DEPTH - 1
        xp, states = _ctx_mixer_call(l, xp, mods, ln1, w_in_p, conv_w, gate_b, ml_g, w_out_b, states)
        xp = _moe_call(l, xp.reshape(nb_ctx * SEQ, D_MODEL), mods, ln2, w_router, b_router,
                       w_gate, w_up, w_down, fg, per_block_row=False, final=last)
        xp = xp.reshape(nb_ctx, SEQ, D_MODEL)

        zc, zn, zm, zg = _lat_proj_call(l, xs, mods, ln1, w_in_p, gate_b)
        loc = _lat_local_call(l, zc, zn, ck, cv, band, conv_w)
        ml = _lat_mlstm_call(l, zm, zg, state_C, st_n, st_m, cos_t, sin_t, ml_g)
        xs = _lat_merge_call(l, xs, loc, ml, mods, w_out_b)
        xs = _moe_call(l, xs, mods, ln2, w_router, b_router, w_gate, w_up, w_down, fg,
                       per_block_row=True, final=last)

    new_k, new_v, new_c, new_n, new_m = states
    return (xp, xs.reshape(nb_lat, DEC_SEQ, D_MODEL),
            new_k.reshape(nb_ctx, DEPTH, SEQ, NA_HEADS, NA_HEAD_DIM),
            new_v.reshape(nb_ctx, DEPTH, SEQ, NA_HEADS, NA_HEAD_DIM),
            new_c,
            new_n.reshape(nb_ctx, DEPTH, N_DIRS, ML_HEADS, ML_HEAD_DIM),
            new_m[..., 0].reshape(nb_ctx, DEPTH, N_DIRS, ML_HEADS))
```

```python
import functools

import numpy as np
import jax
import jax.numpy as jnp
from jax import lax
from jax.experimental import pallas as pl
from jax.experimental.pallas import tpu as pltpu

F32 = jnp.float32
BF16 = jnp.bfloat16

D_MODEL = 1024
SEQ = 256
DEPTH = 4
DEC_SEQ = 1024
PAST_LEN = 512
GRID_W = 64
CONV_DIM = 256
NA_HEADS = 4
NA_HEAD_DIM = 64
NA_DIM = NA_HEADS * NA_HEAD_DIM
NA_WIN_ROWS = 8
NA_WIN_COLS = 16
ML_HEADS = 4
ML_HEAD_DIM = 128
ML_DIM = ML_HEADS * ML_HEAD_DIM
N_DIRS = 2
N_GROUPS = 4
EXPERTS_PER_GROUP = 4
N_EXPERTS = N_GROUPS * EXPERTS_PER_GROUP
EXPERT_FF = 512
ROPE_BASE = 10000.0
EPS = 1e-6
N_MOD = 6

LANES = 128
N_GATE = N_DIRS * ML_HEADS
C_CONV = 0
C_NA = 3 * CONV_DIM
C_ML = C_NA + 3 * NA_DIM
C_IG = C_ML + 4 * ML_DIM
C_FG = C_IG + LANES
W_COLS = C_FG + LANES
CUMSUM_BLOCK = 256
MODS_ROWS = 8
VMEM_LIMIT = 56 * 1024 * 1024

NEG_INF = float("-inf")
HIGHEST = lax.Precision.HIGHEST


def _dot(a, b):
    return jnp.dot(a, b, preferred_element_type=F32)


def _dot_nt(a, b):
    return lax.dot_general(a, b, (((1,), (1,)), ((), ())), preferred_element_type=F32)


def _rms_mod(x, g, sc, sh):
    y = x * lax.rsqrt(jnp.mean(x * x, axis=-1, keepdims=True) + EPS)
    return (y * g) * (1.0 + sc) + sh


def _mod_rows(mods_ref, row, first):
    return [mods_ref[pl.ds(row, 1), pl.ds((first + j) * D_MODEL, D_MODEL)] for j in range(3)]


def _mods_kernel(cv_ref, w_ref, b_ref, o_ref):
    cv = cv_ref[...]
    s = cv * jax.nn.sigmoid(cv)
    o_ref[...] = _dot(s.astype(BF16), w_ref[...].astype(BF16)) + b_ref[...]


def _mods_call(cvecs, w_mod, b_mod):
    tn = 1536
    n = N_MOD * D_MODEL
    return pl.pallas_call(
        _mods_kernel,
        grid=(DEPTH, n // tn),
        in_specs=[pl.BlockSpec((MODS_ROWS, D_MODEL), lambda l, j: (0, 0)),
                  pl.BlockSpec((None, D_MODEL, tn), lambda l, j: (l, 0, j)),
                  pl.BlockSpec((None, 1, tn), lambda l, j: (l, 0, j))],
        out_specs=pl.BlockSpec((None, MODS_ROWS, tn), lambda l, j: (l, 0, j)),
        out_shape=jax.ShapeDtypeStruct((DEPTH, MODS_ROWS, n), F32),
        compiler_params=pltpu.CompilerParams(vmem_limit_bytes=VMEM_LIMIT),
        name="adaln_mods",
    )(cvecs, w_mod, b_mod.reshape(DEPTH, 1, n))


def _short_conv(zc, cw):
    T = zc.shape[0]
    cb = zc[:, 0:CONV_DIM]
    u = zc[:, CONV_DIM:2 * CONV_DIM] * zc[:, 2 * CONV_DIM:3 * CONV_DIM]
    t = lax.broadcasted_iota(jnp.int32, u.shape, 0)
    u_prev = jnp.where(t == 0, 0.0, pltpu.roll(u, 1, axis=0))
    u_next = jnp.where(t == T - 1, 0.0, pltpu.roll(u, T - 1, axis=0))
    return cb * (cw[0:1, :] * u_prev + cw[1:2, :] * u + cw[2:3, :] * u_next)


def _softmax_attention(nq, nk, nv):
    outs = []
    for h in range(NA_HEADS):
        sl = slice(h * NA_HEAD_DIM, (h + 1) * NA_HEAD_DIM)
        q = (nq[:, sl] * NA_HEAD_DIM ** -0.5).astype(BF16)
        s = _dot_nt(q, nk[:, sl].astype(BF16))
        p = jnp.exp(s - jnp.max(s, axis=-1, keepdims=True))
        o = _dot(p.astype(BF16), nv[:, sl].astype(BF16)) / jnp.sum(p, axis=-1, keepdims=True)
        outs.append(o)
    return jnp.concatenate(outs, axis=-1)


def _log_sigmoid(x):
    return jnp.minimum(x, 0.0) - jnp.log(1.0 + jnp.exp(-jnp.abs(x)))


def _gate_terms(zi, zf):
    T = zi.shape[0]
    lf = _log_sigmoid(zf)
    blk = min(T, CUMSUM_BLOCK)
    r = lax.broadcasted_iota(jnp.int32, (blk, blk), 0)
    c = lax.broadcasted_iota(jnp.int32, (blk, blk), 1)
    tril = jnp.where(c <= r, 1.0, 0.0).astype(F32)
    parts, carry = [], None
    for r0 in range(0, T, blk):
        b = jnp.dot(tril, lf[r0:r0 + blk], precision=HIGHEST, preferred_element_type=F32)
        if carry is not None:
            b = b + carry
        carry = b[blk - 1:blk, :]
        parts.append(b)
    b_fwd = parts[0] if len(parts) == 1 else jnp.concatenate(parts, axis=0)
    b_bwd = (b_fwd[T - 1:T, :] - b_fwd) + lf
    lane = lax.broadcasted_iota(jnp.int32, zi.shape, 1)
    b_col = jnp.where(lane < ML_HEADS, b_fwd, b_bwd)
    row_t = jnp.transpose(zi - b_col)
    return b_col, row_t


def _mlstm_outputs(q, k, v, b_col, li_row, m0, c0, n0, backward, q_block):
    T = q.shape[0]
    qb, kb, vb = q.astype(BF16), k.astype(BF16), v.astype(BF16)
    c0b = None if c0 is None else c0.astype(BF16)
    outs = []
    for r0 in range(0, T, q_block):
        ks, ke = (r0, T) if backward else (0, r0 + q_block)
        bq = b_col[r0:r0 + q_block]
        d = bq + li_row[:, ks:ke]
        t_idx = r0 + lax.broadcasted_iota(jnp.int32, d.shape, 0)
        s_idx = ks + lax.broadcasted_iota(jnp.int32, d.shape, 1)
        d = jnp.where((s_idx >= t_idx) if backward else (s_idx <= t_idx), d, NEG_INF)
        inter = bq + m0
        m_t = jnp.maximum(jnp.max(d, axis=-1, keepdims=True), inter)
        w = jnp.exp(d - m_t)
        s = _dot_nt(qb[r0:r0 + q_block], kb[ks:ke]) * w
        num = _dot(s.astype(BF16), vb[ks:ke])
        den = jnp.sum(s, axis=-1, keepdims=True)
        if c0 is not None:
            a = jnp.exp(inter - m_t)
            num = num + a * _dot(qb[r0:r0 + q_block], c0b)
            den = den + a * jnp.sum(q[r0:r0 + q_block] * n0, axis=-1, keepdims=True)
        outs.append(num / jnp.maximum(jnp.abs(den), jnp.exp(-m_t)))
    return outs[0] if len(outs) == 1 else jnp.concatenate(outs, axis=0)


def _mlstm_state(k, v, b_col, li_col, backward):
    T = k.shape[0]
    b_tot = b_col[0:1] if backward else b_col[T - 1:T]
    g = (b_tot - b_col) + li_col
    m_new = jnp.maximum(b_tot, jnp.max(g, axis=0, keepdims=True))
    kw = k * jnp.exp(g - m_new)
    c_new = _dot(jnp.transpose(kw).astype(BF16), v.astype(BF16))
    n_new = jnp.sum(kw, axis=0, keepdims=True)
    return c_new, n_new, m_new


def _head_readout(h, mo, g):
    hn = h * lax.rsqrt(jnp.mean(h * h, axis=-1, keepdims=True) + EPS)
    return hn * g * jax.nn.sigmoid(mo)


def _ctx_mixer_kernel(x_ref, mods_ref, ln_ref, win_ref, convw_ref, gb_ref, mlg_ref, wout_ref, *rest):
    xo_ref, ko_ref, vo_ref, co_ref, no_ref, mo_ref = rest[-6:]
    T = SEQ
    sh1, sc1, g1 = _mod_rows(mods_ref, 0, 0)
    x = x_ref[...]
    h = _rms_mod(x, ln_ref[...], sc1, sh1).astype(BF16)

    conv_o = _short_conv(_dot(h, win_ref[:, C_CONV:C_NA]), convw_ref[...])

    zn = _dot(h, win_ref[:, C_NA:C_ML])
    nk, nv = zn[:, NA_DIM:2 * NA_DIM], zn[:, 2 * NA_DIM:3 * NA_DIM]
    ko_ref[...] = nk
    vo_ref[...] = nv
    na_o = _softmax_attention(zn[:, 0:NA_DIM], nk, nv)

    zg = _dot(h, win_ref[:, C_IG:W_COLS])
    zi = zg[:, 0:LANES] + gb_ref[:, 0:LANES]
    zf = zg[:, LANES:2 * LANES] + gb_ref[:, LANES:2 * LANES]
    b_col, row_t = _gate_terms(zi, zf)
    m0 = jnp.zeros((1, 1), F32)

    zm = _dot(h, win_ref[:, C_ML:C_IG])
    ml_parts = []
    for hh in range(ML_HEADS):
        zq, zk, zv, zo = (zm[:, p * ML_DIM + hh * ML_HEAD_DIM:p * ML_DIM + (hh + 1) * ML_HEAD_DIM]
                          for p in range(4))
        zk = zk * ML_HEAD_DIM ** -0.5
        hsum = None
        for d in range(N_DIRS):
            j = d * ML_HEADS + hh
            bc = b_col[:, j:j + 1]
            ho = _mlstm_outputs(zq, zk, zv, bc, row_t[j:j + 1, :], m0, None, None, d == 1, T)
            hsum = ho if hsum is None else hsum + ho
            c_new, n_new, m_new = _mlstm_state(zk, zv, bc, zi[:, j:j + 1], d == 1)
            co_ref[d, hh] = c_new
            no_ref[j:j + 1, :] = n_new
            mo_ref[j:j + 1, :] = jnp.broadcast_to(m_new, (1, LANES))
        ml_parts.append(_head_readout(hsum, zo, mlg_ref[:, hh * ML_HEAD_DIM:(hh + 1) * ML_HEAD_DIM]))

    mix = jnp.concatenate([conv_o, na_o] + ml_parts, axis=-1).astype(BF16)
    xo_ref[...] = x + g1 * _dot(mix, wout_ref[...])


def _ctx_mixer_call(layer, x, mods, ln1, w_in, conv_w, gate_b, ml_g, w_out, prev):
    B = x.shape[0]
    T = SEQ
    state_shapes = [
        jax.ShapeDtypeStruct((B, DEPTH, T, NA_DIM), F32),
        jax.ShapeDtypeStruct((B, DEPTH, T, NA_DIM), F32),
        jax.ShapeDtypeStruct((B, DEPTH, N_DIRS, ML_HEADS, ML_HEAD_DIM, ML_HEAD_DIM), F32),
        jax.ShapeDtypeStruct((B, DEPTH, N_GATE, ML_HEAD_DIM), F32),
        jax.ShapeDtypeStruct((B, DEPTH, N_GATE, LANES), F32),
    ]
    in_specs = [
        pl.BlockSpec((None, T, D_MODEL), lambda b: (b, 0, 0)),
        pl.BlockSpec((None, MODS_ROWS, N_MOD * D_MODEL), lambda b: (layer, 0, 0)),
        pl.BlockSpec((None, 1, D_MODEL), lambda b: (layer, 0, 0)),
        pl.BlockSpec((None, D_MODEL, W_COLS), lambda b: (layer, 0, 0)),
        pl.BlockSpec((None, 3, CONV_DIM), lambda b: (layer, 0, 0)),
        pl.BlockSpec((None, 1, 2 * LANES), lambda b: (layer, 0, 0)),
        pl.BlockSpec((None, 1, ML_DIM), lambda b: (layer, 0, 0)),
        pl.BlockSpec((None, D_MODEL, D_MODEL), lambda b: (layer, 0, 0)),
    ]
    args = [x, mods, ln1, w_in, conv_w, gate_b, ml_g, w_out]
    aliases = {}
    if prev is not None:
        in_specs += [pl.BlockSpec(memory_space=pl.ANY)] * len(prev)
        aliases = {len(args) + i: 1 + i for i in range(len(prev))}
        args += list(prev)
    out_specs = [
        pl.BlockSpec((None, T, D_MODEL), lambda b: (b, 0, 0)),
        pl.BlockSpec((None, None, T, NA_DIM), lambda b: (b, layer, 0, 0)),
        pl.BlockSpec((None, None, T, NA_DIM), lambda b: (b, layer, 0, 0)),
        pl.BlockSpec((None, None, N_DIRS, ML_HEADS, ML_HEAD_DIM, ML_HEAD_DIM),
                     lambda b: (b, layer, 0, 0, 0, 0)),
        pl.BlockSpec((None, None, N_GATE, ML_HEAD_DIM), lambda b: (b, layer, 0, 0)),
        pl.BlockSpec((None, None, N_GATE, LANES), lambda b: (b, layer, 0, 0)),
    ]
    outs = pl.pallas_call(
        _ctx_mixer_kernel,
        grid=(B,),
        in_specs=in_specs,
        out_specs=out_specs,
        out_shape=[jax.ShapeDtypeStruct(x.shape, F32)] + state_shapes,
        input_output_aliases=aliases,
        compiler_params=pltpu.CompilerParams(vmem_limit_bytes=VMEM_LIMIT),
        name="ctx_mixer",
    )(*args)
    return outs[0], tuple(outs[1:])


def _route(lg):
    lane = lax.broadcasted_iota(jnp.int32, lg.shape, 1)
    lane_f = lane.astype(F32)
    big = float(LANES)
    is_g = (lane >= N_EXPERTS) & (lane < N_EXPERTS + N_GROUPS)
    glm = jnp.where(is_g, lg, NEG_INF)
    gmax = jnp.max(glm, axis=-1, keepdims=True)
    g_top = jnp.min(jnp.where(glm == gmax, lane_f, big), axis=-1, keepdims=True) - N_EXPERTS
    gp = 1.0 / jnp.sum(jnp.where(is_g, jnp.exp(lg - gmax), 0.0), axis=-1, keepdims=True)
    grp = (lane >> (EXPERTS_PER_GROUP.bit_length() - 1)).astype(F32)
    in_grp = (lane < N_EXPERTS) & (grp == g_top)
    e1 = jnp.where(in_grp, lg, NEG_INF)
    v1 = jnp.max(e1, axis=-1, keepdims=True)
    i1 = jnp.min(jnp.where(e1 == v1, lane_f, big), axis=-1, keepdims=True)
    e2 = jnp.where(in_grp & (lane_f != i1), lg, NEG_INF)
    v2 = jnp.max(e2, axis=-1, keepdims=True)
    i2 = jnp.min(jnp.where(e2 == v2, lane_f, big), axis=-1, keepdims=True)
    t = jnp.exp(v2 - v1)
    w1 = 1.0 / (1.0 + t)
    w2 = t / (1.0 + t)
    gates = gp * (jnp.where(lane_f == i1, w1, 0.0) + jnp.where(lane_f == i2, w2, 0.0))
    return gates, g_top


MOE_TB = 1024
MOE_TM = 128


def _group_slots(member):
    tb = member.shape[0]
    blk = CUMSUM_BLOCK
    r = lax.broadcasted_iota(jnp.int32, (blk, blk), 0)
    c = lax.broadcasted_iota(jnp.int32, (blk, blk), 1)
    tril = jnp.where(c <= r, 1.0, 0.0).astype(BF16)
    triu = jnp.where(r <= c, 1.0, 0.0).astype(BF16)
    member_t = jnp.transpose(member)
    cols, rows = [], []
    carry_c = jnp.zeros((1, LANES), F32)
    carry_r = jnp.zeros((LANES, 1), F32)
    for r0 in range(0, tb, blk):
        cc = _dot(tril, member[r0:r0 + blk].astype(BF16)) + carry_c
        rr = _dot(member_t[:, r0:r0 + blk].astype(BF16), triu) + carry_r
        carry_c = cc[blk - 1:blk, :]
        carry_r = rr[:, blk - 1:blk]
        cols.append(cc)
        rows.append(rr)
    rank_c = jnp.concatenate(cols, axis=0)
    rank_r = jnp.concatenate(rows, axis=1)
    slot_c = jnp.where(member > 0.0, rank_c - 1.0, -1.0)
    slot_r = jnp.where(member_t > 0.0, rank_r - 1.0, -1.0)
    return slot_c, slot_r, carry_c


def _moe_kernel(x_ref, mods_ref, ln_ref, wr_ref, br_ref, wg_ref, wu_ref, wd_ref, fg_ref, o_ref,
                h2_scr, gates_scr, slotc_scr, slotr_scr, xg_scr, gg_scr, yg_scr, acc_scr, cnt_smem,
                *, per_block_row, final):
    e = pl.program_id(1)
    grp = e // EXPERTS_PER_GROUP
    sub = e % EXPERTS_PER_GROUP
    tb, tm = MOE_TB, MOE_TM
    row = (1 + pl.program_id(0)) if per_block_row else 0
    sh2, sc2, g2 = _mod_rows(mods_ref, row, 3)

    @pl.when(e == 0)
    def _():
        h2 = _rms_mod(x_ref[...], ln_ref[...], sc2, sh2)
        h2_scr[...] = h2.astype(BF16)
        lg = jnp.dot(h2, wr_ref[...], precision=HIGHEST, preferred_element_type=F32) + br_ref[...]
        gates, g_top = _route(lg)
        gates_scr[...] = gates
        lane = lax.broadcasted_iota(jnp.int32, gates.shape, 1)
        member = jnp.where((lane.astype(F32) == g_top) & (lane < N_GROUPS), 1.0, 0.0)
        slot_c, slot_r, totals = _group_slots(member)
        slotc_scr[...] = slot_c
        slotr_scr[...] = slot_r
        for g in range(N_GROUPS):
            cnt_smem[g] = jnp.max(totals[:, g:g + 1]).astype(jnp.int32)
        acc_scr[...] = jnp.zeros_like(acc_scr)

    n_tiles = (cnt_smem[grp] + (tm - 1)) // tm

    @pl.when(sub == 0)
    def _():
        slot_row = slotr_scr[pl.ds(grp, 1), :]

        def gather_tile(i, carry):
            base = pl.multiple_of(i * tm, tm)
            want = (lax.broadcasted_iota(jnp.int32, (tm, tb), 0) + base).astype(F32)
            onehot = jnp.where(slot_row == want, 1.0, 0.0)
            xg_scr[pl.ds(base, tm), :] = _dot(onehot.astype(BF16), h2_scr[...]).astype(BF16)
            gg_scr[pl.ds(base, tm), :] = jnp.dot(onehot, gates_scr[...], precision=HIGHEST,
                                                 preferred_element_type=F32)
            yg_scr[pl.ds(base, tm), :] = jnp.zeros((tm, D_MODEL), F32)
            return carry

        lax.fori_loop(0, n_tiles, gather_tile, 0)

    wg = wg_ref[...].astype(BF16)
    wu = wu_ref[...].astype(BF16)
    wd = wd_ref[...].astype(BF16)

    def expert_tile(i, carry):
        base = pl.multiple_of(i * tm, tm)
        xt = xg_scr[pl.ds(base, tm), :]
        gt = gg_scr[pl.ds(base, tm), :]
        lane = lax.broadcasted_iota(jnp.int32, gt.shape, 1)
        ge = jnp.sum(jnp.where(lane == e, gt, 0.0), axis=-1, keepdims=True)
        a = jax.nn.silu(_dot(xt, wg)) * _dot(xt, wu)
        yg_scr[pl.ds(base, tm), :] += _dot((a * ge).astype(BF16), wd)
        return carry

    lax.fori_loop(0, n_tiles, expert_tile, 0)

    @pl.when(sub == EXPERTS_PER_GROUP - 1)
    def _():
        lane = lax.broadcasted_iota(jnp.int32, (tb, LANES), 1)
        slot_col = jnp.sum(jnp.where(lane == grp, slotc_scr[...], 0.0), axis=-1, keepdims=True)

        def scatter_tile(i, carry):
            base = pl.multiple_of(i * tm, tm)
            want = (lax.broadcasted_iota(jnp.int32, (tb, tm), 1) + base).astype(F32)
            onehot_t = jnp.where(slot_col == want, 1.0, 0.0).astype(BF16)
            y = yg_scr[pl.ds(base, tm), :]
            y_hi = y.astype(BF16)
            y_lo = (y - y_hi.astype(F32)).astype(BF16)
            acc_scr[...] += _dot(jnp.concatenate([onehot_t, onehot_t], axis=1),
                                 jnp.concatenate([y_hi, y_lo], axis=0))
            return carry

        lax.fori_loop(0, n_tiles, scatter_tile, 0)

    @pl.when(e == N_EXPERTS - 1)
    def _():
        y = x_ref[...] + g2 * acc_scr[...]
        if final:
            y = y * lax.rsqrt(jnp.mean(y * y, axis=-1, keepdims=True) + EPS) * fg_ref[...]
        o_ref[...] = y


def _moe_call(layer, x2d, mods, ln2, w_router, b_router, w_gate, w_up, w_down, final_g, *,
              per_block_row, final):
    n = x2d.shape[0]
    tb = MOE_TB
    kern = functools.partial(_moe_kernel, per_block_row=per_block_row, final=final)
    return pl.pallas_call(
        kern,
        grid=(n // tb, N_EXPERTS),
        in_specs=[
            pl.BlockSpec((tb, D_MODEL), lambda i, e: (i, 0)),
            pl.BlockSpec((None, MODS_ROWS, N_MOD * D_MODEL), lambda i, e: (layer, 0, 0)),
            pl.BlockSpec((None, 1, D_MODEL), lambda i, e: (layer, 0, 0)),
            pl.BlockSpec((None, D_MODEL, LANES), lambda i, e: (layer, 0, 0)),
            pl.BlockSpec((None, 1, LANES), lambda i, e: (layer, 0, 0)),
            pl.BlockSpec((None, None, D_MODEL, EXPERT_FF), lambda i, e: (layer, e, 0, 0)),
            pl.BlockSpec((None, None, D_MODEL, EXPERT_FF), lambda i, e: (layer, e, 0, 0)),
            pl.BlockSpec((None, None, EXPERT_FF, D_MODEL), lambda i, e: (layer, e, 0, 0)),
            pl.BlockSpec((1, D_MODEL), lambda i, e: (0, 0)),
        ],
        out_specs=pl.BlockSpec((tb, D_MODEL), lambda i, e: (i, 0)),
        out_shape=jax.ShapeDtypeStruct(x2d.shape, F32),
        scratch_shapes=[pltpu.VMEM((tb, D_MODEL), BF16),
                        pltpu.VMEM((tb, LANES), F32),
                        pltpu.VMEM((tb, LANES), F32),
                        pltpu.VMEM((LANES, tb), F32),
                        pltpu.VMEM((tb,D_MODEL), BF16),
                        pltpu.VMEM((tb,LANES), F32),
                        pltpu.VMEM((tb,D_MODEL), F32),
                        pltpu.VMEM((tb, D_MODEL), F32),
                        pltpu.SMEM((N_GROUPS,), jnp.int32)],
        compiler_params=pltpu.CompilerParams(vmem_limit_bytes=VMEM_LIMIT),
        name="moe",
    )(x2d, mods, ln2, w_router, b_router, w_gate, w_up, w_down, final_g)


LAT_PROJ_ROWS = 512


def _lat_proj_kernel(x_ref, mods_ref, ln_ref, win_ref, gb_ref, zc_ref, zn_ref, zm_ref, zg_ref):
    row = 1 + pl.program_id(0) // (DEC_SEQ // LAT_PROJ_ROWS)
    sh1, sc1, _ = _mod_rows(mods_ref, row, 0)
    h = _rms_mod(x_ref[...], ln_ref[...], sc1, sh1).astype(BF16)
    zc_ref[...] = _dot(h, win_ref[:, C_CONV:C_NA])
    zn_ref[...] = _dot(h, win_ref[:, C_NA:C_ML])
    zm_ref[...] = _dot(h, win_ref[:, C_ML:C_IG])
    zg_ref[...] = _dot(h, win_ref[:, C_IG:W_COLS]) + gb_ref[...]


def _lat_proj_call(layer, x2d, mods, ln1, w_in, gate_b):
    n = x2d.shape[0]
    tb = LAT_PROJ_ROWS
    widths = (C_NA - C_CONV, C_ML - C_NA, C_IG - C_ML, W_COLS - C_IG)
    return pl.pallas_call(
        _lat_proj_kernel,
        grid=(n // tb,),
        in_specs=[
            pl.BlockSpec((tb, D_MODEL), lambda i: (i, 0)),
            pl.BlockSpec((None, MODS_ROWS, N_MOD * D_MODEL), lambda i: (layer, 0, 0)),
            pl.BlockSpec((None, 1, D_MODEL), lambda i: (layer, 0, 0)),
            pl.BlockSpec((None, D_MODEL, W_COLS), lambda i: (layer, 0, 0)),
            pl.BlockSpec((None, 1, 2 * LANES), lambda i: (layer, 0, 0)),
        ],
        out_specs=[pl.BlockSpec((tb, w), lambda i: (i, 0)) for w in widths],
        out_shape=[jax.ShapeDtypeStruct((n, w), F32) for w in widths],
        compiler_params=pltpu.CompilerParams(vmem_limit_bytes=VMEM_LIMIT),
        name="lat_proj",
    )(x2d, mods, ln1, w_in, gate_b)


def _na_row_start(r):
    rows = DEC_SEQ // GRID_W
    return min(max(r - NA_WIN_ROWS // 2, 0), rows - NA_WIN_ROWS)


def _lat_local_kernel(zc_ref, zn_ref, kc_ref, vc_ref, band_ref, convw_ref, o_ref):
    o_ref[:, 0:CONV_DIM] = _short_conv(zc_ref[...], convw_ref[...])
    rows = DEC_SEQ // GRID_W
    blk = NA_WIN_ROWS * GRID_W
    q_all = (zn_ref[:, 0:NA_DIM] * NA_HEAD_DIM ** -0.5).astype(BF16)
    k_all = zn_ref[:, NA_DIM:2 * NA_DIM].astype(BF16)
    v_all = zn_ref[:, 2 * NA_DIM:3 * NA_DIM].astype(BF16)
    kc_all = kc_ref[...].astype(BF16)
    vc_all = vc_ref[...].astype(BF16)
    heads = []
    for h in range(NA_HEADS):
        sl = slice(h * NA_HEAD_DIM, (h + 1) * NA_HEAD_DIM)
        q, k, v, kc, vc = q_all[:, sl], k_all[:, sl], v_all[:, sl], kc_all[:, sl], vc_all[:, sl]
        band = band_ref[h]
        outs = []
        for r in range(rows):
            start = _na_row_start(r)
            off = (start - r + NA_WIN_ROWS - 1) * GRID_W
            q_r = q[r * GRID_W:(r + 1) * GRID_W]
            s_loc = _dot_nt(q_r, k[start * GRID_W:start * GRID_W + blk]) + band[:, off:off + blk]
            s_ctx = _dot_nt(q_r, kc)
            m = jnp.maximum(jnp.max(s_loc, axis=-1, keepdims=True), jnp.max(s_ctx, axis=-1, keepdims=True))
            p_loc = jnp.exp(s_loc - m)
            p_ctx = jnp.exp(s_ctx - m)
            den = jnp.sum(p_loc, axis=-1, keepdims=True) + jnp.sum(p_ctx, axis=-1, keepdims=True)
            o = _dot(p_loc.astype(BF16), v[start * GRID_W:start * GRID_W + blk]) + _dot(p_ctx.astype(BF16), vc)
            outs.append(o / den)
        heads.append(jnp.concatenate(outs, axis=0))
    o_ref[:, CONV_DIM:CONV_DIM + NA_DIM] = jnp.concatenate(heads, axis=-1)


def _lat_local_call(layer, zc, zn, cache_k, cache_v, band, conv_w):
    nb = zc.shape[0] // DEC_SEQ
    T = DEC_SEQ
    return pl.pallas_call(
        _lat_local_kernel,
        grid=(nb,),
        in_specs=[
            pl.BlockSpec((T, 3 * CONV_DIM), lambda b: (b, 0)),
            pl.BlockSpec((T, 3 * NA_DIM), lambda b: (b, 0)),
            pl.BlockSpec((None, None, PAST_LEN, NA_DIM), lambda b: (b, layer, 0, 0)),
            pl.BlockSpec((None, None, PAST_LEN, NA_DIM), lambda b: (b, layer, 0, 0)),
            pl.BlockSpec((None, NA_HEADS, GRID_W, band.shape[-1]), lambda b: (layer, 0, 0, 0)),
            pl.BlockSpec((None, 3, CONV_DIM), lambda b: (layer, 0, 0)),
        ],
        out_specs=pl.BlockSpec((T, CONV_DIM + NA_DIM), lambda b: (b, 0)),
        out_shape=jax.ShapeDtypeStruct((zc.shape[0], CONV_DIM + NA_DIM), F32),
        compiler_params=pltpu.CompilerParams(vmem_limit_bytes=VMEM_LIMIT),
        name="lat_conv_na",
    )(zc, zn, cache_k, cache_v, band, conv_w)


def _rope(x, cos, sin_signed):
    w = x.shape[-1]
    half = NA_HEAD_DIM // 2
    lane = lax.broadcasted_iota(jnp.int32, x.shape, 1)
    partner = jnp.where(lane % (2 * half) < half, pltpu.roll(x, w - half, axis=1), pltpu.roll(x, half, axis=1))
    return x * cos + partner * sin_signed


def _lat_mlstm_kernel(q_ref, k_ref, v_ref, o_ref, zg_ref, c0_ref, n0_ref, m0_ref, cos_ref, sin_ref,
                      mlg_ref, out_ref, bcol_scr, rowt_scr):
    hh = pl.program_id(1)

    @pl.when(hh == 0)
    def _():
        b_all, r_all = _gate_terms(zg_ref[:, 0:LANES], zg_ref[:, LANES:2 * LANES])
        bcol_scr[...] = b_all
        rowt_scr[...] = r_all

    b_col = bcol_scr[...]
    cos, sin_signed = cos_ref[...], sin_ref[...]
    q = _rope(q_ref[...], cos, sin_signed)
    k = _rope(k_ref[...], cos, sin_signed) * ML_HEAD_DIM ** -0.5
    v = v_ref[...]
    lane = lax.broadcasted_iota(jnp.int32, (DEC_SEQ, LANES), 1)
    hsum = None
    for d in range(N_DIRS):
        j = d * ML_HEADS + hh
        bc = jnp.sum(jnp.where(lane == j, b_col, 0.0), axis=-1, keepdims=True)
        lr = rowt_scr[pl.ds(j, 1), :]
        m0 = m0_ref[d, :, 0:1]
        ho = _mlstm_outputs(q, k, v, bc, lr, m0, c0_ref[d], n0_ref[d], d == 1, 256)
        hsum = ho if hsum is None else hsum + ho
    out_ref[...] = _head_readout(hsum, o_ref[...], mlg_ref[...])


def _lat_mlstm_call(layer, zm, zg, state_c, state_n, state_m, cos, sin_signed, ml_g):
    nb = zm.shape[0] // DEC_SEQ
    T = DEC_SEQ
    hd = ML_HEAD_DIM

    def col(part):
        return pl.BlockSpec((T, hd), lambda b, h: (b, part * ML_HEADS + h))

    return pl.pallas_call(
        _lat_mlstm_kernel,
        grid=(nb, ML_HEADS),
        in_specs=[
            col(0), col(1), col(2), col(3),
            pl.BlockSpec((T, 2 * LANES), lambda b, h: (b, 0)),
            pl.BlockSpec((None, None, N_DIRS, None, hd, hd), lambda b, h: (b, layer, 0, h, 0, 0)),
            pl.BlockSpec((None, None, N_DIRS, None, 1, hd), lambda b, h: (b, layer, 0, h, 0, 0)),
            pl.BlockSpec((None, None, N_DIRS, None, 1, LANES), lambda b, h: (b, layer, 0, h, 0, 0)),
            pl.BlockSpec((T, hd), lambda b, h: (0, 0)),
            pl.BlockSpec((T, hd), lambda b, h: (0, 0)),
            pl.BlockSpec((None, 1, hd), lambda b, h: (layer, 0, h)),
        ],
        out_specs=pl.BlockSpec((T, hd), lambda b, h: (b, h)),
        out_shape=jax.ShapeDtypeStruct((zm.shape[0], ML_DIM), F32),
        scratch_shapes=[pltpu.VMEM((T, LANES), F32), pltpu.VMEM((LANES, T), F32)],
        compiler_params=pltpu.CompilerParams(vmem_limit_bytes=VMEM_LIMIT),
        name="lat_mlstm",
    )(zm, zm, zm, zm, zg, state_c, state_n, state_m, cos, sin_signed, ml_g)


def _lat_merge_kernel(x_ref, loc_ref, ml_ref, mods_ref, wout_ref, o_ref):
    row = 1 + pl.program_id(0) // (DEC_SEQ // LAT_PROJ_ROWS)
    g1 = _mod_rows(mods_ref, row, 0)[2]
    split = CONV_DIM + NA_DIM
    y = _dot(loc_ref[...].astype(BF16), wout_ref[0:split, :]) + _dot(ml_ref[...].astype(BF16), wout_ref[split:, :])
    o_ref[...] = x_ref[...] + g1 * y


def _lat_merge_call(layer, x2d, loc, ml, mods, w_out):
    n = x2d.shape[0]
    tb = LAT_PROJ_ROWS
    return pl.pallas_call(
        _lat_merge_kernel,
        grid=(n // tb,),
        in_specs=[
            pl.BlockSpec((tb, D_MODEL), lambda i: (i, 0)),
            pl.BlockSpec((tb, CONV_DIM + NA_DIM), lambda i: (i, 0)),
            pl.BlockSpec((tb, ML_DIM), lambda i: (i, 0)),
            pl.BlockSpec((None, MODS_ROWS, N_MOD * D_MODEL), lambda i: (layer, 0, 0)),
            pl.BlockSpec((None, D_MODEL, D_MODEL), lambda i: (layer, 0, 0)),
        ],
        out_specs=pl.BlockSpec((tb, D_MODEL), lambda i: (i, 0)),
        out_shape=jax.ShapeDtypeStruct(x2d.shape, F32),
        compiler_params=pltpu.CompilerParams(vmem_limit_bytes=VMEM_LIMIT),
        name="lat_merge",
    )(x2d, loc, ml, mods, w_out)


def _pad_lanes(a, width):
    return jnp.pad(a, [(0, 0)] * (a.ndim - 1) + [(0, width - a.shape[-1])])


def _pack_w_in(w_in):
    main = w_in[..., :C_IG]
    gates = w_in[..., C_IG:]
    packed = jnp.concatenate([main, _pad_lanes(gates[..., :N_GATE], LANES),
                              _pad_lanes(gates[..., N_GATE:], LANES)], axis=-1)
    return packed.astype(BF16)


def _pack_gate_bias(ml_gate_b):
    gb = ml_gate_b.reshape(DEPTH, 2, N_GATE).astype(F32)
    return jnp.concatenate([_pad_lanes(gb[:, 0], LANES), _pad_lanes(gb[:, 1], LANES)], axis=-1)[:, None, :]


def _rpb_band(na_rpb):
    cols = np.arange(GRID_W)
    col_idx = np.clip(cols[None, :] - cols[:, None] + NA_WIN_COLS - 1, 0, 2 * NA_WIN_COLS - 2)
    col_start = np.clip(cols - NA_WIN_COLS // 2, 0, GRID_W - NA_WIN_COLS)
    col_mask = (cols[None, :] >= col_start[:, None]) & (cols[None, :] < col_start[:, None] + NA_WIN_COLS)
    t = na_rpb.astype(F32)[:, :, :, col_idx]
    t = jnp.where(col_mask[None, None, None], t, NEG_INF)
    t = jnp.transpose(t, (0, 1, 3, 2, 4))
    t = t.reshape(DEPTH, NA_HEADS, GRID_W, (2 * NA_WIN_ROWS - 1) * GRID_W)
    return _pad_lanes(t, 2 * NA_WIN_ROWS * GRID_W)


def _rope_tables():
    t = np.arange(DEC_SEQ)
    pos = np.stack([t // GRID_W, t % GRID_W], axis=-1).astype(np.float32)
    nf = ML_HEAD_DIM // 4
    inv = jnp.asarray(ROPE_BASE, F32) ** (-jnp.arange(nf, dtype=F32) / nf)
    ang = jnp.asarray(pos)[:, :, None] * inv
    cos = jnp.cos(ang)
    sin = jnp.sin(ang)
    cos_t = jnp.concatenate([cos, cos], axis=-1).reshape(DEC_SEQ, ML_HEAD_DIM)
    sin_t = jnp.concatenate([-sin, sin], axis=-1).reshape(DEC_SEQ, ML_HEAD_DIM)
    return cos_t, sin_t


def kernel(x_prompt, x_sample, cache_k, cache_v, state_C, state_n, state_m, c, c_ctx, w_mod, b_mod,
           ln1_g, w_in, conv_w, na_rpb, ml_gate_b, ml_norm_g, w_out, ln2_g, w_rg, b_rg, w_re, b_re,
           w_gate, w_up, w_down, final_g):
    nb_ctx = x_prompt.shape[0]
    nb_lat = x_sample.shape[0]
    assert 1 + nb_lat <= MODS_ROWS

    cvecs = jnp.concatenate([c_ctx[None, :], c,
                             jnp.zeros((MODS_ROWS - 1 - nb_lat, D_MODEL), F32)], axis=0)
    mods = _mods_call(cvecs, w_mod, b_mod)

    w_in_p = _pack_w_in(w_in)
    w_out_b = w_out.astype(BF16)
    gate_b = _pack_gate_bias(ml_gate_b)
    ln1 = ln1_g.reshape(DEPTH, 1, D_MODEL)
    ln2 = ln2_g.reshape(DEPTH, 1, D_MODEL)
    ml_g = ml_norm_g.reshape(DEPTH, 1, ML_DIM)
    w_router = _pad_lanes(jnp.concatenate([w_re, w_rg], axis=-1), LANES)
    b_router = _pad_lanes(jnp.concatenate([b_re, b_rg], axis=-1), LANES)[:, None, :]
    fg = final_g.reshape(1, D_MODEL)
    band = _rpb_band(na_rpb)
    cos_t, sin_t = _rope_tables()
    ck = cache_k.reshape(nb_lat, DEPTH, PAST_LEN, NA_DIM)
    cv = cache_v.reshape(nb_lat, DEPTH, PAST_LEN, NA_DIM)
    st_n = state_n.reshape(nb_lat, DEPTH, N_DIRS, ML_HEADS, 1, ML_HEAD_DIM)
    st_m = jnp.broadcast_to(state_m[..., None, None], (nb_lat, DEPTH, N_DIRS, ML_HEADS, 1, LANES))

    xp = x_prompt
    xs = x_sample.reshape(nb_lat * DEC_SEQ, D_MODEL)
    states = None
    for l in range(DEPTH):
        last = l == DEPTH - 1
        xp, states = _ctx_mixer_call(l, xp, mods, ln1, w_in_p, conv_w, gate_b, ml_g, w_out_b, states)
        xp = _moe_call(l, xp.reshape(nb_ctx * SEQ, D_MODEL), mods, ln2, w_router, b_router,
                       w_gate, w_up, w_down, fg, per_block_row=False, final=last)
        xp = xp.reshape(nb_ctx, SEQ, D_MODEL)

        zc, zn, zm, zg = _lat_proj_call(l, xs, mods, ln1, w_in_p, gate_b)
        loc = _lat_local_call(l, zc, zn, ck, cv, band, conv_w)
        ml = _lat_mlstm_call(l, zm, zg, state_C, st_n, st_m, cos_t, sin_t, ml_g)
        xs = _lat_merge_call(l, xs, loc, ml, mods, w_out_b)
        xs = _moe_call(l, xs, mods, ln2, w_router, b_router, w_gate, w_up, w_down, fg,
                       per_block_row=True, final=last)

    new_k, new_v, new_c, new_n, new_m = states
    return (xp, xs.reshape(nb_lat, DEC_SEQ, D_MODEL),
            new_k.reshape(nb_ctx, DEPTH, SEQ, NA_HEADS, NA_HEAD_DIM),
            new_v.reshape(nb_ctx, DEPTH, SEQ, NA_HEADS, NA_HEAD_DIM),
            new_c,
            new_n.reshape(nb_ctx, DEPTH, N_DIRS, ML_HEADS, ML_HEAD_DIM),
            new_m[..., 0].reshape(nb_ctx, DEPTH, N_DIRS, ML_HEADS))
```

```python
import functools

import numpy as np
import jax
import jax.numpy as jnp
from jax import lax
from jax.experimental import pallas as pl
from jax.experimental.pallas import tpu as pltpu

F32 = jnp.float32
BF16 = jnp.bfloat16

D_MODEL = 1024
SEQ = 256
DEPTH = 4
DEC_SEQ = 1024
PAST_LEN = 512
GRID_W = 64
CONV_DIM = 256
NA_HEADS = 4
NA_HEAD_DIM = 64
NA_DIM = NA_HEADS * NA_HEAD_DIM
NA_WIN_ROWS = 8
NA_WIN_COLS = 16
ML_HEADS = 4
ML_HEAD_DIM = 128
ML_DIM = ML_HEADS * ML_HEAD_DIM
N_DIRS = 2
N_GROUPS = 4
EXPERTS_PER_GROUP = 4
N_EXPERTS = N_GROUPS * EXPERTS_PER_GROUP
EXPERT_FF = 512
ROPE_BASE = 10000.0
EPS = 1e-6
N_MOD = 6

LANES = 128
N_GATE = N_DIRS * ML_HEADS
C_CONV = 0
C_NA = 3 * CONV_DIM
C_ML = C_NA + 3 * NA_DIM
C_IG = C_ML + 4 * ML_DIM
C_FG = C_IG + LANES
W_COLS = C_FG + LANES
CUMSUM_BLOCK = 256
MODS_ROWS = 8
VMEM_LIMIT = 56 * 1024 * 1024

NEG_INF = float("-inf")
HIGHEST = lax.Precision.HIGHEST


def _dot(a, b):
    return jnp.dot(a, b, preferred_element_type=F32)


def _dot_nt(a, b):
    return lax.dot_general(a, b, (((1,), (1,)), ((), ())), preferred_element_type=F32)


def _rms_mod(x, g, sc, sh):
    y = x * lax.rsqrt(jnp.mean(x * x, axis=-1, keepdims=True) + EPS)
    return (y * g) * (1.0 + sc) + sh


def _mod_rows(mods_ref, row, first):
    return [mods_ref[pl.ds(row, 1), pl.ds((first + j) * D_MODEL, D_MODEL)] for j in range(3)]


def _mods_kernel(cv_ref, w_ref, b_ref, o_ref):
    cv = cv_ref[...]
    s = cv * jax.nn.sigmoid(cv)
    o_ref[...] = _dot(s.astype(BF16), w_ref[...].astype(BF16)) + b_ref[...]


def _mods_call(cvecs, w_mod, b_mod):
    tn = 1536
    n = N_MOD * D_MODEL
    return pl.pallas_call(
        _mods_kernel,
        grid=(DEPTH, n // tn),
        in_specs=[pl.BlockSpec((MODS_ROWS, D_MODEL), lambda l, j: (0, 0)),
                  pl.BlockSpec((None, D_MODEL, tn), lambda l, j: (l, 0, j)),
                  pl.BlockSpec((None, 1, tn), lambda l, j: (l, 0, j))],
        out_specs=pl.BlockSpec((None, MODS_ROWS, tn), lambda l, j: (l, 0, j)),
        out_shape=jax.ShapeDtypeStruct((DEPTH, MODS_ROWS, n), F32),
        compiler_params=pltpu.CompilerParams(vmem_limit_bytes=VMEM_LIMIT),
        name="adaln_mods",
    )(cvecs, w_mod, b_mod.reshape(DEPTH, 1, n))


def _short_conv(zc, cw):
    T = zc.shape[0]
    cb = zc[:, 0:CONV_DIM]
    u = zc[:, CONV_DIM:2 * CONV_DIM] * zc[:, 2 * CONV_DIM:3 * CONV_DIM]
    t = lax.broadcasted_iota(jnp.int32, u.shape, 0)
    u_prev = jnp.where(t == 0, 0.0, pltpu.roll(u, 1, axis=0))
    u_next = jnp.where(t == T - 1, 0.0, pltpu.roll(u, T - 1, axis=0))
    return cb * (cw[0:1, :] * u_prev + cw[1:2, :] * u + cw[2:3, :] * u_next)


def _softmax_attention(nq, nk, nv):
    outs = []
    for h in range(NA_HEADS):
        sl = slice(h * NA_HEAD_DIM, (h + 1) * NA_HEAD_DIM)
        q = (nq[:, sl] * NA_HEAD_DIM ** -0.5).astype(BF16)
        s = _dot_nt(q, nk[:, sl].astype(BF16))
        p = jnp.exp(s - jnp.max(s, axis=-1, keepdims=True))
        o = _dot(p.astype(BF16), nv[:, sl].astype(BF16)) / jnp.sum(p, axis=-1, keepdims=True)
        outs.append(o)
    return jnp.concatenate(outs, axis=-1)


def _log_sigmoid(x):
    return jnp.minimum(x, 0.0) - jnp.log(1.0 + jnp.exp(-jnp.abs(x)))


def _gate_terms(zi, zf):
    T = zi.shape[0]
    lf = _log_sigmoid(zf)
    blk = min(T, CUMSUM_BLOCK)
    r = lax.broadcasted_iota(jnp.int32, (blk, blk), 0)
    c = lax.broadcasted_iota(jnp.int32, (blk, blk), 1)
    tril = jnp.where(c <= r, 1.0, 0.0).astype(F32)
    parts, carry = [], None
    for r0 in range(0, T, blk):
        b = jnp.dot(tril, lf[r0:r0 + blk], precision=HIGHEST, preferred_element_type=F32)
        if carry is not None:
            b = b + carry
        carry = b[blk - 1:blk, :]
        parts.append(b)
    b_fwd = parts[0] if len(parts) == 1 else jnp.concatenate(parts, axis=0)
    b_bwd = (b_fwd[T - 1:T, :] - b_fwd) + lf
    lane = lax.broadcasted_iota(jnp.int32, zi.shape, 1)
    b_col = jnp.where(lane < ML_HEADS, b_fwd, b_bwd)
    row_t = jnp.transpose(zi - b_col)
    return b_col, row_t


def _mlstm_outputs(q, k, v, b_col, li_row, m0, c0, n0, backward, q_block):
    T = q.shape[0]
    qb, kb, vb = q.astype(BF16), k.astype(BF16), v.astype(BF16)
    c0b = None if c0 is None else c0.astype(BF16)
    outs = []
    for r0 in range(0, T, q_block):
        ks, ke = (r0, T) if backward else (0, r0 + q_block)
        bq = b_col[r0:r0 + q_block]
        d = bq + li_row[:, ks:ke]
        t_idx = r0 + lax.broadcasted_iota(jnp.int32, d.shape, 0)
        s_idx = ks + lax.broadcasted_iota(jnp.int32, d.shape, 1)
        d = jnp.where((s_idx >= t_idx) if backward else (s_idx <= t_idx), d, NEG_INF)
        inter = bq + m0
        m_t = jnp.maximum(jnp.max(d, axis=-1, keepdims=True), inter)
        w = jnp.exp(d - m_t)
        s = _dot_nt(qb[r0:r0 + q_block], kb[ks:ke]) * w
        num = _dot(s.astype(BF16), vb[ks:ke])
        den = jnp.sum(s, axis=-1, keepdims=True)
        if c0 is not None:
            a = jnp.exp(inter - m_t)
            num = num + a * _dot(qb[r0:r0 + q_block], c0b)
            den = den + a * jnp.sum(q[r0:r0 + q_block] * n0, axis=-1, keepdims=True)
        outs.append(num / jnp.maximum(jnp.abs(den), jnp.exp(-m_t)))
    return outs[0] if len(outs) == 1 else jnp.concatenate(outs, axis=0)


def _mlstm_state(k, v, b_col, li_col, backward):
    T = k.shape[0]
    b_tot = b_col[0:1] if backward else b_col[T - 1:T]
    g = (b_tot - b_col) + li_col
    m_new = jnp.maximum(b_tot, jnp.max(g, axis=0, keepdims=True))
    kw = k * jnp.exp(g - m_new)
    c_new = _dot(jnp.transpose(kw).astype(BF16), v.astype(BF16))
    n_new = jnp.sum(kw, axis=0, keepdims=True)
    return c_new, n_new, m_new


def _head_readout(h, mo, g):
    hn = h * lax.rsqrt(jnp.mean(h * h, axis=-1, keepdims=True) + EPS)
    return hn * g * jax.nn.sigmoid(mo)


def _ctx_mixer_kernel(x_ref, mods_ref, ln_ref, win_ref, convw_ref, gb_ref, mlg_ref, wout_ref, *rest):
    xo_ref, ko_ref, vo_ref, co_ref, no_ref, mo_ref = rest[-6:]
    T = SEQ
    sh1, sc1, g1 = _mod_rows(mods_ref, 0, 0)
    x = x_ref[...]
    h = _rms_mod(x, ln_ref[...], sc1, sh1).astype(BF16)

    conv_o = _short_conv(_dot(h, win_ref[:, C_CONV:C_NA]), convw_ref[...])

    zn = _dot(h, win_ref[:, C_NA:C_ML])
    nk, nv = zn[:, NA_DIM:2 * NA_DIM], zn[:, 2 * NA_DIM:3 * NA_DIM]
    ko_ref[...] = nk
    vo_ref[...] = nv
    na_o = _softmax_attention(zn[:, 0:NA_DIM], nk, nv)

    zg = _dot(h, win_ref[:, C_IG:W_COLS])
    zi = zg[:, 0:LANES] + gb_ref[:, 0:LANES]
    zf = zg[:, LANES:2 * LANES] + gb_ref[:, LANES:2 * LANES]
    b_col, row_t = _gate_terms(zi, zf)
    m0 = jnp.zeros((1, 1), F32)

    zm = _dot(h, win_ref[:, C_ML:C_IG])
    ml_parts = []
    for hh in range(ML_HEADS):
        zq, zk, zv, zo = (zm[:, p * ML_DIM + hh * ML_HEAD_DIM:p * ML_DIM + (hh + 1) * ML_HEAD_DIM]
                          for p in range(4))
        zk = zk * ML_HEAD_DIM ** -0.5
        hsum = None
        for d in range(N_DIRS):
            j = d * ML_HEADS + hh
            bc = b_col[:, j:j + 1]
            ho = _mlstm_outputs(zq, zk, zv, bc, row_t[j:j + 1, :], m0, None, None, d == 1, T)
            hsum = ho if hsum is None else hsum + ho
            c_new, n_new, m_new = _mlstm_state(zk, zv, bc, zi[:, j:j + 1], d == 1)
            co_ref[d, hh] = c_new
            no_ref[j:j + 1, :] = n_new
            mo_ref[j:j + 1, :] = jnp.broadcast_to(m_new, (1, LANES))
        ml_parts.append(_head_readout(hsum, zo, mlg_ref[:, hh * ML_HEAD_DIM:(hh + 1) * ML_HEAD_DIM]))

    mix = jnp.concatenate([conv_o, na_o] + ml_parts, axis=-1).astype(BF16)
    xo_ref[...] = x + g1 * _dot(mix, wout_ref[...])


def _ctx_mixer_call(layer, x, mods, ln1, w_in, conv_w, gate_b, ml_g, w_out, prev):
    B = x.shape[0]
    T = SEQ
    state_shapes = [
        jax.ShapeDtypeStruct((B, DEPTH, T, NA_DIM), F32),
        jax.ShapeDtypeStruct((B, DEPTH, T, NA_DIM), F32),
        jax.ShapeDtypeStruct((B, DEPTH, N_DIRS, ML_HEADS, ML_HEAD_DIM, ML_HEAD_DIM), F32),
        jax.ShapeDtypeStruct((B, DEPTH, N_GATE, ML_HEAD_DIM), F32),
        jax.ShapeDtypeStruct((B, DEPTH, N_GATE, LANES), F32),
    ]
    in_specs = [
        pl.BlockSpec((None, T, D_MODEL), lambda b: (b, 0, 0)),
        pl.BlockSpec((None, MODS_ROWS, N_MOD * D_MODEL), lambda b: (layer, 0, 0)),
        pl.BlockSpec((None, 1, D_MODEL), lambda b: (layer, 0, 0)),
        pl.BlockSpec((None, D_MODEL, W_COLS), lambda b: (layer, 0, 0)),
        pl.BlockSpec((None, 3, CONV_DIM), lambda b: (layer, 0, 0)),
        pl.BlockSpec((None, 1, 2 * LANES), lambda b: (layer, 0, 0)),
        pl.BlockSpec((None, 1, ML_DIM), lambda b: (layer, 0, 0)),
        pl.BlockSpec((None, D_MODEL, D_MODEL), lambda b: (layer, 0, 0)),
    ]
    args = [x, mods, ln1, w_in, conv_w, gate_b, ml_g, w_out]
    aliases = {}
    if prev is not None:
        in_specs += [pl.BlockSpec(memory_space=pl.ANY)] * len(prev)
        aliases = {len(args) + i: 1 + i for i in range(len(prev))}
        args += list(prev)
    out_specs = [
        pl.BlockSpec((None, T, D_MODEL), lambda b: (b, 0, 0)),
        pl.BlockSpec((None, None, T, NA_DIM), lambda b: (b, layer, 0, 0)),
        pl.BlockSpec((None, None, T, NA_DIM), lambda b: (b, layer, 0, 0)),
        pl.BlockSpec((None, None, N_DIRS, ML_HEADS, ML_HEAD_DIM, ML_HEAD_DIM),
                     lambda b: (b, layer, 0, 0, 0, 0)),
        pl.BlockSpec((None, None, N_GATE, ML_HEAD_DIM), lambda b: (b, layer, 0, 0)),
        pl.BlockSpec((None, None, N_GATE, LANES), lambda b: (b, layer, 0, 0)),
    ]
    outs = pl.pallas_call(
        _ctx_mixer_kernel,
        grid=(B,),
        in_specs=in_specs,
        out_specs=out_specs,
        out_shape=[jax.ShapeDtypeStruct(x.shape, F32)] + state_shapes,
        input_output_aliases=aliases,
        compiler_params=pltpu.CompilerParams(vmem_limit_bytes=VMEM_LIMIT),
        name="ctx_mixer",
    )(*args)
    return outs[0], tuple(outs[1:])


def _route(lg):
    lane = lax.broadcasted_iota(jnp.int32, lg.shape, 1)
    lane_f = lane.astype(F32)
    big = float(LANES)
    is_g = (lane >= N_EXPERTS) & (lane < N_EXPERTS + N_GROUPS)
    glm = jnp.where(is_g, lg, NEG_INF)
    gmax = jnp.max(glm, axis=-1, keepdims=True)
    g_top = jnp.min(jnp.where(glm == gmax, lane_f, big), axis=-1, keepdims=True) - N_EXPERTS
    gp = 1.0 / jnp.sum(jnp.where(is_g, jnp.exp(lg - gmax), 0.0), axis=-1, keepdims=True)
    grp = (lane >> (EXPERTS_PER_GROUP.bit_length() - 1)).astype(F32)
    in_grp = (lane < N_EXPERTS) & (grp == g_top)
    e1 = jnp.where(in_grp, lg, NEG_INF)
    v1 = jnp.max(e1, axis=-1, keepdims=True)
    i1 = jnp.min(jnp.where(e1 == v1, lane_f, big), axis=-1, keepdims=True)
    e2 = jnp.where(in_grp & (lane_f != i1), lg, NEG_INF)
    v2 = jnp.max(e2, axis=-1, keepdims=True)
    i2 = jnp.min(jnp.where(e2 == v2, lane_f, big), axis=-1, keepdims=True)
    t = jnp.exp(v2 - v1)
    w1 = 1.0 / (1.0 + t)
    w2 = t / (1.0 + t)
    return i1, i2, gp * w1, gp * w2


SLAB = D_MODEL // LANES
DSP_SB = 2048
DSP_TM = 128
DSP_CH = 512
DSP_PMAX = 2 * DSP_SB + N_EXPERTS * DSP_TM
DSP_UNROLL = 8


def _to_slabs(ref, x):
    n = x.shape[0]
    for s in range(SLAB):
        ref[pl.ds(s, n, stride=SLAB), :] = x[:, s * LANES:(s + 1) * LANES]


def _from_slabs(ref, n):
    return jnp.concatenate([ref[pl.ds(s, n, stride=SLAB), :] for s in range(SLAB)], axis=-1)


def _dsp_mods(mods_ref, chunk, chunks_per_row, first):
    row = 0 if chunks_per_row is None else 1 + chunk // chunks_per_row
    return _mod_rows(mods_ref, row, first)


def _dsp_route_kernel(x_ref, mods_ref, ln_ref, wr_ref, br_ref, h_ref, r_ref, *, chunks_per_row):
    sh2, sc2, _ = _dsp_mods(mods_ref, pl.program_id(0), chunks_per_row, 3)
    h2 = _rms_mod(x_ref[...], ln_ref[...], sc2, sh2)
    _to_slabs(h_ref, h2)
    lg = jnp.dot(h2, wr_ref[...], precision=HIGHEST, preferred_element_type=F32) + br_ref[...]
    i1, i2, w1, w2 = _route(lg)
    lane = lax.broadcasted_iota(jnp.int32, lg.shape, 1)
    r_ref[...] = jnp.where(lane == 0, i1, jnp.where(lane == 1, i2, jnp.where(lane == 2, w1,
                           jnp.where(lane == 3, w2, 0.0))))


def _dsp_route_call(layer, x2d, mods, ln2, w_router, b_router, chunks_per_row):
    n = x2d.shape[0]
    ch = DSP_CH
    return pl.pallas_call(
        functools.partial(_dsp_route_kernel, chunks_per_row=chunks_per_row),
        grid=(n // ch,),
        in_specs=[
            pl.BlockSpec((ch, D_MODEL), lambda c: (c, 0)),
            pl.BlockSpec((None, MODS_ROWS, N_MOD * D_MODEL), lambda c: (layer, 0, 0)),
            pl.BlockSpec((None, 1, D_MODEL), lambda c: (layer, 0, 0)),
            pl.BlockSpec((None, D_MODEL, LANES), lambda c: (layer, 0, 0)),
            pl.BlockSpec((None, 1, LANES), lambda c: (layer, 0, 0)),
        ],
        out_specs=[pl.BlockSpec((ch * SLAB, LANES), lambda c: (c, 0)),
                   pl.BlockSpec((ch, LANES), lambda c: (c, 0))],
        out_shape=[jax.ShapeDtypeStruct((n * SLAB, LANES), F32),
                   jax.ShapeDtypeStruct((n, LANES), F32)],
        compiler_params=pltpu.CompilerParams(vmem_limit_bytes=VMEM_LIMIT),
        name="moe_route",
    )(x2d, mods, ln2, w_router, b_router)


def _dispatch_tables(route, n_sb):
    e = route[:, 0:2].astype(jnp.int32).reshape(n_sb, 2 * DSP_SB)
    w = route[:, 2:4].reshape(n_sb, 1, 2 * DSP_SB)
    onehot = (e[..., None] == jnp.arange(N_EXPERTS, dtype=jnp.int32)).astype(jnp.int32)
    csum = jnp.cumsum(onehot, axis=1)
    rank = jnp.sum(csum * onehot, axis=-1) - 1
    cnt = csum[:, -1, :]
    ntile = (cnt + (DSP_TM - 1)) // DSP_TM
    off = (jnp.cumsum(ntile, axis=1) - ntile) * DSP_TM
    pos = jnp.sum(off[:, None, :] * onehot, axis=-1) + rank
    return pos.reshape(n_sb, 1, 2 * DSP_SB), w, ntile.reshape(-1), off.reshape(-1), cnt.reshape(-1)


def _dsp_expert_kernel(ntile_ref, off_ref, cnt_ref, pos_ref, w_ref, h_ref, wg_ref, wu_ref, wd_ref,
                       o_ref, row_smem, wgt_smem, xs_scr, ys_scr):
    sb = pl.program_id(0)
    e = pl.program_id(1)
    tm, u = DSP_TM, DSP_UNROLL
    slab_shift = SLAB.bit_length() - 1
    spare_row = DSP_SB * SLAB
    last_row = (DSP_SB - 1) * SLAB

    @pl.when(e == 0)
    def _():
        for ee in range(N_EXPERTS):
            q = sb * N_EXPERTS + ee

            def pad(p, carry):
                row_smem[p] = spare_row
                wgt_smem[p] = 0.0
                return carry

            lax.fori_loop(off_ref[q] + cnt_ref[q], off_ref[q] + ntile_ref[q] * tm, pad, 0)

        def invert(j, carry):
            for k in range(u):
                a = j * u + k
                p = pos_ref[0, a]
                row_smem[p] = lax.shift_left(lax.shift_right_logical(a, 1), slab_shift)
                wgt_smem[p] = w_ref[0, a]
            return carry

        lax.fori_loop(0, 2 * DSP_SB // u, invert, 0)
        o_ref[...] = jnp.zeros_like(o_ref)

    base0 = off_ref[sb * N_EXPERTS + e]

    def tile(i, carry):
        base = base0 + i * tm

        def gather(j, c2):
            for k in range(u):
                r = j * u + k
                src = jnp.minimum(row_smem[base + r], last_row)
                xs_scr[pl.ds(pl.multiple_of(r * SLAB, SLAB), SLAB), :] = (
                    h_ref[pl.ds(pl.multiple_of(src, SLAB), SLAB), :])
            return c2

        lax.fori_loop(0, tm // u, gather, 0)
        x = _from_slabs(xs_scr, tm).astype(BF16)
        a = jax.nn.silu(_dot(x, wg_ref[...])) * _dot(x, wu_ref[...])
        _to_slabs(ys_scr, _dot(a.astype(BF16), wd_ref[...]))

        def scatter(j, c2):
            upd = []
            for k in range(u):
                r = j * u + k
                dst = pl.multiple_of(row_smem[base + r], SLAB)
                y = ys_scr[pl.ds(pl.multiple_of(r * SLAB, SLAB), SLAB), :]
                upd.append((dst, o_ref[pl.ds(dst, SLAB), :] + wgt_smem[base + r] * y))
            for dst, v in upd:
                o_ref[pl.ds(dst, SLAB), :] = v
            return c2

        lax.fori_loop(0, tm // u, scatter, 0)
        return carry

    lax.fori_loop(0, ntile_ref[sb * N_EXPERTS + e], tile, 0)


def _dsp_expert_call(layer, h_slabs, pos, w, ntile, off, cnt, w_gate, w_up, w_down):
    n_sb = pos.shape[0]
    out_rows = (DSP_SB + DSP_CH) * SLAB

    def per_sb(s, e, *prefetch):
        return (s, 0, 0)

    def w_idx(s, e, *prefetch):
        return (layer, e, 0, 0)

    grid_spec = pltpu.PrefetchScalarGridSpec(
        num_scalar_prefetch=3,
        grid=(n_sb, N_EXPERTS),
        in_specs=[
            pl.BlockSpec((None, 1, 2 * DSP_SB), per_sb, memory_space=pltpu.SMEM),
            pl.BlockSpec((None, 1, 2 * DSP_SB), per_sb, memory_space=pltpu.SMEM),
            pl.BlockSpec((DSP_SB * SLAB, LANES), lambda s, e, *prefetch: (s, 0)),
            pl.BlockSpec((None, None, D_MODEL, EXPERT_FF), w_idx),
            pl.BlockSpec((None, None, D_MODEL, EXPERT_FF), w_idx),
            pl.BlockSpec((None, None, EXPERT_FF, D_MODEL), w_idx),
        ],
        out_specs=pl.BlockSpec((None, out_rows, LANES), per_sb),
        scratch_shapes=[pltpu.SMEM((DSP_PMAX,), jnp.int32),
                        pltpu.SMEM((DSP_PMAX,), F32),
                        pltpu.VMEM((DSP_TM * SLAB, LANES), F32),
                        pltpu.VMEM((DSP_TM * SLAB, LANES), F32)],
    )
    return pl.pallas_call(
        _dsp_expert_kernel,
        grid_spec=grid_spec,
        out_shape=jax.ShapeDtypeStruct((n_sb, out_rows, LANES), F32),
        compiler_params=pltpu.CompilerParams(vmem_limit_bytes=VMEM_LIMIT),
        name="moe_experts",
    )(ntile, off, cnt, pos, w, h_slabs, w_gate, w_up, w_down)


def _dsp_combine_kernel(x_ref, m_ref, mods_ref, fg_ref, o_ref, *, chunks_per_row, final):
    g2 = _dsp_mods(mods_ref, pl.program_id(0), chunks_per_row, 3)[2]
    y = x_ref[...] + g2 * _from_slabs(m_ref, DSP_CH)
    if final:
        y = y * lax.rsqrt(jnp.mean(y * y, axis=-1, keepdims=True) + EPS) * fg_ref[...]
    o_ref[...] = y


def _dsp_combine_call(layer, x2d, m_slabs, mods, final_g, chunks_per_row, final):
    n = x2d.shape[0]
    ch = DSP_CH
    per_sb = DSP_SB // ch
    return pl.pallas_call(
        functools.partial(_dsp_combine_kernel, chunks_per_row=chunks_per_row, final=final),
        grid=(n // ch,),
        in_specs=[
            pl.BlockSpec((ch, D_MODEL), lambda c: (c, 0)),
            pl.BlockSpec((None, ch * SLAB, LANES), lambda c: (c // per_sb, c % per_sb, 0)),
            pl.BlockSpec((None, MODS_ROWS, N_MOD * D_MODEL), lambda c: (layer, 0, 0)),
            pl.BlockSpec((1, D_MODEL), lambda c: (0, 0)),
        ],
        out_specs=pl.BlockSpec((ch, D_MODEL), lambda c: (c, 0)),
        out_shape=jax.ShapeDtypeStruct(x2d.shape, F32),
        compiler_params=pltpu.CompilerParams(vmem_limit_bytes=VMEM_LIMIT),
        name="moe_combine",
    )(x2d, m_slabs, mods, final_g)


def _sparse_moe(layer, x2d, mods, ln2, w_router, b_router, w_gate, w_up, w_down, final_g, *,
                chunks_per_row, final):
    n_sb = x2d.shape[0] // DSP_SB
    h_slabs, route = _dsp_route_call(layer, x2d, mods, ln2, w_router, b_router, chunks_per_row)
    pos, w, ntile, off, cnt = _dispatch_tables(route, n_sb)
    m_slabs = _dsp_expert_call(layer, h_slabs, pos, w, ntile, off, cnt, w_gate, w_up, w_down)
    return _dsp_combine_call(layer, x2d, m_slabs, mods, final_g, chunks_per_row, final)


LAT_PROJ_ROWS = 512


def _lat_proj_kernel(x_ref, mods_ref, ln_ref, win_ref, gb_ref, zc_ref, zn_ref, zm_ref, zg_ref):
    row = 1 + pl.program_id(0) // (DEC_SEQ // LAT_PROJ_ROWS)
    sh1, sc1, _ = _mod_rows(mods_ref, row, 0)
    h = _rms_mod(x_ref[...], ln_ref[...], sc1, sh1).astype(BF16)
    zc_ref[...] = _dot(h, win_ref[:, C_CONV:C_NA])
    zn_ref[...] = _dot(h, win_ref[:, C_NA:C_ML])
    zm_ref[...] = _dot(h, win_ref[:, C_ML:C_IG])
    zg_ref[...] = _dot(h, win_ref[:, C_IG:W_COLS]) + gb_ref[...]


def _lat_proj_call(layer, x2d, mods, ln1, w_in, gate_b):
    n = x2d.shape[0]
    tb = LAT_PROJ_ROWS
    widths = (C_NA - C_CONV, C_ML - C_NA, C_IG - C_ML, W_COLS - C_IG)
    return pl.pallas_call(
        _lat_proj_kernel,
        grid=(n // tb,),
        in_specs=[
            pl.BlockSpec((tb, D_MODEL), lambda i: (i, 0)),
            pl.BlockSpec((None, MODS_ROWS, N_MOD * D_MODEL), lambda i: (layer, 0, 0)),
            pl.BlockSpec((None, 1, D_MODEL), lambda i: (layer, 0, 0)),
            pl.BlockSpec((None, D_MODEL, W_COLS), lambda i: (layer, 0, 0)),
            pl.BlockSpec((None, 1, 2 * LANES), lambda i: (layer, 0, 0)),
        ],
        out_specs=[pl.BlockSpec((tb, w), lambda i: (i, 0)) for w in widths],
        out_shape=[jax.ShapeDtypeStruct((n, w), F32) for w in widths],
        compiler_params=pltpu.CompilerParams(vmem_limit_bytes=VMEM_LIMIT),
        name="lat_proj",
    )(x2d, mods, ln1, w_in, gate_b)


def _na_row_start(r):
    rows = DEC_SEQ // GRID_W
    return min(max(r - NA_WIN_ROWS // 2, 0), rows - NA_WIN_ROWS)


def _lat_local_kernel(zc_ref, zn_ref, kc_ref, vc_ref, band_ref, convw_ref, o_ref):
    o_ref[:, 0:CONV_DIM] = _short_conv(zc_ref[...], convw_ref[...])
    rows = DEC_SEQ // GRID_W
    blk = NA_WIN_ROWS * GRID_W
    q_all = (zn_ref[:, 0:NA_DIM] * NA_HEAD_DIM ** -0.5).astype(BF16)
    k_all = zn_ref[:, NA_DIM:2 * NA_DIM].astype(BF16)
    v_all = zn_ref[:, 2 * NA_DIM:3 * NA_DIM].astype(BF16)
    kc_all = kc_ref[...].astype(BF16)
    vc_all = vc_ref[...].astype(BF16)
    heads = []
    for h in range(NA_HEADS):
        sl = slice(h * NA_HEAD_DIM, (h + 1) * NA_HEAD_DIM)
        q, k, v, kc, vc = q_all[:, sl], k_all[:, sl], v_all[:, sl], kc_all[:, sl], vc_all[:, sl]
        band = band_ref[h]
        outs = []
        for r in range(rows):
            start = _na_row_start(r)
            off = (start - r + NA_WIN_ROWS - 1) * GRID_W
            q_r = q[r * GRID_W:(r + 1) * GRID_W]
            s_loc = _dot_nt(q_r, k[start * GRID_W:start * GRID_W + blk]) + band[:, off:off + blk]
            s_ctx = _dot_nt(q_r, kc)
            m = jnp.maximum(jnp.max(s_loc, axis=-1, keepdims=True), jnp.max(s_ctx, axis=-1, keepdims=True))
            p_loc = jnp.exp(s_loc - m)
            p_ctx = jnp.exp(s_ctx - m)
            den = jnp.sum(p_loc, axis=-1, keepdims=True) + jnp.sum(p_ctx, axis=-1, keepdims=True)
            o = _dot(p_loc.astype(BF16), v[start * GRID_W:start * GRID_W + blk]) + _dot(p_ctx.astype(BF16), vc)
            outs.append(o / den)
        heads.append(jnp.concatenate(outs, axis=0))
    o_ref[:, CONV_DIM:CONV_DIM + NA_DIM] = jnp.concatenate(heads, axis=-1)


def _lat_local_call(layer, zc, zn, cache_k, cache_v, band, conv_w):
    nb = zc.shape[0] // DEC_SEQ
    T = DEC_SEQ
    return pl.pallas_call(
        _lat_local_kernel,
        grid=(nb,),
        in_specs=[
            pl.BlockSpec((T, 3 * CONV_DIM), lambda b: (b, 0)),
            pl.BlockSpec((T, 3 * NA_DIM), lambda b: (b, 0)),
            pl.BlockSpec((None, None, PAST_LEN, NA_DIM), lambda b: (b, layer, 0, 0)),
            pl.BlockSpec((None, None, PAST_LEN, NA_DIM), lambda b: (b, layer, 0, 0)),
            pl.BlockSpec((None, NA_HEADS, GRID_W, band.shape[-1]), lambda b: (layer, 0, 0, 0)),
            pl.BlockSpec((None, 3, CONV_DIM), lambda b: (layer, 0, 0)),
        ],
        out_specs=pl.BlockSpec((T, CONV_DIM + NA_DIM), lambda b: (b, 0)),
        out_shape=jax.ShapeDtypeStruct((zc.shape[0], CONV_DIM + NA_DIM), F32),
        compiler_params=pltpu.CompilerParams(vmem_limit_bytes=VMEM_LIMIT),
        name="lat_conv_na",
    )(zc, zn, cache_k, cache_v, band, conv_w)


def _rope(x, cos, sin_signed):
    w = x.shape[-1]
    half = NA_HEAD_DIM // 2
    lane = lax.broadcasted_iota(jnp.int32, x.shape, 1)
    partner = jnp.where(lane % (2 * half) < half, pltpu.roll(x, w - half, axis=1), pltpu.roll(x, half, axis=1))
    return x * cos + partner * sin_signed


def _lat_mlstm_kernel(q_ref, k_ref, v_ref, o_ref, zg_ref, c0_ref, n0_ref, m0_ref, cos_ref, sin_ref,
                      mlg_ref, out_ref, bcol_scr, rowt_scr):
    hh = pl.program_id(1)

    @pl.when(hh == 0)
    def _():
        b_all, r_all = _gate_terms(zg_ref[:, 0:LANES], zg_ref[:, LANES:2 * LANES])
        bcol_scr[...] = b_all
        rowt_scr[...] = r_all

    b_col = bcol_scr[...]
    cos, sin_signed = cos_ref[...], sin_ref[...]
    q = _rope(q_ref[...], cos, sin_signed)
    k = _rope(k_ref[...], cos, sin_signed) * ML_HEAD_DIM ** -0.5
    v = v_ref[...]
    lane = lax.broadcasted_iota(jnp.int32, (DEC_SEQ, LANES), 1)
    hsum = None
    for d in range(N_DIRS):
        j = d * ML_HEADS + hh
        bc = jnp.sum(jnp.where(lane == j, b_col, 0.0), axis=-1, keepdims=True)
        lr = rowt_scr[pl.ds(j, 1), :]
        m0 = m0_ref[d, :, 0:1]
        ho = _mlstm_outputs(q, k, v, bc, lr, m0, c0_ref[d], n0_ref[d], d == 1, 256)
        hsum = ho if hsum is None else hsum + ho
    out_ref[...] = _head_readout(hsum, o_ref[...], mlg_ref[...])


def _lat_mlstm_call(layer, zm, zg, state_c, state_n, state_m, cos, sin_signed, ml_g):
    nb = zm.shape[0] // DEC_SEQ
    T = DEC_SEQ
    hd = ML_HEAD_DIM

    def col(part):
        return pl.BlockSpec((T, hd), lambda b, h: (b, part * ML_HEADS + h))

    return pl.pallas_call(
        _lat_mlstm_kernel,
        grid=(nb, ML_HEADS),
        in_specs=[
            col(0), col(1), col(2), col(3),
            pl.BlockSpec((T, 2 * LANES), lambda b, h: (b, 0)),
            pl.BlockSpec((None, None, N_DIRS, None, hd, hd), lambda b, h: (b, layer, 0, h, 0, 0)),
            pl.BlockSpec((None, None, N_DIRS, None, 1, hd), lambda b, h: (b, layer, 0, h, 0, 0)),
            pl.BlockSpec((None, None, N_DIRS, None, 1, LANES), lambda b, h: (b, layer, 0, h, 0, 0)),
            pl.BlockSpec((T, hd), lambda b, h: (0, 0)),
            pl.BlockSpec((T, hd), lambda b, h: (0, 0)),
            pl.BlockSpec((None, 1, hd), lambda b, h: (layer, 0, h)),
        ],
        out_specs=pl.BlockSpec((T, hd), lambda b, h: (b, h)),
        out_shape=jax.ShapeDtypeStruct((zm.shape[0], ML_DIM), F32),
        scratch_shapes=[pltpu.VMEM((T, LANES), F32), pltpu.VMEM((LANES, T), F32)],
        compiler_params=pltpu.CompilerParams(vmem_limit_bytes=VMEM_LIMIT),
        name="lat_mlstm",
    )(zm, zm, zm, zm, zg, state_c, state_n, state_m, cos, sin_signed, ml_g)


def _lat_merge_kernel(x_ref, loc_ref, ml_ref, mods_ref, wout_ref, o_ref):
    row = 1 + pl.program_id(0) // (DEC_SEQ // LAT_PROJ_ROWS)
    g1 = _mod_rows(mods_ref, row, 0)[2]
    split = CONV_DIM + NA_DIM
    y = _dot(loc_ref[...].astype(BF16), wout_ref[0:split, :]) + _dot(ml_ref[...].astype(BF16), wout_ref[split:, :])
    o_ref[...] = x_ref[...] + g1 * y


def _lat_merge_call(layer, x2d, loc, ml, mods, w_out):
    n = x2d.shape[0]
    tb = LAT_PROJ_ROWS
    return pl.pallas_call(
        _lat_merge_kernel,
        grid=(n // tb,),
        in_specs=[
            pl.BlockSpec((tb, D_MODEL), lambda i: (i, 0)),
            pl.BlockSpec((tb, CONV_DIM + NA_DIM), lambda i: (i, 0)),
            pl.BlockSpec((tb, ML_DIM), lambda i: (i, 0)),
            pl.BlockSpec((None, MODS_ROWS, N_MOD * D_MODEL), lambda i: (layer, 0, 0)),
            pl.BlockSpec((None, D_MODEL, D_MODEL), lambda i: (layer, 0, 0)),
        ],
        out_specs=pl.BlockSpec((tb, D_MODEL), lambda i: (i, 0)),
        out_shape=jax.ShapeDtypeStruct(x2d.shape, F32),
        compiler_params=pltpu.CompilerParams(vmem_limit_bytes=VMEM_LIMIT),
        name="lat_merge",
    )(x2d, loc, ml, mods, w_out)


def _pad_lanes(a, width):
    return jnp.pad(a, [(0, 0)] * (a.ndim - 1) + [(0, width - a.shape[-1])])


def _pack_w_in(w_in):
    main = w_in[..., :C_IG]
    gates = w_in[..., C_IG:]
    packed = jnp.concatenate([main, _pad_lanes(gates[..., :N_GATE], LANES),
                              _pad_lanes(gates[..., N_GATE:], LANES)], axis=-1)
    return packed.astype(BF16)


def _pack_gate_bias(ml_gate_b):
    gb = ml_gate_b.reshape(DEPTH, 2, N_GATE).astype(F32)
    return jnp.concatenate([_pad_lanes(gb[:, 0], LANES), _pad_lanes(gb[:, 1], LANES)], axis=-1)[:, None, :]


def _rpb_band(na_rpb):
    cols = np.arange(GRID_W)
    col_idx = np.clip(cols[None, :] - cols[:, None] + NA_WIN_COLS - 1, 0, 2 * NA_WIN_COLS - 2)
    col_start = np.clip(cols - NA_WIN_COLS // 2, 0, GRID_W - NA_WIN_COLS)
    col_mask = (cols[None, :] >= col_start[:, None]) & (cols[None, :] < col_start[:, None] + NA_WIN_COLS)
    t = na_rpb.astype(F32)[:, :, :, col_idx]
    t = jnp.where(col_mask[None, None, None], t, NEG_INF)
    t = jnp.transpose(t, (0, 1, 3, 2, 4))
    t = t.reshape(DEPTH, NA_HEADS, GRID_W, (2 * NA_WIN_ROWS - 1) * GRID_W)
    return _pad_lanes(t, 2 * NA_WIN_ROWS * GRID_W)


def _rope_tables():
    t = np.arange(DEC_SEQ)
    pos = np.stack([t // GRID_W, t % GRID_W], axis=-1).astype(np.float32)
    nf = ML_HEAD_DIM // 4
    inv = jnp.asarray(ROPE_BASE, F32) ** (-jnp.arange(nf, dtype=F32) / nf)
    ang = jnp.asarray(pos)[:, :, None] * inv
    cos = jnp.cos(ang)
    sin = jnp.sin(ang)
    cos_t = jnp.concatenate([cos, cos], axis=-1).reshape(DEC_SEQ, ML_HEAD_DIM)
    sin_t = jnp.concatenate([-sin, sin], axis=-1).reshape(DEC_SEQ, ML_HEAD_DIM)
    return cos_t, sin_t


def kernel(x_prompt, x_sample, cache_k, cache_v, state_C, state_n, state_m, c, c_ctx, w_mod, b_mod,
           ln1_g, w_in, conv_w, na_rpb, ml_gate_b, ml_norm_g, w_out, ln2_g, w_rg, b_rg, w_re, b_re,
           w_gate, w_up, w_down, final_g):
    nb_ctx = x_prompt.shape[0]
    nb_lat = x_sample.shape[0]
    assert 1 + nb_lat <= MODS_ROWS

    cvecs = jnp.concatenate([c_ctx[None, :], c,
                             jnp.zeros((MODS_ROWS - 1 - nb_lat, D_MODEL), F32)], axis=0)
    mods = _mods_call(cvecs, w_mod, b_mod)

    w_in_p = _pack_w_in(w_in)
    w_out_b = w_out.astype(BF16)
    w_gate_b, w_up_b, w_down_b = w_gate.astype(BF16), w_up.astype(BF16), w_down.astype(BF16)
    gate_b = _pack_gate_bias(ml_gate_b)
    ln1 = ln1_g.reshape(DEPTH, 1, D_MODEL)
    ln2 = ln2_g.reshape(DEPTH, 1, D_MODEL)
    ml_g = ml_norm_g.reshape(DEPTH, 1, ML_DIM)
    w_router = _pad_lanes(jnp.concatenate([w_re, w_rg], axis=-1), LANES)
    b_router = _pad_lanes(jnp.concatenate([b_re, b_rg], axis=-1), LANES)[:, None, :]
    fg = final_g.reshape(1, D_MODEL)
    band = _rpb_band(na_rpb)
    cos_t, sin_t = _rope_tables()
    ck = cache_k.reshape(nb_lat, DEPTH, PAST_LEN, NA_DIM)
    cv = cache_v.reshape(nb_lat, DEPTH, PAST_LEN, NA_DIM)
    st_n = state_n.reshape(nb_lat, DEPTH, N_DIRS, ML_HEADS, 1, ML_HEAD_DIM)
    st_m = jnp.broadcast_to(state_m[..., None, None], (nb_lat, DEPTH, N_DIRS, ML_HEADS, 1, LANES))

    xp = x_prompt
    xs = x_sample.reshape(nb_lat * DEC_SEQ, D_MODEL)
    states = None
    for l in range(DEPTH):
        last = l == DEPTH - 1
        xp, states = _ctx_mixer_call(l, xp, mods, ln1, w_in_p, conv_w, gate_b, ml_g, w_out_b, states)
        xp = _sparse_moe(l, xp.reshape(nb_ctx * SEQ, D_MODEL), mods, ln2, w_router, b_router,
                         w_gate_b, w_up_b, w_down_b, fg, chunks_per_row=None, final=last)
        xp = xp.reshape(nb_ctx, SEQ, D_MODEL)

        zc, zn, zm, zg = _lat_proj_call(l, xs, mods, ln1, w_in_p, gate_b)
        loc = _lat_local_call(l, zc, zn, ck, cv, band, conv_w)
        ml = _lat_mlstm_call(l, zm, zg, state_C, st_n, st_m, cos_t, sin_t, ml_g)
        xs = _lat_merge_call(l, xs, loc, ml, mods, w_out_b)
        xs = _sparse_moe(l, xs, mods, ln2, w_router, b_router, w_gate_b, w_up_b, w_down_b, fg,
                         chunks_per_row=DEC_SEQ // DSP_CH, final=last)

    new_k, new_v, new_c, new_n, new_m = states
    return (xp, xs.reshape(nb_lat, DEC_SEQ, D_MODEL),
            new_k.reshape(nb_ctx, DEPTH, SEQ, NA_HEADS, NA_HEAD_DIM),
            new_v.reshape(nb_ctx, DEPTH, SEQ, NA_HEADS, NA_HEAD_DIM),
            new_c,
            new_n.reshape(nb_ctx, DEPTH, N_DIRS, ML_HEADS, ML_HEAD_DIM),
            new_m[..., 0].reshape(nb_ctx, DEPTH, N_DIRS, ML_HEADS))
```

```python
import functools

import numpy as np
import jax
import jax.numpy as jnp
from jax import lax
from jax.experimental import pallas as pl
from jax.experimental.pallas import tpu as pltpu

F32 = jnp.float32
BF16 = jnp.bfloat16

D_MODEL = 1024
SEQ = 256
DEPTH = 4
DEC_SEQ = 1024
PAST_LEN = 512
GRID_W = 64
CONV_DIM = 256
NA_HEADS = 4
NA_HEAD_DIM = 64
NA_DIM = NA_HEADS * NA_HEAD_DIM
NA_WIN_ROWS = 8
NA_WIN_COLS = 16
ML_HEADS = 4
ML_HEAD_DIM = 128
ML_DIM = ML_HEADS * ML_HEAD_DIM
N_DIRS = 2
N_GROUPS = 4
EXPERTS_PER_GROUP = 4
N_EXPERTS = N_GROUPS * EXPERTS_PER_GROUP
EXPERT_FF = 512
ROPE_BASE = 10000.0
EPS = 1e-6
N_MOD = 6

LANES = 128
N_GATE = N_DIRS * ML_HEADS
C_CONV = 0
C_NA = 3 * CONV_DIM
C_ML = C_NA + 3 * NA_DIM
C_IG = C_ML + 4 * ML_DIM
C_FG = C_IG + LANES
W_COLS = C_FG + LANES
CUMSUM_BLOCK = 256
MODS_ROWS = 8
VMEM_LIMIT = 56 * 1024 * 1024

NEG_INF = float("-inf")
HIGHEST = lax.Precision.HIGHEST


def _dot(a, b):
    return jnp.dot(a, b, preferred_element_type=F32)


def _dot_nt(a, b):
    return lax.dot_general(a, b, (((1,), (1,)), ((), ())), preferred_element_type=F32)


def _rms_mod(x, g, sc, sh):
    y = x * lax.rsqrt(jnp.mean(x * x, axis=-1, keepdims=True) + EPS)
    return (y * g) * (1.0 + sc) + sh


def _mod_rows(mods_ref, row, first):
    return [mods_ref[pl.ds(row, 1), pl.ds((first + j) * D_MODEL, D_MODEL)] for j in range(3)]


def _mods_kernel(cv_ref, w_ref, b_ref, o_ref):
    cv = cv_ref[...]
    s = cv * jax.nn.sigmoid(cv)
    o_ref[...] = _dot(s.astype(BF16), w_ref[...].astype(BF16)) + b_ref[...]


def _mods_call(cvecs, w_mod, b_mod):
    tn = 1536
    n = N_MOD * D_MODEL
    return pl.pallas_call(
        _mods_kernel,
        grid=(DEPTH, n // tn),
        in_specs=[pl.BlockSpec((MODS_ROWS, D_MODEL), lambda l, j: (0, 0)),
                  pl.BlockSpec((None, D_MODEL, tn), lambda l, j: (l, 0, j)),
                  pl.BlockSpec((None, 1, tn), lambda l, j: (l, 0, j))],
        out_specs=pl.BlockSpec((None, MODS_ROWS, tn), lambda l, j: (l, 0, j)),
        out_shape=jax.ShapeDtypeStruct((DEPTH, MODS_ROWS, n), F32),
        compiler_params=pltpu.CompilerParams(vmem_limit_bytes=VMEM_LIMIT),
        name="adaln_mods",
    )(cvecs, w_mod, b_mod.reshape(DEPTH, 1, n))


def _short_conv(zc, cw):
    T = zc.shape[0]
    cb = zc[:, 0:CONV_DIM]
    u = zc[:, CONV_DIM:2 * CONV_DIM] * zc[:, 2 * CONV_DIM:3 * CONV_DIM]
    t = lax.broadcasted_iota(jnp.int32, u.shape, 0)
    u_prev = jnp.where(t == 0, 0.0, pltpu.roll(u, 1, axis=0))
    u_next = jnp.where(t == T - 1, 0.0, pltpu.roll(u, T - 1, axis=0))
    return cb * (cw[0:1, :] * u_prev + cw[1:2, :] * u + cw[2:3, :] * u_next)


def _softmax_attention(nq, nk, nv):
    outs = []
    for h in range(NA_HEADS):
        sl = slice(h * NA_HEAD_DIM, (h + 1) * NA_HEAD_DIM)
        q = (nq[:, sl] * NA_HEAD_DIM ** -0.5).astype(BF16)
        s = _dot_nt(q, nk[:, sl].astype(BF16))
        p = jnp.exp(s - jnp.max(s, axis=-1, keepdims=True))
        o = _dot(p.astype(BF16), nv[:, sl].astype(BF16)) / jnp.sum(p, axis=-1, keepdims=True)
        outs.append(o)
    return jnp.concatenate(outs, axis=-1)


def _log_sigmoid(x):
    return jnp.minimum(x, 0.0) - jnp.log(1.0 + jnp.exp(-jnp.abs(x)))


def _gate_terms(zi, zf):
    T = zi.shape[0]
    lf = _log_sigmoid(zf)
    blk = min(T, CUMSUM_BLOCK)
    r = lax.broadcasted_iota(jnp.int32, (blk, blk), 0)
    c = lax.broadcasted_iota(jnp.int32, (blk, blk), 1)
    tril = jnp.where(c <= r, 1.0, 0.0).astype(F32)
    parts, carry = [], None
    for r0 in range(0, T, blk):
        b = jnp.dot(tril, lf[r0:r0 + blk], precision=HIGHEST, preferred_element_type=F32)
        if carry is not None:
            b = b + carry
        carry = b[blk - 1:blk, :]
        parts.append(b)
    b_fwd = parts[0] if len(parts) == 1 else jnp.concatenate(parts, axis=0)
    b_bwd = (b_fwd[T - 1:T, :] - b_fwd) + lf
    lane = lax.broadcasted_iota(jnp.int32, zi.shape, 1)
    b_col = jnp.where(lane < ML_HEADS, b_fwd, b_bwd)
    row_t = jnp.transpose(zi - b_col)
    return b_col, row_t


def _mlstm_outputs(q, k, v, b_col, li_row, m0, c0, n0, backward, q_block):
    T = q.shape[0]
    qb, kb, vb = q.astype(BF16), k.astype(BF16), v.astype(BF16)
    c0b = None if c0 is None else c0.astype(BF16)
    outs = []
    for r0 in range(0, T, q_block):
        ks, ke = (r0, T) if backward else (0, r0 + q_block)
        bq = b_col[r0:r0 + q_block]
        d = bq + li_row[:, ks:ke]
        t_idx = r0 + lax.broadcasted_iota(jnp.int32, d.shape, 0)
        s_idx = ks + lax.broadcasted_iota(jnp.int32, d.shape, 1)
        d = jnp.where((s_idx >= t_idx) if backward else (s_idx <= t_idx), d, NEG_INF)
        inter = bq + m0
        m_t = jnp.maximum(jnp.max(d, axis=-1, keepdims=True), inter)
        w = jnp.exp(d - m_t)
        s = _dot_nt(qb[r0:r0 + q_block], kb[ks:ke]) * w
        num = _dot(s.astype(BF16), vb[ks:ke])
        den = jnp.sum(s, axis=-1, keepdims=True)
        if c0 is not None:
            a = jnp.exp(inter - m_t)
            num = num + a * _dot(qb[r0:r0 + q_block], c0b)
            den = den + a * jnp.sum(q[r0:r0 + q_block] * n0, axis=-1, keepdims=True)
        outs.append(num / jnp.maximum(jnp.abs(den), jnp.exp(-m_t)))
    return outs[0] if len(outs) == 1 else jnp.concatenate(outs, axis=0)


def _mlstm_state(k, v, b_col, li_col, backward):
    T = k.shape[0]
    b_tot = b_col[0:1] if backward else b_col[T - 1:T]
    g = (b_tot - b_col) + li_col
    m_new = jnp.maximum(b_tot, jnp.max(g, axis=0, keepdims=True))
    kw = k * jnp.exp(g - m_new)
    c_new = _dot(jnp.transpose(kw).astype(BF16), v.astype(BF16))
    n_new = jnp.sum(kw, axis=0, keepdims=True)
    return c_new, n_new, m_new


def _head_readout(h, mo, g):
    hn = h * lax.rsqrt(jnp.mean(h * h, axis=-1, keepdims=True) + EPS)
    return hn * g * jax.nn.sigmoid(mo)


def _ctx_mixer_kernel(x_ref, mods_ref, ln_ref, win_ref, convw_ref, gb_ref, mlg_ref, wout_ref, *rest):
    xo_ref, ko_ref, vo_ref, co_ref, no_ref, mo_ref = rest[-6:]
    T = SEQ
    sh1, sc1, g1 = _mod_rows(mods_ref, 0, 0)
    x = x_ref[...]
    h = _rms_mod(x, ln_ref[...], sc1, sh1).astype(BF16)

    conv_o = _short_conv(_dot(h, win_ref[:, C_CONV:C_NA]), convw_ref[...])

    zn = _dot(h, win_ref[:, C_NA:C_ML])
    nk, nv = zn[:, NA_DIM:2 * NA_DIM], zn[:, 2 * NA_DIM:3 * NA_DIM]
    ko_ref[...] = nk
    vo_ref[...] = nv
    na_o = _softmax_attention(zn[:, 0:NA_DIM], nk, nv)

    zg = _dot(h, win_ref[:, C_IG:W_COLS])
    zi = zg[:, 0:LANES] + gb_ref[:, 0:LANES]
    zf = zg[:, LANES:2 * LANES] + gb_ref[:, LANES:2 * LANES]
    b_col, row_t = _gate_terms(zi, zf)
    m0 = jnp.zeros((1, 1), F32)

    zm = _dot(h, win_ref[:, C_ML:C_IG])
    ml_parts = []
    for hh in range(ML_HEADS):
        zq, zk, zv, zo = (zm[:, p * ML_DIM + hh * ML_HEAD_DIM:p * ML_DIM + (hh + 1) * ML_HEAD_DIM]
                          for p in range(4))
        zk = zk * ML_HEAD_DIM ** -0.5
        hsum = None
        for d in range(N_DIRS):
            j = d * ML_HEADS + hh
            bc = b_col[:, j:j + 1]
            ho = _mlstm_outputs(zq, zk, zv, bc, row_t[j:j + 1, :], m0, None, None, d == 1, T)
            hsum = ho if hsum is None else hsum + ho
            c_new, n_new, m_new = _mlstm_state(zk, zv, bc, zi[:, j:j + 1], d == 1)
            co_ref[d, hh] = c_new
            no_ref[j:j + 1, :] = n_new
            mo_ref[j:j + 1, :] = jnp.broadcast_to(m_new, (1, LANES))
        ml_parts.append(_head_readout(hsum, zo, mlg_ref[:, hh * ML_HEAD_DIM:(hh + 1) * ML_HEAD_DIM]))

    mix = jnp.concatenate([conv_o, na_o] + ml_parts, axis=-1).astype(BF16)
    xo_ref[...] = x + g1 * _dot(mix, wout_ref[...])


def _ctx_mixer_call(layer, x, mods, ln1, w_in, conv_w, gate_b, ml_g, w_out, prev):
    B = x.shape[0]
    T = SEQ
    state_shapes = [
        jax.ShapeDtypeStruct((B, DEPTH, T, NA_DIM), F32),
        jax.ShapeDtypeStruct((B, DEPTH, T, NA_DIM), F32),
        jax.ShapeDtypeStruct((B, DEPTH, N_DIRS, ML_HEADS, ML_HEAD_DIM, ML_HEAD_DIM), F32),
        jax.ShapeDtypeStruct((B, DEPTH, N_GATE, ML_HEAD_DIM), F32),
        jax.ShapeDtypeStruct((B, DEPTH, N_GATE, LANES), F32),
    ]
    in_specs = [
        pl.BlockSpec((None, T, D_MODEL), lambda b: (b, 0, 0)),
        pl.BlockSpec((None, MODS_ROWS, N_MOD * D_MODEL), lambda b: (layer, 0, 0)),
        pl.BlockSpec((None, 1, D_MODEL), lambda b: (layer, 0, 0)),
        pl.BlockSpec((None, D_MODEL, W_COLS), lambda b: (layer, 0, 0)),
        pl.BlockSpec((None, 3, CONV_DIM), lambda b: (layer, 0, 0)),
        pl.BlockSpec((None, 1, 2 * LANES), lambda b: (layer, 0, 0)),
        pl.BlockSpec((None, 1, ML_DIM), lambda b: (layer, 0, 0)),
        pl.BlockSpec((None, D_MODEL, D_MODEL), lambda b: (layer, 0, 0)),
    ]
    args = [x, mods, ln1, w_in, conv_w, gate_b, ml_g, w_out]
    aliases = {}
    if prev is not None:
        in_specs += [pl.BlockSpec(memory_space=pl.ANY)] * len(prev)
        aliases = {len(args) + i: 1 + i for i in range(len(prev))}
        args += list(prev)
    out_specs = [
        pl.BlockSpec((None, T, D_MODEL), lambda b: (b, 0, 0)),
        pl.BlockSpec((None, None, T, NA_DIM), lambda b: (b, layer, 0, 0)),
        pl.BlockSpec((None, None, T, NA_DIM), lambda b: (b, layer, 0, 0)),
        pl.BlockSpec((None, None, N_DIRS, ML_HEADS, ML_HEAD_DIM, ML_HEAD_DIM),
                     lambda b: (b, layer, 0, 0, 0, 0)),
        pl.BlockSpec((None, None, N_GATE, ML_HEAD_DIM), lambda b: (b, layer, 0, 0)),
        pl.BlockSpec((None, None, N_GATE, LANES), lambda b: (b, layer, 0, 0)),
    ]
    outs = pl.pallas_call(
        _ctx_mixer_kernel,
        grid=(B,),
        in_specs=in_specs,
        out_specs=out_specs,
        out_shape=[jax.ShapeDtypeStruct(x.shape, F32)] + state_shapes,
        input_output_aliases=aliases,
        compiler_params=pltpu.CompilerParams(vmem_limit_bytes=VMEM_LIMIT),
        name="ctx_mixer",
    )(*args)
    return outs[0], tuple(outs[1:])


def _route(lg):
    lane = lax.broadcasted_iota(jnp.int32, lg.shape, 1)
    lane_f = lane.astype(F32)
    big = float(LANES)
    is_g = (lane >= N_EXPERTS) & (lane < N_EXPERTS + N_GROUPS)
    glm = jnp.where(is_g, lg, NEG_INF)
    gmax = jnp.max(glm, axis=-1, keepdims=True)
    g_top = jnp.min(jnp.where(glm == gmax, lane_f, big), axis=-1, keepdims=True) - N_EXPERTS
    gp = 1.0 / jnp.sum(jnp.where(is_g, jnp.exp(lg - gmax), 0.0), axis=-1, keepdims=True)
    grp = (lane >> (EXPERTS_PER_GROUP.bit_length() - 1)).astype(F32)
    in_grp = (lane < N_EXPERTS) & (grp == g_top)
    e1 = jnp.where(in_grp, lg, NEG_INF)
    v1 = jnp.max(e1, axis=-1, keepdims=True)
    i1 = jnp.min(jnp.where(e1 == v1, lane_f, big), axis=-1, keepdims=True)
    e2 = jnp.where(in_grp & (lane_f != i1), lg, NEG_INF)
    v2 = jnp.max(e2, axis=-1, keepdims=True)
    i2 = jnp.min(jnp.where(e2 == v2, lane_f, big), axis=-1, keepdims=True)
    t = jnp.exp(v2 - v1)
    w1 = 1.0 / (1.0 + t)
    w2 = t / (1.0 + t)
    return i1, i2, gp * w1, gp * w2


SLAB = D_MODEL // LANES
DSP_SB = 2048
DSP_TM = 128
DSP_CH = 512
DSP_PMAX = 2 * DSP_SB + N_EXPERTS * DSP_TM
DSP_UNROLL = 8


def _to_slabs(ref, x):
    n = x.shape[0]
    for s in range(SLAB):
        ref[pl.ds(s, n, stride=SLAB), :] = x[:, s * LANES:(s + 1) * LANES]


def _from_slabs(ref, n):
    return jnp.concatenate([ref[pl.ds(s, n, stride=SLAB), :] for s in range(SLAB)], axis=-1)


def _dsp_mods(mods_ref, chunk, chunks_per_row, first):
    row = 0 if chunks_per_row is None else 1 + chunk // chunks_per_row
    return _mod_rows(mods_ref, row, first)


def _dsp_route_kernel(x_ref, mods_ref, ln_ref, wr_ref, br_ref, h_ref, r_ref, cnt_ref, run_scr, *,
                      chunks_per_row):
    c = pl.program_id(0)
    ch = DSP_CH

    @pl.when(c % (DSP_SB // ch) == 0)
    def _():
        run_scr[...] = jnp.zeros_like(run_scr)

    sh2, sc2, _ = _dsp_mods(mods_ref, c, chunks_per_row, 3)
    h2 = _rms_mod(x_ref[...], ln_ref[...], sc2, sh2)
    _to_slabs(h_ref, h2)
    lg = jnp.dot(h2, wr_ref[...], precision=HIGHEST, preferred_element_type=F32) + br_ref[...]
    i1, i2, w1, w2 = _route(lg)
    lane = lax.broadcasted_iota(jnp.int32, lg.shape, 1)
    lane_f = lane.astype(F32)
    oh1 = jnp.where(lane_f == i1, 1.0, 0.0)
    oh2 = jnp.where(lane_f == i2, 1.0, 0.0)
    both = oh1 + oh2
    r = lax.broadcasted_iota(jnp.int32, (ch, ch), 0)
    s = lax.broadcasted_iota(jnp.int32, (ch, ch), 1)
    earlier = jnp.where(s < r, 1.0, 0.0).astype(BF16)
    run = run_scr[0:1, :]
    before = _dot(earlier, both.astype(BF16)) + run
    rank1 = jnp.sum(before * oh1, axis=-1, keepdims=True)
    rank2 = jnp.sum(before * oh2, axis=-1, keepdims=True)
    run = run + jnp.sum(both, axis=0, keepdims=True)
    run_scr[0:1, :] = run
    cnt_ref[...] = jnp.broadcast_to(run, cnt_ref.shape)
    cols = (i1, i2, w1, w2, rank1, rank2)
    out = jnp.zeros(lg.shape, F32)
    for j, col in enumerate(cols):
        out = jnp.where(lane == j, col, out)
    r_ref[...] = out


def _dsp_route_call(layer, x2d, mods, ln2, w_router, b_router, chunks_per_row):
    n = x2d.shape[0]
    ch = DSP_CH
    per_sb = DSP_SB // ch
    return pl.pallas_call(
        functools.partial(_dsp_route_kernel, chunks_per_row=chunks_per_row),
        grid=(n // ch,),
        in_specs=[
            pl.BlockSpec((ch, D_MODEL), lambda c: (c, 0)),
            pl.BlockSpec((None, MODS_ROWS, N_MOD * D_MODEL), lambda c: (layer, 0, 0)),
            pl.BlockSpec((None, 1, D_MODEL), lambda c: (layer, 0, 0)),
            pl.BlockSpec((None, D_MODEL, LANES), lambda c: (layer, 0, 0)),
            pl.BlockSpec((None, 1, LANES), lambda c: (layer, 0, 0)),
        ],
        out_specs=[pl.BlockSpec((ch * SLAB, LANES), lambda c: (c, 0)),
                   pl.BlockSpec((ch, LANES), lambda c: (c, 0)),
                   pl.BlockSpec((None, SLAB, LANES), lambda c: (c // per_sb, 0, 0))],
        out_shape=[jax.ShapeDtypeStruct((n * SLAB, LANES), F32),
                   jax.ShapeDtypeStruct((n, LANES), F32),
                   jax.ShapeDtypeStruct((n // DSP_SB, SLAB, LANES), F32)],
        scratch_shapes=[pltpu.VMEM((SLAB, LANES), F32)],
        compiler_params=pltpu.CompilerParams(vmem_limit_bytes=VMEM_LIMIT),
        name="moe_route",
    )(x2d, mods, ln2, w_router, b_router)


EXPERT_BITS = (N_EXPERTS - 1).bit_length()


def _dispatch_tables(route, counts, n_sb):
    e = route[:, 0:2].astype(jnp.int32)
    rank = route[:, 4:6].astype(jnp.int32)
    code = (lax.shift_left(rank, EXPERT_BITS) | e).reshape(n_sb, 1, 2 * DSP_SB)
    w = route[:, 2:4].reshape(n_sb, 1, 2 * DSP_SB)
    cnt = counts[:, 0, :N_EXPERTS].astype(jnp.int32)
    ntile = (cnt + (DSP_TM - 1)) // DSP_TM
    off = (jnp.cumsum(ntile, axis=1) - ntile) * DSP_TM
    return code, w, ntile.reshape(-1), off.reshape(-1), cnt.reshape(-1)


def _dsp_expert_kernel(ntile_ref, off_ref, cnt_ref, code_ref, w_ref, h_ref, wg_ref, wu_ref, wd_ref,
                       o_ref, row_smem, wgt_smem, xs_scr, ys_scr):
    sb = pl.program_id(0)
    e = pl.program_id(1)
    tm, u = DSP_TM, DSP_UNROLL
    slab_shift = SLAB.bit_length() - 1
    spare_row = DSP_SB * SLAB
    last_row = (DSP_SB - 1) * SLAB

    @pl.when(e == 0)
    def _():
        for ee in range(N_EXPERTS):
            q = sb * N_EXPERTS + ee

            def pad(p, carry):
                row_smem[p] = spare_row
                wgt_smem[p] = 0.0
                return carry

            lax.fori_loop(off_ref[q] + cnt_ref[q], off_ref[q] + ntile_ref[q] * tm, pad, 0)

        def invert(j, carry):
            for k in range(u):
                a = j * u + k
                code = code_ref[0, a]
                p = (off_ref[sb * N_EXPERTS + (code & (N_EXPERTS - 1))]
                     + lax.shift_right_logical(code, EXPERT_BITS))
                row_smem[p] = lax.shift_left(lax.shift_right_logical(a, 1), slab_shift)
                wgt_smem[p] = w_ref[0, a]
            return carry

        lax.fori_loop(0, 2 * DSP_SB // u, invert, 0)
        o_ref[...] = jnp.zeros_like(o_ref)

    base0 = off_ref[sb * N_EXPERTS + e]
    wg = wg_ref[...].astype(BF16)
    wu = wu_ref[...].astype(BF16)
    wd = wd_ref[...].astype(BF16)

    def tile(i, carry):
        base = base0 + i * tm

        def gather(j, c2):
            for k in range(u):
                r = j * u + k
                src = jnp.minimum(row_smem[base + r], last_row)
                xs_scr[pl.ds(pl.multiple_of(r * SLAB, SLAB), SLAB), :] = (
                    h_ref[pl.ds(pl.multiple_of(src, SLAB), SLAB), :])
            return c2

        lax.fori_loop(0, tm // u, gather, 0)
        x = _from_slabs(xs_scr, tm).astype(BF16)
        a = jax.nn.silu(_dot(x, wg)) * _dot(x, wu)
        _to_slabs(ys_scr, _dot(a.astype(BF16), wd))

        def scatter(j, c2):
            upd = []
            for k in range(u):
                r = j * u + k
                dst = pl.multiple_of(row_smem[base + r], SLAB)
                y = ys_scr[pl.ds(pl.multiple_of(r * SLAB, SLAB), SLAB), :]
                upd.append((dst, o_ref[pl.ds(dst, SLAB), :] + wgt_smem[base + r] * y))
            for dst, v in upd:
                o_ref[pl.ds(dst, SLAB), :] = v
            return c2

        lax.fori_loop(0, tm // u, scatter, 0)
        return carry

    lax.fori_loop(0, ntile_ref[sb * N_EXPERTS + e], tile, 0)


def _dsp_expert_call(layer, h_slabs, code, w, ntile, off, cnt, w_gate, w_up, w_down):
    n_sb = code.shape[0]
    out_rows = (DSP_SB + DSP_CH) * SLAB

    def per_sb(s, e, *prefetch):
        return (s, 0, 0)

    def w_idx(s, e, *prefetch):
        return (layer, e, 0, 0)

    grid_spec = pltpu.PrefetchScalarGridSpec(
        num_scalar_prefetch=3,
        grid=(n_sb, N_EXPERTS),
        in_specs=[
            pl.BlockSpec((None, 1, 2 * DSP_SB), per_sb, memory_space=pltpu.SMEM),
            pl.BlockSpec((None, 1, 2 * DSP_SB), per_sb, memory_space=pltpu.SMEM),
            pl.BlockSpec((DSP_SB * SLAB, LANES), lambda s, e, *prefetch: (s, 0)),
            pl.BlockSpec((None, None, D_MODEL, EXPERT_FF), w_idx),
            pl.BlockSpec((None, None, D_MODEL, EXPERT_FF), w_idx),
            pl.BlockSpec((None, None, EXPERT_FF, D_MODEL), w_idx),
        ],
        out_specs=pl.BlockSpec((None, out_rows, LANES), per_sb),
        scratch_shapes=[pltpu.SMEM((DSP_PMAX,), jnp.int32),
                        pltpu.SMEM((DSP_PMAX,), F32),
                        pltpu.VMEM((DSP_TM * SLAB, LANES), F32),
                        pltpu.VMEM((DSP_TM * SLAB, LANES), F32)],
    )
    return pl.pallas_call(
        _dsp_expert_kernel,
        grid_spec=grid_spec,
        out_shape=jax.ShapeDtypeStruct((n_sb, out_rows, LANES), F32),
        compiler_params=pltpu.CompilerParams(vmem_limit_bytes=VMEM_LIMIT),
        name="moe_experts",
    )(ntile, off, cnt, code, w, h_slabs, w_gate, w_up, w_down)


def _dsp_combine_kernel(x_ref, m_ref, mods_ref, fg_ref, o_ref, *, chunks_per_row, final):
    g2 = _dsp_mods(mods_ref, pl.program_id(0), chunks_per_row, 3)[2]
    y = x_ref[...] + g2 * _from_slabs(m_ref, DSP_CH)
    if final:
        y = y * lax.rsqrt(jnp.mean(y * y, axis=-1, keepdims=True) + EPS) * fg_ref[...]
    o_ref[...] = y


def _dsp_combine_call(layer, x2d, m_slabs, mods, final_g, chunks_per_row, final):
    n = x2d.shape[0]
    ch = DSP_CH
    per_sb = DSP_SB // ch
    return pl.pallas_call(
        functools.partial(_dsp_combine_kernel, chunks_per_row=chunks_per_row, final=final),
        grid=(n // ch,),
        in_specs=[
            pl.BlockSpec((ch, D_MODEL), lambda c: (c, 0)),
            pl.BlockSpec((None, ch * SLAB, LANES), lambda c: (c // per_sb, c % per_sb, 0)),
            pl.BlockSpec((None, MODS_ROWS, N_MOD * D_MODEL), lambda c: (layer, 0, 0)),
            pl.BlockSpec((1, D_MODEL), lambda c: (0, 0)),
        ],
        out_specs=pl.BlockSpec((ch, D_MODEL), lambda c: (c, 0)),
        out_shape=jax.ShapeDtypeStruct(x2d.shape, F32),
        compiler_params=pltpu.CompilerParams(vmem_limit_bytes=VMEM_LIMIT),
        name="moe_combine",
    )(x2d, m_slabs, mods, final_g)


def _sparse_moe(layer, x2d, mods, ln2, w_router, b_router, w_gate, w_up, w_down, final_g, *,
                chunks_per_row, final):
    n_sb = x2d.shape[0] // DSP_SB
    h_slabs, route, counts = _dsp_route_call(layer, x2d, mods, ln2, w_router, b_router,
                                             chunks_per_row)
    code, w, ntile, off, cnt = _dispatch_tables(route, counts, n_sb)
    m_slabs = _dsp_expert_call(layer, h_slabs, code, w, ntile, off, cnt, w_gate, w_up, w_down)
    return _dsp_combine_call(layer, x2d, m_slabs, mods, final_g, chunks_per_row, final)


LAT_PROJ_ROWS = 512


def _lat_proj_kernel(x_ref, mods_ref, ln_ref, win_ref, gb_ref, zc_ref, zn_ref, zm_ref, zg_ref):
    row = 1 + pl.program_id(0) // (DEC_SEQ // LAT_PROJ_ROWS)
    sh1, sc1, _ = _mod_rows(mods_ref, row, 0)
    h = _rms_mod(x_ref[...], ln_ref[...], sc1, sh1).astype(BF16)
    zc_ref[...] = _dot(h, win_ref[:, C_CONV:C_NA])
    zn_ref[...] = _dot(h, win_ref[:, C_NA:C_ML])
    zm_ref[...] = _dot(h, win_ref[:, C_ML:C_IG])
    zg_ref[...] = _dot(h, win_ref[:, C_IG:W_COLS]) + gb_ref[...]


def _lat_proj_call(layer, x2d, mods, ln1, w_in, gate_b):
    n = x2d.shape[0]
    tb = LAT_PROJ_ROWS
    widths = (C_NA - C_CONV, C_ML - C_NA, C_IG - C_ML, W_COLS - C_IG)
    return pl.pallas_call(
        _lat_proj_kernel,
        grid=(n // tb,),
        in_specs=[
            pl.BlockSpec((tb, D_MODEL), lambda i: (i, 0)),
            pl.BlockSpec((None, MODS_ROWS, N_MOD * D_MODEL), lambda i: (layer, 0, 0)),
            pl.BlockSpec((None, 1, D_MODEL), lambda i: (layer, 0, 0)),
            pl.BlockSpec((None, D_MODEL, W_COLS), lambda i: (layer, 0, 0)),
            pl.BlockSpec((None, 1, 2 * LANES), lambda i: (layer, 0, 0)),
        ],
        out_specs=[pl.BlockSpec((tb, w), lambda i: (i, 0)) for w in widths],
        out_shape=[jax.ShapeDtypeStruct((n, w), F32) for w in widths],
        compiler_params=pltpu.CompilerParams(vmem_limit_bytes=VMEM_LIMIT),
        name="lat_proj",
    )(x2d, mods, ln1, w_in, gate_b)


def _na_row_start(r):
    rows = DEC_SEQ // GRID_W
    return min(max(r - NA_WIN_ROWS // 2, 0), rows - NA_WIN_ROWS)


def _lat_local_kernel(zc_ref, zn_ref, kc_ref, vc_ref, band_ref, convw_ref, o_ref):
    o_ref[:, 0:CONV_DIM] = _short_conv(zc_ref[...], convw_ref[...])
    rows = DEC_SEQ // GRID_W
    blk = NA_WIN_ROWS * GRID_W
    q_all = (zn_ref[:, 0:NA_DIM] * NA_HEAD_DIM ** -0.5).astype(BF16)
    k_all = zn_ref[:, NA_DIM:2 * NA_DIM].astype(BF16)
    v_all = zn_ref[:, 2 * NA_DIM:3 * NA_DIM].astype(BF16)
    kc_all = kc_ref[...].astype(BF16)
    vc_all = vc_ref[...].astype(BF16)
    heads = []
    for h in range(NA_HEADS):
        sl = slice(h * NA_HEAD_DIM, (h + 1) * NA_HEAD_DIM)
        q, k, v, kc, vc = q_all[:, sl], k_all[:, sl], v_all[:, sl], kc_all[:, sl], vc_all[:, sl]
        band = band_ref[h]
        outs = []
        for r in range(rows):
            start = _na_row_start(r)
            off = (start - r + NA_WIN_ROWS - 1) * GRID_W
            q_r = q[r * GRID_W:(r + 1) * GRID_W]
            s_loc = _dot_nt(q_r, k[start * GRID_W:start * GRID_W + blk]) + band[:, off:off + blk]
            s_ctx = _dot_nt(q_r, kc)
            m = jnp.maximum(jnp.max(s_loc, axis=-1, keepdims=True), jnp.max(s_ctx, axis=-1, keepdims=True))
            p_loc = jnp.exp(s_loc - m)
            p_ctx = jnp.exp(s_ctx - m)
            den = jnp.sum(p_loc, axis=-1, keepdims=True) + jnp.sum(p_ctx, axis=-1, keepdims=True)
            o = _dot(p_loc.astype(BF16), v[start * GRID_W:start * GRID_W + blk]) + _dot(p_ctx.astype(BF16), vc)
            outs.append(o / den)
        heads.append(jnp.concatenate(outs, axis=0))
    o_ref[:, CONV_DIM:CONV_DIM + NA_DIM] = jnp.concatenate(heads, axis=-1)


def _lat_local_call(layer, zc, zn, cache_k, cache_v, band, conv_w):
    nb = zc.shape[0] // DEC_SEQ
    T = DEC_SEQ
    return pl.pallas_call(
        _lat_local_kernel,
        grid=(nb,),
        in_specs=[
            pl.BlockSpec((T, 3 * CONV_DIM), lambda b: (b, 0)),
            pl.BlockSpec((T, 3 * NA_DIM), lambda b: (b, 0)),
            pl.BlockSpec((None, None, PAST_LEN, NA_DIM), lambda b: (b, layer, 0, 0)),
            pl.BlockSpec((None, None, PAST_LEN, NA_DIM), lambda b: (b, layer, 0, 0)),
            pl.BlockSpec((None, NA_HEADS, GRID_W, band.shape[-1]), lambda b: (layer, 0, 0, 0)),
            pl.BlockSpec((None, 3, CONV_DIM), lambda b: (layer, 0, 0)),
        ],
        out_specs=pl.BlockSpec((T, CONV_DIM + NA_DIM), lambda b: (b, 0)),
        out_shape=jax.ShapeDtypeStruct((zc.shape[0], CONV_DIM + NA_DIM), F32),
        compiler_params=pltpu.CompilerParams(vmem_limit_bytes=VMEM_LIMIT),
        name="lat_conv_na",
    )(zc, zn, cache_k, cache_v, band, conv_w)


def _rope(x, cos, sin_signed):
    w = x.shape[-1]
    half = NA_HEAD_DIM // 2
    lane = lax.broadcasted_iota(jnp.int32, x.shape, 1)
    partner = jnp.where(lane % (2 * half) < half, pltpu.roll(x, w - half, axis=1), pltpu.roll(x, half, axis=1))
    return x * cos + partner * sin_signed


def _lat_mlstm_kernel(q_ref, k_ref, v_ref, o_ref, zg_ref, c0_ref, n0_ref, m0_ref, cos_ref, sin_ref,
                      mlg_ref, out_ref, bcol_scr, rowt_scr):
    hh = pl.program_id(1)

    @pl.when(hh == 0)
    def _():
        b_all, r_all = _gate_terms(zg_ref[:, 0:LANES], zg_ref[:, LANES:2 * LANES])
        bcol_scr[...] = b_all
        rowt_scr[...] = r_all

    b_col = bcol_scr[...]
    cos, sin_signed = cos_ref[...], sin_ref[...]
    q = _rope(q_ref[...], cos, sin_signed)
    k = _rope(k_ref[...], cos, sin_signed) * ML_HEAD_DIM ** -0.5
    v = v_ref[...]
    lane = lax.broadcasted_iota(jnp.int32, (DEC_SEQ, LANES), 1)
    hsum = None
    for d in range(N_DIRS):
        j = d * ML_HEADS + hh
        bc = jnp.sum(jnp.where(lane == j, b_col, 0.0), axis=-1, keepdims=True)
        lr = rowt_scr[pl.ds(j, 1), :]
        m0 = m0_ref[d, :, 0:1]
        ho = _mlstm_outputs(q, k, v, bc, lr, m0, c0_ref[d], n0_ref[d], d == 1, 256)
        hsum = ho if hsum is None else hsum + ho
    out_ref[...] = _head_readout(hsum, o_ref[...], mlg_ref[...])


def _lat_mlstm_call(layer, zm, zg, state_c, state_n, state_m, cos, sin_signed, ml_g):
    nb = zm.shape[0] // DEC_SEQ
    T = DEC_SEQ
    hd = ML_HEAD_DIM

    def col(part):
        return pl.BlockSpec((T, hd), lambda b, h: (b, part * ML_HEADS + h))

    return pl.pallas_call(
        _lat_mlstm_kernel,
        grid=(nb, ML_HEADS),
        in_specs=[
            col(0), col(1), col(2), col(3),
            pl.BlockSpec((T, 2 * LANES), lambda b, h: (b, 0)),
            pl.BlockSpec((None, None, N_DIRS, None, hd, hd), lambda b, h: (b, layer, 0, h, 0, 0)),
            pl.BlockSpec((None, None, N_DIRS, None, 1, hd), lambda b, h: (b, layer, 0, h, 0, 0)),
            pl.BlockSpec((None, None, N_DIRS, None, 1, LANES), lambda b, h: (b, layer, 0, h, 0, 0)),
            pl.BlockSpec((T, hd), lambda b, h: (0, 0)),
            pl.BlockSpec((T, hd), lambda b, h: (0, 0)),
            pl.BlockSpec((None, 1, hd), lambda b, h: (layer, 0, h)),
        ],
        out_specs=pl.BlockSpec((T, hd), lambda b, h: (b, h)),
        out_shape=jax.ShapeDtypeStruct((zm.shape[0], ML_DIM), F32),
        scratch_shapes=[pltpu.VMEM((T, LANES), F32), pltpu.VMEM((LANES, T), F32)],
        compiler_params=pltpu.CompilerParams(vmem_limit_bytes=VMEM_LIMIT),
        name="lat_mlstm",
    )(zm, zm, zm, zm, zg, state_c, state_n, state_m, cos, sin_signed, ml_g)


def _lat_merge_kernel(x_ref, loc_ref, ml_ref, mods_ref, wout_ref, o_ref):
    row = 1 + pl.program_id(0) // (DEC_SEQ // LAT_PROJ_ROWS)
    g1 = _mod_rows(mods_ref, row, 0)[2]
    split = CONV_DIM + NA_DIM
    y = _dot(loc_ref[...].astype(BF16), wout_ref[0:split, :]) + _dot(ml_ref[...].astype(BF16), wout_ref[split:, :])
    o_ref[...] = x_ref[...] + g1 * y


def _lat_merge_call(layer, x2d, loc, ml, mods, w_out):
    n = x2d.shape[0]
    tb = LAT_PROJ_ROWS
    return pl.pallas_call(
        _lat_merge_kernel,
        grid=(n // tb,),
        in_specs=[
            pl.BlockSpec((tb, D_MODEL), lambda i: (i, 0)),
            pl.BlockSpec((tb, CONV_DIM + NA_DIM), lambda i: (i, 0)),
            pl.BlockSpec((tb, ML_DIM), lambda i: (i, 0)),
            pl.BlockSpec((None, MODS_ROWS, N_MOD * D_MODEL), lambda i: (layer, 0, 0)),
            pl.BlockSpec((None, D_MODEL, D_MODEL), lambda i: (layer, 0, 0)),
        ],
        out_specs=pl.BlockSpec((tb, D_MODEL), lambda i: (i, 0)),
        out_shape=jax.ShapeDtypeStruct(x2d.shape, F32),
        compiler_params=pltpu.CompilerParams(vmem_limit_bytes=VMEM_LIMIT),
        name="lat_merge",
    )(x2d, loc, ml, mods, w_out)


def _pad_lanes(a, width):
    return jnp.pad(a, [(0, 0)] * (a.ndim - 1) + [(0, width - a.shape[-1])])


def _pack_w_in(w_in):
    main = w_in[..., :C_IG]
    gates = w_in[..., C_IG:]
    packed = jnp.concatenate([main, _pad_lanes(gates[..., :N_GATE], LANES),
                              _pad_lanes(gates[..., N_GATE:], LANES)], axis=-1)
    return packed.astype(BF16)


def _pack_gate_bias(ml_gate_b):
    gb = ml_gate_b.reshape(DEPTH, 2, N_GATE).astype(F32)
    return jnp.concatenate([_pad_lanes(gb[:, 0], LANES), _pad_lanes(gb[:, 1], LANES)], axis=-1)[:, None, :]


def _rpb_band(na_rpb):
    cols = np.arange(GRID_W)
    col_idx = np.clip(cols[None, :] - cols[:, None] + NA_WIN_COLS - 1, 0, 2 * NA_WIN_COLS - 2)
    col_start = np.clip(cols - NA_WIN_COLS // 2, 0, GRID_W - NA_WIN_COLS)
    col_mask = (cols[None, :] >= col_start[:, None]) & (cols[None, :] < col_start[:, None] + NA_WIN_COLS)
    t = na_rpb.astype(F32)[:, :, :, col_idx]
    t = jnp.where(col_mask[None, None, None], t, NEG_INF)
    t = jnp.transpose(t, (0, 1, 3, 2, 4))
    t = t.reshape(DEPTH, NA_HEADS, GRID_W, (2 * NA_WIN_ROWS - 1) * GRID_W)
    return _pad_lanes(t, 2 * NA_WIN_ROWS * GRID_W)


def _rope_tables():
    t = np.arange(DEC_SEQ)
    pos = np.stack([t // GRID_W, t % GRID_W], axis=-1).astype(np.float32)
    nf = ML_HEAD_DIM // 4
    inv = jnp.asarray(ROPE_BASE, F32) ** (-jnp.arange(nf, dtype=F32) / nf)
    ang = jnp.asarray(pos)[:, :, None] * inv
    cos = jnp.cos(ang)
    sin = jnp.sin(ang)
    cos_t = jnp.concatenate([cos, cos], axis=-1).reshape(DEC_SEQ, ML_HEAD_DIM)
    sin_t = jnp.concatenate([-sin, sin], axis=-1).reshape(DEC_SEQ, ML_HEAD_DIM)
    return cos_t, sin_t


def kernel(x_prompt, x_sample, cache_k, cache_v, state_C, state_n, state_m, c, c_ctx, w_mod, b_mod,
           ln1_g, w_in, conv_w, na_rpb, ml_gate_b, ml_norm_g, w_out, ln2_g, w_rg, b_rg, w_re, b_re,
           w_gate, w_up, w_down, final_g):
    nb_ctx = x_prompt.shape[0]
    nb_lat = x_sample.shape[0]
    assert 1 + nb_lat <= MODS_ROWS

    cvecs = jnp.concatenate([c_ctx[None, :], c,
                             jnp.zeros((MODS_ROWS - 1 - nb_lat, D_MODEL), F32)], axis=0)
    mods = _mods_call(cvecs, w_mod, b_mod)

    w_in_p = _pack_w_in(w_in)
    w_out_b = w_out.astype(BF16)
    gate_b = _pack_gate_bias(ml_gate_b)
    ln1 = ln1_g.reshape(DEPTH, 1, D_MODEL)
    ln2 = ln2_g.reshape(DEPTH, 1, D_MODEL)
    ml_g = ml_norm_g.reshape(DEPTH, 1, ML_DIM)
    w_router = _pad_lanes(jnp.concatenate([w_re, w_rg], axis=-1), LANES)
    b_router = _pad_lanes(jnp.concatenate([b_re, b_rg], axis=-1), LANES)[:, None, :]
    fg = final_g.reshape(1, D_MODEL)
    band = _rpb_band(na_rpb)
    cos_t, sin_t = _rope_tables()
    ck = cache_k.reshape(nb_lat, DEPTH, PAST_LEN, NA_DIM)
    cv = cache_v.reshape(nb_lat, DEPTH, PAST_LEN, NA_DIM)
    st_n = state_n.reshape(nb_lat, DEPTH, N_DIRS, ML_HEADS, 1, ML_HEAD_DIM)
    st_m = jnp.broadcast_to(state_m[..., None, None], (nb_lat, DEPTH, N_DIRS, ML_HEADS, 1, LANES))

    xp = x_prompt
    xs = x_sample.reshape(nb_lat * DEC_SEQ, D_MODEL)
    states = None
    for l in range(DEPTH):
        last = l == DEPTH - 1
        xp, states = _ctx_mixer_call(l, xp, mods, ln1, w_in_p, conv_w, gate_b, ml_g, w_out_b, states)
        xp = _sparse_moe(l, xp.reshape(nb_ctx * SEQ, D_MODEL), mods, ln2, w_router, b_router,
                         w_gate, w_up, w_down, fg, chunks_per_row=None, final=last)
        xp = xp.reshape(nb_ctx, SEQ, D_MODEL)

        zc, zn, zm, zg = _lat_proj_call(l, xs, mods, ln1, w_in_p, gate_b)
        loc = _lat_local_call(l, zc, zn, ck, cv, band, conv_w)
        ml = _lat_mlstm_call(l, zm, zg, state_C, st_n, st_m, cos_t, sin_t, ml_g)
        xs = _lat_merge_call(l, xs, loc, ml, mods, w_out_b)
        xs = _sparse_moe(l, xs, mods, ln2, w_router, b_router, w_gate, w_up, w_down, fg,
                         chunks_per_row=DEC_SEQ // DSP_CH, final=last)

    new_k, new_v, new_c, new_n, new_m = states
    return (xp, xs.reshape(nb_lat, DEC_SEQ, D_MODEL),
            new_k.reshape(nb_ctx, DEPTH, SEQ, NA_HEADS, NA_HEAD_DIM),
            new_v.reshape(nb_ctx, DEPTH, SEQ, NA_HEADS, NA_HEAD_DIM),
            new_c,
            new_n.reshape(nb_ctx, DEPTH, N_DIRS, ML_HEADS, ML_HEAD_DIM),
            new_m[..., 0].reshape(nb_ctx, DEPTH, N_DIRS, ML_HEADS))
```

```python
import functools

import numpy as np
import jax
import jax.numpy as jnp
from jax import lax
from jax.experimental import pallas as pl
from jax.experimental.pallas import tpu as pltpu

F32 = jnp.float32
BF16 = jnp.bfloat16

D_MODEL = 1024
SEQ = 256
DEPTH = 4
DEC_SEQ = 1024
PAST_LEN = 512
GRID_W = 64
CONV_DIM = 256
NA_HEADS = 4
NA_HEAD_DIM = 64
NA_DIM = NA_HEADS * NA_HEAD_DIM
NA_WIN_ROWS = 8
NA_WIN_COLS = 16
ML_HEADS = 4
ML_HEAD_DIM = 128
ML_DIM = ML_HEADS * ML_HEAD_DIM
N_DIRS = 2
N_GROUPS = 4
EXPERTS_PER_GROUP = 4
N_EXPERTS = N_GROUPS * EXPERTS_PER_GROUP
EXPERT_FF = 512
ROPE_BASE = 10000.0
EPS = 1e-6
N_MOD = 6

LANES = 128
N_GATE = N_DIRS * ML_HEADS
C_CONV = 0
C_NA = 3 * CONV_DIM
C_ML = C_NA + 3 * NA_DIM
C_IG = C_ML + 4 * ML_DIM
C_FG = C_IG + LANES
W_COLS = C_FG + LANES
CUMSUM_BLOCK = 256
MODS_ROWS = 8
VMEM_LIMIT = 56 * 1024 * 1024

NEG_INF = float("-inf")
HIGHEST = lax.Precision.HIGHEST


def _dot(a, b):
    return jnp.dot(a, b, preferred_element_type=F32)


def _dot_nt(a, b):
    return lax.dot_general(a, b, (((1,), (1,)), ((), ())), preferred_element_type=F32)


def _rms_mod(x, g, sc, sh):
    y = x * lax.rsqrt(jnp.mean(x * x, axis=-1, keepdims=True) + EPS)
    return (y * g) * (1.0 + sc) + sh


def _mod_rows(mods_ref, row, first):
    return [mods_ref[pl.ds(row, 1), pl.ds((first + j) * D_MODEL, D_MODEL)] for j in range(3)]


def _mods_kernel(cv_ref, w_ref, b_ref, o_ref):
    cv = cv_ref[...]
    s = cv * jax.nn.sigmoid(cv)
    o_ref[...] = _dot(s.astype(BF16), w_ref[...].astype(BF16)) + b_ref[...]


def _mods_call(cvecs, w_mod, b_mod):
    tn = 1536
    n = N_MOD * D_MODEL
    return pl.pallas_call(
        _mods_kernel,
        grid=(DEPTH, n // tn),
        in_specs=[pl.BlockSpec((MODS_ROWS, D_MODEL), lambda l, j: (0, 0)),
                  pl.BlockSpec((None, D_MODEL, tn), lambda l, j: (l, 0, j)),
                  pl.BlockSpec((None, 1, tn), lambda l, j: (l, 0, j))],
        out_specs=pl.BlockSpec((None, MODS_ROWS, tn), lambda l, j: (l, 0, j)),
        out_shape=jax.ShapeDtypeStruct((DEPTH, MODS_ROWS, n), F32),
        compiler_params=pltpu.CompilerParams(vmem_limit_bytes=VMEM_LIMIT),
        name="adaln_mods",
    )(cvecs, w_mod, b_mod.reshape(DEPTH, 1, n))


def _short_conv(zc, cw):
    T = zc.shape[0]
    cb = zc[:, 0:CONV_DIM]
    u = zc[:, CONV_DIM:2 * CONV_DIM] * zc[:, 2 * CONV_DIM:3 * CONV_DIM]
    t = lax.broadcasted_iota(jnp.int32, u.shape, 0)
    u_prev = jnp.where(t == 0, 0.0, pltpu.roll(u, 1, axis=0))
    u_next = jnp.where(t == T - 1, 0.0, pltpu.roll(u, T - 1, axis=0))
    return cb * (cw[0:1, :] * u_prev + cw[1:2, :] * u + cw[2:3, :] * u_next)


def _softmax_attention(nq, nk, nv):
    outs = []
    for h in range(NA_HEADS):
        sl = slice(h * NA_HEAD_DIM, (h + 1) * NA_HEAD_DIM)
        q = (nq[:, sl] * NA_HEAD_DIM ** -0.5).astype(BF16)
        s = _dot_nt(q, nk[:, sl].astype(BF16))
        p = jnp.exp(s - jnp.max(s, axis=-1, keepdims=True))
        o = _dot(p.astype(BF16), nv[:, sl].astype(BF16)) / jnp.sum(p, axis=-1, keepdims=True)
        outs.append(o)
    return jnp.concatenate(outs, axis=-1)


def _log_sigmoid(x):
    return jnp.minimum(x, 0.0) - jnp.log(1.0 + jnp.exp(-jnp.abs(x)))


def _gate_terms(zi, zf):
    T = zi.shape[0]
    lf = _log_sigmoid(zf)
    blk = min(T, CUMSUM_BLOCK)
    r = lax.broadcasted_iota(jnp.int32, (blk, blk), 0)
    c = lax.broadcasted_iota(jnp.int32, (blk, blk), 1)
    tril = jnp.where(c <= r, 1.0, 0.0).astype(F32)
    parts, carry = [], None
    for r0 in range(0, T, blk):
        b = jnp.dot(tril, lf[r0:r0 + blk], precision=HIGHEST, preferred_element_type=F32)
        if carry is not None:
            b = b + carry
        carry = b[blk - 1:blk, :]
        parts.append(b)
    b_fwd = parts[0] if len(parts) == 1 else jnp.concatenate(parts, axis=0)
    b_bwd = (b_fwd[T - 1:T, :] - b_fwd) + lf
    lane = lax.broadcasted_iota(jnp.int32, zi.shape, 1)
    b_col = jnp.where(lane < ML_HEADS, b_fwd, b_bwd)
    row_t = jnp.transpose(zi - b_col)
    return b_col, row_t


def _mlstm_outputs(q, k, v, b_col, li_row, m0, c0, n0, backward, q_block):
    T = q.shape[0]
    qb, kb, vb = q.astype(BF16), k.astype(BF16), v.astype(BF16)
    c0b = None if c0 is None else c0.astype(BF16)
    outs = []
    for r0 in range(0, T, q_block):
        ks, ke = (r0, T) if backward else (0, r0 + q_block)
        bq = b_col[r0:r0 + q_block]
        d = bq + li_row[:, ks:ke]
        t_idx = r0 + lax.broadcasted_iota(jnp.int32, d.shape, 0)
        s_idx = ks + lax.broadcasted_iota(jnp.int32, d.shape, 1)
        d = jnp.where((s_idx >= t_idx) if backward else (s_idx <= t_idx), d, NEG_INF)
        inter = bq + m0
        m_t = jnp.maximum(jnp.max(d, axis=-1, keepdims=True), inter)
        w = jnp.exp(d - m_t)
        s = _dot_nt(qb[r0:r0 + q_block], kb[ks:ke]) * w
        num = _dot(s.astype(BF16), vb[ks:ke])
        den = jnp.sum(s, axis=-1, keepdims=True)
        if c0 is not None:
            a = jnp.exp(inter - m_t)
            num = num + a * _dot(qb[r0:r0 + q_block], c0b)
            den = den + a * jnp.sum(q[r0:r0 + q_block] * n0, axis=-1, keepdims=True)
        outs.append(num / jnp.maximum(jnp.abs(den), jnp.exp(-m_t)))
    return outs[0] if len(outs) == 1 else jnp.concatenate(outs, axis=0)


def _mlstm_state(k, v, b_col, li_col, backward):
    T = k.shape[0]
    b_tot = b_col[0:1] if backward else b_col[T - 1:T]
    g = (b_tot - b_col) + li_col
    m_new = jnp.maximum(b_tot, jnp.max(g, axis=0, keepdims=True))
    kw = k * jnp.exp(g - m_new)
    c_new = _dot(jnp.transpose(kw).astype(BF16), v.astype(BF16))
    n_new = jnp.sum(kw, axis=0, keepdims=True)
    return c_new, n_new, m_new


def _head_readout(h, mo, g):
    hn = h * lax.rsqrt(jnp.mean(h * h, axis=-1, keepdims=True) + EPS)
    return hn * g * jax.nn.sigmoid(mo)


CTX_G = 1


def _ctx_mixer_kernel(x_ref, mods_ref, ln_ref, win_ref, convw_ref, gb_ref, mlg_ref, wout_ref, *rest):
    xo_ref, ko_ref, vo_ref, co_ref, no_ref, mo_ref = rest[-6:]
    T = SEQ
    sh1, sc1, g1 = _mod_rows(mods_ref, 0, 0)
    x = x_ref[...].reshape(CTX_G * T, D_MODEL)
    h = _rms_mod(x, ln_ref[...], sc1, sh1).astype(BF16)
    zc_all = _dot(h, win_ref[:, C_CONV:C_NA])
    zn_all = _dot(h, win_ref[:, C_NA:C_ML])
    zg_all = _dot(h, win_ref[:, C_IG:W_COLS])
    zm_all = _dot(h, win_ref[:, C_ML:C_IG])
    m0 = jnp.zeros((1, 1), F32)

    mixes = []
    for g in range(CTX_G):
        rows = slice(g * T, (g + 1) * T)
        conv_o = _short_conv(zc_all[rows], convw_ref[...])

        zn = zn_all[rows]
        nk, nv = zn[:, NA_DIM:2 * NA_DIM], zn[:, 2 * NA_DIM:3 * NA_DIM]
        ko_ref[g] = nk
        vo_ref[g] = nv
        na_o = _softmax_attention(zn[:, 0:NA_DIM], nk, nv)

        zg = zg_all[rows]
        zi = zg[:, 0:LANES] + gb_ref[:, 0:LANES]
        zf = zg[:, LANES:2 * LANES] + gb_ref[:, LANES:2 * LANES]
        b_col, row_t = _gate_terms(zi, zf)

        zm = zm_all[rows]
        ml_parts = []
        for hh in range(ML_HEADS):
            zq, zk, zv, zo = (zm[:, p * ML_DIM + hh * ML_HEAD_DIM:p * ML_DIM + (hh + 1) * ML_HEAD_DIM]
                              for p in range(4))
            zk = zk * ML_HEAD_DIM ** -0.5
            hsum = None
            for d in range(N_DIRS):
                j = d * ML_HEADS + hh
                bc = b_col[:, j:j + 1]
                ho = _mlstm_outputs(zq, zk, zv, bc, row_t[j:j + 1, :], m0, None, None, d == 1, T)
                hsum = ho if hsum is None else hsum + ho
                c_new, n_new, m_new = _mlstm_state(zk, zv, bc, zi[:, j:j + 1], d == 1)
                co_ref[g, d, hh] = c_new
                no_ref[g, j:j + 1, :] = n_new
                mo_ref[g, j:j + 1, :] = jnp.broadcast_to(m_new, (1, LANES))
            ml_parts.append(_head_readout(hsum, zo, mlg_ref[:, hh * ML_HEAD_DIM:(hh + 1) * ML_HEAD_DIM]))
        mixes.append(jnp.concatenate([conv_o, na_o] + ml_parts, axis=-1).astype(BF16))

    y = x + g1 * _dot(jnp.concatenate(mixes, axis=0), wout_ref[...])
    xo_ref[...] = y.reshape(CTX_G, T, D_MODEL)


def _ctx_mixer_call(layer, x, mods, ln1, w_in, conv_w, gate_b, ml_g, w_out, prev):
    B = x.shape[0]
    T = SEQ
    G = CTX_G
    state_shapes = [
        jax.ShapeDtypeStruct((B, DEPTH, T, NA_DIM), F32),
        jax.ShapeDtypeStruct((B, DEPTH, T, NA_DIM), F32),
        jax.ShapeDtypeStruct((B, DEPTH, N_DIRS, ML_HEADS, ML_HEAD_DIM, ML_HEAD_DIM), F32),
        jax.ShapeDtypeStruct((B, DEPTH, N_GATE, ML_HEAD_DIM), F32),
        jax.ShapeDtypeStruct((B, DEPTH, N_GATE, LANES), F32),
    ]
    in_specs = [
        pl.BlockSpec((G, T, D_MODEL), lambda b: (b, 0, 0)),
        pl.BlockSpec((None, MODS_ROWS, N_MOD * D_MODEL), lambda b: (layer, 0, 0)),
        pl.BlockSpec((None, 1, D_MODEL), lambda b: (layer, 0, 0)),
        pl.BlockSpec((None, D_MODEL, W_COLS), lambda b: (layer, 0, 0)),
        pl.BlockSpec((None, 3, CONV_DIM), lambda b: (layer, 0, 0)),
        pl.BlockSpec((None, 1, 2 * LANES), lambda b: (layer, 0, 0)),
        pl.BlockSpec((None, 1, ML_DIM), lambda b: (layer, 0, 0)),
        pl.BlockSpec((None, D_MODEL, D_MODEL), lambda b: (layer, 0, 0)),
    ]
    args = [x, mods, ln1, w_in, conv_w, gate_b, ml_g, w_out]
    aliases = {}
    if prev is not None:
        in_specs += [pl.BlockSpec(memory_space=pl.ANY)] * len(prev)
        aliases = {len(args) + i: 1 + i for i in range(len(prev))}
        args += list(prev)
    out_specs = [
        pl.BlockSpec((G, T, D_MODEL), lambda b: (b, 0, 0)),
        pl.BlockSpec((G, None, T, NA_DIM), lambda b: (b, layer, 0, 0)),
        pl.BlockSpec((G, None, T, NA_DIM), lambda b: (b, layer, 0, 0)),
        pl.BlockSpec((G, None, N_DIRS, ML_HEADS, ML_HEAD_DIM, ML_HEAD_DIM),
                     lambda b: (b, layer, 0, 0, 0, 0)),
        pl.BlockSpec((G, None, N_GATE, ML_HEAD_DIM), lambda b: (b, layer, 0, 0)),
        pl.BlockSpec((G, None, N_GATE, LANES), lambda b: (b, layer, 0, 0)),
    ]
    outs = pl.pallas_call(
        _ctx_mixer_kernel,
        grid=(B // G,),
        in_specs=in_specs,
        out_specs=out_specs,
        out_shape=[jax.ShapeDtypeStruct(x.shape, F32)] + state_shapes,
        input_output_aliases=aliases,
        compiler_params=pltpu.CompilerParams(vmem_limit_bytes=VMEM_LIMIT),
        name="ctx_mixer",
    )(*args)
    return outs[0], tuple(outs[1:])


def _route(lg):
    lane = lax.broadcasted_iota(jnp.int32, lg.shape, 1)
    lane_f = lane.astype(F32)
    big = float(LANES)
    is_g = (lane >= N_EXPERTS) & (lane < N_EXPERTS + N_GROUPS)
    glm = jnp.where(is_g, lg, NEG_INF)
    gmax = jnp.max(glm, axis=-1, keepdims=True)
    g_top = jnp.min(jnp.where(glm == gmax, lane_f, big), axis=-1, keepdims=True) - N_EXPERTS
    gp = 1.0 / jnp.sum(jnp.where(is_g, jnp.exp(lg - gmax), 0.0), axis=-1, keepdims=True)
    grp = (lane >> (EXPERTS_PER_GROUP.bit_length() - 1)).astype(F32)
    in_grp = (lane < N_EXPERTS) & (grp == g_top)
    e1 = jnp.where(in_grp, lg, NEG_INF)
    v1 = jnp.max(e1, axis=-1, keepdims=True)
    i1 = jnp.min(jnp.where(e1 == v1, lane_f, big), axis=-1, keepdims=True)
    e2 = jnp.where(in_grp & (lane_f != i1), lg, NEG_INF)
    v2 = jnp.max(e2, axis=-1, keepdims=True)
    i2 = jnp.min(jnp.where(e2 == v2, lane_f, big), axis=-1, keepdims=True)
    t = jnp.exp(v2 - v1)
    w1 = 1.0 / (1.0 + t)
    w2 = t / (1.0 + t)
    return i1, i2, gp * w1, gp * w2


SLAB = D_MODEL // LANES
DSP_SB = 2048
DSP_TM = 128
DSP_CH = 512
DSP_PMAX = 2 * DSP_SB + N_EXPERTS * DSP_TM
DSP_UNROLL = 8


def _to_slabs(ref, x):
    n = x.shape[0]
    for s in range(SLAB):
        ref[pl.ds(s, n, stride=SLAB), :] = x[:, s * LANES:(s + 1) * LANES]


def _from_slabs(ref, n):
    return jnp.concatenate([ref[pl.ds(s, n, stride=SLAB), :] for s in range(SLAB)], axis=-1)


def _dsp_mods(mods_ref, chunk, chunks_per_row, first):
    row = 0 if chunks_per_row is None else 1 + chunk // chunks_per_row
    return _mod_rows(mods_ref, row, first)


def _dsp_route_kernel(x_ref, mods_ref, ln_ref, wr_ref, br_ref, h_ref, r_ref, cnt_ref, run_scr, *,
                      chunks_per_row):
    c = pl.program_id(0)
    ch = DSP_CH

    @pl.when(c % (DSP_SB // ch) == 0)
    def _():
        run_scr[...] = jnp.zeros_like(run_scr)

    sh2, sc2, _ = _dsp_mods(mods_ref, c, chunks_per_row, 3)
    h2 = _rms_mod(x_ref[...], ln_ref[...], sc2, sh2)
    _to_slabs(h_ref, h2)
    lg = jnp.dot(h2, wr_ref[...], precision=HIGHEST, preferred_element_type=F32) + br_ref[...]
    i1, i2, w1, w2 = _route(lg)
    lane = lax.broadcasted_iota(jnp.int32, lg.shape, 1)
    lane_f = lane.astype(F32)
    oh1 = jnp.where(lane_f == i1, 1.0, 0.0)
    oh2 = jnp.where(lane_f == i2, 1.0, 0.0)
    both = oh1 + oh2
    r = lax.broadcasted_iota(jnp.int32, (ch, ch), 0)
    s = lax.broadcasted_iota(jnp.int32, (ch, ch), 1)
    earlier = jnp.where(s < r, 1.0, 0.0).astype(BF16)
    run = run_scr[0:1, :]
    before = _dot(earlier, both.astype(BF16)) + run
    rank1 = jnp.sum(before * oh1, axis=-1, keepdims=True)
    rank2 = jnp.sum(before * oh2, axis=-1, keepdims=True)
    run = run + jnp.sum(both, axis=0, keepdims=True)
    run_scr[0:1, :] = run
    cnt_ref[...] = jnp.broadcast_to(run, cnt_ref.shape)
    cols = (i1, i2, w1, w2, rank1, rank2)
    out = jnp.zeros(lg.shape, F32)
    for j, col in enumerate(cols):
        out = jnp.where(lane == j, col, out)
    r_ref[...] = out


def _dsp_route_call(layer, x2d, mods, ln2, w_router, b_router, chunks_per_row):
    n = x2d.shape[0]
    ch = DSP_CH
    per_sb = DSP_SB // ch
    return pl.pallas_call(
        functools.partial(_dsp_route_kernel, chunks_per_row=chunks_per_row),
        grid=(n // ch,),
        in_specs=[
            pl.BlockSpec((ch, D_MODEL), lambda c: (c, 0)),
            pl.BlockSpec((None, MODS_ROWS, N_MOD * D_MODEL), lambda c: (layer, 0, 0)),
            pl.BlockSpec((None, 1, D_MODEL), lambda c: (layer, 0, 0)),
            pl.BlockSpec((None, D_MODEL, LANES), lambda c: (layer, 0, 0)),
            pl.BlockSpec((None, 1, LANES), lambda c: (layer, 0, 0)),
        ],
        out_specs=[pl.BlockSpec((ch * SLAB, LANES), lambda c: (c, 0)),
                   pl.BlockSpec((ch, LANES), lambda c: (c, 0)),
                   pl.BlockSpec((None, SLAB, LANES), lambda c: (c // per_sb, 0, 0))],
        out_shape=[jax.ShapeDtypeStruct((n * SLAB, LANES), F32),
                   jax.ShapeDtypeStruct((n, LANES), F32),
                   jax.ShapeDtypeStruct((n // DSP_SB, SLAB, LANES), F32)],
        scratch_shapes=[pltpu.VMEM((SLAB, LANES), F32)],
        compiler_params=pltpu.CompilerParams(vmem_limit_bytes=VMEM_LIMIT),
        name="moe_route",
    )(x2d, mods, ln2, w_router, b_router)


def _dispatch_tables(route, counts, n_sb):
    e = route[:, 0:2].astype(jnp.int32).reshape(n_sb, 2 * DSP_SB)
    rank = route[:, 4:6].astype(jnp.int32).reshape(n_sb, 2 * DSP_SB)
    w = route[:, 2:4].reshape(n_sb, 1, 2 * DSP_SB)
    cnt = counts[:, 0, :N_EXPERTS].astype(jnp.int32)
    ntile = (cnt + (DSP_TM - 1)) // DSP_TM
    off = (jnp.cumsum(ntile, axis=1) - ntile) * DSP_TM
    pos = (jnp.take_along_axis(off, e, axis=1) + rank).reshape(n_sb, 1, 2 * DSP_SB)
    return pos, w, ntile.reshape(-1), off.reshape(-1), cnt.reshape(-1)


def _dsp_expert_kernel(ntile_ref, off_ref, cnt_ref, pos_ref, w_ref, h_ref, wg_ref, wu_ref, wd_ref,
                       o_ref, row_smem, wgt_smem, xs_scr, ys_scr, wg_scr, wu_scr, wd_scr):
    sb = pl.program_id(0)
    e = pl.program_id(1)
    tm, u = DSP_TM, DSP_UNROLL
    slab_shift = SLAB.bit_length() - 1
    spare_row = DSP_SB * SLAB
    last_row = (DSP_SB - 1) * SLAB

    @pl.when(e == 0)
    def _():
        for ee in range(N_EXPERTS):
            q = sb * N_EXPERTS + ee

            def pad(p, carry):
                row_smem[p] = spare_row
                wgt_smem[p] = 0.0
                return carry

            lax.fori_loop(off_ref[q] + cnt_ref[q], off_ref[q] + ntile_ref[q] * tm, pad, 0)

        def invert(j, carry):
            for k in range(u):
                a = j * u + k
                p = pos_ref[0, a]
                row_smem[p] = lax.shift_left(lax.shift_right_logical(a, 1), slab_shift)
                wgt_smem[p] = w_ref[0, a]
            return carry

        lax.fori_loop(0, 2 * DSP_SB // u, invert, 0)
        o_ref[...] = jnp.zeros_like(o_ref)

    base0 = off_ref[sb * N_EXPERTS + e]
    wg_scr[...] = wg_ref[...].astype(BF16)
    wu_scr[...] = wu_ref[...].astype(BF16)
    wd_scr[...] = wd_ref[...].astype(BF16)

    def tile(i, carry):
        base = base0 + i * tm

        def gather(j, c2):
            for k in range(u):
                r = j * u + k
                src = jnp.minimum(row_smem[base + r], last_row)
                xs_scr[pl.ds(pl.multiple_of(r * SLAB, SLAB), SLAB), :] = (
                    h_ref[pl.ds(pl.multiple_of(src, SLAB), SLAB), :])
            return c2

        lax.fori_loop(0, tm // u, gather, 0)
        x = _from_slabs(xs_scr, tm).astype(BF16)
        a = jax.nn.silu(_dot(x, wg_scr[...])) * _dot(x, wu_scr[...])
        _to_slabs(ys_scr, _dot(a.astype(BF16), wd_scr[...]))

        def scatter(j, c2):
            upd = []
            for k in range(u):
                r = j * u + k
                dst = pl.multiple_of(row_smem[base + r], SLAB)
                y = ys_scr[pl.ds(pl.multiple_of(r * SLAB, SLAB), SLAB), :]
                upd.append((dst, o_ref[pl.ds(dst, SLAB), :] + wgt_smem[base + r] * y))
            for dst, v in upd:
                o_ref[pl.ds(dst, SLAB), :] = v
            return c2

        lax.fori_loop(0, tm // u, scatter, 0)
        return carry

    lax.fori_loop(0, ntile_ref[sb * N_EXPERTS + e], tile, 0)


def _dsp_expert_call(layer, h_slabs, pos, w, ntile, off, cnt, w_gate, w_up, w_down):
    n_sb = pos.shape[0]
    out_rows = (DSP_SB + DSP_CH) * SLAB

    def per_sb(s, e, *prefetch):
        return (s, 0, 0)

    def w_idx(s, e, *prefetch):
        return (layer, e, 0, 0)

    grid_spec = pltpu.PrefetchScalarGridSpec(
        num_scalar_prefetch=3,
        grid=(n_sb, N_EXPERTS),
        in_specs=[
            pl.BlockSpec((None, 1, 2 * DSP_SB), per_sb, memory_space=pltpu.SMEM),
            pl.BlockSpec((None, 1, 2 * DSP_SB), per_sb, memory_space=pltpu.SMEM),
            pl.BlockSpec((DSP_SB * SLAB, LANES), lambda s, e, *prefetch: (s, 0)),
            pl.BlockSpec((None, None, D_MODEL, EXPERT_FF), w_idx),
            pl.BlockSpec((None, None, D_MODEL, EXPERT_FF), w_idx),
            pl.BlockSpec((None, None, EXPERT_FF, D_MODEL), w_idx),
        ],
        out_specs=pl.BlockSpec((None, out_rows, LANES), per_sb),
        scratch_shapes=[pltpu.SMEM((DSP_PMAX,), jnp.int32),
                        pltpu.SMEM((DSP_PMAX,), F32),
                        pltpu.VMEM((DSP_TM * SLAB, LANES), F32),
                        pltpu.VMEM((DSP_TM * SLAB, LANES), F32),
                        pltpu.VMEM((D_MODEL, EXPERT_FF), BF16),
                        pltpu.VMEM((D_MODEL, EXPERT_FF), BF16),
                        pltpu.VMEM((EXPERT_FF, D_MODEL), BF16)],
    )
    return pl.pallas_call(
        _dsp_expert_kernel,
        grid_spec=grid_spec,
        out_shape=jax.ShapeDtypeStruct((n_sb, out_rows, LANES), F32),
        compiler_params=pltpu.CompilerParams(vmem_limit_bytes=VMEM_LIMIT),
        name="moe_experts",
    )(ntile, off, cnt, pos, w, h_slabs, w_gate, w_up, w_down)


def _dsp_combine_kernel(x_ref, m_ref, mods_ref, fg_ref, o_ref, *, chunks_per_row, final):
    g2 = _dsp_mods(mods_ref, pl.program_id(0), chunks_per_row, 3)[2]
    y = x_ref[...] + g2 * _from_slabs(m_ref, DSP_CH)
    if final:
        y = y * lax.rsqrt(jnp.mean(y * y, axis=-1, keepdims=True) + EPS) * fg_ref[...]
    o_ref[...] = y


def _dsp_combine_call(layer, x2d, m_slabs, mods, final_g, chunks_per_row, final):
    n = x2d.shape[0]
    ch = DSP_CH
    per_sb = DSP_SB // ch
    return pl.pallas_call(
        functools.partial(_dsp_combine_kernel, chunks_per_row=chunks_per_row, final=final),
        grid=(n // ch,),
        in_specs=[
            pl.BlockSpec((ch, D_MODEL), lambda c: (c, 0)),
            pl.BlockSpec((None, ch * SLAB, LANES), lambda c: (c // per_sb, c % per_sb, 0)),
            pl.BlockSpec((None, MODS_ROWS, N_MOD * D_MODEL), lambda c: (layer, 0, 0)),
            pl.BlockSpec((1, D_MODEL), lambda c: (0, 0)),
        ],
        out_specs=pl.BlockSpec((ch, D_MODEL), lambda c: (c, 0)),
        out_shape=jax.ShapeDtypeStruct(x2d.shape, F32),
        compiler_params=pltpu.CompilerParams(vmem_limit_bytes=VMEM_LIMIT),
        name="moe_combine",
    )(x2d, m_slabs, mods, final_g)


def _sparse_moe(layer, x2d, mods, ln2, w_router, b_router, w_gate, w_up, w_down, final_g, *,
                chunks_per_row, final):
    n_sb = x2d.shape[0] // DSP_SB
    h_slabs, route, counts = _dsp_route_call(layer, x2d, mods, ln2, w_router, b_router,
                                             chunks_per_row)
    pos, w, ntile, off, cnt = _dispatch_tables(route, counts, n_sb)
    m_slabs = _dsp_expert_call(layer, h_slabs, pos, w, ntile, off, cnt, w_gate, w_up, w_down)
    return _dsp_combine_call(layer, x2d, m_slabs, mods, final_g, chunks_per_row, final)


LAT_PROJ_ROWS = 512


def _lat_proj_kernel(x_ref, mods_ref, ln_ref, win_ref, gb_ref, zc_ref, zn_ref, zm_ref, zg_ref):
    row = 1 + pl.program_id(0) // (DEC_SEQ // LAT_PROJ_ROWS)
    sh1, sc1, _ = _mod_rows(mods_ref, row, 0)
    h = _rms_mod(x_ref[...], ln_ref[...], sc1, sh1).astype(BF16)
    zc_ref[...] = _dot(h, win_ref[:, C_CONV:C_NA])
    zn_ref[...] = _dot(h, win_ref[:, C_NA:C_ML])
    zm_ref[...] = _dot(h, win_ref[:, C_ML:C_IG])
    zg_ref[...] = _dot(h, win_ref[:, C_IG:W_COLS]) + gb_ref[...]


def _lat_proj_call(layer, x2d, mods, ln1, w_in, gate_b):
    n = x2d.shape[0]
    tb = LAT_PROJ_ROWS
    widths = (C_NA - C_CONV, C_ML - C_NA, C_IG - C_ML, W_COLS - C_IG)
    return pl.pallas_call(
        _lat_proj_kernel,
        grid=(n // tb,),
        in_specs=[
            pl.BlockSpec((tb, D_MODEL), lambda i: (i, 0)),
            pl.BlockSpec((None, MODS_ROWS, N_MOD * D_MODEL), lambda i: (layer, 0, 0)),
            pl.BlockSpec((None, 1, D_MODEL), lambda i: (layer, 0, 0)),
            pl.BlockSpec((None, D_MODEL, W_COLS), lambda i: (layer, 0, 0)),
            pl.BlockSpec((None, 1, 2 * LANES), lambda i: (layer, 0, 0)),
        ],
        out_specs=[pl.BlockSpec((tb, w), lambda i: (i, 0)) for w in widths],
        out_shape=[jax.ShapeDtypeStruct((n, w), F32) for w in widths],
        compiler_params=pltpu.CompilerParams(vmem_limit_bytes=VMEM_LIMIT),
        name="lat_proj",
    )(x2d, mods, ln1, w_in, gate_b)


def _na_row_start(r):
    rows = DEC_SEQ // GRID_W
    return min(max(r - NA_WIN_ROWS // 2, 0), rows - NA_WIN_ROWS)


def _lat_local_kernel(zc_ref, zn_ref, kc_ref, vc_ref, band_ref, convw_ref, o_ref):
    o_ref[:, 0:CONV_DIM] = _short_conv(zc_ref[...], convw_ref[...])
    rows = DEC_SEQ // GRID_W
    blk = NA_WIN_ROWS * GRID_W
    q_all = (zn_ref[:, 0:NA_DIM] * NA_HEAD_DIM ** -0.5).astype(BF16)
    k_all = zn_ref[:, NA_DIM:2 * NA_DIM].astype(BF16)
    v_all = zn_ref[:, 2 * NA_DIM:3 * NA_DIM].astype(BF16)
    kc_all = kc_ref[...].astype(BF16)
    vc_all = vc_ref[...].astype(BF16)
    heads = []
    for h in range(NA_HEADS):
        sl = slice(h * NA_HEAD_DIM, (h + 1) * NA_HEAD_DIM)
        q, k, v, kc, vc = q_all[:, sl], k_all[:, sl], v_all[:, sl], kc_all[:, sl], vc_all[:, sl]
        band = band_ref[h]
        outs = []
        for r in range(rows):
            start = _na_row_start(r)
            off = (start - r + NA_WIN_ROWS - 1) * GRID_W
            q_r = q[r * GRID_W:(r + 1) * GRID_W]
            s_loc = _dot_nt(q_r, k[start * GRID_W:start * GRID_W + blk]) + band[:, off:off + blk]
            s_ctx = _dot_nt(q_r, kc)
            m = jnp.maximum(jnp.max(s_loc, axis=-1, keepdims=True), jnp.max(s_ctx, axis=-1, keepdims=True))
            p_loc = jnp.exp(s_loc - m)
            p_ctx = jnp.exp(s_ctx - m)
            den = jnp.sum(p_loc, axis=-1, keepdims=True) + jnp.sum(p_ctx, axis=-1, keepdims=True)
            o = _dot(p_loc.astype(BF16), v[start * GRID_W:start * GRID_W + blk]) + _dot(p_ctx.astype(BF16), vc)
            outs.append(o / den)
        heads.append(jnp.concatenate(outs, axis=0))
    o_ref[:, CONV_DIM:CONV_DIM + NA_DIM] = jnp.concatenate(heads, axis=-1)


def _lat_local_call(layer, zc, zn, cache_k, cache_v, band, conv_w):
    nb = zc.shape[0] // DEC_SEQ
    T = DEC_SEQ
    return pl.pallas_call(
        _lat_local_kernel,
        grid=(nb,),
        in_specs=[
            pl.BlockSpec((T, 3 * CONV_DIM), lambda b: (b, 0)),
            pl.BlockSpec((T, 3 * NA_DIM), lambda b: (b, 0)),
            pl.BlockSpec((None, None, PAST_LEN, NA_DIM), lambda b: (b, layer, 0, 0)),
            pl.BlockSpec((None, None, PAST_LEN, NA_DIM), lambda b: (b, layer, 0, 0)),
            pl.BlockSpec((None, NA_HEADS, GRID_W, band.shape[-1]), lambda b: (layer, 0, 0, 0)),
            pl.BlockSpec((None, 3, CONV_DIM), lambda b: (layer, 0, 0)),
        ],
        out_specs=pl.BlockSpec((T, CONV_DIM + NA_DIM), lambda b: (b, 0)),
        out_shape=jax.ShapeDtypeStruct((zc.shape[0], CONV_DIM + NA_DIM), F32),
        compiler_params=pltpu.CompilerParams(vmem_limit_bytes=VMEM_LIMIT),
        name="lat_conv_na",
    )(zc, zn, cache_k, cache_v, band, conv_w)


def _rope(x, cos, sin_signed):
    w = x.shape[-1]
    half = NA_HEAD_DIM // 2
    lane = lax.broadcasted_iota(jnp.int32, x.shape, 1)
    partner = jnp.where(lane % (2 * half) < half, pltpu.roll(x, w - half, axis=1), pltpu.roll(x, half, axis=1))
    return x * cos + partner * sin_signed


def _lat_mlstm_kernel(q_ref, k_ref, v_ref, o_ref, zg_ref, c0_ref, n0_ref, m0_ref, cos_ref, sin_ref,
                      mlg_ref, out_ref, bcol_scr, rowt_scr):
    hh = pl.program_id(1)

    @pl.when(hh == 0)
    def _():
        b_all, r_all = _gate_terms(zg_ref[:, 0:LANES], zg_ref[:, LANES:2 * LANES])
        bcol_scr[...] = b_all
        rowt_scr[...] = r_all

    b_col = bcol_scr[...]
    cos, sin_signed = cos_ref[...], sin_ref[...]
    q = _rope(q_ref[...], cos, sin_signed)
    k = _rope(k_ref[...], cos, sin_signed) * ML_HEAD_DIM ** -0.5
    v = v_ref[...]
    lane = lax.broadcasted_iota(jnp.int32, (DEC_SEQ, LANES), 1)
    hsum = None
    for d in range(N_DIRS):
        j = d * ML_HEADS + hh
        bc = jnp.sum(jnp.where(lane == j, b_col, 0.0), axis=-1, keepdims=True)
        lr = rowt_scr[pl.ds(j, 1), :]
        m0 = m0_ref[d, :, 0:1]
        ho = _mlstm_outputs(q, k, v, bc, lr, m0, c0_ref[d], n0_ref[d], d == 1, 256)
        hsum = ho if hsum is None else hsum + ho
    out_ref[...] = _head_readout(hsum, o_ref[...], mlg_ref[...])


def _lat_mlstm_call(layer, zm, zg, state_c, state_n, state_m, cos, sin_signed, ml_g):
    nb = zm.shape[0] // DEC_SEQ
    T = DEC_SEQ
    hd = ML_HEAD_DIM

    def col(part):
        return pl.BlockSpec((T, hd), lambda b, h: (b, part * ML_HEADS + h))

    return pl.pallas_call(
        _lat_mlstm_kernel,
        grid=(nb, ML_HEADS),
        in_specs=[
            col(0), col(1), col(2), col(3),
            pl.BlockSpec((T, 2 * LANES), lambda b, h: (b, 0)),
            pl.BlockSpec((None, None, N_DIRS, None, hd, hd), lambda b, h: (b, layer, 0, h, 0, 0)),
            pl.BlockSpec((None, None, N_DIRS, None, 1, hd), lambda b, h: (b, layer, 0, h, 0, 0)),
            pl.BlockSpec((None, None, N_DIRS, None, 1, LANES), lambda b, h: (b, layer, 0, h, 0, 0)),
            pl.BlockSpec((T, hd), lambda b, h: (0, 0)),
            pl.BlockSpec((T, hd), lambda b, h: (0, 0)),
            pl.BlockSpec((None, 1, hd), lambda b, h: (layer, 0, h)),
        ],
        out_specs=pl.BlockSpec((T, hd), lambda b, h: (b, h)),
        out_shape=jax.ShapeDtypeStruct((zm.shape[0], ML_DIM), F32),
        scratch_shapes=[pltpu.VMEM((T, LANES), F32), pltpu.VMEM((LANES, T), F32)],
        compiler_params=pltpu.CompilerParams(vmem_limit_bytes=VMEM_LIMIT),
        name="lat_mlstm",
    )(zm, zm, zm, zm, zg, state_c, state_n, state_m, cos, sin_signed, ml_g)


def _lat_merge_kernel(x_ref, loc_ref, ml_ref, mods_ref, wout_ref, o_ref):
    row = 1 + pl.program_id(0) // (DEC_SEQ // LAT_PROJ_ROWS)
    g1 = _mod_rows(mods_ref, row, 0)[2]
    split = CONV_DIM + NA_DIM
    y = _dot(loc_ref[...].astype(BF16), wout_ref[0:split, :]) + _dot(ml_ref[...].astype(BF16), wout_ref[split:, :])
    o_ref[...] = x_ref[...] + g1 * y


def _lat_merge_call(layer, x2d, loc, ml, mods, w_out):
    n = x2d.shape[0]
    tb = LAT_PROJ_ROWS
    return pl.pallas_call(
        _lat_merge_kernel,
        grid=(n // tb,),
        in_specs=[
            pl.BlockSpec((tb, D_MODEL), lambda i: (i, 0)),
            pl.BlockSpec((tb, CONV_DIM + NA_DIM), lambda i: (i, 0)),
            pl.BlockSpec((tb, ML_DIM), lambda i: (i, 0)),
            pl.BlockSpec((None, MODS_ROWS, N_MOD * D_MODEL), lambda i: (layer, 0, 0)),
            pl.BlockSpec((None, D_MODEL, D_MODEL), lambda i: (layer, 0, 0)),
        ],
        out_specs=pl.BlockSpec((tb, D_MODEL), lambda i: (i, 0)),
        out_shape=jax.ShapeDtypeStruct(x2d.shape, F32),
        compiler_params=pltpu.CompilerParams(vmem_limit_bytes=VMEM_LIMIT),
        name="lat_merge",
    )(x2d, loc, ml, mods, w_out)


def _pad_lanes(a, width):
    return jnp.pad(a, [(0, 0)] * (a.ndim - 1) + [(0, width - a.shape[-1])])


def _pack_w_in(w_in):
    main = w_in[..., :C_IG]
    gates = w_in[..., C_IG:]
    packed = jnp.concatenate([main, _pad_lanes(gates[..., :N_GATE], LANES),
                              _pad_lanes(gates[..., N_GATE:], LANES)], axis=-1)
    return packed.astype(BF16)


def _pack_gate_bias(ml_gate_b):
    gb = ml_gate_b.reshape(DEPTH, 2, N_GATE).astype(F32)
    return jnp.concatenate([_pad_lanes(gb[:, 0], LANES), _pad_lanes(gb[:, 1], LANES)], axis=-1)[:, None, :]


def _rpb_band(na_rpb):
    cols = np.arange(GRID_W)
    col_idx = np.clip(cols[None, :] - cols[:, None] + NA_WIN_COLS - 1, 0, 2 * NA_WIN_COLS - 2)
    col_start = np.clip(cols - NA_WIN_COLS // 2, 0, GRID_W - NA_WIN_COLS)
    col_mask = (cols[None, :] >= col_start[:, None]) & (cols[None, :] < col_start[:, None] + NA_WIN_COLS)
    pick = (col_idx[None] == np.arange(2 * NA_WIN_COLS - 1)[:, None, None]).astype(np.float32)
    t = jnp.einsum('lhri,iqk->lhqrk', na_rpb.astype(F32), jnp.asarray(pick), precision=HIGHEST)
    t = jnp.where(col_mask[None, None, :, None, :], t, NEG_INF)
    t = t.reshape(DEPTH, NA_HEADS, GRID_W, (2 * NA_WIN_ROWS - 1) * GRID_W)
    return _pad_lanes(t, 2 * NA_WIN_ROWS * GRID_W)


def _rope_tables():
    t = np.arange(DEC_SEQ)
    pos = np.stack([t // GRID_W, t % GRID_W], axis=-1).astype(np.float32)
    nf = ML_HEAD_DIM // 4
    inv = jnp.asarray(ROPE_BASE, F32) ** (-jnp.arange(nf, dtype=F32) / nf)
    ang = jnp.asarray(pos)[:, :, None] * inv
    cos = jnp.cos(ang)
    sin = jnp.sin(ang)
    cos_t = jnp.concatenate([cos, cos], axis=-1).reshape(DEC_SEQ, ML_HEAD_DIM)
    sin_t = jnp.concatenate([-sin, sin], axis=-1).reshape(DEC_SEQ, ML_HEAD_DIM)
    return cos_t, sin_t


def kernel(x_prompt, x_sample, cache_k, cache_v, state_C, state_n, state_m, c, c_ctx, w_mod, b_mod,
           ln1_g, w_in, conv_w, na_rpb, ml_gate_b, ml_norm_g, w_out, ln2_g, w_rg, b_rg, w_re, b_re,
           w_gate, w_up, w_down, final_g):
    nb_ctx = x_prompt.shape[0]
    nb_lat = x_sample.shape[0]
    assert 1 + nb_lat <= MODS_ROWS

    cvecs = jnp.concatenate([c_ctx[None, :], c,
                             jnp.zeros((MODS_ROWS - 1 - nb_lat, D_MODEL), F32)], axis=0)
    mods = _mods_call(cvecs, w_mod, b_mod)

    w_in_p = _pack_w_in(w_in)
    w_out_b = w_out.astype(BF16)
    gate_b = _pack_gate_bias(ml_gate_b)
    ln1 = ln1_g.reshape(DEPTH, 1, D_MODEL)
    ln2 = ln2_g.reshape(DEPTH, 1, D_MODEL)
    ml_g = ml_norm_g.reshape(DEPTH, 1, ML_DIM)
    w_router = _pad_lanes(jnp.concatenate([w_re, w_rg], axis=-1), LANES)
    b_router = _pad_lanes(jnp.concatenate([b_re, b_rg], axis=-1), LANES)[:, None, :]
    fg = final_g.reshape(1, D_MODEL)
    band = _rpb_band(na_rpb)
    cos_t, sin_t = _rope_tables()
    ck = cache_k.reshape(nb_lat, DEPTH, PAST_LEN, NA_DIM)
    cv = cache_v.reshape(nb_lat, DEPTH, PAST_LEN, NA_DIM)
    st_n = state_n.reshape(nb_lat, DEPTH, N_DIRS, ML_HEADS, 1, ML_HEAD_DIM)
    st_m = jnp.broadcast_to(state_m[..., None, None], (nb_lat, DEPTH, N_DIRS, ML_HEADS, 1, LANES))

    xp = x_prompt
    xs = x_sample.reshape(nb_lat * DEC_SEQ, D_MODEL)
    states = None
    for l in range(DEPTH):
        last = l == DEPTH - 1
        xp, states = _ctx_mixer_call(l, xp, mods, ln1, w_in_p, conv_w, gate_b, ml_g, w_out_b, states)
        xp = _sparse_moe(l, xp.reshape(nb_ctx * SEQ, D_MODEL), mods, ln2, w_router, b_router,
                         w_gate, w_up, w_down, fg, chunks_per_row=None, final=last)
        xp = xp.reshape(nb_ctx, SEQ, D_MODEL)

        zc, zn, zm, zg = _lat_proj_call(l, xs, mods, ln1, w_in_p, gate_b)
        loc = _lat_local_call(l, zc, zn, ck, cv, band, conv_w)
        ml = _lat_mlstm_call(l, zm, zg, state_C, st_n, st_m, cos_t, sin_t, ml_g)
        xs = _lat_merge_call(l, xs, loc, ml, mods, w_out_b)
        xs = _sparse_moe(l, xs, mods, ln2, w_router, b_router, w_gate, w_up, w_down, fg,
                         chunks_per_row=DEC_SEQ // DSP_CH, final=last)

    new_k, new_v, new_c, new_n, new_m = states
    return (xp, xs.reshape(nb_lat, DEC_SEQ, D_MODEL),
            new_k.reshape(nb_ctx, DEPTH, SEQ, NA_HEADS, NA_HEAD_DIM),
            new_v.reshape(nb_ctx, DEPTH, SEQ, NA_HEADS, NA_HEAD_DIM),
            new_c,
            new_n.reshape(nb_ctx, DEPTH, N_DIRS, ML_HEADS, ML_HEAD_DIM),
            new_m[..., 0].reshape(nb_ctx, DEPTH, N_DIRS, ML_HEADS))
```

```python
import functools

import numpy as np
import jax
import jax.numpy as jnp
from jax import lax
from jax.experimental import pallas as pl
from jax.experimental.pallas import tpu as pltpu

F32 = jnp.float32
BF16 = jnp.bfloat16

D_MODEL = 1024
SEQ = 256
DEPTH = 4
DEC_SEQ = 1024
PAST_LEN = 512
GRID_W = 64
CONV_DIM = 256
NA_HEADS = 4
NA_HEAD_DIM = 64
NA_DIM = NA_HEADS * NA_HEAD_DIM
NA_WIN_ROWS = 8
NA_WIN_COLS = 16
ML_HEADS = 4
ML_HEAD_DIM = 128
ML_DIM = ML_HEADS * ML_HEAD_DIM
N_DIRS = 2
N_GROUPS = 4
EXPERTS_PER_GROUP = 4
N_EXPERTS = N_GROUPS * EXPERTS_PER_GROUP
EXPERT_FF = 512
ROPE_BASE = 10000.0
EPS = 1e-6
N_MOD = 6

LANES = 128
N_GATE = N_DIRS * ML_HEADS
C_CONV = 0
C_NA = 3 * CONV_DIM
C_ML = C_NA + 3 * NA_DIM
C_IG = C_ML + 4 * ML_DIM
C_FG = C_IG + LANES
W_COLS = C_FG + LANES
CUMSUM_BLOCK = 256
MODS_ROWS = 8
VMEM_LIMIT = 56 * 1024 * 1024

NEG_INF = float("-inf")
HIGHEST = lax.Precision.HIGHEST


def _dot(a, b):
    return jnp.dot(a, b, preferred_element_type=F32)


def _dot_nt(a, b):
    return lax.dot_general(a, b, (((1,), (1,)), ((), ())), preferred_element_type=F32)


def _rms_mod(x, g, sc, sh):
    y = x * lax.rsqrt(jnp.mean(x * x, axis=-1, keepdims=True) + EPS)
    return (y * g) * (1.0 + sc) + sh


def _mod_rows(mods_ref, row, first):
    return [mods_ref[pl.ds(row, 1), pl.ds((first + j) * D_MODEL, D_MODEL)] for j in range(3)]


def _mods_kernel(cv_ref, w_ref, b_ref, o_ref):
    cv = cv_ref[...]
    s = cv * jax.nn.sigmoid(cv)
    o_ref[...] = _dot(s.astype(BF16), w_ref[...].astype(BF16)) + b_ref[...]


def _mods_call(cvecs, w_mod, b_mod):
    tn = 1536
    n = N_MOD * D_MODEL
    return pl.pallas_call(
        _mods_kernel,
        grid=(DEPTH, n // tn),
        in_specs=[pl.BlockSpec((MODS_ROWS, D_MODEL), lambda l, j: (0, 0)),
                  pl.BlockSpec((None, D_MODEL, tn), lambda l, j: (l, 0, j)),
                  pl.BlockSpec((None, 1, tn), lambda l, j: (l, 0, j))],
        out_specs=pl.BlockSpec((None, MODS_ROWS, tn), lambda l, j: (l, 0, j)),
        out_shape=jax.ShapeDtypeStruct((DEPTH, MODS_ROWS, n), F32),
        compiler_params=pltpu.CompilerParams(vmem_limit_bytes=VMEM_LIMIT),
        name="adaln_mods",
    )(cvecs, w_mod, b_mod.reshape(DEPTH, 1, n))


def _short_conv(zc, cw):
    T = zc.shape[0]
    cb = zc[:, 0:CONV_DIM]
    u = zc[:, CONV_DIM:2 * CONV_DIM] * zc[:, 2 * CONV_DIM:3 * CONV_DIM]
    t = lax.broadcasted_iota(jnp.int32, u.shape, 0)
    u_prev = jnp.where(t == 0, 0.0, pltpu.roll(u, 1, axis=0))
    u_next = jnp.where(t == T - 1, 0.0, pltpu.roll(u, T - 1, axis=0))
    return cb * (cw[0:1, :] * u_prev + cw[1:2, :] * u + cw[2:3, :] * u_next)


def _softmax_attention(nq, nk, nv):
    outs = []
    for h in range(NA_HEADS):
        sl = slice(h * NA_HEAD_DIM, (h + 1) * NA_HEAD_DIM)
        q = (nq[:, sl] * NA_HEAD_DIM ** -0.5).astype(BF16)
        s = _dot_nt(q, nk[:, sl].astype(BF16))
        p = jnp.exp(s - jnp.max(s, axis=-1, keepdims=True))
        o = _dot(p.astype(BF16), nv[:, sl].astype(BF16)) / jnp.sum(p, axis=-1, keepdims=True)
        outs.append(o)
    return jnp.concatenate(outs, axis=-1)


def _log_sigmoid(x):
    return jnp.minimum(x, 0.0) - jnp.log(1.0 + jnp.exp(-jnp.abs(x)))


def _gate_terms(zi, zf):
    T = zi.shape[0]
    lf = _log_sigmoid(zf)
    blk = min(T, CUMSUM_BLOCK)
    r = lax.broadcasted_iota(jnp.int32, (blk, blk), 0)
    c = lax.broadcasted_iota(jnp.int32, (blk, blk), 1)
    tril = jnp.where(c <= r, 1.0, 0.0).astype(F32)
    parts, carry = [], None
    for r0 in range(0, T, blk):
        b = jnp.dot(tril, lf[r0:r0 + blk], precision=HIGHEST, preferred_element_type=F32)
        if carry is not None:
            b = b + carry
        carry = b[blk - 1:blk, :]
        parts.append(b)
    b_fwd = parts[0] if len(parts) == 1 else jnp.concatenate(parts, axis=0)
    b_bwd = (b_fwd[T - 1:T, :] - b_fwd) + lf
    lane = lax.broadcasted_iota(jnp.int32, zi.shape, 1)
    b_col = jnp.where(lane < ML_HEADS, b_fwd, b_bwd)
    row_t = jnp.transpose(zi - b_col)
    return b_col, row_t


def _mlstm_outputs(q, k, v, b_col, li_row, m0, c0, n0, backward, q_block):
    T = q.shape[0]
    qb, kb, vb = q.astype(BF16), k.astype(BF16), v.astype(BF16)
    c0b = None if c0 is None else c0.astype(BF16)
    outs = []
    for r0 in range(0, T, q_block):
        ks, ke = (r0, T) if backward else (0, r0 + q_block)
        bq = b_col[r0:r0 + q_block]
        d = bq + li_row[:, ks:ke]
        t_idx = r0 + lax.broadcasted_iota(jnp.int32, d.shape, 0)
        s_idx = ks + lax.broadcasted_iota(jnp.int32, d.shape, 1)
        d = jnp.where((s_idx >= t_idx) if backward else (s_idx <= t_idx), d, NEG_INF)
        inter = bq + m0
        m_t = jnp.maximum(jnp.max(d, axis=-1, keepdims=True), inter)
        w = jnp.exp(d - m_t)
        s = _dot_nt(qb[r0:r0 + q_block], kb[ks:ke]) * w
        num = _dot(s.astype(BF16), vb[ks:ke])
        den = jnp.sum(s, axis=-1, keepdims=True)
        if c0 is not None:
            a = jnp.exp(inter - m_t)
            num = num + a * _dot(qb[r0:r0 + q_block], c0b)
            den = den + a * jnp.sum(q[r0:r0 + q_block] * n0, axis=-1, keepdims=True)
        outs.append(num / jnp.maximum(jnp.abs(den), jnp.exp(-m_t)))
    return outs[0] if len(outs) == 1 else jnp.concatenate(outs, axis=0)


def _mlstm_state(k, v, b_col, li_col, backward):
    T = k.shape[0]
    b_tot = b_col[0:1] if backward else b_col[T - 1:T]
    g = (b_tot - b_col) + li_col
    m_new = jnp.maximum(b_tot, jnp.max(g, axis=0, keepdims=True))
    kw = k * jnp.exp(g - m_new)
    c_new = _dot(jnp.transpose(kw).astype(BF16), v.astype(BF16))
    n_new = jnp.sum(kw, axis=0, keepdims=True)
    return c_new, n_new, m_new


def _head_readout(h, mo, g):
    hn = h * lax.rsqrt(jnp.mean(h * h, axis=-1, keepdims=True) + EPS)
    return hn * g * jax.nn.sigmoid(mo)


def _ctx_mixer_kernel(x_ref, mods_ref, ln_ref, win_ref, convw_ref, gb_ref, mlg_ref, wout_ref, *rest):
    xo_ref, ko_ref, vo_ref, co_ref, no_ref, mo_ref = rest[-6:]
    T = SEQ
    sh1, sc1, g1 = _mod_rows(mods_ref, 0, 0)
    x = x_ref[...]
    h = _rms_mod(x, ln_ref[...], sc1, sh1).astype(BF16)

    conv_o = _short_conv(_dot(h, win_ref[:, C_CONV:C_NA]), convw_ref[...])

    zn = _dot(h, win_ref[:, C_NA:C_ML])
    nk, nv = zn[:, NA_DIM:2 * NA_DIM], zn[:, 2 * NA_DIM:3 * NA_DIM]
    ko_ref[...] = nk
    vo_ref[...] = nv
    na_o = _softmax_attention(zn[:, 0:NA_DIM], nk, nv)

    zg = _dot(h, win_ref[:, C_IG:W_COLS])
    zi = zg[:, 0:LANES] + gb_ref[:, 0:LANES]
    zf = zg[:, LANES:2 * LANES] + gb_ref[:, LANES:2 * LANES]
    b_col, row_t = _gate_terms(zi, zf)
    m0 = jnp.zeros((1, 1), F32)

    zm = _dot(h, win_ref[:, C_ML:C_IG])
    ml_parts = []
    for hh in range(ML_HEADS):
        zq, zk, zv, zo = (zm[:, p * ML_DIM + hh * ML_HEAD_DIM:p * ML_DIM + (hh + 1) * ML_HEAD_DIM]
                          for p in range(4))
        zk = zk * ML_HEAD_DIM ** -0.5
        hsum = None
        for d in range(N_DIRS):
            j = d * ML_HEADS + hh
            bc = b_col[:, j:j + 1]
            ho = _mlstm_outputs(zq, zk, zv, bc, row_t[j:j + 1, :], m0, None, None, d == 1, T)
            hsum = ho if hsum is None else hsum + ho
            c_new, n_new, m_new = _mlstm_state(zk, zv, bc, zi[:, j:j + 1], d == 1)
            co_ref[d, hh] = c_new
            no_ref[j:j + 1, :] = n_new
            mo_ref[j:j + 1, :] = jnp.broadcast_to(m_new, (1, LANES))
        ml_parts.append(_head_readout(hsum, zo, mlg_ref[:, hh * ML_HEAD_DIM:(hh + 1) * ML_HEAD_DIM]))

    mix = jnp.concatenate([conv_o, na_o] + ml_parts, axis=-1).astype(BF16)
    xo_ref[...] = x + g1 * _dot(mix, wout_ref[...])


def _ctx_mixer_call(layer, x, mods, ln1, w_in, conv_w, gate_b, ml_g, w_out, prev):
    B = x.shape[0]
    T = SEQ
    G = None
    state_shapes = [
        jax.ShapeDtypeStruct((B, DEPTH, T, NA_DIM), F32),
        jax.ShapeDtypeStruct((B, DEPTH, T, NA_DIM), F32),
        jax.ShapeDtypeStruct((B, DEPTH, N_DIRS, ML_HEADS, ML_HEAD_DIM, ML_HEAD_DIM), F32),
        jax.ShapeDtypeStruct((B, DEPTH, N_GATE, ML_HEAD_DIM), F32),
        jax.ShapeDtypeStruct((B, DEPTH, N_GATE, LANES), F32),
    ]
    in_specs = [
        pl.BlockSpec((G, T, D_MODEL), lambda b: (b, 0, 0)),
        pl.BlockSpec((None, MODS_ROWS, N_MOD * D_MODEL), lambda b: (layer, 0, 0)),
        pl.BlockSpec((None, 1, D_MODEL), lambda b: (layer, 0, 0)),
        pl.BlockSpec((None, D_MODEL, W_COLS), lambda b: (layer, 0, 0)),
        pl.BlockSpec((None, 3, CONV_DIM), lambda b: (layer, 0, 0)),
        pl.BlockSpec((None, 1, 2 * LANES), lambda b: (layer, 0, 0)),
        pl.BlockSpec((None, 1, ML_DIM), lambda b: (layer, 0, 0)),
        pl.BlockSpec((None, D_MODEL, D_MODEL), lambda b: (layer, 0, 0)),
    ]
    args = [x, mods, ln1, w_in, conv_w, gate_b, ml_g, w_out]
    aliases = {}
    if prev is not None:
        in_specs += [pl.BlockSpec(memory_space=pl.ANY)] * len(prev)
        aliases = {len(args) + i: 1 + i for i in range(len(prev))}
        args += list(prev)
    out_specs = [
        pl.BlockSpec((G, T, D_MODEL), lambda b: (b, 0, 0)),
        pl.BlockSpec((G, None, T, NA_DIM), lambda b: (b, layer, 0, 0)),
        pl.BlockSpec((G, None, T, NA_DIM), lambda b: (b, layer, 0, 0)),
        pl.BlockSpec((G, None, N_DIRS, ML_HEADS, ML_HEAD_DIM, ML_HEAD_DIM),
                     lambda b: (b, layer, 0, 0, 0, 0)),
        pl.BlockSpec((G, None, N_GATE, ML_HEAD_DIM), lambda b: (b, layer, 0, 0)),
        pl.BlockSpec((G, None, N_GATE, LANES), lambda b: (b, layer, 0, 0)),
    ]
    outs = pl.pallas_call(
        _ctx_mixer_kernel,
        grid=(B,),
        in_specs=in_specs,
        out_specs=out_specs,
        out_shape=[jax.ShapeDtypeStruct(x.shape, F32)] + state_shapes,
        input_output_aliases=aliases,
        compiler_params=pltpu.CompilerParams(vmem_limit_bytes=VMEM_LIMIT),
        name="ctx_mixer",
    )(*args)
    return outs[0], tuple(outs[1:])


def _route(lg):
    lane = lax.broadcasted_iota(jnp.int32, lg.shape, 1)
    lane_f = lane.astype(F32)
    big = float(LANES)
    is_g = (lane >= N_EXPERTS) & (lane < N_EXPERTS + N_GROUPS)
    glm = jnp.where(is_g, lg, NEG_INF)
    gmax = jnp.max(glm, axis=-1, keepdims=True)
    g_top = jnp.min(jnp.where(glm == gmax, lane_f, big), axis=-1, keepdims=True) - N_EXPERTS
    gp = 1.0 / jnp.sum(jnp.where(is_g, jnp.exp(lg - gmax), 0.0), axis=-1, keepdims=True)
    grp = (lane >> (EXPERTS_PER_GROUP.bit_length() - 1)).astype(F32)
    in_grp = (lane < N_EXPERTS) & (grp == g_top)
    e1 = jnp.where(in_grp, lg, NEG_INF)
    v1 = jnp.max(e1, axis=-1, keepdims=True)
    i1 = jnp.min(jnp.where(e1 == v1, lane_f, big), axis=-1, keepdims=True)
    e2 = jnp.where(in_grp & (lane_f != i1), lg, NEG_INF)
    v2 = jnp.max(e2, axis=-1, keepdims=True)
    i2 = jnp.min(jnp.where(e2 == v2, lane_f, big), axis=-1, keepdims=True)
    t = jnp.exp(v2 - v1)
    w1 = 1.0 / (1.0 + t)
    w2 = t / (1.0 + t)
    return i1, i2, gp * w1, gp * w2


SLAB = D_MODEL // LANES
DSP_SB = 2048
DSP_TM = 128
DSP_CH = 512
DSP_PMAX = 2 * DSP_SB + N_EXPERTS * DSP_TM
DSP_UNROLL = 8


def _to_slabs(ref, x):
    n = x.shape[0]
    for s in range(SLAB):
        ref[pl.ds(s, n, stride=SLAB), :] = x[:, s * LANES:(s + 1) * LANES]


def _from_slabs(ref, n):
    return jnp.concatenate([ref[pl.ds(s, n, stride=SLAB), :] for s in range(SLAB)], axis=-1)


def _dsp_mods(mods_ref, chunk, chunks_per_row, first):
    row = 0 if chunks_per_row is None else 1 + chunk // chunks_per_row
    return _mod_rows(mods_ref, row, first)


def _dsp_route_kernel(x_ref, mods_ref, ln_ref, wr_ref, br_ref, h_ref, r_ref, cnt_ref, run_scr, *,
                      chunks_per_row):
    c = pl.program_id(0)
    ch = DSP_CH

    @pl.when(c % (DSP_SB // ch) == 0)
    def _():
        run_scr[...] = jnp.zeros_like(run_scr)

    sh2, sc2, _ = _dsp_mods(mods_ref, c, chunks_per_row, 3)
    h2 = _rms_mod(x_ref[...], ln_ref[...], sc2, sh2)
    _to_slabs(h_ref, h2)
    lg = jnp.dot(h2, wr_ref[...], precision=HIGHEST, preferred_element_type=F32) + br_ref[...]
    i1, i2, w1, w2 = _route(lg)
    lane = lax.broadcasted_iota(jnp.int32, lg.shape, 1)
    lane_f = lane.astype(F32)
    oh1 = jnp.where(lane_f == i1, 1.0, 0.0)
    oh2 = jnp.where(lane_f == i2, 1.0, 0.0)
    both = oh1 + oh2
    r = lax.broadcasted_iota(jnp.int32, (ch, ch), 0)
    s = lax.broadcasted_iota(jnp.int32, (ch, ch), 1)
    earlier = jnp.where(s < r, 1.0, 0.0).astype(BF16)
    run = run_scr[0:1, :]
    before = _dot(earlier, both.astype(BF16)) + run
    rank1 = jnp.sum(before * oh1, axis=-1, keepdims=True)
    rank2 = jnp.sum(before * oh2, axis=-1, keepdims=True)
    run = run + jnp.sum(both, axis=0, keepdims=True)
    run_scr[0:1, :] = run
    cnt_ref[...] = jnp.broadcast_to(run, cnt_ref.shape)
    cols = (i1, i2, w1, w2, rank1, rank2)
    out = jnp.zeros(lg.shape, F32)
    for j, col in enumerate(cols):
        out = jnp.where(lane == j, col, out)
    r_ref[...] = out


def _dsp_route_call(layer, x2d, mods, ln2, w_router, b_router, chunks_per_row):
    n = x2d.shape[0]
    ch = DSP_CH
    per_sb = DSP_SB // ch
    return pl.pallas_call(
        functools.partial(_dsp_route_kernel, chunks_per_row=chunks_per_row),
        grid=(n // ch,),
        in_specs=[
            pl.BlockSpec((ch, D_MODEL), lambda c: (c, 0)),
            pl.BlockSpec((None, MODS_ROWS, N_MOD * D_MODEL), lambda c: (layer, 0, 0)),
            pl.BlockSpec((None, 1, D_MODEL), lambda c: (layer, 0, 0)),
            pl.BlockSpec((None, D_MODEL, LANES), lambda c: (layer, 0, 0)),
            pl.BlockSpec((None, 1, LANES), lambda c: (layer, 0, 0)),
        ],
        out_specs=[pl.BlockSpec((ch * SLAB, LANES), lambda c: (c, 0)),
                   pl.BlockSpec((ch, LANES), lambda c: (c, 0)),
                   pl.BlockSpec((None, SLAB, LANES), lambda c: (c // per_sb, 0, 0))],
        out_shape=[jax.ShapeDtypeStruct((n * SLAB, LANES), F32),
                   jax.ShapeDtypeStruct((n, LANES), F32),
                   jax.ShapeDtypeStruct((n // DSP_SB, SLAB, LANES), F32)],
        scratch_shapes=[pltpu.VMEM((SLAB, LANES), F32)],
        compiler_params=pltpu.CompilerParams(vmem_limit_bytes=VMEM_LIMIT),
        name="moe_route",
    )(x2d, mods, ln2, w_router, b_router)


def _dispatch_tables(route, counts, n_sb):
    e = route[:, 0:2].astype(jnp.int32).reshape(n_sb, 2 * DSP_SB)
    rank = route[:, 4:6].astype(jnp.int32).reshape(n_sb, 2 * DSP_SB)
    w = route[:, 2:4].reshape(n_sb, 1, 2 * DSP_SB)
    cnt = counts[:, 0, :N_EXPERTS].astype(jnp.int32)
    ntile = (cnt + (DSP_TM - 1)) // DSP_TM
    off = (jnp.cumsum(ntile, axis=1) - ntile) * DSP_TM
    is_e = e[:, None, :] == jnp.arange(N_EXPERTS, dtype=jnp.int32)[None, :, None]
    pos = (jnp.sum(jnp.where(is_e, off[:, :, None], 0), axis=1) + rank).reshape(n_sb, 1, 2 * DSP_SB)
    return pos, w, ntile.reshape(-1), off.reshape(-1), cnt.reshape(-1)


def _dsp_expert_kernel(ntile_ref, off_ref, cnt_ref, pos_ref, w_ref, h_ref, wg_ref, wu_ref, wd_ref,
                       o_ref, row_smem, wgt_smem, xs_scr, ys_scr, wg_scr, wu_scr, wd_scr):
    sb = pl.program_id(0)
    e = pl.program_id(1)
    tm, u = DSP_TM, DSP_UNROLL
    slab_shift = SLAB.bit_length() - 1
    spare_row = DSP_SB * SLAB
    last_row = (DSP_SB - 1) * SLAB

    @pl.when(e == 0)
    def _():
        for ee in range(N_EXPERTS):
            q = sb * N_EXPERTS + ee

            def pad(p, carry):
                row_smem[p] = spare_row
                wgt_smem[p] = 0.0
                return carry

            lax.fori_loop(off_ref[q] + cnt_ref[q], off_ref[q] + ntile_ref[q] * tm, pad, 0)

        def invert(j, carry):
            for k in range(u):
                a = j * u + k
                p = pos_ref[0, a]
                row_smem[p] = lax.shift_left(lax.shift_right_logical(a, 1), slab_shift)
                wgt_smem[p] = w_ref[0, a]
            return carry

        lax.fori_loop(0, 2 * DSP_SB // u, invert, 0)
        o_ref[...] = jnp.zeros_like(o_ref)

    base0 = off_ref[sb * N_EXPERTS + e]
    wg_scr[...] = wg_ref[...].astype(BF16)
    wu_scr[...] = wu_ref[...].astype(BF16)
    wd_scr[...] = wd_ref[...].astype(BF16)

    def tile(i, carry):
        base = base0 + i * tm

        def gather(j, c2):
            for k in range(u):
                r = j * u + k
                src = jnp.minimum(row_smem[base + r], last_row)
                xs_scr[pl.ds(pl.multiple_of(r * SLAB, SLAB), SLAB), :] = (
                    h_ref[pl.ds(pl.multiple_of(src, SLAB), SLAB), :])
            return c2

        lax.fori_loop(0, tm // u, gather, 0)
        x = _from_slabs(xs_scr, tm).astype(BF16)
        a = jax.nn.silu(_dot(x, wg_scr[...])) * _dot(x, wu_scr[...])
        _to_slabs(ys_scr, _dot(a.astype(BF16), wd_scr[...]))

        def scatter(j, c2):
            upd = []
            for k in range(u):
                r = j * u + k
                dst = pl.multiple_of(row_smem[base + r], SLAB)
                y = ys_scr[pl.ds(pl.multiple_of(r * SLAB, SLAB), SLAB), :]
                upd.append((dst, o_ref[pl.ds(dst, SLAB), :] + wgt_smem[base + r] * y))
            for dst, v in upd:
                o_ref[pl.ds(dst, SLAB), :] = v
            return c2

        lax.fori_loop(0, tm // u, scatter, 0)
        return carry

    lax.fori_loop(0, ntile_ref[sb * N_EXPERTS + e], tile, 0)


def _dsp_expert_call(layer, h_slabs, pos, w, ntile, off, cnt, w_gate, w_up, w_down):
    n_sb = pos.shape[0]
    out_rows = (DSP_SB + DSP_CH) * SLAB

    def per_sb(s, e, *prefetch):
        return (s, 0, 0)

    def w_idx(s, e, *prefetch):
        return (layer, e, 0, 0)

    grid_spec = pltpu.PrefetchScalarGridSpec(
        num_scalar_prefetch=3,
        grid=(n_sb, N_EXPERTS),
        in_specs=[
            pl.BlockSpec((None, 1, 2 * DSP_SB), per_sb, memory_space=pltpu.SMEM),
            pl.BlockSpec((None, 1, 2 * DSP_SB), per_sb, memory_space=pltpu.SMEM),
            pl.BlockSpec((DSP_SB * SLAB, LANES), lambda s, e, *prefetch: (s, 0)),
            pl.BlockSpec((None, None, D_MODEL, EXPERT_FF), w_idx),
            pl.BlockSpec((None, None, D_MODEL, EXPERT_FF), w_idx),
            pl.BlockSpec((None, None, EXPERT_FF, D_MODEL), w_idx),
        ],
        out_specs=pl.BlockSpec((None, out_rows, LANES), per_sb),
        scratch_shapes=[pltpu.SMEM((DSP_PMAX,), jnp.int32),
                        pltpu.SMEM((DSP_PMAX,), F32),
                        pltpu.VMEM((DSP_TM * SLAB, LANES), F32),
                        pltpu.VMEM((DSP_TM * SLAB, LANES), F32),
                        pltpu.VMEM((D_MODEL, EXPERT_FF), BF16),
                        pltpu.VMEM((D_MODEL, EXPERT_FF), BF16),
                        pltpu.VMEM((EXPERT_FF, D_MODEL), BF16)],
    )
    return pl.pallas_call(
        _dsp_expert_kernel,
        grid_spec=grid_spec,
        out_shape=jax.ShapeDtypeStruct((n_sb, out_rows, LANES), F32),
        compiler_params=pltpu.CompilerParams(vmem_limit_bytes=VMEM_LIMIT),
        name="moe_experts",
    )(ntile, off, cnt, pos, w, h_slabs, w_gate, w_up, w_down)


def _dsp_combine_kernel(x_ref, m_ref, mods_ref, fg_ref, o_ref, *, chunks_per_row, final):
    g2 = _dsp_mods(mods_ref, pl.program_id(0), chunks_per_row, 3)[2]
    y = x_ref[...] + g2 * _from_slabs(m_ref, DSP_CH)
    if final:
        y = y * lax.rsqrt(jnp.mean(y * y, axis=-1, keepdims=True) + EPS) * fg_ref[...]
    o_ref[...] = y


def _dsp_combine_call(layer, x2d, m_slabs, mods, final_g, chunks_per_row, final):
    n = x2d.shape[0]
    ch = DSP_CH
    per_sb = DSP_SB // ch
    return pl.pallas_call(
        functools.partial(_dsp_combine_kernel, chunks_per_row=chunks_per_row, final=final),
        grid=(n // ch,),
        in_specs=[
            pl.BlockSpec((ch, D_MODEL), lambda c: (c, 0)),
            pl.BlockSpec((None, ch * SLAB, LANES), lambda c: (c // per_sb, c % per_sb, 0)),
            pl.BlockSpec((None, MODS_ROWS, N_MOD * D_MODEL), lambda c: (layer, 0, 0)),
            pl.BlockSpec((1, D_MODEL), lambda c: (0, 0)),
        ],
        out_specs=pl.BlockSpec((ch, D_MODEL), lambda c: (c, 0)),
        out_shape=jax.ShapeDtypeStruct(x2d.shape, F32),
        compiler_params=pltpu.CompilerParams(vmem_limit_bytes=VMEM_LIMIT),
        name="moe_combine",
    )(x2d, m_slabs, mods, final_g)


def _sparse_moe(layer, x2d, mods, ln2, w_router, b_router, w_gate, w_up, w_down, final_g, *,
                chunks_per_row, final):
    n_sb = x2d.shape[0] // DSP_SB
    h_slabs, route, counts = _dsp_route_call(layer, x2d, mods, ln2, w_router, b_router,
                                             chunks_per_row)
    pos, w, ntile, off, cnt = _dispatch_tables(route, counts, n_sb)
    m_slabs = _dsp_expert_call(layer, h_slabs, pos, w, ntile, off, cnt, w_gate, w_up, w_down)
    return _dsp_combine_call(layer, x2d, m_slabs, mods, final_g, chunks_per_row, final)


LAT_PROJ_ROWS = 512


def _lat_proj_kernel(x_ref, mods_ref, ln_ref, win_ref, gb_ref, zc_ref, zn_ref, zm_ref, zg_ref):
    row = 1 + pl.program_id(0) // (DEC_SEQ // LAT_PROJ_ROWS)
    sh1, sc1, _ = _mod_rows(mods_ref, row, 0)
    h = _rms_mod(x_ref[...], ln_ref[...], sc1, sh1).astype(BF16)
    zc_ref[...] = _dot(h, win_ref[:, C_CONV:C_NA])
    zn_ref[...] = _dot(h, win_ref[:, C_NA:C_ML])
    zm_ref[...] = _dot(h, win_ref[:, C_ML:C_IG])
    zg_ref[...] = _dot(h, win_ref[:, C_IG:W_COLS]) + gb_ref[...]


def _lat_proj_call(layer, x2d, mods, ln1, w_in, gate_b):
    n = x2d.shape[0]
    tb = LAT_PROJ_ROWS
    widths = (C_NA - C_CONV, C_ML - C_NA, C_IG - C_ML, W_COLS - C_IG)
    return pl.pallas_call(
        _lat_proj_kernel,
        grid=(n // tb,),
        in_specs=[
            pl.BlockSpec((tb, D_MODEL), lambda i: (i, 0)),
            pl.BlockSpec((None, MODS_ROWS, N_MOD * D_MODEL), lambda i: (layer, 0, 0)),
            pl.BlockSpec((None, 1, D_MODEL), lambda i: (layer, 0, 0)),
            pl.BlockSpec((None, D_MODEL, W_COLS), lambda i: (layer, 0, 0)),
            pl.BlockSpec((None, 1, 2 * LANES), lambda i: (layer, 0, 0)),
        ],
        out_specs=[pl.BlockSpec((tb, w), lambda i: (i, 0)) for w in widths],
        out_shape=[jax.ShapeDtypeStruct((n, w), F32) for w in widths],
        compiler_params=pltpu.CompilerParams(vmem_limit_bytes=VMEM_LIMIT),
        name="lat_proj",
    )(x2d, mods, ln1, w_in, gate_b)


def _na_row_start(r):
    rows = DEC_SEQ // GRID_W
    return min(max(r - NA_WIN_ROWS // 2, 0), rows - NA_WIN_ROWS)


def _lat_local_kernel(zc_ref, zn_ref, kc_ref, vc_ref, band_ref, convw_ref, o_ref):
    o_ref[:, 0:CONV_DIM] = _short_conv(zc_ref[...], convw_ref[...])
    rows = DEC_SEQ // GRID_W
    blk = NA_WIN_ROWS * GRID_W
    q_all = (zn_ref[:, 0:NA_DIM] * NA_HEAD_DIM ** -0.5).astype(BF16)
    k_all = zn_ref[:, NA_DIM:2 * NA_DIM].astype(BF16)
    v_all = zn_ref[:, 2 * NA_DIM:3 * NA_DIM].astype(BF16)
    kc_all = kc_ref[...].astype(BF16)
    vc_all = vc_ref[...].astype(BF16)
    heads = []
    for h in range(NA_HEADS):
        sl = slice(h * NA_HEAD_DIM, (h + 1) * NA_HEAD_DIM)
        q, k, v, kc, vc = q_all[:, sl], k_all[:, sl], v_all[:, sl], kc_all[:, sl], vc_all[:, sl]
        band = band_ref[h]
        outs = []
        for r in range(rows):
            start = _na_row_start(r)
            off = (start - r + NA_WIN_ROWS - 1) * GRID_W
            q_r = q[r * GRID_W:(r + 1) * GRID_W]
            s_loc = _dot_nt(q_r, k[start * GRID_W:start * GRID_W + blk]) + band[:, off:off + blk]
            s_ctx = _dot_nt(q_r, kc)
            m = jnp.maximum(jnp.max(s_loc, axis=-1, keepdims=True), jnp.max(s_ctx, axis=-1, keepdims=True))
            p_loc = jnp.exp(s_loc - m)
            p_ctx = jnp.exp(s_ctx - m)
            den = jnp.sum(p_loc, axis=-1, keepdims=True) + jnp.sum(p_ctx, axis=-1, keepdims=True)
            o = _dot(p_loc.astype(BF16), v[start * GRID_W:start * GRID_W + blk]) + _dot(p_ctx.astype(BF16), vc)
            outs.append(o / den)
        heads.append(jnp.concatenate(outs, axis=0))
    o_ref[:, CONV_DIM:CONV_DIM + NA_DIM] = jnp.concatenate(heads, axis=-1)


def _lat_local_call(layer, zc, zn, cache_k, cache_v, band, conv_w):
    nb = zc.shape[0] // DEC_SEQ
    T = DEC_SEQ
    return pl.pallas_call(
        _lat_local_kernel,
        grid=(nb,),
        in_specs=[
            pl.BlockSpec((T, 3 * CONV_DIM), lambda b: (b, 0)),
            pl.BlockSpec((T, 3 * NA_DIM), lambda b: (b, 0)),
            pl.BlockSpec((None, None, PAST_LEN, NA_DIM), lambda b: (b, layer, 0, 0)),
            pl.BlockSpec((None, None, PAST_LEN, NA_DIM), lambda b: (b, layer, 0, 0)),
            pl.BlockSpec((None, NA_HEADS, GRID_W, band.shape[-1]), lambda b: (layer, 0, 0, 0)),
            pl.BlockSpec((None, 3, CONV_DIM), lambda b: (layer, 0, 0)),
        ],
        out_specs=pl.BlockSpec((T, CONV_DIM + NA_DIM), lambda b: (b, 0)),
        out_shape=jax.ShapeDtypeStruct((zc.shape[0], CONV_DIM + NA_DIM), F32),
        compiler_params=pltpu.CompilerParams(vmem_limit_bytes=VMEM_LIMIT),
        name="lat_conv_na",
    )(zc, zn, cache_k, cache_v, band, conv_w)


def _rope(x, cos, sin_signed):
    w = x.shape[-1]
    half = NA_HEAD_DIM // 2
    lane = lax.broadcasted_iota(jnp.int32, x.shape, 1)
    partner = jnp.where(lane % (2 * half) < half, pltpu.roll(x, w - half, axis=1), pltpu.roll(x, half, axis=1))
    return x * cos + partner * sin_signed


def _lat_mlstm_kernel(q_ref, k_ref, v_ref, o_ref, zg_ref, c0_ref, n0_ref, m0_ref, cos_ref, sin_ref,
                      mlg_ref, out_ref, bcol_scr, rowt_scr):
    hh = pl.program_id(1)

    @pl.when(hh == 0)
    def _():
        b_all, r_all = _gate_terms(zg_ref[:, 0:LANES], zg_ref[:, LANES:2 * LANES])
        bcol_scr[...] = b_all
        rowt_scr[...] = r_all

    b_col = bcol_scr[...]
    cos, sin_signed = cos_ref[...], sin_ref[...]
    q = _rope(q_ref[...], cos, sin_signed)
    k = _rope(k_ref[...], cos, sin_signed) * ML_HEAD_DIM ** -0.5
    v = v_ref[...]
    lane = lax.broadcasted_iota(jnp.int32, (DEC_SEQ, LANES), 1)
    hsum = None
    for d in range(N_DIRS):
        j = d * ML_HEADS + hh
        bc = jnp.sum(jnp.where(lane == j, b_col, 0.0), axis=-1, keepdims=True)
        lr = rowt_scr[pl.ds(j, 1), :]
        m0 = m0_ref[d, :, 0:1]
        ho = _mlstm_outputs(q, k, v, bc, lr, m0, c0_ref[d], n0_ref[d], d == 1, 256)
        hsum = ho if hsum is None else hsum + ho
    out_ref[...] = _head_readout(hsum, o_ref[...], mlg_ref[...])


def _lat_mlstm_call(layer, zm, zg, state_c, state_n, state_m, cos, sin_signed, ml_g):
    nb = zm.shape[0] // DEC_SEQ
    T = DEC_SEQ
    hd = ML_HEAD_DIM

    def col(part):
        return pl.BlockSpec((T, hd), lambda b, h: (b, part * ML_HEADS + h))

    return pl.pallas_call(
        _lat_mlstm_kernel,
        grid=(nb, ML_HEADS),
        in_specs=[
            col(0), col(1), col(2), col(3),
            pl.BlockSpec((T, 2 * LANES), lambda b, h: (b, 0)),
            pl.BlockSpec((None, None, N_DIRS, None, hd, hd), lambda b, h: (b, layer, 0, h, 0, 0)),
            pl.BlockSpec((None, None, N_DIRS, None, 1, hd), lambda b, h: (b, layer, 0, h, 0, 0)),
            pl.BlockSpec((None, None, N_DIRS, None, 1, LANES), lambda b, h: (b, layer, 0, h, 0, 0)),
            pl.BlockSpec((T, hd), lambda b, h: (0, 0)),
            pl.BlockSpec((T, hd), lambda b, h: (0, 0)),
            pl.BlockSpec((None, 1, hd), lambda b, h: (layer, 0, h)),
        ],
        out_specs=pl.BlockSpec((T, hd), lambda b, h: (b, h)),
        out_shape=jax.ShapeDtypeStruct((zm.shape[0], ML_DIM), F32),
        scratch_shapes=[pltpu.VMEM((T, LANES), F32), pltpu.VMEM((LANES, T), F32)],
        compiler_params=pltpu.CompilerParams(vmem_limit_bytes=VMEM_LIMIT),
        name="lat_mlstm",
    )(zm, zm, zm, zm, zg, state_c, state_n, state_m, cos, sin_signed, ml_g)


def _lat_merge_kernel(x_ref, loc_ref, ml_ref, mods_ref, wout_ref, o_ref):
    row = 1 + pl.program_id(0) // (DEC_SEQ // LAT_PROJ_ROWS)
    g1 = _mod_rows(mods_ref, row, 0)[2]
    split = CONV_DIM + NA_DIM
    y = _dot(loc_ref[...].astype(BF16), wout_ref[0:split, :]) + _dot(ml_ref[...].astype(BF16), wout_ref[split:, :])
    o_ref[...] = x_ref[...] + g1 * y


def _lat_merge_call(layer, x2d, loc, ml, mods, w_out):
    n = x2d.shape[0]
    tb = LAT_PROJ_ROWS
    return pl.pallas_call(
        _lat_merge_kernel,
        grid=(n // tb,),
        in_specs=[
            pl.BlockSpec((tb, D_MODEL), lambda i: (i, 0)),
            pl.BlockSpec((tb, CONV_DIM + NA_DIM), lambda i: (i, 0)),
            pl.BlockSpec((tb, ML_DIM), lambda i: (i, 0)),
            pl.BlockSpec((None, MODS_ROWS, N_MOD * D_MODEL), lambda i: (layer, 0, 0)),
            pl.BlockSpec((None, D_MODEL, D_MODEL), lambda i: (layer, 0, 0)),
        ],
        out_specs=pl.BlockSpec((tb, D_MODEL), lambda i: (i, 0)),
        out_shape=jax.ShapeDtypeStruct(x2d.shape, F32),
        compiler_params=pltpu.CompilerParams(vmem_limit_bytes=VMEM_LIMIT),
        name="lat_merge",
    )(x2d, loc, ml, mods, w_out)


def _pad_lanes(a, width):
    return jnp.pad(a, [(0, 0)] * (a.ndim - 1) + [(0, width - a.shape[-1])])


def _pack_w_in(w_in):
    main = w_in[..., :C_IG]
    gates = w_in[..., C_IG:]
    packed = jnp.concatenate([main, _pad_lanes(gates[..., :N_GATE], LANES),
                              _pad_lanes(gates[..., N_GATE:], LANES)], axis=-1)
    return packed.astype(BF16)


def _pack_gate_bias(ml_gate_b):
    gb = ml_gate_b.reshape(DEPTH, 2, N_GATE).astype(F32)
    return jnp.concatenate([_pad_lanes(gb[:, 0], LANES), _pad_lanes(gb[:, 1], LANES)], axis=-1)[:, None, :]


def _rpb_band(na_rpb):
    cols = np.arange(GRID_W)
    col_idx = np.clip(cols[None, :] - cols[:, None] + NA_WIN_COLS - 1, 0, 2 * NA_WIN_COLS - 2)
    col_start = np.clip(cols - NA_WIN_COLS // 2, 0, GRID_W - NA_WIN_COLS)
    col_mask = (cols[None, :] >= col_start[:, None]) & (cols[None, :] < col_start[:, None] + NA_WIN_COLS)
    pick = (col_idx[None] == np.arange(2 * NA_WIN_COLS - 1)[:, None, None]).astype(np.float32)
    t = jnp.einsum('lhri,iqk->lhqrk', na_rpb.astype(F32), jnp.asarray(pick), precision=HIGHEST)
    t = jnp.where(col_mask[None, None, :, None, :], t, NEG_INF)
    t = t.reshape(DEPTH, NA_HEADS, GRID_W, (2 * NA_WIN_ROWS - 1) * GRID_W)
    return _pad_lanes(t, 2 * NA_WIN_ROWS * GRID_W)


def _rope_tables():
    t = np.arange(DEC_SEQ)
    pos = np.stack([t // GRID_W, t % GRID_W], axis=-1).astype(np.float32)
    nf = ML_HEAD_DIM // 4
    inv = jnp.asarray(ROPE_BASE, F32) ** (-jnp.arange(nf, dtype=F32) / nf)
    ang = jnp.asarray(pos)[:, :, None] * inv
    cos = jnp.cos(ang)
    sin = jnp.sin(ang)
    cos_t = jnp.concatenate([cos, cos], axis=-1).reshape(DEC_SEQ, ML_HEAD_DIM)
    sin_t = jnp.concatenate([-sin, sin], axis=-1).reshape(DEC_SEQ, ML_HEAD_DIM)
    return cos_t, sin_t


def kernel(x_prompt, x_sample, cache_k, cache_v, state_C, state_n, state_m, c, c_ctx, w_mod, b_mod,
           ln1_g, w_in, conv_w, na_rpb, ml_gate_b, ml_norm_g, w_out, ln2_g, w_rg, b_rg, w_re, b_re,
           w_gate, w_up, w_down, final_g):
    nb_ctx = x_prompt.shape[0]
    nb_lat = x_sample.shape[0]
    assert 1 + nb_lat <= MODS_ROWS

    cvecs = jnp.concatenate([c_ctx[None, :], c,
                             jnp.zeros((MODS_ROWS - 1 - nb_lat, D_MODEL), F32)], axis=0)
    mods = _mods_call(cvecs, w_mod, b_mod)

    w_in_p = _pack_w_in(w_in)
    w_out_b = w_out.astype(BF16)
    gate_b = _pack_gate_bias(ml_gate_b)
    ln1 = ln1_g.reshape(DEPTH, 1, D_MODEL)
    ln2 = ln2_g.reshape(DEPTH, 1, D_MODEL)
    ml_g = ml_norm_g.reshape(DEPTH, 1, ML_DIM)
    w_router = _pad_lanes(jnp.concatenate([w_re, w_rg], axis=-1), LANES)
    b_router = _pad_lanes(jnp.concatenate([b_re, b_rg], axis=-1), LANES)[:, None, :]
    fg = final_g.reshape(1, D_MODEL)
    band = _rpb_band(na_rpb)
    cos_t, sin_t = _rope_tables()
    ck = cache_k.reshape(nb_lat, DEPTH, PAST_LEN, NA_DIM)
    cv = cache_v.reshape(nb_lat, DEPTH, PAST_LEN, NA_DIM)
    st_n = state_n.reshape(nb_lat, DEPTH, N_DIRS, ML_HEADS, 1, ML_HEAD_DIM)
    st_m = jnp.broadcast_to(state_m[..., None, None], (nb_lat, DEPTH, N_DIRS, ML_HEADS, 1, LANES))

    xp = x_prompt
    xs = x_sample.reshape(nb_lat * DEC_SEQ, D_MODEL)
    states = None
    for l in range(DEPTH):
        last = l == DEPTH - 1
        xp, states = _ctx_mixer_call(l, xp, mods, ln1, w_in_p, conv_w, gate_b, ml_g, w_out_b, states)
        xp = _sparse_moe(l, xp.reshape(nb_ctx * SEQ, D_MODEL), mods, ln2, w_router, b_router,
                         w_gate, w_up, w_down, fg, chunks_per_row=None, final=last)
        xp = xp.reshape(nb_ctx, SEQ, D_MODEL)

        zc, zn, zm, zg = _lat_proj_call(l, xs, mods, ln1, w_in_p, gate_b)
        loc = _lat_local_call(l, zc, zn, ck, cv, band, conv_w)
        ml = _lat_mlstm_call(l, zm, zg, state_C, st_n, st_m, cos_t, sin_t, ml_g)
        xs = _lat_merge_call(l, xs, loc, ml, mods, w_out_b)
        xs = _sparse_moe(l, xs, mods, ln2, w_router, b_router, w_gate, w_up, w_down, fg,
                         chunks_per_row=DEC_SEQ // DSP_CH, final=last)

    new_k, new_v, new_c, new_n, new_m = states
    return (xp, xs.reshape(nb_lat, DEC_SEQ, D_MODEL),
            new_k.reshape(nb_ctx, DEPTH, SEQ, NA_HEADS, NA_HEAD_DIM),
            new_v.reshape(nb_ctx, DEPTH, SEQ, NA_HEADS, NA_HEAD_DIM),
            new_c,
            new_n.reshape(nb_ctx, DEPTH, N_DIRS, ML_HEADS, ML_HEAD_DIM),
            new_m[..., 0].reshape(nb_ctx, DEPTH, N_DIRS, ML_HEADS))
```

```python
import functools

import numpy as np
import jax
import jax.numpy as jnp
from jax import lax
from jax.experimental import pallas as pl
from jax.experimental.pallas import tpu as pltpu

F32 = jnp.float32
BF16 = jnp.bfloat16

D_MODEL = 1024
SEQ = 256
DEPTH = 4
DEC_SEQ = 1024
PAST_LEN = 512
GRID_W = 64
CONV_DIM = 256
NA_HEADS = 4
NA_HEAD_DIM = 64
NA_DIM = NA_HEADS * NA_HEAD_DIM
NA_WIN_ROWS = 8
NA_WIN_COLS = 16
ML_HEADS = 4
ML_HEAD_DIM = 128
ML_DIM = ML_HEADS * ML_HEAD_DIM
N_DIRS = 2
N_GROUPS = 4
EXPERTS_PER_GROUP = 4
N_EXPERTS = N_GROUPS * EXPERTS_PER_GROUP
EXPERT_FF = 512
ROPE_BASE = 10000.0
EPS = 1e-6
N_MOD = 6

LANES = 128
N_GATE = N_DIRS * ML_HEADS
C_CONV = 0
C_NA = 3 * CONV_DIM
C_ML = C_NA + 3 * NA_DIM
C_IG = C_ML + 4 * ML_DIM
C_FG = C_IG + LANES
W_COLS = C_FG + LANES
CUMSUM_BLOCK = 256
MODS_ROWS = 8
VMEM_LIMIT = 56 * 1024 * 1024

NEG_INF = float("-inf")
HIGHEST = lax.Precision.HIGHEST


def _dot(a, b):
    return jnp.dot(a, b, preferred_element_type=F32)


def _dot_nt(a, b):
    return lax.dot_general(a, b, (((1,), (1,)), ((), ())), preferred_element_type=F32)


def _rms_mod(x, g, sc, sh):
    y = x * lax.rsqrt(jnp.mean(x * x, axis=-1, keepdims=True) + EPS)
    return (y * g) * (1.0 + sc) + sh


def _mod_rows(mods_ref, row, first):
    return [mods_ref[pl.ds(row, 1), pl.ds((first + j) * D_MODEL, D_MODEL)] for j in range(3)]


def _mods_kernel(cv_ref, w_ref, b_ref, o_ref):
    cv = cv_ref[...]
    s = cv * jax.nn.sigmoid(cv)
    o_ref[...] = _dot(s.astype(BF16), w_ref[...].astype(BF16)) + b_ref[...]


def _mods_call(cvecs, w_mod, b_mod):
    tn = 1536
    n = N_MOD * D_MODEL
    return pl.pallas_call(
        _mods_kernel,
        grid=(DEPTH, n // tn),
        in_specs=[pl.BlockSpec((MODS_ROWS, D_MODEL), lambda l, j: (0, 0)),
                  pl.BlockSpec((None, D_MODEL, tn), lambda l, j: (l, 0, j)),
                  pl.BlockSpec((None, 1, tn), lambda l, j: (l, 0, j))],
        out_specs=pl.BlockSpec((None, MODS_ROWS, tn), lambda l, j: (l, 0, j)),
        out_shape=jax.ShapeDtypeStruct((DEPTH, MODS_ROWS, n), F32),
        compiler_params=pltpu.CompilerParams(vmem_limit_bytes=VMEM_LIMIT),
        name="adaln_mods",
    )(cvecs, w_mod, b_mod.reshape(DEPTH, 1, n))


def _short_conv(zc, cw):
    T = zc.shape[0]
    cb = zc[:, 0:CONV_DIM]
    u = zc[:, CONV_DIM:2 * CONV_DIM] * zc[:, 2 * CONV_DIM:3 * CONV_DIM]
    t = lax.broadcasted_iota(jnp.int32, u.shape, 0)
    u_prev = jnp.where(t == 0, 0.0, pltpu.roll(u, 1, axis=0))
    u_next = jnp.where(t == T - 1, 0.0, pltpu.roll(u, T - 1, axis=0))
    return cb * (cw[0:1, :] * u_prev + cw[1:2, :] * u + cw[2:3, :] * u_next)


def _softmax_attention(nq, nk, nv):
    outs = []
    for h in range(NA_HEADS):
        sl = slice(h * NA_HEAD_DIM, (h + 1) * NA_HEAD_DIM)
        q = (nq[:, sl] * NA_HEAD_DIM ** -0.5).astype(BF16)
        s = _dot_nt(q, nk[:, sl].astype(BF16))
        p = jnp.exp(s - jnp.max(s, axis=-1, keepdims=True))
        o = _dot(p.astype(BF16), nv[:, sl].astype(BF16)) / jnp.sum(p, axis=-1, keepdims=True)
        outs.append(o)
    return jnp.concatenate(outs, axis=-1)


def _log_sigmoid(x):
    return jnp.minimum(x, 0.0) - jnp.log(1.0 + jnp.exp(-jnp.abs(x)))


def _gate_terms(zi, zf):
    T = zi.shape[0]
    lf = _log_sigmoid(zf)
    blk = min(T, CUMSUM_BLOCK)
    r = lax.broadcasted_iota(jnp.int32, (blk, blk), 0)
    c = lax.broadcasted_iota(jnp.int32, (blk, blk), 1)
    tril = jnp.where(c <= r, 1.0, 0.0).astype(F32)
    parts, carry = [], None
    for r0 in range(0, T, blk):
        b = jnp.dot(tril, lf[r0:r0 + blk], precision=HIGHEST, preferred_element_type=F32)
        if carry is not None:
            b = b + carry
        carry = b[blk - 1:blk, :]
        parts.append(b)
    b_fwd = parts[0] if len(parts) == 1 else jnp.concatenate(parts, axis=0)
    b_bwd = (b_fwd[T - 1:T, :] - b_fwd) + lf
    lane = lax.broadcasted_iota(jnp.int32, zi.shape, 1)
    b_col = jnp.where(lane < ML_HEADS, b_fwd, b_bwd)
    row_t = jnp.transpose(zi - b_col)
    return b_col, row_t


def _mlstm_outputs(q, k, v, b_col, li_row, m0, c0, n0, backward, q_block):
    T = q.shape[0]
    qb, kb, vb = q.astype(BF16), k.astype(BF16), v.astype(BF16)
    c0b = None if c0 is None else c0.astype(BF16)
    outs = []
    for r0 in range(0, T, q_block):
        ks, ke = (r0, T) if backward else (0, r0 + q_block)
        bq = b_col[r0:r0 + q_block]
        d = bq + li_row[:, ks:ke]
        t_idx = r0 + lax.broadcasted_iota(jnp.int32, d.shape, 0)
        s_idx = ks + lax.broadcasted_iota(jnp.int32, d.shape, 1)
        d = jnp.where((s_idx >= t_idx) if backward else (s_idx <= t_idx), d, NEG_INF)
        inter = bq + m0
        m_t = jnp.maximum(jnp.max(d, axis=-1, keepdims=True), inter)
        w = jnp.exp(d - m_t)
        s = _dot_nt(qb[r0:r0 + q_block], kb[ks:ke]) * w
        num = _dot(s.astype(BF16), vb[ks:ke])
        den = jnp.sum(s, axis=-1, keepdims=True)
        if c0 is not None:
            a = jnp.exp(inter - m_t)
            num = num + a * _dot(qb[r0:r0 + q_block], c0b)
            den = den + a * jnp.sum(q[r0:r0 + q_block] * n0, axis=-1, keepdims=True)
        outs.append(num / jnp.maximum(jnp.abs(den), jnp.exp(-m_t)))
    return outs[0] if len(outs) == 1 else jnp.concatenate(outs, axis=0)


def _mlstm_state(k, v, b_col, li_col, backward):
    T = k.shape[0]
    b_tot = b_col[0:1] if backward else b_col[T - 1:T]
    g = (b_tot - b_col) + li_col
    m_new = jnp.maximum(b_tot, jnp.max(g, axis=0, keepdims=True))
    kw = k * jnp.exp(g - m_new)
    c_new = _dot(jnp.transpose(kw).astype(BF16), v.astype(BF16))
    n_new = jnp.sum(kw, axis=0, keepdims=True)
    return c_new, n_new, m_new


def _head_readout(h, mo, g):
    hn = h * lax.rsqrt(jnp.mean(h * h, axis=-1, keepdims=True) + EPS)
    return hn * g * jax.nn.sigmoid(mo)


def _ctx_mixer_kernel(x_ref, mods_ref, ln_ref, win_ref, convw_ref, gb_ref, mlg_ref, wout_ref, *rest):
    xo_ref, ko_ref, vo_ref, co_ref, no_ref, mo_ref = rest[-6:]
    T = SEQ
    sh1, sc1, g1 = _mod_rows(mods_ref, 0, 0)
    x = x_ref[...]
    h = _rms_mod(x, ln_ref[...], sc1, sh1).astype(BF16)

    conv_o = _short_conv(_dot(h, win_ref[:, C_CONV:C_NA]), convw_ref[...])

    zn = _dot(h, win_ref[:, C_NA:C_ML])
    nk, nv = zn[:, NA_DIM:2 * NA_DIM], zn[:, 2 * NA_DIM:3 * NA_DIM]
    ko_ref[...] = nk
    vo_ref[...] = nv
    na_o = _softmax_attention(zn[:, 0:NA_DIM], nk, nv)

    zg = _dot(h, win_ref[:, C_IG:W_COLS])
    zi = zg[:, 0:LANES] + gb_ref[:, 0:LANES]
    zf = zg[:, LANES:2 * LANES] + gb_ref[:, LANES:2 * LANES]
    b_col, row_t = _gate_terms(zi, zf)
    m0 = jnp.zeros((1, 1), F32)

    zm = _dot(h, win_ref[:, C_ML:C_IG])
    ml_parts = []
    for hh in range(ML_HEADS):
        zq, zk, zv, zo = (zm[:, p * ML_DIM + hh * ML_HEAD_DIM:p * ML_DIM + (hh + 1) * ML_HEAD_DIM]
                          for p in range(4))
        zk = zk * ML_HEAD_DIM ** -0.5
        hsum = None
        for d in range(N_DIRS):
            j = d * ML_HEADS + hh
            bc = b_col[:, j:j + 1]
            ho = _mlstm_outputs(zq, zk, zv, bc, row_t[j:j + 1, :], m0, None, None, d == 1, T)
            hsum = ho if hsum is None else hsum + ho
            c_new, n_new, m_new = _mlstm_state(zk, zv, bc, zi[:, j:j + 1], d == 1)
            co_ref[d, hh] = c_new
            no_ref[j:j + 1, :] = n_new
            mo_ref[j:j + 1, :] = jnp.broadcast_to(m_new, (1, LANES))
        ml_parts.append(_head_readout(hsum, zo, mlg_ref[:, hh * ML_HEAD_DIM:(hh + 1) * ML_HEAD_DIM]))

    mix = jnp.concatenate([conv_o, na_o] + ml_parts, axis=-1).astype(BF16)
    xo_ref[...] = x + g1 * _dot(mix, wout_ref[...])


def _ctx_mixer_call(layer, x, mods, ln1, w_in, conv_w, gate_b, ml_g, w_out, prev):
    B = x.shape[0]
    T = SEQ
    G = None
    state_shapes = [
        jax.ShapeDtypeStruct((B, DEPTH, T, NA_DIM), F32),
        jax.ShapeDtypeStruct((B, DEPTH, T, NA_DIM), F32),
        jax.ShapeDtypeStruct((B, DEPTH, N_DIRS, ML_HEADS, ML_HEAD_DIM, ML_HEAD_DIM), F32),
        jax.ShapeDtypeStruct((B, DEPTH, N_GATE, ML_HEAD_DIM), F32),
        jax.ShapeDtypeStruct((B, DEPTH, N_GATE, LANES), F32),
    ]
    in_specs = [
        pl.BlockSpec((G, T, D_MODEL), lambda b: (b, 0, 0)),
        pl.BlockSpec((None, MODS_ROWS, N_MOD * D_MODEL), lambda b: (layer, 0, 0)),
        pl.BlockSpec((None, 1, D_MODEL), lambda b: (layer, 0, 0)),
        pl.BlockSpec((None, D_MODEL, W_COLS), lambda b: (layer, 0, 0)),
        pl.BlockSpec((None, 3, CONV_DIM), lambda b: (layer, 0, 0)),
        pl.BlockSpec((None, 1, 2 * LANES), lambda b: (layer, 0, 0)),
        pl.BlockSpec((None, 1, ML_DIM), lambda b: (layer, 0, 0)),
        pl.BlockSpec((None, D_MODEL, D_MODEL), lambda b: (layer, 0, 0)),
    ]
    args = [x, mods, ln1, w_in, conv_w, gate_b, ml_g, w_out]
    aliases = {}
    if prev is not None:
        in_specs += [pl.BlockSpec(memory_space=pl.ANY)] * len(prev)
        aliases = {len(args) + i: 1 + i for i in range(len(prev))}
        args += list(prev)
    out_specs = [
        pl.BlockSpec((G, T, D_MODEL), lambda b: (b, 0, 0)),
        pl.BlockSpec((G, None, T, NA_DIM), lambda b: (b, layer, 0, 0)),
        pl.BlockSpec((G, None, T, NA_DIM), lambda b: (b, layer, 0, 0)),
        pl.BlockSpec((G, None, N_DIRS, ML_HEADS, ML_HEAD_DIM, ML_HEAD_DIM),
                     lambda b: (b, layer, 0, 0, 0, 0)),
        pl.BlockSpec((G, None, N_GATE, ML_HEAD_DIM), lambda b: (b, layer, 0, 0)),
        pl.BlockSpec((G, None, N_GATE, LANES), lambda b: (b, layer, 0, 0)),
    ]
    outs = pl.pallas_call(
        _ctx_mixer_kernel,
        grid=(B,),
        in_specs=in_specs,
        out_specs=out_specs,
        out_shape=[jax.ShapeDtypeStruct(x.shape, F32)] + state_shapes,
        input_output_aliases=aliases,
        compiler_params=pltpu.CompilerParams(vmem_limit_bytes=VMEM_LIMIT),
        name="ctx_mixer",
    )(*args)
    return outs[0], tuple(outs[1:])


def _route(lg):
    lane = lax.broadcasted_iota(jnp.int32, lg.shape, 1)
    lane_f = lane.astype(F32)
    big = float(LANES)
    is_g = (lane >= N_EXPERTS) & (lane < N_EXPERTS + N_GROUPS)
    glm = jnp.where(is_g, lg, NEG_INF)
    gmax = jnp.max(glm, axis=-1, keepdims=True)
    g_top = jnp.min(jnp.where(glm == gmax, lane_f, big), axis=-1, keepdims=True) - N_EXPERTS
    gp = 1.0 / jnp.sum(jnp.where(is_g, jnp.exp(lg - gmax), 0.0), axis=-1, keepdims=True)
    grp = (lane >> (EXPERTS_PER_GROUP.bit_length() - 1)).astype(F32)
    in_grp = (lane < N_EXPERTS) & (grp == g_top)
    e1 = jnp.where(in_grp, lg, NEG_INF)
    v1 = jnp.max(e1, axis=-1, keepdims=True)
    i1 = jnp.min(jnp.where(e1 == v1, lane_f, big), axis=-1, keepdims=True)
    e2 = jnp.where(in_grp & (lane_f != i1), lg, NEG_INF)
    v2 = jnp.max(e2, axis=-1, keepdims=True)
    i2 = jnp.min(jnp.where(e2 == v2, lane_f, big), axis=-1, keepdims=True)
    t = jnp.exp(v2 - v1)
    w1 = 1.0 / (1.0 + t)
    w2 = t / (1.0 + t)
    return i1, i2, gp * w1, gp * w2


SLAB = D_MODEL // LANES
DSP_SB = 2048
DSP_TM = 160
DSP_CH = 512
DSP_PMAX = 2 * DSP_SB + N_EXPERTS * DSP_TM
DSP_NULL = DSP_PMAX + 2 * DSP_TM
DSP_SLOTS = DSP_NULL + 2 * DSP_TM
DSP_UNROLL = 8


def _to_slabs(ref, x):
    n = x.shape[0]
    for s in range(SLAB):
        ref[pl.ds(s, n, stride=SLAB), :] = x[:, s * LANES:(s + 1) * LANES]


def _from_slabs(ref, n):
    return jnp.concatenate([ref[pl.ds(s, n, stride=SLAB), :] for s in range(SLAB)], axis=-1)


def _dsp_mods(mods_ref, chunk, chunks_per_row, first):
    row = 0 if chunks_per_row is None else 1 + chunk // chunks_per_row
    return _mod_rows(mods_ref, row, first)


def _dsp_route_kernel(x_ref, mods_ref, ln_ref, wr_ref, br_ref, h_ref, r_ref, cnt_ref, run_scr, *,
                      chunks_per_row):
    c = pl.program_id(0)
    ch = DSP_CH

    @pl.when(c % (DSP_SB // ch) == 0)
    def _():
        run_scr[...] = jnp.zeros_like(run_scr)

    sh2, sc2, _ = _dsp_mods(mods_ref, c, chunks_per_row, 3)
    h2 = _rms_mod(x_ref[...], ln_ref[...], sc2, sh2)
    _to_slabs(h_ref, h2)
    lg = jnp.dot(h2, wr_ref[...], precision=HIGHEST, preferred_element_type=F32) + br_ref[...]
    i1, i2, w1, w2 = _route(lg)
    lane = lax.broadcasted_iota(jnp.int32, lg.shape, 1)
    lane_f = lane.astype(F32)
    oh1 = jnp.where(lane_f == i1, 1.0, 0.0)
    oh2 = jnp.where(lane_f == i2, 1.0, 0.0)
    both = oh1 + oh2
    r = lax.broadcasted_iota(jnp.int32, (ch, ch), 0)
    s = lax.broadcasted_iota(jnp.int32, (ch, ch), 1)
    earlier = jnp.where(s < r, 1.0, 0.0).astype(BF16)
    run = run_scr[0:1, :]
    before = _dot(earlier, both.astype(BF16)) + run
    rank1 = jnp.sum(before * oh1, axis=-1, keepdims=True)
    rank2 = jnp.sum(before * oh2, axis=-1, keepdims=True)
    run = run + jnp.sum(both, axis=0, keepdims=True)
    run_scr[0:1, :] = run
    cnt_ref[...] = jnp.broadcast_to(run, cnt_ref.shape)
    cols = (i1, i2, w1, w2, rank1, rank2)
    out = jnp.zeros(lg.shape, F32)
    for j, col in enumerate(cols):
        out = jnp.where(lane == j, col, out)
    r_ref[...] = out


def _dsp_route_call(layer, x2d, mods, ln2, w_router, b_router, chunks_per_row):
    n = x2d.shape[0]
    ch = DSP_CH
    per_sb = DSP_SB // ch
    return pl.pallas_call(
        functools.partial(_dsp_route_kernel, chunks_per_row=chunks_per_row),
        grid=(n // ch,),
        in_specs=[
            pl.BlockSpec((ch, D_MODEL), lambda c: (c, 0)),
            pl.BlockSpec((None, MODS_ROWS, N_MOD * D_MODEL), lambda c: (layer, 0, 0)),
            pl.BlockSpec((None, 1, D_MODEL), lambda c: (layer, 0, 0)),
            pl.BlockSpec((None, D_MODEL, LANES), lambda c: (layer, 0, 0)),
            pl.BlockSpec((None, 1, LANES), lambda c: (layer, 0, 0)),
        ],
        out_specs=[pl.BlockSpec((ch * SLAB, LANES), lambda c: (c, 0)),
                   pl.BlockSpec((ch, LANES), lambda c: (c, 0)),
                   pl.BlockSpec((None, SLAB, LANES), lambda c: (c // per_sb, 0, 0))],
        out_shape=[jax.ShapeDtypeStruct((n * SLAB, LANES), F32),
                   jax.ShapeDtypeStruct((n, LANES), F32),
                   jax.ShapeDtypeStruct((n // DSP_SB, SLAB, LANES), F32)],
        scratch_shapes=[pltpu.VMEM((SLAB, LANES), F32)],
        compiler_params=pltpu.CompilerParams(vmem_limit_bytes=VMEM_LIMIT),
        name="moe_route",
    )(x2d, mods, ln2, w_router, b_router)


def _dispatch_tables(route, counts, n_sb):
    e = route[:, 0:2].astype(jnp.int32).reshape(n_sb, 2 * DSP_SB)
    rank = route[:, 4:6].astype(jnp.int32).reshape(n_sb, 2 * DSP_SB)
    w = route[:, 2:4].reshape(n_sb, 1, 2 * DSP_SB)
    cnt = counts[:, 0, :N_EXPERTS].astype(jnp.int32)
    ntile = (cnt + (DSP_TM - 1)) // DSP_TM
    off = (jnp.cumsum(ntile, axis=1) - ntile) * DSP_TM
    is_e = e[:, None, :] == jnp.arange(N_EXPERTS, dtype=jnp.int32)[None, :, None]
    pos = (jnp.sum(jnp.where(is_e, off[:, :, None], 0), axis=1) + rank).reshape(n_sb, 1, 2 * DSP_SB)
    return pos, w, ntile.reshape(-1), off.reshape(-1), cnt.reshape(-1)


def _dsp_expert_kernel(ntile_ref, off_ref, cnt_ref, pos_ref, w_ref, h_ref, wg_ref, wu_ref, wd_ref,
                       o_ref, row_smem, wgt_smem, sched_smem, pend_smem, xs0, xs1, ys0, ys1,
                       wg_scr, wu_scr, wd_scr):
    sb = pl.program_id(0)
    e = pl.program_id(1)
    tm, u = DSP_TM, DSP_UNROLL
    xs_scr, ys_scr = (xs0, xs1), (ys0, ys1)
    slab_shift = SLAB.bit_length() - 1
    spare_row = DSP_SB * SLAB
    last_row = (DSP_SB - 1) * SLAB

    @pl.when(e == 0)
    def _():
        for ee in range(N_EXPERTS):
            q = sb * N_EXPERTS + ee

            def pad(p, carry):
                row_smem[p] = spare_row
                wgt_smem[p] = 0.0
                return carry

            lax.fori_loop(off_ref[q] + cnt_ref[q], off_ref[q] + ntile_ref[q] * tm, pad, 0)

        def invert(j, carry):
            for k in range(u):
                a = j * u + k
                p = pos_ref[0, a]
                row_smem[p] = lax.shift_left(lax.shift_right_logical(a, 1), slab_shift)
                wgt_smem[p] = w_ref[0, a]
            return carry

        lax.fori_loop(0, 2 * DSP_SB // u, invert, 0)

        last_q = sb * N_EXPERTS + (N_EXPERTS - 1)
        used_end = off_ref[last_q] + ntile_ref[last_q] * tm

        def pad_end(p, carry):
            row_smem[used_end + p] = spare_row
            wgt_smem[used_end + p] = 0.0
            row_smem[DSP_NULL + p] = spare_row
            wgt_smem[DSP_NULL + p] = 0.0
            return carry

        lax.fori_loop(0, 2 * tm, pad_end, 0)
        sched_smem[0] = -1
        sched_smem[1] = DSP_NULL
        pend_smem[0] = 0.0
        o_ref[...] = jnp.zeros_like(o_ref)
        for ys in ys_scr:
            ys[...] = jnp.zeros_like(ys)

    q = sb * N_EXPERTS + e
    base0 = off_ref[q]
    nt = ntile_ref[q]
    wg_scr[...] = wg_ref[...].astype(BF16)
    wu_scr[...] = wu_ref[...].astype(BF16)
    wd_scr[...] = wd_ref[...].astype(BF16)

    def gather(base, buf):
        for r in range(tm):
            src = jnp.minimum(row_smem[base + r], last_row)
            xs_scr[buf][r * SLAB:(r + 1) * SLAB, :] = h_ref[pl.ds(pl.multiple_of(src, SLAB), SLAB), :]

    def compute(buf):
        x = _from_slabs(xs_scr[buf], tm).astype(BF16)
        a = jax.nn.silu(_dot(x, wg_scr[...])) * _dot(x, wu_scr[...])
        _to_slabs(ys_scr[buf], _dot(a.astype(BF16), wd_scr[...]))

    def scatter(base, buf, scale):
        for j in range(tm // u):
            upd = []
            for k in range(u):
                r = j * u + k
                dst = pl.multiple_of(row_smem[base + r], SLAB)
                y = ys_scr[buf][r * SLAB:(r + 1) * SLAB, :]
                upd.append((dst, o_ref[pl.ds(dst, SLAB), :] + (wgt_smem[base + r] * scale) * y))
            for dst, v in upd:
                o_ref[pl.ds(dst, SLAB), :] = v

    n_pairs = lax.shift_right_logical(nt + 1, 1)

    @pl.when((n_pairs > 0) & (sched_smem[0] != base0))
    def _():
        gather(base0, 0)

    pend_base = sched_smem[1]
    pend_scale = pend_smem[0]

    def pair(k, carry):
        b = base0 + 2 * k * tm
        first = k == 0
        gather(b + tm, 1)
        compute(0)
        scatter(jnp.where(first, pend_base, b - tm), 1, jnp.where(first, pend_scale, 1.0))
        gather(b + 2 * tm, 0)
        compute(1)
        scatter(b, 0, 1.0)
        return carry

    lax.fori_loop(0, n_pairs, pair, 0)

    @pl.when(n_pairs > 0)
    def _():
        last = 2 * n_pairs - 1
        sched_smem[0] = base0 + (last + 1) * tm
        sched_smem[1] = base0 + last * tm
        pend_smem[0] = jnp.where(last < nt, 1.0, 0.0)

    @pl.when(e == N_EXPERTS - 1)
    def _():
        scatter(sched_smem[1], 1, pend_smem[0])


def _dsp_expert_call(layer, h_slabs, pos, w, ntile, off, cnt, w_gate, w_up, w_down):
    n_sb = pos.shape[0]
    out_rows = (DSP_SB + DSP_CH) * SLAB

    def per_sb(s, e, *prefetch):
        return (s, 0, 0)

    def w_idx(s, e, *prefetch):
        return (layer, e, 0, 0)

    grid_spec = pltpu.PrefetchScalarGridSpec(
        num_scalar_prefetch=3,
        grid=(n_sb, N_EXPERTS),
        in_specs=[
            pl.BlockSpec((None, 1, 2 * DSP_SB), per_sb, memory_space=pltpu.SMEM),
            pl.BlockSpec((None, 1, 2 * DSP_SB), per_sb, memory_space=pltpu.SMEM),
            pl.BlockSpec((DSP_SB * SLAB, LANES), lambda s, e, *prefetch: (s, 0)),
            pl.BlockSpec((None, None, D_MODEL, EXPERT_FF), w_idx),
            pl.BlockSpec((None, None, D_MODEL, EXPERT_FF), w_idx),
            pl.BlockSpec((None, None, EXPERT_FF, D_MODEL), w_idx),
        ],
        out_specs=pl.BlockSpec((None, out_rows, LANES), per_sb),
        scratch_shapes=[pltpu.SMEM((DSP_SLOTS,), jnp.int32),
                        pltpu.SMEM((DSP_SLOTS,), F32),
                        pltpu.SMEM((2,), jnp.int32),
                        pltpu.SMEM((1,), F32),
                        pltpu.VMEM((DSP_TM * SLAB, LANES), F32),
                        pltpu.VMEM((DSP_TM * SLAB, LANES), F32),
                        pltpu.VMEM((DSP_TM * SLAB, LANES), F32),
                        pltpu.VMEM((DSP_TM * SLAB, LANES), F32),
                        pltpu.VMEM((D_MODEL, EXPERT_FF), BF16),
                        pltpu.VMEM((D_MODEL, EXPERT_FF), BF16),
                        pltpu.VMEM((EXPERT_FF, D_MODEL), BF16)],
    )
    return pl.pallas_call(
        _dsp_expert_kernel,
        grid_spec=grid_spec,
        out_shape=jax.ShapeDtypeStruct((n_sb, out_rows, LANES), F32),
        compiler_params=pltpu.CompilerParams(vmem_limit_bytes=VMEM_LIMIT),
        name="moe_experts",
    )(ntile, off, cnt, pos, w, h_slabs, w_gate, w_up, w_down)


def _dsp_combine_kernel(x_ref, m_ref, mods_ref, fg_ref, o_ref, *, chunks_per_row, final):
    g2 = _dsp_mods(mods_ref, pl.program_id(0), chunks_per_row, 3)[2]
    y = x_ref[...] + g2 * _from_slabs(m_ref, DSP_CH)
    if final:
        y = y * lax.rsqrt(jnp.mean(y * y, axis=-1, keepdims=True) + EPS) * fg_ref[...]
    o_ref[...] = y


def _dsp_combine_call(layer, x2d, m_slabs, mods, final_g, chunks_per_row, final):
    n = x2d.shape[0]
    ch = DSP_CH
    per_sb = DSP_SB // ch
    return pl.pallas_call(
        functools.partial(_dsp_combine_kernel, chunks_per_row=chunks_per_row, final=final),
        grid=(n // ch,),
        in_specs=[
            pl.BlockSpec((ch, D_MODEL), lambda c: (c, 0)),
            pl.BlockSpec((None, ch * SLAB, LANES), lambda c: (c // per_sb, c % per_sb, 0)),
            pl.BlockSpec((None, MODS_ROWS, N_MOD * D_MODEL), lambda c: (layer, 0, 0)),
            pl.BlockSpec((1, D_MODEL), lambda c: (0, 0)),
        ],
        out_specs=pl.BlockSpec((ch, D_MODEL), lambda c: (c, 0)),
        out_shape=jax.ShapeDtypeStruct(x2d.shape, F32),
        compiler_params=pltpu.CompilerParams(vmem_limit_bytes=VMEM_LIMIT),
        name="moe_combine",
    )(x2d, m_slabs, mods, final_g)


def _sparse_moe(layer, x2d, mods, ln2, w_router, b_router, w_gate, w_up, w_down, final_g, *,
                chunks_per_row, final):
    n_sb = x2d.shape[0] // DSP_SB
    h_slabs, route, counts = _dsp_route_call(layer, x2d, mods, ln2, w_router, b_router,
                                             chunks_per_row)
    pos, w, ntile, off, cnt = _dispatch_tables(route, counts, n_sb)
    m_slabs = _dsp_expert_call(layer, h_slabs, pos, w, ntile, off, cnt, w_gate, w_up, w_down)
    return _dsp_combine_call(layer, x2d, m_slabs, mods, final_g, chunks_per_row, final)


LAT_PROJ_ROWS = 512


def _lat_proj_kernel(x_ref, mods_ref, ln_ref, win_ref, gb_ref, zc_ref, zn_ref, zm_ref, zg_ref):
    row = 1 + pl.program_id(0) // (DEC_SEQ // LAT_PROJ_ROWS)
    sh1, sc1, _ = _mod_rows(mods_ref, row, 0)
    h = _rms_mod(x_ref[...], ln_ref[...], sc1, sh1).astype(BF16)
    zc_ref[...] = _dot(h, win_ref[:, C_CONV:C_NA])
    zn_ref[...] = _dot(h, win_ref[:, C_NA:C_ML])
    zm_ref[...] = _dot(h, win_ref[:, C_ML:C_IG])
    zg_ref[...] = _dot(h, win_ref[:, C_IG:W_COLS]) + gb_ref[...]


def _lat_proj_call(layer, x2d, mods, ln1, w_in, gate_b):
    n = x2d.shape[0]
    tb = LAT_PROJ_ROWS
    widths = (C_NA - C_CONV, C_ML - C_NA, C_IG - C_ML, W_COLS - C_IG)
    return pl.pallas_call(
        _lat_proj_kernel,
        grid=(n // tb,),
        in_specs=[
            pl.BlockSpec((tb, D_MODEL), lambda i: (i, 0)),
            pl.BlockSpec((None, MODS_ROWS, N_MOD * D_MODEL), lambda i: (layer, 0, 0)),
            pl.BlockSpec((None, 1, D_MODEL), lambda i: (layer, 0, 0)),
            pl.BlockSpec((None, D_MODEL, W_COLS), lambda i: (layer, 0, 0)),
            pl.BlockSpec((None, 1, 2 * LANES), lambda i: (layer, 0, 0)),
        ],
        out_specs=[pl.BlockSpec((tb, w), lambda i: (i, 0)) for w in widths],
        out_shape=[jax.ShapeDtypeStruct((n, w), F32) for w in widths],
        compiler_params=pltpu.CompilerParams(vmem_limit_bytes=VMEM_LIMIT),
        name="lat_proj",
    )(x2d, mods, ln1, w_in, gate_b)


def _na_row_start(r):
    rows = DEC_SEQ // GRID_W
    return min(max(r - NA_WIN_ROWS // 2, 0), rows - NA_WIN_ROWS)


def _lat_local_kernel(zc_ref, zn_ref, kc_ref, vc_ref, band_ref, convw_ref, o_ref):
    o_ref[:, 0:CONV_DIM] = _short_conv(zc_ref[...], convw_ref[...])
    rows = DEC_SEQ // GRID_W
    blk = NA_WIN_ROWS * GRID_W
    q_all = (zn_ref[:, 0:NA_DIM] * NA_HEAD_DIM ** -0.5).astype(BF16)
    k_all = zn_ref[:, NA_DIM:2 * NA_DIM].astype(BF16)
    v_all = zn_ref[:, 2 * NA_DIM:3 * NA_DIM].astype(BF16)
    kc_all = kc_ref[...].astype(BF16)
    vc_all = vc_ref[...].astype(BF16)
    heads = []
    for h in range(NA_HEADS):
        sl = slice(h * NA_HEAD_DIM, (h + 1) * NA_HEAD_DIM)
        q, k, v, kc, vc = q_all[:, sl], k_all[:, sl], v_all[:, sl], kc_all[:, sl], vc_all[:, sl]
        band = band_ref[h]
        outs = []
        for r in range(rows):
            start = _na_row_start(r)
            off = (start - r + NA_WIN_ROWS - 1) * GRID_W
            q_r = q[r * GRID_W:(r + 1) * GRID_W]
            s_loc = _dot_nt(q_r, k[start * GRID_W:start * GRID_W + blk]) + band[:, off:off + blk]
            s_ctx = _dot_nt(q_r, kc)
            m = jnp.maximum(jnp.max(s_loc, axis=-1, keepdims=True), jnp.max(s_ctx, axis=-1, keepdims=True))
            p_loc = jnp.exp(s_loc - m)
            p_ctx = jnp.exp(s_ctx - m)
            den = jnp.sum(p_loc, axis=-1, keepdims=True) + jnp.sum(p_ctx, axis=-1, keepdims=True)
            o = _dot(p_loc.astype(BF16), v[start * GRID_W:start * GRID_W + blk]) + _dot(p_ctx.astype(BF16), vc)
            outs.append(o / den)
        heads.append(jnp.concatenate(outs, axis=0))
    o_ref[:, CONV_DIM:CONV_DIM + NA_DIM] = jnp.concatenate(heads, axis=-1)


def _lat_local_call(layer, zc, zn, cache_k, cache_v, band, conv_w):
    nb = zc.shape[0] // DEC_SEQ
    T = DEC_SEQ
    return pl.pallas_call(
        _lat_local_kernel,
        grid=(nb,),
        in_specs=[
            pl.BlockSpec((T, 3 * CONV_DIM), lambda b: (b, 0)),
            pl.BlockSpec((T, 3 * NA_DIM), lambda b: (b, 0)),
            pl.BlockSpec((None, None, PAST_LEN, NA_DIM), lambda b: (b, layer, 0, 0)),
            pl.BlockSpec((None, None, PAST_LEN, NA_DIM), lambda b: (b, layer, 0, 0)),
            pl.BlockSpec((None, NA_HEADS, GRID_W, band.shape[-1]), lambda b: (layer, 0, 0, 0)),
            pl.BlockSpec((None, 3, CONV_DIM), lambda b: (layer, 0, 0)),
        ],
        out_specs=pl.BlockSpec((T, CONV_DIM + NA_DIM), lambda b: (b, 0)),
        out_shape=jax.ShapeDtypeStruct((zc.shape[0], CONV_DIM + NA_DIM), F32),
        compiler_params=pltpu.CompilerParams(vmem_limit_bytes=VMEM_LIMIT),
        name="lat_conv_na",
    )(zc, zn, cache_k, cache_v, band, conv_w)


def _rope(x, cos, sin_signed):
    w = x.shape[-1]
    half = NA_HEAD_DIM // 2
    lane = lax.broadcasted_iota(jnp.int32, x.shape, 1)
    partner = jnp.where(lane % (2 * half) < half, pltpu.roll(x, w - half, axis=1), pltpu.roll(x, half, axis=1))
    return x * cos + partner * sin_signed


def _lat_mlstm_kernel(q_ref, k_ref, v_ref, o_ref, zg_ref, c0_ref, n0_ref, m0_ref, cos_ref, sin_ref,
                      mlg_ref, out_ref, bcol_scr, rowt_scr):
    hh = pl.program_id(1)

    @pl.when(hh == 0)
    def _():
        b_all, r_all = _gate_terms(zg_ref[:, 0:LANES], zg_ref[:, LANES:2 * LANES])
        bcol_scr[...] = b_all
        rowt_scr[...] = r_all

    b_col = bcol_scr[...]
    cos, sin_signed = cos_ref[...], sin_ref[...]
    q = _rope(q_ref[...], cos, sin_signed)
    k = _rope(k_ref[...], cos, sin_signed) * ML_HEAD_DIM ** -0.5
    v = v_ref[...]
    lane = lax.broadcasted_iota(jnp.int32, (DEC_SEQ, LANES), 1)
    hsum = None
    for d in range(N_DIRS):
        j = d * ML_HEADS + hh
        bc = jnp.sum(jnp.where(lane == j, b_col, 0.0), axis=-1, keepdims=True)
        lr = rowt_scr[pl.ds(j, 1), :]
        m0 = m0_ref[d, :, 0:1]
        ho = _mlstm_outputs(q, k, v, bc, lr, m0, c0_ref[d], n0_ref[d], d == 1, 256)
        hsum = ho if hsum is None else hsum + ho
    out_ref[...] = _head_readout(hsum, o_ref[...], mlg_ref[...])


def _lat_mlstm_call(layer, zm, zg, state_c, state_n, state_m, cos, sin_signed, ml_g):
    nb = zm.shape[0] // DEC_SEQ
    T = DEC_SEQ
    hd = ML_HEAD_DIM

    def col(part):
        return pl.BlockSpec((T, hd), lambda b, h: (b, part * ML_HEADS + h))

    return pl.pallas_call(
        _lat_mlstm_kernel,
        grid=(nb, ML_HEADS),
        in_specs=[
            col(0), col(1), col(2), col(3),
            pl.BlockSpec((T, 2 * LANES), lambda b, h: (b, 0)),
            pl.BlockSpec((None, None, N_DIRS, None, hd, hd), lambda b, h: (b, layer, 0, h, 0, 0)),
            pl.BlockSpec((None, None, N_DIRS, None, 1, hd), lambda b, h: (b, layer, 0, h, 0, 0)),
            pl.BlockSpec((None, None, N_DIRS, None, 1, LANES), lambda b, h: (b, layer, 0, h, 0, 0)),
            pl.BlockSpec((T, hd), lambda b, h: (0, 0)),
            pl.BlockSpec((T, hd), lambda b, h: (0, 0)),
            pl.BlockSpec((None, 1, hd), lambda b, h: (layer, 0, h)),
        ],
        out_specs=pl.BlockSpec((T, hd), lambda b, h: (b, h)),
        out_shape=jax.ShapeDtypeStruct((zm.shape[0], ML_DIM), F32),
        scratch_shapes=[pltpu.VMEM((T, LANES), F32), pltpu.VMEM((LANES, T), F32)],
        compiler_params=pltpu.CompilerParams(vmem_limit_bytes=VMEM_LIMIT),
        name="lat_mlstm",
    )(zm, zm, zm, zm, zg, state_c, state_n, state_m, cos, sin_signed, ml_g)


def _lat_merge_kernel(x_ref, loc_ref, ml_ref, mods_ref, wout_ref, o_ref):
    row = 1 + pl.program_id(0) // (DEC_SEQ // LAT_PROJ_ROWS)
    g1 = _mod_rows(mods_ref, row, 0)[2]
    split = CONV_DIM + NA_DIM
    y = _dot(loc_ref[...].astype(BF16), wout_ref[0:split, :]) + _dot(ml_ref[...].astype(BF16), wout_ref[split:, :])
    o_ref[...] = x_ref[...] + g1 * y


def _lat_merge_call(layer, x2d, loc, ml, mods, w_out):
    n = x2d.shape[0]
    tb = LAT_PROJ_ROWS
    return pl.pallas_call(
        _lat_merge_kernel,
        grid=(n // tb,),
        in_specs=[
            pl.BlockSpec((tb, D_MODEL), lambda i: (i, 0)),
            pl.BlockSpec((tb, CONV_DIM + NA_DIM), lambda i: (i, 0)),
            pl.BlockSpec((tb, ML_DIM), lambda i: (i, 0)),
            pl.BlockSpec((None, MODS_ROWS, N_MOD * D_MODEL), lambda i: (layer, 0, 0)),
            pl.BlockSpec((None, D_MODEL, D_MODEL), lambda i: (layer, 0, 0)),
        ],
        out_specs=pl.BlockSpec((tb, D_MODEL), lambda i: (i, 0)),
        out_shape=jax.ShapeDtypeStruct(x2d.shape, F32),
        compiler_params=pltpu.CompilerParams(vmem_limit_bytes=VMEM_LIMIT),
        name="lat_merge",
    )(x2d, loc, ml, mods, w_out)


def _pad_lanes(a, width):
    return jnp.pad(a, [(0, 0)] * (a.ndim - 1) + [(0, width - a.shape[-1])])


def _pack_w_in(w_in):
    main = w_in[..., :C_IG]
    gates = w_in[..., C_IG:]
    packed = jnp.concatenate([main, _pad_lanes(gates[..., :N_GATE], LANES),
                              _pad_lanes(gates[..., N_GATE:], LANES)], axis=-1)
    return packed.astype(BF16)


def _pack_gate_bias(ml_gate_b):
    gb = ml_gate_b.reshape(DEPTH, 2, N_GATE).astype(F32)
    return jnp.concatenate([_pad_lanes(gb[:, 0], LANES), _pad_lanes(gb[:, 1], LANES)], axis=-1)[:, None, :]


def _rpb_band(na_rpb):
    cols = np.arange(GRID_W)
    col_idx = np.clip(cols[None, :] - cols[:, None] + NA_WIN_COLS - 1, 0, 2 * NA_WIN_COLS - 2)
    col_start = np.clip(cols - NA_WIN_COLS // 2, 0, GRID_W - NA_WIN_COLS)
    col_mask = (cols[None, :] >= col_start[:, None]) & (cols[None, :] < col_start[:, None] + NA_WIN_COLS)
    pick = (col_idx[None] == np.arange(2 * NA_WIN_COLS - 1)[:, None, None]).astype(np.float32)
    t = jnp.einsum('lhri,iqk->lhqrk', na_rpb.astype(F32), jnp.asarray(pick), precision=HIGHEST)
    t = jnp.where(col_mask[None, None, :, None, :], t, NEG_INF)
    t = t.reshape(DEPTH, NA_HEADS, GRID_W, (2 * NA_WIN_ROWS - 1) * GRID_W)
    return _pad_lanes(t, 2 * NA_WIN_ROWS * GRID_W)


def _rope_tables():
    t = np.arange(DEC_SEQ)
    pos = np.stack([t // GRID_W, t % GRID_W], axis=-1).astype(np.float32)
    nf = ML_HEAD_DIM // 4
    inv = jnp.asarray(ROPE_BASE, F32) ** (-jnp.arange(nf, dtype=F32) / nf)
    ang = jnp.asarray(pos)[:, :, None] * inv
    cos = jnp.cos(ang)
    sin = jnp.sin(ang)
    cos_t = jnp.concatenate([cos, cos], axis=-1).reshape(DEC_SEQ, ML_HEAD_DIM)
    sin_t = jnp.concatenate([-sin, sin], axis=-1).reshape(DEC_SEQ, ML_HEAD_DIM)
    return cos_t, sin_t


def kernel(x_prompt, x_sample, cache_k, cache_v, state_C, state_n, state_m, c, c_ctx, w_mod, b_mod,
           ln1_g, w_in, conv_w, na_rpb, ml_gate_b, ml_norm_g, w_out, ln2_g, w_rg, b_rg, w_re, b_re,
           w_gate, w_up, w_down, final_g):
    nb_ctx = x_prompt.shape[0]
    nb_lat = x_sample.shape[0]
    assert 1 + nb_lat <= MODS_ROWS

    cvecs = jnp.concatenate([c_ctx[None, :], c,
                             jnp.zeros((MODS_ROWS - 1 - nb_lat, D_MODEL), F32)], axis=0)
    mods = _mods_call(cvecs, w_mod, b_mod)

    w_in_p = _pack_w_in(w_in)
    w_out_b = w_out.astype(BF16)
    gate_b = _pack_gate_bias(ml_gate_b)
    ln1 = ln1_g.reshape(DEPTH, 1, D_MODEL)
    ln2 = ln2_g.reshape(DEPTH, 1, D_MODEL)
    ml_g = ml_norm_g.reshape(DEPTH, 1, ML_DIM)
    w_router = _pad_lanes(jnp.concatenate([w_re, w_rg], axis=-1), LANES)
    b_router = _pad_lanes(jnp.concatenate([b_re, b_rg], axis=-1), LANES)[:, None, :]
    fg = final_g.reshape(1, D_MODEL)
    band = _rpb_band(na_rpb)
    cos_t, sin_t = _rope_tables()
    ck = cache_k.reshape(nb_lat, DEPTH, PAST_LEN, NA_DIM)
    cv = cache_v.reshape(nb_lat, DEPTH, PAST_LEN, NA_DIM)
    st_n = state_n.reshape(nb_lat, DEPTH, N_DIRS, ML_HEADS, 1, ML_HEAD_DIM)
    st_m = jnp.broadcast_to(state_m[..., None, None], (nb_lat, DEPTH, N_DIRS, ML_HEADS, 1, LANES))

    xp = x_prompt
    xs = x_sample.reshape(nb_lat * DEC_SEQ, D_MODEL)
    states = None
    for l in range(DEPTH):
        last = l == DEPTH - 1
        xp, states = _ctx_mixer_call(l, xp, mods, ln1, w_in_p, conv_w, gate_b, ml_g, w_out_b, states)
        xp = _sparse_moe(l, xp.reshape(nb_ctx * SEQ, D_MODEL), mods, ln2, w_router, b_router,
                         w_gate, w_up, w_down, fg, chunks_per_row=None, final=last)
        xp = xp.reshape(nb_ctx, SEQ, D_MODEL)

        zc, zn, zm, zg = _lat_proj_call(l, xs, mods, ln1, w_in_p, gate_b)
        loc = _lat_local_call(l, zc, zn, ck, cv, band, conv_w)
        ml = _lat_mlstm_call(l, zm, zg, state_C, st_n, st_m, cos_t, sin_t, ml_g)
        xs = _lat_merge_call(l, xs, loc, ml, mods, w_out_b)
        xs = _sparse_moe(l, xs, mods, ln2, w_router, b_router, w_gate, w_up, w_down, fg,
                         chunks_per_row=DEC_SEQ // DSP_CH, final=last)

    new_k, new_v, new_c, new_n, new_m = states
    return (xp, xs.reshape(nb_lat, DEC_SEQ, D_MODEL),
            new_k.reshape(nb_ctx, DEPTH, SEQ, NA_HEADS, NA_HEAD_DIM),
            new_v.reshape(nb_ctx, DEPTH, SEQ, NA_HEADS, NA_HEAD_DIM),
            new_c,
            new_n.reshape(nb_ctx, DEPTH, N_DIRS, ML_HEADS, ML_HEAD_DIM),
            new_m[..., 0].reshape(nb_ctx, DEPTH, N_DIRS, ML_HEADS))
```

```python
import functools

import numpy as np
import jax
import jax.numpy as jnp
from jax import lax
from jax.experimental import pallas as pl
from jax.experimental.pallas import tpu as pltpu

F32 = jnp.float32
BF16 = jnp.bfloat16

D_MODEL = 1024
SEQ = 256
DEPTH = 4
DEC_SEQ = 1024
PAST_LEN = 512
GRID_W = 64
CONV_DIM = 256
NA_HEADS = 4
NA_HEAD_DIM = 64
NA_DIM = NA_HEADS * NA_HEAD_DIM
NA_WIN_ROWS = 8
NA_WIN_COLS = 16
ML_HEADS = 4
ML_HEAD_DIM = 128
ML_DIM = ML_HEADS * ML_HEAD_DIM
N_DIRS = 2
N_GROUPS = 4
EXPERTS_PER_GROUP = 4
N_EXPERTS = N_GROUPS * EXPERTS_PER_GROUP
EXPERT_FF = 512
ROPE_BASE = 10000.0
EPS = 1e-6
N_MOD = 6

LANES = 128
N_GATE = N_DIRS * ML_HEADS
C_CONV = 0
C_NA = 3 * CONV_DIM
C_ML = C_NA + 3 * NA_DIM
C_IG = C_ML + 4 * ML_DIM
C_FG = C_IG + LANES
W_COLS = C_FG + LANES
CUMSUM_BLOCK = 256
MODS_ROWS = 8
VMEM_LIMIT = 56 * 1024 * 1024

NEG_INF = float("-inf")
HIGHEST = lax.Precision.HIGHEST


def _dot(a, b):
    return jnp.dot(a, b, preferred_element_type=F32)


def _dot_nt(a, b):
    return lax.dot_general(a, b, (((1,), (1,)), ((), ())), preferred_element_type=F32)


def _rms_mod(x, g, sc, sh):
    y = x * lax.rsqrt(jnp.mean(x * x, axis=-1, keepdims=True) + EPS)
    return (y * g) * (1.0 + sc) + sh


def _mod_rows(mods_ref, row, first):
    return [mods_ref[pl.ds(row, 1), pl.ds((first + j) * D_MODEL, D_MODEL)] for j in range(3)]


def _mods_kernel(cv_ref, w_ref, b_ref, o_ref):
    cv = cv_ref[...]
    s = cv * jax.nn.sigmoid(cv)
    o_ref[...] = _dot(s.astype(BF16), w_ref[...].astype(BF16)) + b_ref[...]


def _mods_call(cvecs, w_mod, b_mod):
    tn = 1536
    n = N_MOD * D_MODEL
    return pl.pallas_call(
        _mods_kernel,
        grid=(DEPTH, n // tn),
        in_specs=[pl.BlockSpec((MODS_ROWS, D_MODEL), lambda l, j: (0, 0)),
                  pl.BlockSpec((None, D_MODEL, tn), lambda l, j: (l, 0, j)),
                  pl.BlockSpec((None, 1, tn), lambda l, j: (l, 0, j))],
        out_specs=pl.BlockSpec((None, MODS_ROWS, tn), lambda l, j: (l, 0, j)),
        out_shape=jax.ShapeDtypeStruct((DEPTH, MODS_ROWS, n), F32),
        compiler_params=pltpu.CompilerParams(vmem_limit_bytes=VMEM_LIMIT),
        name="adaln_mods",
    )(cvecs, w_mod, b_mod.reshape(DEPTH, 1, n))


def _short_conv(zc, cw):
    T = zc.shape[0]
    cb = zc[:, 0:CONV_DIM]
    u = zc[:, CONV_DIM:2 * CONV_DIM] * zc[:, 2 * CONV_DIM:3 * CONV_DIM]
    t = lax.broadcasted_iota(jnp.int32, u.shape, 0)
    u_prev = jnp.where(t == 0, 0.0, pltpu.roll(u, 1, axis=0))
    u_next = jnp.where(t == T - 1, 0.0, pltpu.roll(u, T - 1, axis=0))
    return cb * (cw[0:1, :] * u_prev + cw[1:2, :] * u + cw[2:3, :] * u_next)


def _softmax_attention(nq, nk, nv):
    outs = []
    for h in range(NA_HEADS):
        sl = slice(h * NA_HEAD_DIM, (h + 1) * NA_HEAD_DIM)
        q = (nq[:, sl] * NA_HEAD_DIM ** -0.5).astype(BF16)
        s = _dot_nt(q, nk[:, sl].astype(BF16))
        p = jnp.exp(s - jnp.max(s, axis=-1, keepdims=True))
        o = _dot(p.astype(BF16), nv[:, sl].astype(BF16)) / jnp.sum(p, axis=-1, keepdims=True)
        outs.append(o)
    return jnp.concatenate(outs, axis=-1)


def _log_sigmoid(x):
    return jnp.minimum(x, 0.0) - jnp.log(1.0 + jnp.exp(-jnp.abs(x)))


def _gate_terms(zi, zf):
    T = zi.shape[0]
    lf = _log_sigmoid(zf)
    blk = min(T, CUMSUM_BLOCK)
    r = lax.broadcasted_iota(jnp.int32, (blk, blk), 0)
    c = lax.broadcasted_iota(jnp.int32, (blk, blk), 1)
    tril = jnp.where(c <= r, 1.0, 0.0).astype(F32)
    parts, carry = [], None
    for r0 in range(0, T, blk):
        b = jnp.dot(tril, lf[r0:r0 + blk], precision=HIGHEST, preferred_element_type=F32)
        if carry is not None:
            b = b + carry
        carry = b[blk - 1:blk, :]
        parts.append(b)
    b_fwd = parts[0] if len(parts) == 1 else jnp.concatenate(parts, axis=0)
    b_bwd = (b_fwd[T - 1:T, :] - b_fwd) + lf
    lane = lax.broadcasted_iota(jnp.int32, zi.shape, 1)
    b_col = jnp.where(lane < ML_HEADS, b_fwd, b_bwd)
    row_t = jnp.transpose(zi - b_col)
    return b_col, row_t


def _mlstm_outputs(q, k, v, b_col, li_row, m0, c0, n0, backward, q_block):
    T = q.shape[0]
    qb, kb, vb = q.astype(BF16), k.astype(BF16), v.astype(BF16)
    c0b = None if c0 is None else c0.astype(BF16)
    outs = []
    for r0 in range(0, T, q_block):
        ks, ke = (r0, T) if backward else (0, r0 + q_block)
        bq = b_col[r0:r0 + q_block]
        d = bq + li_row[:, ks:ke]
        t_idx = r0 + lax.broadcasted_iota(jnp.int32, d.shape, 0)
        s_idx = ks + lax.broadcasted_iota(jnp.int32, d.shape, 1)
        d = jnp.where((s_idx >= t_idx) if backward else (s_idx <= t_idx), d, NEG_INF)
        inter = bq + m0
        m_t = jnp.maximum(jnp.max(d, axis=-1, keepdims=True), inter)
        w = jnp.exp(d - m_t)
        s = _dot_nt(qb[r0:r0 + q_block], kb[ks:ke]) * w
        num = _dot(s.astype(BF16), vb[ks:ke])
        den = jnp.sum(s, axis=-1, keepdims=True)
        if c0 is not None:
            a = jnp.exp(inter - m_t)
            num = num + a * _dot(qb[r0:r0 + q_block], c0b)
            den = den + a * jnp.sum(q[r0:r0 + q_block] * n0, axis=-1, keepdims=True)
        outs.append(num / jnp.maximum(jnp.abs(den), jnp.exp(-m_t)))
    return outs[0] if len(outs) == 1 else jnp.concatenate(outs, axis=0)


def _mlstm_state(k, v, b_col, li_col, backward):
    T = k.shape[0]
    b_tot = b_col[0:1] if backward else b_col[T - 1:T]
    g = (b_tot - b_col) + li_col
    m_new = jnp.maximum(b_tot, jnp.max(g, axis=0, keepdims=True))
    kw = k * jnp.exp(g - m_new)
    c_new = _dot(jnp.transpose(kw).astype(BF16), v.astype(BF16))
    n_new = jnp.sum(kw, axis=0, keepdims=True)
    return c_new, n_new, m_new


def _head_readout(h, mo, g):
    hn = h * lax.rsqrt(jnp.mean(h * h, axis=-1, keepdims=True) + EPS)
    return hn * g * jax.nn.sigmoid(mo)


def _ctx_mixer_kernel(x_ref, mods_ref, ln_ref, win_ref, convw_ref, gb_ref, mlg_ref, wout_ref, *rest):
    xo_ref, ko_ref, vo_ref, co_ref, no_ref, mo_ref = rest[-6:]
    T = SEQ
    sh1, sc1, g1 = _mod_rows(mods_ref, 0, 0)
    x = x_ref[...]
    h = _rms_mod(x, ln_ref[...], sc1, sh1).astype(BF16)

    conv_o = _short_conv(_dot(h, win_ref[:, C_CONV:C_NA]), convw_ref[...])

    zn = _dot(h, win_ref[:, C_NA:C_ML])
    nk, nv = zn[:, NA_DIM:2 * NA_DIM], zn[:, 2 * NA_DIM:3 * NA_DIM]
    ko_ref[...] = nk
    vo_ref[...] = nv
    na_o = _softmax_attention(zn[:, 0:NA_DIM], nk, nv)

    zg = _dot(h, win_ref[:, C_IG:W_COLS])
    zi = zg[:, 0:LANES] + gb_ref[:, 0:LANES]
    zf = zg[:, LANES:2 * LANES] + gb_ref[:, LANES:2 * LANES]
    b_col, row_t = _gate_terms(zi, zf)
    m0 = jnp.zeros((1, 1), F32)

    zm = _dot(h, win_ref[:, C_ML:C_IG])
    ml_parts = []
    for hh in range(ML_HEADS):
        zq, zk, zv, zo = (zm[:, p * ML_DIM + hh * ML_HEAD_DIM:p * ML_DIM + (hh + 1) * ML_HEAD_DIM]
                          for p in range(4))
        zk = zk * ML_HEAD_DIM ** -0.5
        hsum = None
        for d in range(N_DIRS):
            j = d * ML_HEADS + hh
            bc = b_col[:, j:j + 1]
            ho = _mlstm_outputs(zq, zk, zv, bc, row_t[j:j + 1, :], m0, None, None, d == 1, T)
            hsum = ho if hsum is None else hsum + ho
            c_new, n_new, m_new = _mlstm_state(zk, zv, bc, zi[:, j:j + 1], d == 1)
            co_ref[d, hh] = c_new
            no_ref[j:j + 1, :] = n_new
            mo_ref[j:j + 1, :] = jnp.broadcast_to(m_new, (1, LANES))
        ml_parts.append(_head_readout(hsum, zo, mlg_ref[:, hh * ML_HEAD_DIM:(hh + 1) * ML_HEAD_DIM]))

    mix = jnp.concatenate([conv_o, na_o] + ml_parts, axis=-1).astype(BF16)
    xo_ref[...] = x + g1 * _dot(mix, wout_ref[...])


def _ctx_mixer_call(layer, x, mods, ln1, w_in, conv_w, gate_b, ml_g, w_out, prev):
    B = x.shape[0]
    T = SEQ
    G = None
    state_shapes = [
        jax.ShapeDtypeStruct((B, DEPTH, T, NA_DIM), F32),
        jax.ShapeDtypeStruct((B, DEPTH, T, NA_DIM), F32),
        jax.ShapeDtypeStruct((B, DEPTH, N_DIRS, ML_HEADS, ML_HEAD_DIM, ML_HEAD_DIM), F32),
        jax.ShapeDtypeStruct((B, DEPTH, N_GATE, ML_HEAD_DIM), F32),
        jax.ShapeDtypeStruct((B, DEPTH, N_GATE, LANES), F32),
    ]
    in_specs = [
        pl.BlockSpec((G, T, D_MODEL), lambda b: (b, 0, 0)),
        pl.BlockSpec((None, MODS_ROWS, N_MOD * D_MODEL), lambda b: (layer, 0, 0)),
        pl.BlockSpec((None, 1, D_MODEL), lambda b: (layer, 0, 0)),
        pl.BlockSpec((None, D_MODEL, W_COLS), lambda b: (layer, 0, 0)),
        pl.BlockSpec((None, 3, CONV_DIM), lambda b: (layer, 0, 0)),
        pl.BlockSpec((None, 1, 2 * LANES), lambda b: (layer, 0, 0)),
        pl.BlockSpec((None, 1, ML_DIM), lambda b: (layer, 0, 0)),
        pl.BlockSpec((None, D_MODEL, D_MODEL), lambda b: (layer, 0, 0)),
    ]
    args = [x, mods, ln1, w_in, conv_w, gate_b, ml_g, w_out]
    aliases = {}
    if prev is not None:
        in_specs += [pl.BlockSpec(memory_space=pl.ANY)] * len(prev)
        aliases = {len(args) + i: 1 + i for i in range(len(prev))}
        args += list(prev)
    out_specs = [
        pl.BlockSpec((G, T, D_MODEL), lambda b: (b, 0, 0)),
        pl.BlockSpec((G, None, T, NA_DIM), lambda b: (b, layer, 0, 0)),
        pl.BlockSpec((G, None, T, NA_DIM), lambda b: (b, layer, 0, 0)),
        pl.BlockSpec((G, None, N_DIRS, ML_HEADS, ML_HEAD_DIM, ML_HEAD_DIM),
                     lambda b: (b, layer, 0, 0, 0, 0)),
        pl.BlockSpec((G, None, N_GATE, ML_HEAD_DIM), lambda b: (b, layer, 0, 0)),
        pl.BlockSpec((G, None, N_GATE, LANES), lambda b: (b, layer, 0, 0)),
    ]
    outs = pl.pallas_call(
        _ctx_mixer_kernel,
        grid=(B,),
        in_specs=in_specs,
        out_specs=out_specs,
        out_shape=[jax.ShapeDtypeStruct(x.shape, F32)] + state_shapes,
        input_output_aliases=aliases,
        compiler_params=pltpu.CompilerParams(vmem_limit_bytes=VMEM_LIMIT),
        name="ctx_mixer",
    )(*args)
    return outs[0], tuple(outs[1:])


def _route(lg):
    lane = lax.broadcasted_iota(jnp.int32, lg.shape, 1)
    lane_f = lane.astype(F32)
    big = float(LANES)
    is_g = (lane >= N_EXPERTS) & (lane < N_EXPERTS + N_GROUPS)
    glm = jnp.where(is_g, lg, NEG_INF)
    gmax = jnp.max(glm, axis=-1, keepdims=True)
    g_top = jnp.min(jnp.where(glm == gmax, lane_f, big), axis=-1, keepdims=True) - N_EXPERTS
    gp = 1.0 / jnp.sum(jnp.where(is_g, jnp.exp(lg - gmax), 0.0), axis=-1, keepdims=True)
    grp = (lane >> (EXPERTS_PER_GROUP.bit_length() - 1)).astype(F32)
    in_grp = (lane < N_EXPERTS) & (grp == g_top)
    e1 = jnp.where(in_grp, lg, NEG_INF)
    v1 = jnp.max(e1, axis=-1, keepdims=True)
    i1 = jnp.min(jnp.where(e1 == v1, lane_f, big), axis=-1, keepdims=True)
    e2 = jnp.where(in_grp & (lane_f != i1), lg, NEG_INF)
    v2 = jnp.max(e2, axis=-1, keepdims=True)
    i2 = jnp.min(jnp.where(e2 == v2, lane_f, big), axis=-1, keepdims=True)
    t = jnp.exp(v2 - v1)
    w1 = 1.0 / (1.0 + t)
    w2 = t / (1.0 + t)
    return i1, i2, gp * w1, gp * w2


SLAB = D_MODEL // LANES
DSP_SB = 2048
DSP_TM = 160
DSP_CH = 512
DSP_PMAX = 2 * DSP_SB + N_EXPERTS * DSP_TM
DSP_NULL = DSP_PMAX + 2 * DSP_TM
DSP_SLOTS = DSP_NULL + 2 * DSP_TM
DSP_UNROLL = 8


def _to_slabs(ref, x):
    n = x.shape[0]
    for s in range(SLAB):
        ref[pl.ds(s, n, stride=SLAB), :] = x[:, s * LANES:(s + 1) * LANES]


def _from_slabs(ref, n):
    return jnp.concatenate([ref[pl.ds(s, n, stride=SLAB), :] for s in range(SLAB)], axis=-1)


def _dsp_mods(mods_ref, chunk, chunks_per_row, first):
    row = 0 if chunks_per_row is None else 1 + chunk // chunks_per_row
    return _mod_rows(mods_ref, row, first)


def _dsp_route_kernel(x_ref, mods_ref, ln_ref, wr_ref, br_ref, h_ref, r_ref, cnt_ref, run_scr, *,
                      chunks_per_row):
    c = pl.program_id(0)
    ch = DSP_CH

    @pl.when(c % (DSP_SB // ch) == 0)
    def _():
        run_scr[...] = jnp.zeros_like(run_scr)

    sh2, sc2, _ = _dsp_mods(mods_ref, c, chunks_per_row, 3)
    h2 = _rms_mod(x_ref[...], ln_ref[...], sc2, sh2)
    _to_slabs(h_ref, h2)
    lg = jnp.dot(h2, wr_ref[...], precision=HIGHEST, preferred_element_type=F32) + br_ref[...]
    i1, i2, w1, w2 = _route(lg)
    lane = lax.broadcasted_iota(jnp.int32, lg.shape, 1)
    lane_f = lane.astype(F32)
    oh1 = jnp.where(lane_f == i1, 1.0, 0.0)
    oh2 = jnp.where(lane_f == i2, 1.0, 0.0)
    both = oh1 + oh2
    r = lax.broadcasted_iota(jnp.int32, (ch, ch), 0)
    s = lax.broadcasted_iota(jnp.int32, (ch, ch), 1)
    earlier = jnp.where(s < r, 1.0, 0.0).astype(BF16)
    run = run_scr[0:1, :]
    before = _dot(earlier, both.astype(BF16)) + run
    rank1 = jnp.sum(before * oh1, axis=-1, keepdims=True)
    rank2 = jnp.sum(before * oh2, axis=-1, keepdims=True)
    run = run + jnp.sum(both, axis=0, keepdims=True)
    run_scr[0:1, :] = run
    cnt_ref[...] = jnp.broadcast_to(run, cnt_ref.shape)
    cols = (i1, i2, w1, w2, rank1, rank2)
    out = jnp.zeros(lg.shape, F32)
    for j, col in enumerate(cols):
        out = jnp.where(lane == j, col, out)
    r_ref[...] = out


def _dsp_route_call(layer, x2d, mods, ln2, w_router, b_router, chunks_per_row):
    n = x2d.shape[0]
    ch = DSP_CH
    per_sb = DSP_SB // ch
    return pl.pallas_call(
        functools.partial(_dsp_route_kernel, chunks_per_row=chunks_per_row),
        grid=(n // ch,),
        in_specs=[
            pl.BlockSpec((ch, D_MODEL), lambda c: (c, 0)),
            pl.BlockSpec((None, MODS_ROWS, N_MOD * D_MODEL), lambda c: (layer, 0, 0)),
            pl.BlockSpec((None, 1, D_MODEL), lambda c: (layer, 0, 0)),
            pl.BlockSpec((None, D_MODEL, LANES), lambda c: (layer, 0, 0)),
            pl.BlockSpec((None, 1, LANES), lambda c: (layer, 0, 0)),
        ],
        out_specs=[pl.BlockSpec((ch * SLAB, LANES), lambda c: (c, 0)),
                   pl.BlockSpec((ch, LANES), lambda c: (c, 0)),
                   pl.BlockSpec((None, SLAB, LANES), lambda c: (c // per_sb, 0, 0))],
        out_shape=[jax.ShapeDtypeStruct((n * SLAB, LANES), F32),
                   jax.ShapeDtypeStruct((n, LANES), F32),
                   jax.ShapeDtypeStruct((n // DSP_SB, SLAB, LANES), F32)],
        scratch_shapes=[pltpu.VMEM((SLAB, LANES), F32)],
        compiler_params=pltpu.CompilerParams(vmem_limit_bytes=VMEM_LIMIT),
        name="moe_route",
    )(x2d, mods, ln2, w_router, b_router)


def _dispatch_tables(route, counts, n_sb):
    e = route[:, 0:2].astype(jnp.int32).reshape(n_sb, 2 * DSP_SB)
    rank = route[:, 4:6].astype(jnp.int32).reshape(n_sb, 2 * DSP_SB)
    w = route[:, 2:4].reshape(n_sb, 1, 2 * DSP_SB)
    cnt = counts[:, 0, :N_EXPERTS].astype(jnp.int32)
    ntile = (cnt + (DSP_TM - 1)) // DSP_TM
    off = (jnp.cumsum(ntile, axis=1) - ntile) * DSP_TM
    is_e = e[:, None, :] == jnp.arange(N_EXPERTS, dtype=jnp.int32)[None, :, None]
    pos = (jnp.sum(jnp.where(is_e, off[:, :, None], 0), axis=1) + rank).reshape(n_sb, 1, 2 * DSP_SB)
    return pos, w, ntile.reshape(-1), off.reshape(-1), cnt.reshape(-1)


def _dsp_expert_kernel(ntile_ref, off_ref, cnt_ref, pos_ref, w_ref, h_ref, wg_ref, wu_ref, wd_ref,
                       o_ref, row_smem, wgt_smem, sched_smem, pend_smem, xs0, xs1, ys0, ys1):
    sb = pl.program_id(0)
    e = pl.program_id(1)
    tm, u = DSP_TM, DSP_UNROLL
    xs_scr, ys_scr = (xs0, xs1), (ys0, ys1)
    slab_shift = SLAB.bit_length() - 1
    spare_row = DSP_SB * SLAB
    last_row = (DSP_SB - 1) * SLAB

    @pl.when(e == 0)
    def _():
        for ee in range(N_EXPERTS):
            q = sb * N_EXPERTS + ee

            def pad(p, carry):
                row_smem[p] = spare_row
                wgt_smem[p] = 0.0
                return carry

            lax.fori_loop(off_ref[q] + cnt_ref[q], off_ref[q] + ntile_ref[q] * tm, pad, 0)

        def invert(j, carry):
            for k in range(u):
                a = j * u + k
                p = pos_ref[0, a]
                row_smem[p] = lax.shift_left(lax.shift_right_logical(a, 1), slab_shift)
                wgt_smem[p] = w_ref[0, a]
            return carry

        lax.fori_loop(0, 2 * DSP_SB // u, invert, 0)

        last_q = sb * N_EXPERTS + (N_EXPERTS - 1)
        used_end = off_ref[last_q] + ntile_ref[last_q] * tm

        def pad_end(p, carry):
            row_smem[used_end + p] = spare_row
            wgt_smem[used_end + p] = 0.0
            row_smem[DSP_NULL + p] = spare_row
            wgt_smem[DSP_NULL + p] = 0.0
            return carry

        lax.fori_loop(0, 2 * tm, pad_end, 0)
        sched_smem[0] = -1
        sched_smem[1] = DSP_NULL
        pend_smem[0] = 0.0
        o_ref[...] = jnp.zeros_like(o_ref)
        for ys in ys_scr:
            ys[...] = jnp.zeros_like(ys)

    q = sb * N_EXPERTS + e
    base0 = off_ref[q]
    nt = ntile_ref[q]

    def gather(base, buf):
        for r in range(tm):
            src = jnp.minimum(row_smem[base + r], last_row)
            xs_scr[buf][r * SLAB:(r + 1) * SLAB, :] = h_ref[pl.ds(pl.multiple_of(src, SLAB), SLAB), :]

    def compute(buf):
        x = _from_slabs(xs_scr[buf], tm).astype(BF16)
        a = jax.nn.silu(_dot(x, wg_ref[...])) * _dot(x, wu_ref[...])
        _to_slabs(ys_scr[buf], _dot(a.astype(BF16), wd_ref[...]))

    def scatter(base, buf, scale):
        for j in range(tm // u):
            upd = []
            for k in range(u):
                r = j * u + k
                dst = pl.multiple_of(row_smem[base + r], SLAB)
                y = ys_scr[buf][r * SLAB:(r + 1) * SLAB, :]
                upd.append((dst, o_ref[pl.ds(dst, SLAB), :] + (wgt_smem[base + r] * scale) * y))
            for dst, v in upd:
                o_ref[pl.ds(dst, SLAB), :] = v

    n_pairs = lax.shift_right_logical(nt + 1, 1)

    @pl.when((n_pairs > 0) & (sched_smem[0] != base0))
    def _():
        gather(base0, 0)

    pend_base = sched_smem[1]
    pend_scale = pend_smem[0]

    def pair(k, carry):
        b = base0 + 2 * k * tm
        first = k == 0
        gather(b + tm, 1)
        compute(0)
        scatter(jnp.where(first, pend_base, b - tm), 1, jnp.where(first, pend_scale, 1.0))
        gather(b + 2 * tm, 0)
        compute(1)
        scatter(b, 0, 1.0)
        return carry

    lax.fori_loop(0, n_pairs, pair, 0)

    @pl.when(n_pairs > 0)
    def _():
        last = 2 * n_pairs - 1
        sched_smem[0] = base0 + (last + 1) * tm
        sched_smem[1] = base0 + last * tm
        pend_smem[0] = jnp.where(last < nt, 1.0, 0.0)

    @pl.when(e == N_EXPERTS - 1)
    def _():
        scatter(sched_smem[1], 1, pend_smem[0])


def _dsp_expert_call(layer, h_slabs, pos, w, ntile, off, cnt, w_gate, w_up, w_down):
    n_sb = pos.shape[0]
    out_rows = (DSP_SB + DSP_CH) * SLAB

    def per_sb(s, e, *prefetch):
        return (s, 0, 0)

    def w_idx(s, e, *prefetch):
        return (layer, e, 0, 0)

    grid_spec = pltpu.PrefetchScalarGridSpec(
        num_scalar_prefetch=3,
        grid=(n_sb, N_EXPERTS),
        in_specs=[
            pl.BlockSpec((None, 1, 2 * DSP_SB), per_sb, memory_space=pltpu.SMEM),
            pl.BlockSpec((None, 1, 2 * DSP_SB), per_sb, memory_space=pltpu.SMEM),
            pl.BlockSpec((DSP_SB * SLAB, LANES), lambda s, e, *prefetch: (s, 0)),
            pl.BlockSpec((None, None, D_MODEL, EXPERT_FF), w_idx),
            pl.BlockSpec((None, None, D_MODEL, EXPERT_FF), w_idx),
            pl.BlockSpec((None, None, EXPERT_FF, D_MODEL), w_idx),
        ],
        out_specs=pl.BlockSpec((None, out_rows, LANES), per_sb),
        scratch_shapes=[pltpu.SMEM((DSP_SLOTS,), jnp.int32),
                        pltpu.SMEM((DSP_SLOTS,), F32),
                        pltpu.SMEM((2,), jnp.int32),
                        pltpu.SMEM((1,), F32),
                        pltpu.VMEM((DSP_TM * SLAB, LANES), F32),
                        pltpu.VMEM((DSP_TM * SLAB, LANES), F32),
                        pltpu.VMEM((DSP_TM * SLAB, LANES), F32),
                        pltpu.VMEM((DSP_TM * SLAB, LANES), F32)],
    )
    return pl.pallas_call(
        _dsp_expert_kernel,
        grid_spec=grid_spec,
        out_shape=jax.ShapeDtypeStruct((n_sb, out_rows, LANES), F32),
        compiler_params=pltpu.CompilerParams(vmem_limit_bytes=VMEM_LIMIT),
        name="moe_experts",
    )(ntile, off, cnt, pos, w, h_slabs, w_gate, w_up, w_down)


def _dsp_combine_kernel(x_ref, m_ref, mods_ref, fg_ref, o_ref, *, chunks_per_row, final):
    g2 = _dsp_mods(mods_ref, pl.program_id(0), chunks_per_row, 3)[2]
    y = x_ref[...] + g2 * _from_slabs(m_ref, DSP_CH)
    if final:
        y = y * lax.rsqrt(jnp.mean(y * y, axis=-1, keepdims=True) + EPS) * fg_ref[...]
    o_ref[...] = y


def _dsp_combine_call(layer, x2d, m_slabs, mods, final_g, chunks_per_row, final):
    n = x2d.shape[0]
    ch = DSP_CH
    per_sb = DSP_SB // ch
    return pl.pallas_call(
        functools.partial(_dsp_combine_kernel, chunks_per_row=chunks_per_row, final=final),
        grid=(n // ch,),
        in_specs=[
            pl.BlockSpec((ch, D_MODEL), lambda c: (c, 0)),
            pl.BlockSpec((None, ch * SLAB, LANES), lambda c: (c // per_sb, c % per_sb, 0)),
            pl.BlockSpec((None, MODS_ROWS, N_MOD * D_MODEL), lambda c: (layer, 0, 0)),
            pl.BlockSpec((1, D_MODEL), lambda c: (0, 0)),
        ],
        out_specs=pl.BlockSpec((ch, D_MODEL), lambda c: (c, 0)),
        out_shape=jax.ShapeDtypeStruct(x2d.shape, F32),
        compiler_params=pltpu.CompilerParams(vmem_limit_bytes=VMEM_LIMIT),
        name="moe_combine",
    )(x2d, m_slabs, mods, final_g)


def _sparse_moe(layer, x2d, mods, ln2, w_router, b_router, w_gate, w_up, w_down, final_g, *,
                chunks_per_row, final):
    n_sb = x2d.shape[0] // DSP_SB
    h_slabs, route, counts = _dsp_route_call(layer, x2d, mods, ln2, w_router, b_router,
                                             chunks_per_row)
    pos, w, ntile, off, cnt = _dispatch_tables(route, counts, n_sb)
    m_slabs = _dsp_expert_call(layer, h_slabs, pos, w, ntile, off, cnt, w_gate, w_up, w_down)
    return _dsp_combine_call(layer, x2d, m_slabs, mods, final_g, chunks_per_row, final)


LAT_PROJ_ROWS = 512


def _lat_proj_kernel(x_ref, mods_ref, ln_ref, win_ref, gb_ref, zc_ref, zn_ref, zm_ref, zg_ref):
    row = 1 + pl.program_id(0) // (DEC_SEQ // LAT_PROJ_ROWS)
    sh1, sc1, _ = _mod_rows(mods_ref, row, 0)
    h = _rms_mod(x_ref[...], ln_ref[...], sc1, sh1).astype(BF16)
    zc_ref[...] = _dot(h, win_ref[:, C_CONV:C_NA])
    zn_ref[...] = _dot(h, win_ref[:, C_NA:C_ML])
    zm_ref[...] = _dot(h, win_ref[:, C_ML:C_IG])
    zg_ref[...] = _dot(h, win_ref[:, C_IG:W_COLS]) + gb_ref[...]


def _lat_proj_call(layer, x2d, mods, ln1, w_in, gate_b):
    n = x2d.shape[0]
    tb = LAT_PROJ_ROWS
    widths = (C_NA - C_CONV, C_ML - C_NA, C_IG - C_ML, W_COLS - C_IG)
    return pl.pallas_call(
        _lat_proj_kernel,
        grid=(n // tb,),
        in_specs=[
            pl.BlockSpec((tb, D_MODEL), lambda i: (i, 0)),
            pl.BlockSpec((None, MODS_ROWS, N_MOD * D_MODEL), lambda i: (layer, 0, 0)),
            pl.BlockSpec((None, 1, D_MODEL), lambda i: (layer, 0, 0)),
            pl.BlockSpec((None, D_MODEL, W_COLS), lambda i: (layer, 0, 0)),
            pl.BlockSpec((None, 1, 2 * LANES), lambda i: (layer, 0, 0)),
        ],
        out_specs=[pl.BlockSpec((tb, w), lambda i: (i, 0)) for w in widths],
        out_shape=[jax.ShapeDtypeStruct((n, w), F32) for w in widths],
        compiler_params=pltpu.CompilerParams(vmem_limit_bytes=VMEM_LIMIT),
        name="lat_proj",
    )(x2d, mods, ln1, w_in, gate_b)


def _na_row_start(r):
    rows = DEC_SEQ // GRID_W
    return min(max(r - NA_WIN_ROWS // 2, 0), rows - NA_WIN_ROWS)


def _lat_local_kernel(zc_ref, zn_ref, kc_ref, vc_ref, band_ref, convw_ref, o_ref):
    o_ref[:, 0:CONV_DIM] = _short_conv(zc_ref[...], convw_ref[...])
    rows = DEC_SEQ // GRID_W
    blk = NA_WIN_ROWS * GRID_W
    q_all = (zn_ref[:, 0:NA_DIM] * NA_HEAD_DIM ** -0.5).astype(BF16)
    k_all = zn_ref[:, NA_DIM:2 * NA_DIM].astype(BF16)
    v_all = zn_ref[:, 2 * NA_DIM:3 * NA_DIM].astype(BF16)
    kc_all = kc_ref[...].astype(BF16)
    vc_all = vc_ref[...].astype(BF16)
    heads = []
    for h in range(NA_HEADS):
        sl = slice(h * NA_HEAD_DIM, (h + 1) * NA_HEAD_DIM)
        q, k, v, kc, vc = q_all[:, sl], k_all[:, sl], v_all[:, sl], kc_all[:, sl], vc_all[:, sl]
        band = band_ref[h]
        outs = []
        for r in range(rows):
            start = _na_row_start(r)
            off = (start - r + NA_WIN_ROWS - 1) * GRID_W
            q_r = q[r * GRID_W:(r + 1) * GRID_W]
            s_loc = _dot_nt(q_r, k[start * GRID_W:start * GRID_W + blk]) + band[:, off:off + blk]
            s_ctx = _dot_nt(q_r, kc)
            m = jnp.maximum(jnp.max(s_loc, axis=-1, keepdims=True), jnp.max(s_ctx, axis=-1, keepdims=True))
            p_loc = jnp.exp(s_loc - m)
            p_ctx = jnp.exp(s_ctx - m)
            den = jnp.sum(p_loc, axis=-1, keepdims=True) + jnp.sum(p_ctx, axis=-1, keepdims=True)
            o = _dot(p_loc.astype(BF16), v[start * GRID_W:start * GRID_W + blk]) + _dot(p_ctx.astype(BF16), vc)
            outs.append(o / den)
        heads.append(jnp.concatenate(outs, axis=0))
    o_ref[:, CONV_DIM:CONV_DIM + NA_DIM] = jnp.concatenate(heads, axis=-1)


def _lat_local_call(layer, zc, zn, cache_k, cache_v, band, conv_w):
    nb = zc.shape[0] // DEC_SEQ
    T = DEC_SEQ
    return pl.pallas_call(
        _lat_local_kernel,
        grid=(nb,),
        in_specs=[
            pl.BlockSpec((T, 3 * CONV_DIM), lambda b: (b, 0)),
            pl.BlockSpec((T, 3 * NA_DIM), lambda b: (b, 0)),
            pl.BlockSpec((None, None, PAST_LEN, NA_DIM), lambda b: (b, layer, 0, 0)),
            pl.BlockSpec((None, None, PAST_LEN, NA_DIM), lambda b: (b, layer, 0, 0)),
            pl.BlockSpec((None, NA_HEADS, GRID_W, band.shape[-1]), lambda b: (layer, 0, 0, 0)),
            pl.BlockSpec((None, 3, CONV_DIM), lambda b: (layer, 0, 0)),
        ],
        out_specs=pl.BlockSpec((T, CONV_DIM + NA_DIM), lambda b: (b, 0)),
        out_shape=jax.ShapeDtypeStruct((zc.shape[0], CONV_DIM + NA_DIM), F32),
        compiler_params=pltpu.CompilerParams(vmem_limit_bytes=VMEM_LIMIT),
        name="lat_conv_na",
    )(zc, zn, cache_k, cache_v, band, conv_w)


def _rope(x, cos, sin_signed):
    w = x.shape[-1]
    half = NA_HEAD_DIM // 2
    lane = lax.broadcasted_iota(jnp.int32, x.shape, 1)
    partner = jnp.where(lane % (2 * half) < half, pltpu.roll(x, w - half, axis=1), pltpu.roll(x, half, axis=1))
    return x * cos + partner * sin_signed


def _lat_mlstm_kernel(q_ref, k_ref, v_ref, o_ref, zg_ref, c0_ref, n0_ref, m0_ref, cos_ref, sin_ref,
                      mlg_ref, out_ref, bcol_scr, rowt_scr):
    hh = pl.program_id(1)

    @pl.when(hh == 0)
    def _():
        b_all, r_all = _gate_terms(zg_ref[:, 0:LANES], zg_ref[:, LANES:2 * LANES])
        bcol_scr[...] = b_all
        rowt_scr[...] = r_all

    b_col = bcol_scr[...]
    cos, sin_signed = cos_ref[...], sin_ref[...]
    q = _rope(q_ref[...], cos, sin_signed)
    k = _rope(k_ref[...], cos, sin_signed) * ML_HEAD_DIM ** -0.5
    v = v_ref[...]
    lane = lax.broadcasted_iota(jnp.int32, (DEC_SEQ, LANES), 1)
    hsum = None
    for d in range(N_DIRS):
        j = d * ML_HEADS + hh
        bc = jnp.sum(jnp.where(lane == j, b_col, 0.0), axis=-1, keepdims=True)
        lr = rowt_scr[pl.ds(j, 1), :]
        m0 = m0_ref[d, :, 0:1]
        ho = _mlstm_outputs(q, k, v, bc, lr, m0, c0_ref[d], n0_ref[d], d == 1, 256)
        hsum = ho if hsum is None else hsum + ho
    out_ref[...] = _head_readout(hsum, o_ref[...], mlg_ref[...])


def _lat_mlstm_call(layer, zm, zg, state_c, state_n, state_m, cos, sin_signed, ml_g):
    nb = zm.shape[0] // DEC_SEQ
    T = DEC_SEQ
    hd = ML_HEAD_DIM

    def col(part):
        return pl.BlockSpec((T, hd), lambda b, h: (b, part * ML_HEADS + h))

    return pl.pallas_call(
        _lat_mlstm_kernel,
        grid=(nb, ML_HEADS),
        in_specs=[
            col(0), col(1), col(2), col(3),
            pl.BlockSpec((T, 2 * LANES), lambda b, h: (b, 0)),
            pl.BlockSpec((None, None, N_DIRS, None, hd, hd), lambda b, h: (b, layer, 0, h, 0, 0)),
            pl.BlockSpec((None, None, N_DIRS, None, 1, hd), lambda b, h: (b, layer, 0, h, 0, 0)),
            pl.BlockSpec((None, None, N_DIRS, None, 1, LANES), lambda b, h: (b, layer, 0, h, 0, 0)),
            pl.BlockSpec((T, hd), lambda b, h: (0, 0)),
            pl.BlockSpec((T, hd), lambda b, h: (0, 0)),
            pl.BlockSpec((None, 1, hd), lambda b, h: (layer, 0, h)),
        ],
        out_specs=pl.BlockSpec((T, hd), lambda b, h: (b, h)),
        out_shape=jax.ShapeDtypeStruct((zm.shape[0], ML_DIM), F32),
        scratch_shapes=[pltpu.VMEM((T, LANES), F32), pltpu.VMEM((LANES, T), F32)],
        compiler_params=pltpu.CompilerParams(vmem_limit_bytes=VMEM_LIMIT),
        name="lat_mlstm",
    )(zm, zm, zm, zm, zg, state_c, state_n, state_m, cos, sin_signed, ml_g)


def _lat_merge_kernel(x_ref, loc_ref, ml_ref, mods_ref, wout_ref, o_ref):
    row = 1 + pl.program_id(0) // (DEC_SEQ // LAT_PROJ_ROWS)
    g1 = _mod_rows(mods_ref, row, 0)[2]
    split = CONV_DIM + NA_DIM
    y = _dot(loc_ref[...].astype(BF16), wout_ref[0:split, :]) + _dot(ml_ref[...].astype(BF16), wout_ref[split:, :])
    o_ref[...] = x_ref[...] + g1 * y


def _lat_merge_call(layer, x2d, loc, ml, mods, w_out):
    n = x2d.shape[0]
    tb = LAT_PROJ_ROWS
    return pl.pallas_call(
        _lat_merge_kernel,
        grid=(n // tb,),
        in_specs=[
            pl.BlockSpec((tb, D_MODEL), lambda i: (i, 0)),
            pl.BlockSpec((tb, CONV_DIM + NA_DIM), lambda i: (i, 0)),
            pl.BlockSpec((tb, ML_DIM), lambda i: (i, 0)),
            pl.BlockSpec((None, MODS_ROWS, N_MOD * D_MODEL), lambda i: (layer, 0, 0)),
            pl.BlockSpec((None, D_MODEL, D_MODEL), lambda i: (layer, 0, 0)),
        ],
        out_specs=pl.BlockSpec((tb, D_MODEL), lambda i: (i, 0)),
        out_shape=jax.ShapeDtypeStruct(x2d.shape, F32),
        compiler_params=pltpu.CompilerParams(vmem_limit_bytes=VMEM_LIMIT),
        name="lat_merge",
    )(x2d, loc, ml, mods, w_out)


def _pad_lanes(a, width):
    return jnp.pad(a, [(0, 0)] * (a.ndim - 1) + [(0, width - a.shape[-1])])


def _pack_w_in(w_in):
    main = w_in[..., :C_IG]
    gates = w_in[..., C_IG:]
    packed = jnp.concatenate([main, _pad_lanes(gates[..., :N_GATE], LANES),
                              _pad_lanes(gates[..., N_GATE:], LANES)], axis=-1)
    return packed.astype(BF16)


def _pack_gate_bias(ml_gate_b):
    gb = ml_gate_b.reshape(DEPTH, 2, N_GATE).astype(F32)
    return jnp.concatenate([_pad_lanes(gb[:, 0], LANES), _pad_lanes(gb[:, 1], LANES)], axis=-1)[:, None, :]


def _rpb_band(na_rpb):
    cols = np.arange(GRID_W)
    col_idx = np.clip(cols[None, :] - cols[:, None] + NA_WIN_COLS - 1, 0, 2 * NA_WIN_COLS - 2)
    col_start = np.clip(cols - NA_WIN_COLS // 2, 0, GRID_W - NA_WIN_COLS)
    col_mask = (cols[None, :] >= col_start[:, None]) & (cols[None, :] < col_start[:, None] + NA_WIN_COLS)
    pick = (col_idx[None] == np.arange(2 * NA_WIN_COLS - 1)[:, None, None]).astype(np.float32)
    t = jnp.einsum('lhri,iqk->lhqrk', na_rpb.astype(F32), jnp.asarray(pick), precision=HIGHEST)
    t = jnp.where(col_mask[None, None, :, None, :], t, NEG_INF)
    t = t.reshape(DEPTH, NA_HEADS, GRID_W, (2 * NA_WIN_ROWS - 1) * GRID_W)
    return _pad_lanes(t, 2 * NA_WIN_ROWS * GRID_W)


def _rope_tables():
    t = np.arange(DEC_SEQ)
    pos = np.stack([t // GRID_W, t % GRID_W], axis=-1).astype(np.float32)
    nf = ML_HEAD_DIM // 4
    inv = jnp.asarray(ROPE_BASE, F32) ** (-jnp.arange(nf, dtype=F32) / nf)
    ang = jnp.asarray(pos)[:, :, None] * inv
    cos = jnp.cos(ang)
    sin = jnp.sin(ang)
    cos_t = jnp.concatenate([cos, cos], axis=-1).reshape(DEC_SEQ, ML_HEAD_DIM)
    sin_t = jnp.concatenate([-sin, sin], axis=-1).reshape(DEC_SEQ, ML_HEAD_DIM)
    return cos_t, sin_t


def kernel(x_prompt, x_sample, cache_k, cache_v, state_C, state_n, state_m, c, c_ctx, w_mod, b_mod,
           ln1_g, w_in, conv_w, na_rpb, ml_gate_b, ml_norm_g, w_out, ln2_g, w_rg, b_rg, w_re, b_re,
           w_gate, w_up, w_down, final_g):
    nb_ctx = x_prompt.shape[0]
    nb_lat = x_sample.shape[0]
    assert 1 + nb_lat <= MODS_ROWS

    cvecs = jnp.concatenate([c_ctx[None, :], c,
                             jnp.zeros((MODS_ROWS - 1 - nb_lat, D_MODEL), F32)], axis=0)
    mods = _mods_call(cvecs, w_mod, b_mod)

    w_in_p = _pack_w_in(w_in)
    w_out_b = w_out.astype(BF16)
    w_gate_b, w_up_b, w_down_b = w_gate.astype(BF16), w_up.astype(BF16), w_down.astype(BF16)
    gate_b = _pack_gate_bias(ml_gate_b)
    ln1 = ln1_g.reshape(DEPTH, 1, D_MODEL)
    ln2 = ln2_g.reshape(DEPTH, 1, D_MODEL)
    ml_g = ml_norm_g.reshape(DEPTH, 1, ML_DIM)
    w_router = _pad_lanes(jnp.concatenate([w_re, w_rg], axis=-1), LANES)
    b_router = _pad_lanes(jnp.concatenate([b_re, b_rg], axis=-1), LANES)[:, None, :]
    fg = final_g.reshape(1, D_MODEL)
    band = _rpb_band(na_rpb)
    cos_t, sin_t = _rope_tables()
    ck = cache_k.reshape(nb_lat, DEPTH, PAST_LEN, NA_DIM)
    cv = cache_v.reshape(nb_lat, DEPTH, PAST_LEN, NA_DIM)
    st_n = state_n.reshape(nb_lat, DEPTH, N_DIRS, ML_HEADS, 1, ML_HEAD_DIM)
    st_m = jnp.broadcast_to(state_m[..., None, None], (nb_lat, DEPTH, N_DIRS, ML_HEADS, 1, LANES))

    xp = x_prompt
    xs = x_sample.reshape(nb_lat * DEC_SEQ, D_MODEL)
    states = None
    for l in range(DEPTH):
        last = l == DEPTH - 1
        xp, states = _ctx_mixer_call(l, xp, mods, ln1, w_in_p, conv_w, gate_b, ml_g, w_out_b, states)
        xp = _sparse_moe(l, xp.reshape(nb_ctx * SEQ, D_MODEL), mods, ln2, w_router, b_router,
                         w_gate_b, w_up_b, w_down_b, fg, chunks_per_row=None, final=last)
        xp = xp.reshape(nb_ctx, SEQ, D_MODEL)

        zc, zn, zm, zg = _lat_proj_call(l, xs, mods, ln1, w_in_p, gate_b)
        loc = _lat_local_call(l, zc, zn, ck, cv, band, conv_w)
        ml = _lat_mlstm_call(l, zm, zg, state_C, st_n, st_m, cos_t, sin_t, ml_g)
        xs = _lat_merge_call(l, xs, loc, ml, mods, w_out_b)
        xs = _sparse_moe(l, xs, mods, ln2, w_router, b_router, w_gate_b, w_up_b, w_down_b, fg,
                         chunks_per_row=DEC_SEQ // DSP_CH, final=last)

    new_k, new_v, new_c, new_n, new_m = states
    return (xp, xs.reshape(nb_lat, DEC_SEQ, D_MODEL),
            new_k.reshape(nb_ctx, DEPTH, SEQ, NA_HEADS, NA_HEAD_DIM),
            new_v.reshape(nb_ctx, DEPTH, SEQ, NA_HEADS, NA_HEAD_DIM),
            new_c,
            new_n.reshape(nb_ctx, DEPTH, N_DIRS, ML_HEADS, ML_HEAD_DIM),
            new_m[..., 0].reshape(nb_ctx, DEPTH, N_DIRS, ML_HEADS))
```

```python
import functools

import numpy as np
import jax
import jax.numpy as jnp
from jax import lax
from jax.experimental import pallas as pl
from jax.experimental.pallas import tpu as pltpu

F32 = jnp.float32
BF16 = jnp.bfloat16

D_MODEL = 1024
SEQ = 256
DEPTH = 4
DEC_SEQ = 1024
PAST_LEN = 512
GRID_W = 64
CONV_DIM = 256
NA_HEADS = 4
NA_HEAD_DIM = 64
NA_DIM = NA_HEADS * NA_HEAD_DIM
NA_WIN_ROWS = 8
NA_WIN_COLS = 16
ML_HEADS = 4
ML_HEAD_DIM = 128
ML_DIM = ML_HEADS * ML_HEAD_DIM
N_DIRS = 2
N_GROUPS = 4
EXPERTS_PER_GROUP = 4
N_EXPERTS = N_GROUPS * EXPERTS_PER_GROUP
EXPERT_FF = 512
ROPE_BASE = 10000.0
EPS = 1e-6
N_MOD = 6

LANES = 128
N_GATE = N_DIRS * ML_HEADS
C_CONV = 0
C_NA = 3 * CONV_DIM
C_ML = C_NA + 3 * NA_DIM
C_IG = C_ML + 4 * ML_DIM
C_FG = C_IG + LANES
W_COLS = C_FG + LANES
CUMSUM_BLOCK = 256
MODS_ROWS = 8
VMEM_LIMIT = 56 * 1024 * 1024

NEG_INF = float("-inf")
HIGHEST = lax.Precision.HIGHEST


def _dot(a, b):
    return jnp.dot(a, b, preferred_element_type=F32)


def _dot_nt(a, b):
    return lax.dot_general(a, b, (((1,), (1,)), ((), ())), preferred_element_type=F32)


def _dot_split(a, b):
    a_hi = a.astype(BF16)
    a_lo = (a - a_hi.astype(F32)).astype(BF16)
    b_hi = b.astype(BF16)
    b_lo = (b - b_hi.astype(F32)).astype(BF16)
    return _dot(jnp.concatenate([a_hi, a_hi, a_lo], axis=1), jnp.concatenate([b_hi, b_lo, b_hi], axis=0))


def _rms_mod(x, g, sc, sh):
    y = x * lax.rsqrt(jnp.mean(x * x, axis=-1, keepdims=True) + EPS)
    return (y * g) * (1.0 + sc) + sh


def _mod_rows(mods_ref, row, first):
    return [mods_ref[pl.ds(row, 1), pl.ds((first + j) * D_MODEL, D_MODEL)] for j in range(3)]


def _mods_kernel(cv_ref, w_ref, b_ref, o_ref):
    cv = cv_ref[...]
    s = cv * jax.nn.sigmoid(cv)
    o_ref[...] = _dot(s.astype(BF16), w_ref[...].astype(BF16)) + b_ref[...]


def _mods_call(cvecs, w_mod, b_mod):
    tn = 1536
    n = N_MOD * D_MODEL
    return pl.pallas_call(
        _mods_kernel,
        grid=(DEPTH, n // tn),
        in_specs=[pl.BlockSpec((MODS_ROWS, D_MODEL), lambda l, j: (0, 0)),
                  pl.BlockSpec((None, D_MODEL, tn), lambda l, j: (l, 0, j)),
                  pl.BlockSpec((None, 1, tn), lambda l, j: (l, 0, j))],
        out_specs=pl.BlockSpec((None, MODS_ROWS, tn), lambda l, j: (l, 0, j)),
        out_shape=jax.ShapeDtypeStruct((DEPTH, MODS_ROWS, n), F32),
        compiler_params=pltpu.CompilerParams(vmem_limit_bytes=VMEM_LIMIT),
        name="adaln_mods",
    )(cvecs, w_mod, b_mod.reshape(DEPTH, 1, n))


def _short_conv(zc, cw):
    T = zc.shape[0]
    cb = zc[:, 0:CONV_DIM]
    u = zc[:, CONV_DIM:2 * CONV_DIM] * zc[:, 2 * CONV_DIM:3 * CONV_DIM]
    t = lax.broadcasted_iota(jnp.int32, u.shape, 0)
    u_prev = jnp.where(t == 0, 0.0, pltpu.roll(u, 1, axis=0))
    u_next = jnp.where(t == T - 1, 0.0, pltpu.roll(u, T - 1, axis=0))
    return cb * (cw[0:1, :] * u_prev + cw[1:2, :] * u + cw[2:3, :] * u_next)


def _softmax_attention(nq, nk, nv):
    outs = []
    for h in range(NA_HEADS):
        sl = slice(h * NA_HEAD_DIM, (h + 1) * NA_HEAD_DIM)
        q = (nq[:, sl] * NA_HEAD_DIM ** -0.5).astype(BF16)
        s = _dot_nt(q, nk[:, sl].astype(BF16))
        p = jnp.exp(s - jnp.max(s, axis=-1, keepdims=True))
        o = _dot(p.astype(BF16), nv[:, sl].astype(BF16)) / jnp.sum(p, axis=-1, keepdims=True)
        outs.append(o)
    return jnp.concatenate(outs, axis=-1)


def _log_sigmoid(x):
    return jnp.minimum(x, 0.0) - jnp.log(1.0 + jnp.exp(-jnp.abs(x)))


def _gate_terms(zi, zf):
    T = zi.shape[0]
    lf = _log_sigmoid(zf)
    blk = min(T, CUMSUM_BLOCK)
    r = lax.broadcasted_iota(jnp.int32, (blk, blk), 0)
    c = lax.broadcasted_iota(jnp.int32, (blk, blk), 1)
    tril = jnp.where(c <= r, 1.0, 0.0).astype(F32)
    parts, carry = [], None
    for r0 in range(0, T, blk):
        b = jnp.dot(tril, lf[r0:r0 + blk], precision=HIGHEST, preferred_element_type=F32)
        if carry is not None:
            b = b + carry
        carry = b[blk - 1:blk, :]
        parts.append(b)
    b_fwd = parts[0] if len(parts) == 1 else jnp.concatenate(parts, axis=0)
    b_bwd = (b_fwd[T - 1:T, :] - b_fwd) + lf
    lane = lax.broadcasted_iota(jnp.int32, zi.shape, 1)
    b_col = jnp.where(lane < ML_HEADS, b_fwd, b_bwd)
    row_t = jnp.transpose(zi - b_col)
    return b_col, row_t


def _mlstm_outputs(q, k, v, b_col, li_row, m0, c0, n0, backward, q_block):
    T = q.shape[0]
    qb, kb, vb = q.astype(BF16), k.astype(BF16), v.astype(BF16)
    c0b = None if c0 is None else c0.astype(BF16)
    outs = []
    for r0 in range(0, T, q_block):
        ks, ke = (r0, T) if backward else (0, r0 + q_block)
        bq = b_col[r0:r0 + q_block]
        d = bq + li_row[:, ks:ke]
        t_idx = r0 + lax.broadcasted_iota(jnp.int32, d.shape, 0)
        s_idx = ks + lax.broadcasted_iota(jnp.int32, d.shape, 1)
        d = jnp.where((s_idx >= t_idx) if backward else (s_idx <= t_idx), d, NEG_INF)
        inter = bq + m0
        m_t = jnp.maximum(jnp.max(d, axis=-1, keepdims=True), inter)
        w = jnp.exp(d - m_t)
        s = _dot_nt(qb[r0:r0 + q_block], kb[ks:ke]) * w
        num = _dot(s.astype(BF16), vb[ks:ke])
        den = jnp.sum(s, axis=-1, keepdims=True)
        if c0 is not None:
            a = jnp.exp(inter - m_t)
            num = num + a * _dot(qb[r0:r0 + q_block], c0b)
            den = den + a * jnp.sum(q[r0:r0 + q_block] * n0, axis=-1, keepdims=True)
        outs.append(num / jnp.maximum(jnp.abs(den), jnp.exp(-m_t)))
    return outs[0] if len(outs) == 1 else jnp.concatenate(outs, axis=0)


def _mlstm_state(k, v, b_col, li_col, backward):
    T = k.shape[0]
    b_tot = b_col[0:1] if backward else b_col[T - 1:T]
    g = (b_tot - b_col) + li_col
    m_new = jnp.maximum(b_tot, jnp.max(g, axis=0, keepdims=True))
    kw = k * jnp.exp(g - m_new)
    c_new = _dot(jnp.transpose(kw).astype(BF16), v.astype(BF16))
    n_new = jnp.sum(kw, axis=0, keepdims=True)
    return c_new, n_new, m_new


def _head_readout(h, mo, g):
    hn = h * lax.rsqrt(jnp.mean(h * h, axis=-1, keepdims=True) + EPS)
    return hn * g * jax.nn.sigmoid(mo)


def _ctx_mixer_kernel(x_ref, mods_ref, ln_ref, win_ref, convw_ref, gb_ref, mlg_ref, wout_ref, *rest,
                      has_moe):
    xo_ref, ko_ref, vo_ref, co_ref, no_ref, mo_ref = rest[-6:]
    T = SEQ
    sh1, sc1, g1 = _mod_rows(mods_ref, 0, 0)
    x = x_ref[...]
    if has_moe:
        moe_ref, mods_prev_ref = rest[0], rest[1]
        x = x + _mod_rows(mods_prev_ref, 0, 3)[2] * _from_slabs(moe_ref, T)
    h = _rms_mod(x, ln_ref[...], sc1, sh1).astype(BF16)

    conv_o = _short_conv(_dot(h, win_ref[:, C_CONV:C_NA]), convw_ref[...])

    zn = _dot(h, win_ref[:, C_NA:C_ML])
    nk, nv = zn[:, NA_DIM:2 * NA_DIM], zn[:, 2 * NA_DIM:3 * NA_DIM]
    ko_ref[...] = nk
    vo_ref[...] = nv
    na_o = _softmax_attention(zn[:, 0:NA_DIM], nk, nv)

    zg = _dot(h, win_ref[:, C_IG:W_COLS])
    zi = zg[:, 0:LANES] + gb_ref[:, 0:LANES]
    zf = zg[:, LANES:2 * LANES] + gb_ref[:, LANES:2 * LANES]
    b_col, row_t = _gate_terms(zi, zf)
    m0 = jnp.zeros((1, 1), F32)

    zm = _dot(h, win_ref[:, C_ML:C_IG])
    ml_parts = []
    for hh in range(ML_HEADS):
        zq, zk, zv, zo = (zm[:, p * ML_DIM + hh * ML_HEAD_DIM:p * ML_DIM + (hh + 1) * ML_HEAD_DIM]
                          for p in range(4))
        zk = zk * ML_HEAD_DIM ** -0.5
        hsum = None
        for d in range(N_DIRS):
            j = d * ML_HEADS + hh
            bc = b_col[:, j:j + 1]
            ho = _mlstm_outputs(zq, zk, zv, bc, row_t[j:j + 1, :], m0, None, None, d == 1, T)
            hsum = ho if hsum is None else hsum + ho
            c_new, n_new, m_new = _mlstm_state(zk, zv, bc, zi[:, j:j + 1], d == 1)
            co_ref[d, hh] = c_new
            no_ref[j:j + 1, :] = n_new
            mo_ref[j:j + 1, :] = jnp.broadcast_to(m_new, (1, LANES))
        ml_parts.append(_head_readout(hsum, zo, mlg_ref[:, hh * ML_HEAD_DIM:(hh + 1) * ML_HEAD_DIM]))

    mix = jnp.concatenate([conv_o, na_o] + ml_parts, axis=-1).astype(BF16)
    xo_ref[...] = x + g1 * _dot(mix, wout_ref[...])


def _ctx_mixer_call(layer, x, moe, mods, ln1, w_in, conv_w, gate_b, ml_g, w_out, prev):
    B = x.shape[0]
    T = SEQ
    G = None
    state_shapes = [
        jax.ShapeDtypeStruct((B, DEPTH, T, NA_DIM), F32),
        jax.ShapeDtypeStruct((B, DEPTH, T, NA_DIM), F32),
        jax.ShapeDtypeStruct((B, DEPTH, N_DIRS, ML_HEADS, ML_HEAD_DIM, ML_HEAD_DIM), F32),
        jax.ShapeDtypeStruct((B, DEPTH, N_GATE, ML_HEAD_DIM), F32),
        jax.ShapeDtypeStruct((B, DEPTH, N_GATE, LANES), F32),
    ]
    in_specs = [
        pl.BlockSpec((G, T, D_MODEL), lambda b: (b, 0, 0)),
        pl.BlockSpec((None, MODS_ROWS, N_MOD * D_MODEL), lambda b: (layer, 0, 0)),
        pl.BlockSpec((None, 1, D_MODEL), lambda b: (layer, 0, 0)),
        pl.BlockSpec((None, D_MODEL, W_COLS), lambda b: (layer, 0, 0)),
        pl.BlockSpec((None, 3, CONV_DIM), lambda b: (layer, 0, 0)),
        pl.BlockSpec((None, 1, 2 * LANES), lambda b: (layer, 0, 0)),
        pl.BlockSpec((None, 1, ML_DIM), lambda b: (layer, 0, 0)),
        pl.BlockSpec((None, D_MODEL, D_MODEL), lambda b: (layer, 0, 0)),
    ]
    args = [x, mods, ln1, w_in, conv_w, gate_b, ml_g, w_out]
    if moe is not None:
        per_sb = DSP_SB // T
        in_specs += [pl.BlockSpec((None, T * SLAB, LANES), lambda b: (b // per_sb, b % per_sb, 0)),
                     pl.BlockSpec((None, MODS_ROWS, N_MOD * D_MODEL), lambda b: (layer - 1, 0, 0))]
        args += [moe, mods]
    aliases = {}
    if prev is not None:
        in_specs += [pl.BlockSpec(memory_space=pl.ANY)] * len(prev)
        aliases = {len(args) + i: 1 + i for i in range(len(prev))}
        args += list(prev)
    out_specs = [
        pl.BlockSpec((G, T, D_MODEL), lambda b: (b, 0, 0)),
        pl.BlockSpec((G, None, T, NA_DIM), lambda b: (b, layer, 0, 0)),
        pl.BlockSpec((G, None, T, NA_DIM), lambda b: (b, layer, 0, 0)),
        pl.BlockSpec((G, None, N_DIRS, ML_HEADS, ML_HEAD_DIM, ML_HEAD_DIM),
                     lambda b: (b, layer, 0, 0, 0, 0)),
        pl.BlockSpec((G, None, N_GATE, ML_HEAD_DIM), lambda b: (b, layer, 0, 0)),
        pl.BlockSpec((G, None, N_GATE, LANES), lambda b: (b, layer, 0, 0)),
    ]
    outs = pl.pallas_call(
        functools.partial(_ctx_mixer_kernel, has_moe=moe is not None),
        grid=(B,),
        in_specs=in_specs,
        out_specs=out_specs,
        out_shape=[jax.ShapeDtypeStruct(x.shape, F32)] + state_shapes,
        input_output_aliases=aliases,
        compiler_params=pltpu.CompilerParams(vmem_limit_bytes=VMEM_LIMIT),
        name="ctx_mixer",
    )(*args)
    return outs[0], tuple(outs[1:])


def _route(lg):
    lane = lax.broadcasted_iota(jnp.int32, lg.shape, 1)
    lane_f = lane.astype(F32)
    big = float(LANES)
    is_g = (lane >= N_EXPERTS) & (lane < N_EXPERTS + N_GROUPS)
    glm = jnp.where(is_g, lg, NEG_INF)
    gmax = jnp.max(glm, axis=-1, keepdims=True)
    g_top = jnp.min(jnp.where(glm == gmax, lane_f, big), axis=-1, keepdims=True) - N_EXPERTS
    gp = 1.0 / jnp.sum(jnp.where(is_g, jnp.exp(lg - gmax), 0.0), axis=-1, keepdims=True)
    grp = (lane >> (EXPERTS_PER_GROUP.bit_length() - 1)).astype(F32)
    in_grp = (lane < N_EXPERTS) & (grp == g_top)
    e1 = jnp.where(in_grp, lg, NEG_INF)
    v1 = jnp.max(e1, axis=-1, keepdims=True)
    i1 = jnp.min(jnp.where(e1 == v1, lane_f, big), axis=-1, keepdims=True)
    e2 = jnp.where(in_grp & (lane_f != i1), lg, NEG_INF)
    v2 = jnp.max(e2, axis=-1, keepdims=True)
    i2 = jnp.min(jnp.where(e2 == v2, lane_f, big), axis=-1, keepdims=True)
    t = jnp.exp(v2 - v1)
    w1 = 1.0 / (1.0 + t)
    w2 = t / (1.0 + t)
    return i1, i2, gp * w1, gp * w2


SLAB = D_MODEL // LANES
DSP_SB = 2048
DSP_TM = 160
DSP_CH = 512
DSP_PMAX = 2 * DSP_SB + N_EXPERTS * DSP_TM
DSP_NULL = DSP_PMAX + 2 * DSP_TM
DSP_SLOTS = DSP_NULL + 2 * DSP_TM
DSP_UNROLL = 8


def _to_slabs(ref, x):
    n = x.shape[0]
    for s in range(SLAB):
        ref[pl.ds(s, n, stride=SLAB), :] = x[:, s * LANES:(s + 1) * LANES]


def _from_slabs(ref, n):
    return jnp.concatenate([ref[pl.ds(s, n, stride=SLAB), :] for s in range(SLAB)], axis=-1)


def _dsp_mods(mods_ref, chunk, chunks_per_row, first):
    row = 0 if chunks_per_row is None else 1 + chunk // chunks_per_row
    return _mod_rows(mods_ref, row, first)


def _dsp_route_kernel(x_ref, mods_ref, ln_ref, wr_ref, br_ref, h_ref, r_ref, cnt_ref, run_scr, *,
                      chunks_per_row):
    c = pl.program_id(0)
    ch = DSP_CH

    @pl.when(c % (DSP_SB // ch) == 0)
    def _():
        run_scr[...] = jnp.zeros_like(run_scr)

    sh2, sc2, _ = _dsp_mods(mods_ref, c, chunks_per_row, 3)
    h2 = _rms_mod(x_ref[...], ln_ref[...], sc2, sh2)
    _to_slabs(h_ref, h2)
    lg = _dot_split(h2, wr_ref[...]) + br_ref[...]
    i1, i2, w1, w2 = _route(lg)
    lane = lax.broadcasted_iota(jnp.int32, lg.shape, 1)
    lane_f = lane.astype(F32)
    oh1 = jnp.where(lane_f == i1, 1.0, 0.0)
    oh2 = jnp.where(lane_f == i2, 1.0, 0.0)
    both = oh1 + oh2
    r = lax.broadcasted_iota(jnp.int32, (ch, ch), 0)
    s = lax.broadcasted_iota(jnp.int32, (ch, ch), 1)
    earlier = jnp.where(s < r, 1.0, 0.0).astype(BF16)
    run = run_scr[0:1, :]
    before = _dot(earlier, both.astype(BF16)) + run
    rank1 = jnp.sum(before * oh1, axis=-1, keepdims=True)
    rank2 = jnp.sum(before * oh2, axis=-1, keepdims=True)
    run = run + jnp.sum(both, axis=0, keepdims=True)
    run_scr[0:1, :] = run
    cnt_ref[...] = jnp.broadcast_to(run, cnt_ref.shape)
    cols = (i1, i2, w1, w2, rank1, rank2)
    out = jnp.zeros(lg.shape, F32)
    for j, col in enumerate(cols):
        out = jnp.where(lane == j, col, out)
    r_ref[...] = out


def _dsp_route_call(layer, x2d, mods, ln2, w_router, b_router, chunks_per_row):
    n = x2d.shape[0]
    ch = DSP_CH
    per_sb = DSP_SB // ch
    return pl.pallas_call(
        functools.partial(_dsp_route_kernel, chunks_per_row=chunks_per_row),
        grid=(n // ch,),
        in_specs=[
            pl.BlockSpec((ch, D_MODEL), lambda c: (c, 0)),
            pl.BlockSpec((None, MODS_ROWS, N_MOD * D_MODEL), lambda c: (layer, 0, 0)),
            pl.BlockSpec((None, 1, D_MODEL), lambda c: (layer, 0, 0)),
            pl.BlockSpec((None, D_MODEL, LANES), lambda c: (layer, 0, 0)),
            pl.BlockSpec((None, 1, LANES), lambda c: (layer, 0, 0)),
        ],
        out_specs=[pl.BlockSpec((ch * SLAB, LANES), lambda c: (c, 0)),
                   pl.BlockSpec((ch, LANES), lambda c: (c, 0)),
                   pl.BlockSpec((None, SLAB, LANES), lambda c: (c // per_sb, 0, 0))],
        out_shape=[jax.ShapeDtypeStruct((n * SLAB, LANES), F32),
                   jax.ShapeDtypeStruct((n, LANES), F32),
                   jax.ShapeDtypeStruct((n // DSP_SB, SLAB, LANES), F32)],
        scratch_shapes=[pltpu.VMEM((SLAB, LANES), F32)],
        compiler_params=pltpu.CompilerParams(vmem_limit_bytes=VMEM_LIMIT),
        name="moe_route",
    )(x2d, mods, ln2, w_router, b_router)


def _dispatch_tables(route, counts, n_sb):
    e = route[:, 0:2].astype(jnp.int32).reshape(n_sb, 2 * DSP_SB)
    rank = route[:, 4:6].astype(jnp.int32).reshape(n_sb, 2 * DSP_SB)
    w = route[:, 2:4].reshape(n_sb, 1, 2 * DSP_SB)
    cnt = counts[:, 0, :N_EXPERTS].astype(jnp.int32)
    ntile = (cnt + (DSP_TM - 1)) // DSP_TM
    off = (jnp.cumsum(ntile, axis=1) - ntile) * DSP_TM
    is_e = e[:, None, :] == jnp.arange(N_EXPERTS, dtype=jnp.int32)[None, :, None]
    pos = (jnp.sum(jnp.where(is_e, off[:, :, None], 0), axis=1) + rank).reshape(n_sb, 1, 2 * DSP_SB)
    return pos, w, ntile.reshape(-1), off.reshape(-1), cnt.reshape(-1)


def _dsp_expert_kernel(ntile_ref, off_ref, cnt_ref, pos_ref, w_ref, h_ref, wg_ref, wu_ref, wd_ref,
                       o_ref, row_smem, wgt_smem, sched_smem, pend_smem, xs0, xs1, ys0, ys1,
                       wg_scr, wu_scr, wd_scr):
    sb = pl.program_id(0)
    e = pl.program_id(1)
    tm, u = DSP_TM, DSP_UNROLL
    xs_scr, ys_scr = (xs0, xs1), (ys0, ys1)
    slab_shift = SLAB.bit_length() - 1
    spare_row = DSP_SB * SLAB
    last_row = (DSP_SB - 1) * SLAB

    @pl.when(e == 0)
    def _():
        for ee in range(N_EXPERTS):
            q = sb * N_EXPERTS + ee

            def pad(p, carry):
                row_smem[p] = spare_row
                wgt_smem[p] = 0.0
                return carry

            lax.fori_loop(off_ref[q] + cnt_ref[q], off_ref[q] + ntile_ref[q] * tm, pad, 0)

        def invert(j, carry):
            for k in range(u):
                a = j * u + k
                p = pos_ref[0, a]
                row_smem[p] = lax.shift_left(lax.shift_right_logical(a, 1), slab_shift)
                wgt_smem[p] = w_ref[0, a]
            return carry

        lax.fori_loop(0, 2 * DSP_SB // u, invert, 0)

        last_q = sb * N_EXPERTS + (N_EXPERTS - 1)
        used_end = off_ref[last_q] + ntile_ref[last_q] * tm

        def pad_end(p, carry):
            row_smem[used_end + p] = spare_row
            wgt_smem[used_end + p] = 0.0
            row_smem[DSP_NULL + p] = spare_row
            wgt_smem[DSP_NULL + p] = 0.0
            return carry

        lax.fori_loop(0, 2 * tm, pad_end, 0)
        sched_smem[0] = -1
        sched_smem[1] = DSP_NULL
        pend_smem[0] = 0.0
        o_ref[...] = jnp.zeros_like(o_ref)
        for ys in ys_scr:
            ys[...] = jnp.zeros_like(ys)

    q = sb * N_EXPERTS + e
    base0 = off_ref[q]
    nt = ntile_ref[q]
    wg_scr[...] = wg_ref[...].astype(BF16)
    wu_scr[...] = wu_ref[...].astype(BF16)
    wd_scr[...] = wd_ref[...].astype(BF16)

    def gather(base, buf):
        for r in range(tm):
            src = jnp.minimum(row_smem[base + r], last_row)
            xs_scr[buf][r * SLAB:(r + 1) * SLAB, :] = h_ref[pl.ds(pl.multiple_of(src, SLAB), SLAB), :]

    def compute(buf):
        x = _from_slabs(xs_scr[buf], tm).astype(BF16)
        a = jax.nn.silu(_dot(x, wg_scr[...])) * _dot(x, wu_scr[...])
        _to_slabs(ys_scr[buf], _dot(a.astype(BF16), wd_scr[...]))

    def scatter(base, buf, scale):
        for j in range(tm // u):
            upd = []
            for k in range(u):
                r = j * u + k
                dst = pl.multiple_of(row_smem[base + r], SLAB)
                y = ys_scr[buf][r * SLAB:(r + 1) * SLAB, :]
                upd.append((dst, o_ref[pl.ds(dst, SLAB), :] + (wgt_smem[base + r] * scale) * y))
            for dst, v in upd:
                o_ref[pl.ds(dst, SLAB), :] = v

    n_pairs = lax.shift_right_logical(nt + 1, 1)

    @pl.when((n_pairs > 0) & (sched_smem[0] != base0))
    def _():
        gather(base0, 0)

    pend_base = sched_smem[1]
    pend_scale = pend_smem[0]

    def pair(k, carry):
        b = base0 + 2 * k * tm
        first = k == 0
        gather(b + tm, 1)
        compute(0)
        scatter(jnp.where(first, pend_base, b - tm), 1, jnp.where(first, pend_scale, 1.0))
        gather(b + 2 * tm, 0)
        compute(1)
        scatter(b, 0, 1.0)
        return carry

    lax.fori_loop(0, n_pairs, pair, 0)

    @pl.when(n_pairs > 0)
    def _():
        last = 2 * n_pairs - 1
        sched_smem[0] = base0 + (last + 1) * tm
        sched_smem[1] = base0 + last * tm
        pend_smem[0] = jnp.where(last < nt, 1.0, 0.0)

    @pl.when(e == N_EXPERTS - 1)
    def _():
        scatter(sched_smem[1], 1, pend_smem[0])


def _dsp_expert_call(layer, h_slabs, pos, w, ntile, off, cnt, w_gate, w_up, w_down):
    n_sb = pos.shape[0]
    out_rows = (DSP_SB + DSP_CH) * SLAB

    def per_sb(s, e, *prefetch):
        return (s, 0, 0)

    def w_idx(s, e, *prefetch):
        return (layer, e, 0, 0)

    grid_spec = pltpu.PrefetchScalarGridSpec(
        num_scalar_prefetch=3,
        grid=(n_sb, N_EXPERTS),
        in_specs=[
            pl.BlockSpec((None, 1, 2 * DSP_SB), per_sb, memory_space=pltpu.SMEM),
            pl.BlockSpec((None, 1, 2 * DSP_SB), per_sb, memory_space=pltpu.SMEM),
            pl.BlockSpec((DSP_SB * SLAB, LANES), lambda s, e, *prefetch: (s, 0)),
            pl.BlockSpec((None, None, D_MODEL, EXPERT_FF), w_idx),
            pl.BlockSpec((None, None, D_MODEL, EXPERT_FF), w_idx),
            pl.BlockSpec((None, None, EXPERT_FF, D_MODEL), w_idx),
        ],
        out_specs=pl.BlockSpec((None, out_rows, LANES), per_sb),
        scratch_shapes=[pltpu.SMEM((DSP_SLOTS,), jnp.int32),
                        pltpu.SMEM((DSP_SLOTS,), F32),
                        pltpu.SMEM((2,), jnp.int32),
                        pltpu.SMEM((1,), F32),
                        pltpu.VMEM((DSP_TM * SLAB, LANES), F32),
                        pltpu.VMEM((DSP_TM * SLAB, LANES), F32),
                        pltpu.VMEM((DSP_TM * SLAB, LANES), F32),
                        pltpu.VMEM((DSP_TM * SLAB, LANES), F32),
                        pltpu.VMEM((D_MODEL, EXPERT_FF), BF16),
                        pltpu.VMEM((D_MODEL, EXPERT_FF), BF16),
                        pltpu.VMEM((EXPERT_FF, D_MODEL), BF16)],
    )
    return pl.pallas_call(
        _dsp_expert_kernel,
        grid_spec=grid_spec,
        out_shape=jax.ShapeDtypeStruct((n_sb, out_rows, LANES), F32),
        compiler_params=pltpu.CompilerParams(vmem_limit_bytes=VMEM_LIMIT),
        name="moe_experts",
    )(ntile, off, cnt, pos, w, h_slabs, w_gate, w_up, w_down)


def _dsp_final_kernel(x_ref, m_ref, mods_ref, fg_ref, o_ref, *, chunks_per_row):
    g2 = _dsp_mods(mods_ref, pl.program_id(0), chunks_per_row, 3)[2]
    y = x_ref[...] + g2 * _from_slabs(m_ref, DSP_CH)
    o_ref[...] = y * lax.rsqrt(jnp.mean(y * y, axis=-1, keepdims=True) + EPS) * fg_ref[...]


def _dsp_final_call(layer, x2d, m_slabs, mods, final_g, chunks_per_row):
    n = x2d.shape[0]
    ch = DSP_CH
    per_sb = DSP_SB // ch
    return pl.pallas_call(
        functools.partial(_dsp_final_kernel, chunks_per_row=chunks_per_row),
        grid=(n // ch,),
        in_specs=[
            pl.BlockSpec((ch, D_MODEL), lambda c: (c, 0)),
            pl.BlockSpec((None, ch * SLAB, LANES), lambda c: (c // per_sb, c % per_sb, 0)),
            pl.BlockSpec((None, MODS_ROWS, N_MOD * D_MODEL), lambda c: (layer, 0, 0)),
            pl.BlockSpec((1, D_MODEL), lambda c: (0, 0)),
        ],
        out_specs=pl.BlockSpec((ch, D_MODEL), lambda c: (c, 0)),
        out_shape=jax.ShapeDtypeStruct(x2d.shape, F32),
        compiler_params=pltpu.CompilerParams(vmem_limit_bytes=VMEM_LIMIT),
        name="final_norm",
    )(x2d, m_slabs, mods, final_g)


def _sparse_moe(layer, x2d, mods, ln2, w_router, b_router, w_gate, w_up, w_down, chunks_per_row):
    n_sb = x2d.shape[0] // DSP_SB
    h_slabs, route, counts = _dsp_route_call(layer, x2d, mods, ln2, w_router, b_router,
                                             chunks_per_row)
    pos, w, ntile, off, cnt = _dispatch_tables(route, counts, n_sb)
    return _dsp_expert_call(layer, h_slabs, pos, w, ntile, off, cnt, w_gate, w_up, w_down)


LAT_PROJ_ROWS = 512


def _lat_proj_kernel(x_ref, mods_ref, ln_ref, win_ref, gb_ref, *rest, has_moe):
    zc_ref, zn_ref, zm_ref, zg_ref = rest[-4:]
    row = 1 + pl.program_id(0) // (DEC_SEQ // LAT_PROJ_ROWS)
    sh1, sc1, _ = _mod_rows(mods_ref, row, 0)
    x = x_ref[...]
    if has_moe:
        moe_ref, mods_prev_ref, xo_ref = rest[0], rest[1], rest[2]
        x = x + _mod_rows(mods_prev_ref, row, 3)[2] * _from_slabs(moe_ref, LAT_PROJ_ROWS)
        xo_ref[...] = x
    h = _rms_mod(x, ln_ref[...], sc1, sh1).astype(BF16)
    zc_ref[...] = _dot(h, win_ref[:, C_CONV:C_NA])
    zn_ref[...] = _dot(h, win_ref[:, C_NA:C_ML])
    zm_ref[...] = _dot(h, win_ref[:, C_ML:C_IG])
    zg_ref[...] = _dot(h, win_ref[:, C_IG:W_COLS]) + gb_ref[...]


def _lat_proj_call(layer, x2d, moe, mods, ln1, w_in, gate_b):
    n = x2d.shape[0]
    tb = LAT_PROJ_ROWS
    widths = (C_NA - C_CONV, C_ML - C_NA, C_IG - C_ML, W_COLS - C_IG)
    in_specs = [
        pl.BlockSpec((tb, D_MODEL), lambda i: (i, 0)),
        pl.BlockSpec((None, MODS_ROWS, N_MOD * D_MODEL), lambda i: (layer, 0, 0)),
        pl.BlockSpec((None, 1, D_MODEL), lambda i: (layer, 0, 0)),
        pl.BlockSpec((None, D_MODEL, W_COLS), lambda i: (layer, 0, 0)),
        pl.BlockSpec((None, 1, 2 * LANES), lambda i: (layer, 0, 0)),
    ]
    args = [x2d, mods, ln1, w_in, gate_b]
    out_specs = [pl.BlockSpec((tb, w), lambda i: (i, 0)) for w in widths]
    out_shape = [jax.ShapeDtypeStruct((n, w), F32) for w in widths]
    if moe is not None:
        per_sb = DSP_SB // tb
        in_specs += [pl.BlockSpec((None, tb * SLAB, LANES), lambda i: (i // per_sb, i % per_sb, 0)),
                     pl.BlockSpec((None, MODS_ROWS, N_MOD * D_MODEL), lambda i: (layer - 1, 0, 0))]
        args += [moe, mods]
        out_specs = [pl.BlockSpec((tb, D_MODEL), lambda i: (i, 0))] + out_specs
        out_shape = [jax.ShapeDtypeStruct((n, D_MODEL), F32)] + out_shape
    outs = pl.pallas_call(
        functools.partial(_lat_proj_kernel, has_moe=moe is not None),
        grid=(n // tb,),
        in_specs=in_specs,
        out_specs=out_specs,
        out_shape=out_shape,
        compiler_params=pltpu.CompilerParams(vmem_limit_bytes=VMEM_LIMIT),
        name="lat_proj",
    )(*args)
    return tuple(outs) if moe is not None else (x2d,) + tuple(outs)


def _na_row_start(r):
    rows = DEC_SEQ // GRID_W
    return min(max(r - NA_WIN_ROWS // 2, 0), rows - NA_WIN_ROWS)


def _lat_local_kernel(zc_ref, zn_ref, kc_ref, vc_ref, band_ref, convw_ref, o_ref):
    o_ref[:, 0:CONV_DIM] = _short_conv(zc_ref[...], convw_ref[...])
    rows = DEC_SEQ // GRID_W
    blk = NA_WIN_ROWS * GRID_W
    q_all = (zn_ref[:, 0:NA_DIM] * NA_HEAD_DIM ** -0.5).astype(BF16)
    k_all = zn_ref[:, NA_DIM:2 * NA_DIM].astype(BF16)
    v_all = zn_ref[:, 2 * NA_DIM:3 * NA_DIM].astype(BF16)
    kc_all = kc_ref[...].astype(BF16)
    vc_all = vc_ref[...].astype(BF16)
    heads = []
    for h in range(NA_HEADS):
        sl = slice(h * NA_HEAD_DIM, (h + 1) * NA_HEAD_DIM)
        q, k, v, kc, vc = q_all[:, sl], k_all[:, sl], v_all[:, sl], kc_all[:, sl], vc_all[:, sl]
        band = band_ref[h]
        s_ctx = _dot_nt(q, kc)
        m_ctx = jnp.max(s_ctx, axis=-1, keepdims=True)
        m_rows, l_rows, o_rows = [], [], []
        for r in range(rows):
            start = _na_row_start(r)
            off = (start - r + NA_WIN_ROWS - 1) * GRID_W
            q_r = q[r * GRID_W:(r + 1) * GRID_W]
            s_loc = _dot_nt(q_r, k[start * GRID_W:start * GRID_W + blk]) + band[:, off:off + blk]
            m = jnp.maximum(jnp.max(s_loc, axis=-1, keepdims=True), m_ctx[r * GRID_W:(r + 1) * GRID_W])
            p_loc = jnp.exp(s_loc - m)
            m_rows.append(m)
            l_rows.append(jnp.sum(p_loc, axis=-1, keepdims=True))
            o_rows.append(_dot(p_loc.astype(BF16), v[start * GRID_W:start * GRID_W + blk]))
        p_ctx = jnp.exp(s_ctx - jnp.concatenate(m_rows, axis=0))
        den = jnp.concatenate(l_rows, axis=0) + jnp.sum(p_ctx, axis=-1, keepdims=True)
        o = jnp.concatenate(o_rows, axis=0) + _dot(p_ctx.astype(BF16), vc)
        heads.append(o / den)
    o_ref[:, CONV_DIM:CONV_DIM + NA_DIM] = jnp.concatenate(heads, axis=-1)


def _lat_local_call(layer, zc, zn, cache_k, cache_v, band, conv_w):
    nb = zc.shape[0] // DEC_SEQ
    T = DEC_SEQ
    return pl.pallas_call(
        _lat_local_kernel,
        grid=(nb,),
        in_specs=[
            pl.BlockSpec((T, 3 * CONV_DIM), lambda b: (b, 0)),
            pl.BlockSpec((T, 3 * NA_DIM), lambda b: (b, 0)),
            pl.BlockSpec((None, None, PAST_LEN, NA_DIM), lambda b: (b, layer, 0, 0)),
            pl.BlockSpec((None, None, PAST_LEN, NA_DIM), lambda b: (b, layer, 0, 0)),
            pl.BlockSpec((None, NA_HEADS, GRID_W, band.shape[-1]), lambda b: (layer, 0, 0, 0)),
            pl.BlockSpec((None, 3, CONV_DIM), lambda b: (layer, 0, 0)),
        ],
        out_specs=pl.BlockSpec((T, CONV_DIM + NA_DIM), lambda b: (b, 0)),
        out_shape=jax.ShapeDtypeStruct((zc.shape[0], CONV_DIM + NA_DIM), F32),
        compiler_params=pltpu.CompilerParams(vmem_limit_bytes=VMEM_LIMIT),
        name="lat_conv_na",
    )(zc, zn, cache_k, cache_v, band, conv_w)


def _rope(x, cos, sin_signed):
    w = x.shape[-1]
    half = NA_HEAD_DIM // 2
    lane = lax.broadcasted_iota(jnp.int32, x.shape, 1)
    partner = jnp.where(lane % (2 * half) < half, pltpu.roll(x, w - half, axis=1), pltpu.roll(x, half, axis=1))
    return x * cos + partner * sin_signed


def _lat_mlstm_kernel(q_ref, k_ref, v_ref, o_ref, zg_ref, c0_ref, n0_ref, m0_ref, cos_ref, sin_ref,
                      mlg_ref, out_ref, bcol_scr, rowt_scr):
    hh = pl.program_id(1)

    @pl.when(hh == 0)
    def _():
        b_all, r_all = _gate_terms(zg_ref[:, 0:LANES], zg_ref[:, LANES:2 * LANES])
        bcol_scr[...] = b_all
        rowt_scr[...] = r_all

    b_col = bcol_scr[...]
    cos, sin_signed = cos_ref[...], sin_ref[...]
    q = _rope(q_ref[...], cos, sin_signed)
    k = _rope(k_ref[...], cos, sin_signed) * ML_HEAD_DIM ** -0.5
    v = v_ref[...]
    lane = lax.broadcasted_iota(jnp.int32, (DEC_SEQ, LANES), 1)
    hsum = None
    for d in range(N_DIRS):
        j = d * ML_HEADS + hh
        bc = jnp.sum(jnp.where(lane == j, b_col, 0.0), axis=-1, keepdims=True)
        lr = rowt_scr[pl.ds(j, 1), :]
        m0 = m0_ref[d, :, 0:1]
        ho = _mlstm_outputs(q, k, v, bc, lr, m0, c0_ref[d], n0_ref[d], d == 1, 256)
        hsum = ho if hsum is None else hsum + ho
    out_ref[...] = _head_readout(hsum, o_ref[...], mlg_ref[...])


def _lat_mlstm_call(layer, zm, zg, state_c, state_n, state_m, cos, sin_signed, ml_g):
    nb = zm.shape[0] // DEC_SEQ
    T = DEC_SEQ
    hd = ML_HEAD_DIM

    def col(part):
        return pl.BlockSpec((T, hd), lambda b, h: (b, part * ML_HEADS + h))

    return pl.pallas_call(
        _lat_mlstm_kernel,
        grid=(nb, ML_HEADS),
        in_specs=[
            col(0), col(1), col(2), col(3),
            pl.BlockSpec((T, 2 * LANES), lambda b, h: (b, 0)),
            pl.BlockSpec((None, None, N_DIRS, None, hd, hd), lambda b, h: (b, layer, 0, h, 0, 0)),
            pl.BlockSpec((None, None, N_DIRS, None, 1, hd), lambda b, h: (b, layer, 0, h, 0, 0)),
            pl.BlockSpec((None, None, N_DIRS, None, 1, LANES), lambda b, h: (b, layer, 0, h, 0, 0)),
            pl.BlockSpec((T, hd), lambda b, h: (0, 0)),
            pl.BlockSpec((T, hd), lambda b, h: (0, 0)),
            pl.BlockSpec((None, 1, hd), lambda b, h: (layer, 0, h)),
        ],
        out_specs=pl.BlockSpec((T, hd), lambda b, h: (b, h)),
        out_shape=jax.ShapeDtypeStruct((zm.shape[0], ML_DIM), F32),
        scratch_shapes=[pltpu.VMEM((T, LANES), F32), pltpu.VMEM((LANES, T), F32)],
        compiler_params=pltpu.CompilerParams(vmem_limit_bytes=VMEM_LIMIT),
        name="lat_mlstm",
    )(zm, zm, zm, zm, zg, state_c, state_n, state_m, cos, sin_signed, ml_g)


def _lat_merge_kernel(x_ref, loc_ref, ml_ref, mods_ref, wout_ref, o_ref):
    row = 1 + pl.program_id(0) // (DEC_SEQ // LAT_PROJ_ROWS)
    g1 = _mod_rows(mods_ref, row, 0)[2]
    split = CONV_DIM + NA_DIM
    y = _dot(loc_ref[...].astype(BF16), wout_ref[0:split, :]) + _dot(ml_ref[...].astype(BF16), wout_ref[split:, :])
    o_ref[...] = x_ref[...] + g1 * y


def _lat_merge_call(layer, x2d, loc, ml, mods, w_out):
    n = x2d.shape[0]
    tb = LAT_PROJ_ROWS
    return pl.pallas_call(
        _lat_merge_kernel,
        grid=(n // tb,),
        in_specs=[
            pl.BlockSpec((tb, D_MODEL), lambda i: (i, 0)),
            pl.BlockSpec((tb, CONV_DIM + NA_DIM), lambda i: (i, 0)),
            pl.BlockSpec((tb, ML_DIM), lambda i: (i, 0)),
            pl.BlockSpec((None, MODS_ROWS, N_MOD * D_MODEL), lambda i: (layer, 0, 0)),
            pl.BlockSpec((None, D_MODEL, D_MODEL), lambda i: (layer, 0, 0)),
        ],
        out_specs=pl.BlockSpec((tb, D_MODEL), lambda i: (i, 0)),
        out_shape=jax.ShapeDtypeStruct(x2d.shape, F32),
        compiler_params=pltpu.CompilerParams(vmem_limit_bytes=VMEM_LIMIT),
        name="lat_merge",
    )(x2d, loc, ml, mods, w_out)


def _pad_lanes(a, width):
    return jnp.pad(a, [(0, 0)] * (a.ndim - 1) + [(0, width - a.shape[-1])])


def _pack_w_in(w_in):
    main = w_in[..., :C_IG]
    gates = w_in[..., C_IG:]
    packed = jnp.concatenate([main, _pad_lanes(gates[..., :N_GATE], LANES),
                              _pad_lanes(gates[..., N_GATE:], LANES)], axis=-1)
    return packed.astype(BF16)


def _pack_gate_bias(ml_gate_b):
    gb = ml_gate_b.reshape(DEPTH, 2, N_GATE).astype(F32)
    return jnp.concatenate([_pad_lanes(gb[:, 0], LANES), _pad_lanes(gb[:, 1], LANES)], axis=-1)[:, None, :]


def _rpb_band(na_rpb):
    cols = np.arange(GRID_W)
    col_idx = np.clip(cols[None, :] - cols[:, None] + NA_WIN_COLS - 1, 0, 2 * NA_WIN_COLS - 2)
    col_start = np.clip(cols - NA_WIN_COLS // 2, 0, GRID_W - NA_WIN_COLS)
    col_mask = (cols[None, :] >= col_start[:, None]) & (cols[None, :] < col_start[:, None] + NA_WIN_COLS)
    pick = (col_idx[None] == np.arange(2 * NA_WIN_COLS - 1)[:, None, None]).astype(np.float32)
    t = jnp.einsum('lhri,iqk->lhqrk', na_rpb.astype(F32), jnp.asarray(pick), precision=HIGHEST)
    t = jnp.where(col_mask[None, None, :, None, :], t, NEG_INF)
    t = t.reshape(DEPTH, NA_HEADS, GRID_W, (2 * NA_WIN_ROWS - 1) * GRID_W)
    return _pad_lanes(t, 2 * NA_WIN_ROWS * GRID_W)


def _rope_tables():
    t = np.arange(DEC_SEQ)
    pos = np.stack([t // GRID_W, t % GRID_W], axis=-1).astype(np.float32)
    nf = ML_HEAD_DIM // 4
    inv = jnp.asarray(ROPE_BASE, F32) ** (-jnp.arange(nf, dtype=F32) / nf)
    ang = jnp.asarray(pos)[:, :, None] * inv
    cos = jnp.cos(ang)
    sin = jnp.sin(ang)
    cos_t = jnp.concatenate([cos, cos], axis=-1).reshape(DEC_SEQ, ML_HEAD_DIM)
    sin_t = jnp.concatenate([-sin, sin], axis=-1).reshape(DEC_SEQ, ML_HEAD_DIM)
    return cos_t, sin_t


def kernel(x_prompt, x_sample, cache_k, cache_v, state_C, state_n, state_m, c, c_ctx, w_mod, b_mod,
           ln1_g, w_in, conv_w, na_rpb, ml_gate_b, ml_norm_g, w_out, ln2_g, w_rg, b_rg, w_re, b_re,
           w_gate, w_up, w_down, final_g):
    nb_ctx = x_prompt.shape[0]
    nb_lat = x_sample.shape[0]
    assert 1 + nb_lat <= MODS_ROWS

    cvecs = jnp.concatenate([c_ctx[None, :], c,
                             jnp.zeros((MODS_ROWS - 1 - nb_lat, D_MODEL), F32)], axis=0)
    mods = _mods_call(cvecs, w_mod, b_mod)

    w_in_p = _pack_w_in(w_in)
    w_out_b = w_out.astype(BF16)
    gate_b = _pack_gate_bias(ml_gate_b)
    ln1 = ln1_g.reshape(DEPTH, 1, D_MODEL)
    ln2 = ln2_g.reshape(DEPTH, 1, D_MODEL)
    ml_g = ml_norm_g.reshape(DEPTH, 1, ML_DIM)
    w_router = _pad_lanes(jnp.concatenate([w_re, w_rg], axis=-1), LANES)
    b_router = _pad_lanes(jnp.concatenate([b_re, b_rg], axis=-1), LANES)[:, None, :]
    fg = final_g.reshape(1, D_MODEL)
    band = _rpb_band(na_rpb)
    cos_t, sin_t = _rope_tables()
    ck = cache_k.reshape(nb_lat, DEPTH, PAST_LEN, NA_DIM)
    cv = cache_v.reshape(nb_lat, DEPTH, PAST_LEN, NA_DIM)
    st_n = state_n.reshape(nb_lat, DEPTH, N_DIRS, ML_HEADS, 1, ML_HEAD_DIM)
    st_m = jnp.broadcast_to(state_m[..., None, None], (nb_lat, DEPTH, N_DIRS, ML_HEADS, 1, LANES))

    xp = x_prompt
    xs = x_sample.reshape(nb_lat * DEC_SEQ, D_MODEL)
    states = None
    lat_chunks = DEC_SEQ // DSP_CH
    moe_p = moe_s = None
    for l in range(DEPTH):
        xp, states = _ctx_mixer_call(l, xp, moe_p, mods, ln1, w_in_p, conv_w, gate_b, ml_g, w_out_b,
                                     states)
        moe_p = _sparse_moe(l, xp.reshape(nb_ctx * SEQ, D_MODEL), mods, ln2, w_router, b_router,
                            w_gate, w_up, w_down, None)

        xs, zc, zn, zm, zg = _lat_proj_call(l, xs, moe_s, mods, ln1, w_in_p, gate_b)
        loc = _lat_local_call(l, zc, zn, ck, cv, band, conv_w)
        ml = _lat_mlstm_call(l, zm, zg, state_C, st_n, st_m, cos_t, sin_t, ml_g)
        xs = _lat_merge_call(l, xs, loc, ml, mods, w_out_b)
        moe_s = _sparse_moe(l, xs, mods, ln2, w_router, b_router, w_gate, w_up, w_down, lat_chunks)

    xp = _dsp_final_call(DEPTH - 1, xp.reshape(nb_ctx * SEQ, D_MODEL), moe_p, mods, fg, None)
    xs = _dsp_final_call(DEPTH - 1, xs, moe_s, mods, fg, lat_chunks)
    xp = xp.reshape(nb_ctx, SEQ, D_MODEL)

    new_k, new_v, new_c, new_n, new_m = states
    return (xp, xs.reshape(nb_lat, DEC_SEQ, D_MODEL),
            new_k.reshape(nb_ctx, DEPTH, SEQ, NA_HEADS, NA_HEAD_DIM),
            new_v.reshape(nb_ctx, DEPTH, SEQ, NA_HEADS, NA_HEAD_DIM),
            new_c,
            new_n.reshape(nb_ctx, DEPTH, N_DIRS, ML_HEADS, ML_HEAD_DIM),
            new_m[..., 0].reshape(nb_ctx, DEPTH, N_DIRS, ML_HEADS))
```

```python
import functools

import numpy as np
import jax
import jax.numpy as jnp
from jax import lax
from jax.experimental import pallas as pl
from jax.experimental.pallas import tpu as pltpu

F32 = jnp.float32
BF16 = jnp.bfloat16

D_MODEL = 1024
SEQ = 256
DEPTH = 4
DEC_SEQ = 1024
PAST_LEN = 512
GRID_W = 64
CONV_DIM = 256
NA_HEADS = 4
NA_HEAD_DIM = 64
NA_DIM = NA_HEADS * NA_HEAD_DIM
NA_WIN_ROWS = 8
NA_WIN_COLS = 16
ML_HEADS = 4
ML_HEAD_DIM = 128
ML_DIM = ML_HEADS * ML_HEAD_DIM
N_DIRS = 2
N_GROUPS = 4
EXPERTS_PER_GROUP = 4
N_EXPERTS = N_GROUPS * EXPERTS_PER_GROUP
EXPERT_FF = 512
ROPE_BASE = 10000.0
EPS = 1e-6
N_MOD = 6

LANES = 128
N_GATE = N_DIRS * ML_HEADS
C_CONV = 0
C_NA = 3 * CONV_DIM
C_ML = C_NA + 3 * NA_DIM
C_IG = C_ML + 4 * ML_DIM
CUMSUM_BLOCK = 256
MODS_ROWS = 8
VMEM_LIMIT = 56 * 1024 * 1024

NEG_INF = float("-inf")
HIGHEST = lax.Precision.HIGHEST


def _dot(a, b):
    return jnp.dot(a, b, preferred_element_type=F32)


def _dot_nt(a, b):
    return lax.dot_general(a, b, (((1,), (1,)), ((), ())), preferred_element_type=F32)


def _dot_split(a, b):
    a_hi = a.astype(BF16)
    a_lo = (a - a_hi.astype(F32)).astype(BF16)
    b_hi = b.astype(BF16)
    b_lo = (b - b_hi.astype(F32)).astype(BF16)
    return _dot(jnp.concatenate([a_hi, a_hi, a_lo], axis=1), jnp.concatenate([b_hi, b_lo, b_hi], axis=0))


def _rms_mod(x, g, sc, sh):
    y = x * lax.rsqrt(jnp.mean(x * x, axis=-1, keepdims=True) + EPS)
    return (y * g) * (1.0 + sc) + sh


def _mod_rows(mods_ref, row, first):
    return [mods_ref[pl.ds(row, 1), pl.ds((first + j) * D_MODEL, D_MODEL)] for j in range(3)]


def _mods_kernel(cv_ref, w_ref, b_ref, o_ref):
    cv = cv_ref[...]
    s = cv * jax.nn.sigmoid(cv)
    o_ref[...] = _dot(s.astype(BF16), w_ref[...].astype(BF16)) + b_ref[...]


def _mods_call(cvecs, w_mod, b_mod):
    tn = 1536
    n = N_MOD * D_MODEL
    return pl.pallas_call(
        _mods_kernel,
        grid=(DEPTH, n // tn),
        in_specs=[pl.BlockSpec((MODS_ROWS, D_MODEL), lambda l, j: (0, 0)),
                  pl.BlockSpec((None, D_MODEL, tn), lambda l, j: (l, 0, j)),
                  pl.BlockSpec((None, 1, tn), lambda l, j: (l, 0, j))],
        out_specs=pl.BlockSpec((None, MODS_ROWS, tn), lambda l, j: (l, 0, j)),
        out_shape=jax.ShapeDtypeStruct((DEPTH, MODS_ROWS, n), F32),
        compiler_params=pltpu.CompilerParams(vmem_limit_bytes=VMEM_LIMIT),
        name="adaln_mods",
    )(cvecs, w_mod, b_mod.reshape(DEPTH, 1, n))


def _short_conv(zc, cw):
    T = zc.shape[0]
    cb = zc[:, 0:CONV_DIM]
    u = zc[:, CONV_DIM:2 * CONV_DIM] * zc[:, 2 * CONV_DIM:3 * CONV_DIM]
    t = lax.broadcasted_iota(jnp.int32, u.shape, 0)
    u_prev = jnp.where(t == 0, 0.0, pltpu.roll(u, 1, axis=0))
    u_next = jnp.where(t == T - 1, 0.0, pltpu.roll(u, T - 1, axis=0))
    return cb * (cw[0:1, :] * u_prev + cw[1:2, :] * u + cw[2:3, :] * u_next)


def _softmax_attention(nq, nk, nv):
    outs = []
    for h in range(NA_HEADS):
        sl = slice(h * NA_HEAD_DIM, (h + 1) * NA_HEAD_DIM)
        q = (nq[:, sl] * NA_HEAD_DIM ** -0.5).astype(BF16)
        s = _dot_nt(q, nk[:, sl].astype(BF16))
        p = jnp.exp(s - jnp.max(s, axis=-1, keepdims=True))
        o = _dot(p.astype(BF16), nv[:, sl].astype(BF16)) / jnp.sum(p, axis=-1, keepdims=True)
        outs.append(o)
    return jnp.concatenate(outs, axis=-1)


def _log_sigmoid(x):
    return jnp.minimum(x, 0.0) - jnp.log(1.0 + jnp.exp(-jnp.abs(x)))


def _gate_terms(zi, zf):
    T = zi.shape[0]
    lf = _log_sigmoid(zf)
    blk = min(T, CUMSUM_BLOCK)
    r = lax.broadcasted_iota(jnp.int32, (blk, blk), 0)
    c = lax.broadcasted_iota(jnp.int32, (blk, blk), 1)
    tril = jnp.where(c <= r, 1.0, 0.0).astype(F32)
    parts, carry = [], None
    for r0 in range(0, T, blk):
        b = jnp.dot(tril, lf[r0:r0 + blk], precision=HIGHEST, preferred_element_type=F32)
        if carry is not None:
            b = b + carry
        carry = b[blk - 1:blk, :]
        parts.append(b)
    b_fwd = parts[0] if len(parts) == 1 else jnp.concatenate(parts, axis=0)
    b_bwd = (b_fwd[T - 1:T, :] - b_fwd) + lf
    lane = lax.broadcasted_iota(jnp.int32, zi.shape, 1)
    b_col = jnp.where(lane < ML_HEADS, b_fwd, b_bwd)
    row_t = jnp.transpose(zi - b_col)
    return b_col, row_t


def _mlstm_outputs(q, k, v, b_col, li_row, m0, c0, n0, backward, q_block):
    T = q.shape[0]
    qb, kb, vb = q.astype(BF16), k.astype(BF16), v.astype(BF16)
    c0b = None if c0 is None else c0.astype(BF16)
    outs = []
    for r0 in range(0, T, q_block):
        ks, ke = (r0, T) if backward else (0, r0 + q_block)
        bq = b_col[r0:r0 + q_block]
        d = bq + li_row[:, ks:ke]
        t_idx = r0 + lax.broadcasted_iota(jnp.int32, d.shape, 0)
        s_idx = ks + lax.broadcasted_iota(jnp.int32, d.shape, 1)
        d = jnp.where((s_idx >= t_idx) if backward else (s_idx <= t_idx), d, NEG_INF)
        inter = bq + m0
        m_t = jnp.maximum(jnp.max(d, axis=-1, keepdims=True), inter)
        w = jnp.exp(d - m_t)
        s = _dot_nt(qb[r0:r0 + q_block], kb[ks:ke]) * w
        num = _dot(s.astype(BF16), vb[ks:ke])
        den = jnp.sum(s, axis=-1, keepdims=True)
        if c0 is not None:
            a = jnp.exp(inter - m_t)
            num = num + a * _dot(qb[r0:r0 + q_block], c0b)
            den = den + a * jnp.sum(q[r0:r0 + q_block] * n0, axis=-1, keepdims=True)
        outs.append(num / jnp.maximum(jnp.abs(den), jnp.exp(-m_t)))
    return outs[0] if len(outs) == 1 else jnp.concatenate(outs, axis=0)


def _mlstm_state(k, v, b_col, li_col, backward):
    T = k.shape[0]
    b_tot = b_col[0:1] if backward else b_col[T - 1:T]
    g = (b_tot - b_col) + li_col
    m_new = jnp.maximum(b_tot, jnp.max(g, axis=0, keepdims=True))
    kw = k * jnp.exp(g - m_new)
    c_new = _dot(jnp.transpose(kw).astype(BF16), v.astype(BF16))
    n_new = jnp.sum(kw, axis=0, keepdims=True)
    return c_new, n_new, m_new


def _head_readout(h, mo, g):
    hn = h * lax.rsqrt(jnp.mean(h * h, axis=-1, keepdims=True) + EPS)
    return hn * g * jax.nn.sigmoid(mo)


def _ctx_mixer_kernel(x_ref, mods_ref, ln_ref, win_ref, wgate_ref, convw_ref, gb_ref, mlg_ref, wout_ref,
                      *rest, has_moe):
    xo_ref, ko_ref, vo_ref, co_ref, no_ref, mo_ref = rest[-6:]
    T = SEQ
    sh1, sc1, g1 = _mod_rows(mods_ref, 0, 0)
    x = x_ref[...]
    if has_moe:
        moe_ref, mods_prev_ref = rest[0], rest[1]
        x = x + _mod_rows(mods_prev_ref, 0, 3)[2] * _from_slabs(moe_ref, T)
    h = _rms_mod(x, ln_ref[...], sc1, sh1).astype(BF16)

    conv_o = _short_conv(_dot(h, win_ref[:, C_CONV:C_NA]), convw_ref[...])

    zn = _dot(h, win_ref[:, C_NA:C_ML])
    nk, nv = zn[:, NA_DIM:2 * NA_DIM], zn[:, 2 * NA_DIM:3 * NA_DIM]
    ko_ref[...] = nk
    vo_ref[...] = nv
    na_o = _softmax_attention(zn[:, 0:NA_DIM], nk, nv)

    zg = _dot(h, wgate_ref[...])
    zi = zg[:, 0:LANES] + gb_ref[:, 0:LANES]
    zf = zg[:, LANES:2 * LANES] + gb_ref[:, LANES:2 * LANES]
    b_col, row_t = _gate_terms(zi, zf)
    m0 = jnp.zeros((1, 1), F32)

    zm = _dot(h, win_ref[:, C_ML:C_IG])
    ml_parts = []
    for hh in range(ML_HEADS):
        zq, zk, zv, zo = (zm[:, p * ML_DIM + hh * ML_HEAD_DIM:p * ML_DIM + (hh + 1) * ML_HEAD_DIM]
                          for p in range(4))
        zk = zk * ML_HEAD_DIM ** -0.5
        hsum = None
        for d in range(N_DIRS):
            j = d * ML_HEADS + hh
            bc = b_col[:, j:j + 1]
            ho = _mlstm_outputs(zq, zk, zv, bc, row_t[j:j + 1, :], m0, None, None, d == 1, T)
            hsum = ho if hsum is None else hsum + ho
            c_new, n_new, m_new = _mlstm_state(zk, zv, bc, zi[:, j:j + 1], d == 1)
            co_ref[d, hh] = c_new
            no_ref[j:j + 1, :] = n_new
            mo_ref[j:j + 1, :] = jnp.broadcast_to(m_new, (1, LANES))
        ml_parts.append(_head_readout(hsum, zo, mlg_ref[:, hh * ML_HEAD_DIM:(hh + 1) * ML_HEAD_DIM]))

    mix = jnp.concatenate([conv_o, na_o] + ml_parts, axis=-1).astype(BF16)
    xo_ref[...] = x + g1 * _dot(mix, wout_ref[...])


def _ctx_mixer_call(layer, x, moe, mods, ln1, w_in, conv_w, gate_b, ml_g, w_out, prev):
    B = x.shape[0]
    T = SEQ
    G = None
    state_shapes = [
        jax.ShapeDtypeStruct((B, DEPTH, T, NA_DIM), F32),
        jax.ShapeDtypeStruct((B, DEPTH, T, NA_DIM), F32),
        jax.ShapeDtypeStruct((B, DEPTH, N_DIRS, ML_HEADS, ML_HEAD_DIM, ML_HEAD_DIM), F32),
        jax.ShapeDtypeStruct((B, DEPTH, N_GATE, ML_HEAD_DIM), F32),
        jax.ShapeDtypeStruct((B, DEPTH, N_GATE, LANES), F32),
    ]
    in_specs = [
        pl.BlockSpec((G, T, D_MODEL), lambda b: (b, 0, 0)),
        pl.BlockSpec((None, MODS_ROWS, N_MOD * D_MODEL), lambda b: (layer, 0, 0)),
        pl.BlockSpec((None, 1, D_MODEL), lambda b: (layer, 0, 0)),
        pl.BlockSpec((None, D_MODEL, C_IG), lambda b: (layer, 0, 0)),
        pl.BlockSpec((None, D_MODEL, 2 * LANES), lambda b: (layer, 0, 0)),
        pl.BlockSpec((None, 3, CONV_DIM), lambda b: (layer, 0, 0)),
        pl.BlockSpec((None, 1, 2 * LANES), lambda b: (layer, 0, 0)),
        pl.BlockSpec((None, 1, ML_DIM), lambda b: (layer, 0, 0)),
        pl.BlockSpec((None, D_MODEL, D_MODEL), lambda b: (layer, 0, 0)),
    ]
    args = [x, mods, ln1, w_in[0], w_in[1], conv_w, gate_b, ml_g, w_out]
    if moe is not None:
        per_sb = DSP_SB // T
        in_specs += [pl.BlockSpec((None, T * SLAB, LANES), lambda b: (b // per_sb, b % per_sb, 0)),
                     pl.BlockSpec((None, MODS_ROWS, N_MOD * D_MODEL), lambda b: (layer - 1, 0, 0))]
        args += [moe, mods]
    aliases = {}
    if prev is not None:
        in_specs += [pl.BlockSpec(memory_space=pl.ANY)] * len(prev)
        aliases = {len(args) + i: 1 + i for i in range(len(prev))}
        args += list(prev)
    out_specs = [
        pl.BlockSpec((G, T, D_MODEL), lambda b: (b, 0, 0)),
        pl.BlockSpec((G, None, T, NA_DIM), lambda b: (b, layer, 0, 0)),
        pl.BlockSpec((G, None, T, NA_DIM), lambda b: (b, layer, 0, 0)),
        pl.BlockSpec((G, None, N_DIRS, ML_HEADS, ML_HEAD_DIM, ML_HEAD_DIM),
                     lambda b: (b, layer, 0, 0, 0, 0)),
        pl.BlockSpec((G, None, N_GATE, ML_HEAD_DIM), lambda b: (b, layer, 0, 0)),
        pl.BlockSpec((G, None, N_GATE, LANES), lambda b: (b, layer, 0, 0)),
    ]
    outs = pl.pallas_call(
        functools.partial(_ctx_mixer_kernel, has_moe=moe is not None),
        grid=(B,),
        in_specs=in_specs,
        out_specs=out_specs,
        out_shape=[jax.ShapeDtypeStruct(x.shape, F32)] + state_shapes,
        input_output_aliases=aliases,
        compiler_params=pltpu.CompilerParams(vmem_limit_bytes=VMEM_LIMIT),
        name="ctx_mixer",
    )(*args)
    return outs[0], tuple(outs[1:])


def _route(lg):
    lane = lax.broadcasted_iota(jnp.int32, lg.shape, 1)
    lane_f = lane.astype(F32)
    big = float(LANES)
    is_g = (lane >= N_EXPERTS) & (lane < N_EXPERTS + N_GROUPS)
    glm = jnp.where(is_g, lg, NEG_INF)
    gmax = jnp.max(glm, axis=-1, keepdims=True)
    g_top = jnp.min(jnp.where(glm == gmax, lane_f, big), axis=-1, keepdims=True) - N_EXPERTS
    gp = 1.0 / jnp.sum(jnp.where(is_g, jnp.exp(lg - gmax), 0.0), axis=-1, keepdims=True)
    grp = (lane >> (EXPERTS_PER_GROUP.bit_length() - 1)).astype(F32)
    in_grp = (lane < N_EXPERTS) & (grp == g_top)
    e1 = jnp.where(in_grp, lg, NEG_INF)
    v1 = jnp.max(e1, axis=-1, keepdims=True)
    i1 = jnp.min(jnp.where(e1 == v1, lane_f, big), axis=-1, keepdims=True)
    e2 = jnp.where(in_grp & (lane_f != i1), lg, NEG_INF)
    v2 = jnp.max(e2, axis=-1, keepdims=True)
    i2 = jnp.min(jnp.where(e2 == v2, lane_f, big), axis=-1, keepdims=True)
    t = jnp.exp(v2 - v1)
    w1 = 1.0 / (1.0 + t)
    w2 = t / (1.0 + t)
    return i1, i2, gp * w1, gp * w2


SLAB = D_MODEL // LANES
DSP_SB = 2048
DSP_TM = 160
DSP_CH = 512
DSP_PMAX = 2 * DSP_SB + N_EXPERTS * DSP_TM
DSP_NULL = DSP_PMAX + 2 * DSP_TM
DSP_SLOTS = DSP_NULL + 2 * DSP_TM
DSP_UNROLL = 8


def _to_slabs(ref, x):
    n = x.shape[0]
    for s in range(SLAB):
        ref[pl.ds(s, n, stride=SLAB), :] = x[:, s * LANES:(s + 1) * LANES]


def _from_slabs(ref, n):
    return jnp.concatenate([ref[pl.ds(s, n, stride=SLAB), :] for s in range(SLAB)], axis=-1)


def _dsp_mods(mods_ref, chunk, chunks_per_row, first):
    row = 0 if chunks_per_row is None else 1 + chunk // chunks_per_row
    return _mod_rows(mods_ref, row, first)


def _dsp_route_kernel(x_ref, mods_ref, ln_ref, wr_ref, br_ref, h_ref, r_ref, cnt_ref, run_scr, *,
                      chunks_per_row):
    c = pl.program_id(0)
    ch = DSP_CH

    @pl.when(c % (DSP_SB // ch) == 0)
    def _():
        run_scr[...] = jnp.zeros_like(run_scr)

    sh2, sc2, _ = _dsp_mods(mods_ref, c, chunks_per_row, 3)
    h2 = _rms_mod(x_ref[...], ln_ref[...], sc2, sh2)
    _to_slabs(h_ref, h2)
    lg = _dot_split(h2, wr_ref[...]) + br_ref[...]
    i1, i2, w1, w2 = _route(lg)
    lane = lax.broadcasted_iota(jnp.int32, lg.shape, 1)
    lane_f = lane.astype(F32)
    oh1 = jnp.where(lane_f == i1, 1.0, 0.0)
    oh2 = jnp.where(lane_f == i2, 1.0, 0.0)
    both = oh1 + oh2
    r = lax.broadcasted_iota(jnp.int32, (ch, ch), 0)
    s = lax.broadcasted_iota(jnp.int32, (ch, ch), 1)
    earlier = jnp.where(s < r, 1.0, 0.0).astype(BF16)
    run = run_scr[0:1, :]
    before = _dot(earlier, both.astype(BF16)) + run
    rank1 = jnp.sum(before * oh1, axis=-1, keepdims=True)
    rank2 = jnp.sum(before * oh2, axis=-1, keepdims=True)
    run = run + jnp.sum(both, axis=0, keepdims=True)
    run_scr[0:1, :] = run
    cnt_ref[...] = jnp.broadcast_to(run, cnt_ref.shape)
    cols = (i1, i2, w1, w2, rank1, rank2)
    out = jnp.zeros(lg.shape, F32)
    for j, col in enumerate(cols):
        out = jnp.where(lane == j, col, out)
    r_ref[...] = out


def _dsp_route_call(layer, x2d, mods, ln2, w_router, b_router, chunks_per_row):
    n = x2d.shape[0]
    ch = DSP_CH
    per_sb = DSP_SB // ch
    return pl.pallas_call(
        functools.partial(_dsp_route_kernel, chunks_per_row=chunks_per_row),
        grid=(n // ch,),
        in_specs=[
            pl.BlockSpec((ch, D_MODEL), lambda c: (c, 0)),
            pl.BlockSpec((None, MODS_ROWS, N_MOD * D_MODEL), lambda c: (layer, 0, 0)),
            pl.BlockSpec((None, 1, D_MODEL), lambda c: (layer, 0, 0)),
            pl.BlockSpec((None, D_MODEL, LANES), lambda c: (layer, 0, 0)),
            pl.BlockSpec((None, 1, LANES), lambda c: (layer, 0, 0)),
        ],
        out_specs=[pl.BlockSpec((ch * SLAB, LANES), lambda c: (c, 0)),
                   pl.BlockSpec((ch, LANES), lambda c: (c, 0)),
                   pl.BlockSpec((None, SLAB, LANES), lambda c: (c // per_sb, 0, 0))],
        out_shape=[jax.ShapeDtypeStruct((n * SLAB, LANES), F32),
                   jax.ShapeDtypeStruct((n, LANES), F32),
                   jax.ShapeDtypeStruct((n // DSP_SB, SLAB, LANES), F32)],
        scratch_shapes=[pltpu.VMEM((SLAB, LANES), F32)],
        compiler_params=pltpu.CompilerParams(vmem_limit_bytes=VMEM_LIMIT),
        name="moe_route",
    )(x2d, mods, ln2, w_router, b_router)


def _dispatch_tables(route, counts, n_sb):
    e = route[:, 0:2].astype(jnp.int32).reshape(n_sb, 2 * DSP_SB)
    rank = route[:, 4:6].astype(jnp.int32).reshape(n_sb, 2 * DSP_SB)
    w = route[:, 2:4].reshape(n_sb, 1, 2 * DSP_SB)
    cnt = counts[:, 0, :N_EXPERTS].astype(jnp.int32)
    ntile = (cnt + (DSP_TM - 1)) // DSP_TM
    off = (jnp.cumsum(ntile, axis=1) - ntile) * DSP_TM
    is_e = e[:, None, :] == jnp.arange(N_EXPERTS, dtype=jnp.int32)[None, :, None]
    pos = (jnp.sum(jnp.where(is_e, off[:, :, None], 0), axis=1) + rank).reshape(n_sb, 1, 2 * DSP_SB)
    return pos, w, ntile.reshape(-1), off.reshape(-1), cnt.reshape(-1)


def _dsp_expert_kernel(ntile_ref, off_ref, cnt_ref, pos_ref, w_ref, h_ref, wg_ref, wu_ref, wd_ref,
                       o_ref, row_smem, wgt_smem, sched_smem, pend_smem, xs0, xs1, ys0, ys1,
                       wg_scr, wu_scr, wd_scr):
    sb = pl.program_id(0)
    e = pl.program_id(1)
    tm, u = DSP_TM, DSP_UNROLL
    xs_scr, ys_scr = (xs0, xs1), (ys0, ys1)
    slab_shift = SLAB.bit_length() - 1
    spare_row = DSP_SB * SLAB
    last_row = (DSP_SB - 1) * SLAB

    @pl.when(e == 0)
    def _():
        for ee in range(N_EXPERTS):
            q = sb * N_EXPERTS + ee

            def pad(p, carry):
                row_smem[p] = spare_row
                wgt_smem[p] = 0.0
                return carry

            lax.fori_loop(off_ref[q] + cnt_ref[q], off_ref[q] + ntile_ref[q] * tm, pad, 0)

        def invert(j, carry):
            for k in range(u):
                a = j * u + k
                p = pos_ref[0, a]
                row_smem[p] = lax.shift_left(lax.shift_right_logical(a, 1), slab_shift)
                wgt_smem[p] = w_ref[0, a]
            return carry

        lax.fori_loop(0, 2 * DSP_SB // u, invert, 0)

        last_q = sb * N_EXPERTS + (N_EXPERTS - 1)
        used_end = off_ref[last_q] + ntile_ref[last_q] * tm

        def pad_end(p, carry):
            row_smem[used_end + p] = spare_row
            wgt_smem[used_end + p] = 0.0
            row_smem[DSP_NULL + p] = spare_row
            wgt_smem[DSP_NULL + p] = 0.0
            return carry

        lax.fori_loop(0, 2 * tm, pad_end, 0)
        sched_smem[0] = -1
        sched_smem[1] = DSP_NULL
        pend_smem[0] = 0.0
        o_ref[...] = jnp.zeros_like(o_ref)
        for ys in ys_scr:
            ys[...] = jnp.zeros_like(ys)

    q = sb * N_EXPERTS + e
    base0 = off_ref[q]
    nt = ntile_ref[q]
    wg_scr[...] = wg_ref[...].astype(BF16)
    wu_scr[...] = wu_ref[...].astype(BF16)
    wd_scr[...] = wd_ref[...].astype(BF16)

    def gather(base, buf):
        for r in range(tm):
            src = jnp.minimum(row_smem[base + r], last_row)
            xs_scr[buf][r * SLAB:(r + 1) * SLAB, :] = h_ref[pl.ds(pl.multiple_of(src, SLAB), SLAB), :]

    def compute(buf):
        x = _from_slabs(xs_scr[buf], tm).astype(BF16)
        a = jax.nn.silu(_dot(x, wg_scr[...])) * _dot(x, wu_scr[...])
        _to_slabs(ys_scr[buf], _dot(a.astype(BF16), wd_scr[...]))

    def scatter(base, buf, scale):
        for j in range(tm // u):
            upd = []
            for k in range(u):
                r = j * u + k
                dst = pl.multiple_of(row_smem[base + r], SLAB)
                y = ys_scr[buf][r * SLAB:(r + 1) * SLAB, :]
                upd.append((dst, o_ref[pl.ds(dst, SLAB), :] + (wgt_smem[base + r] * scale) * y))
            for dst, v in upd:
                o_ref[pl.ds(dst, SLAB), :] = v

    n_pairs = lax.shift_right_logical(nt + 1, 1)

    @pl.when((n_pairs > 0) & (sched_smem[0] != base0))
    def _():
        gather(base0, 0)

    pend_base = sched_smem[1]
    pend_scale = pend_smem[0]

    def pair(k, carry):
        b = base0 + 2 * k * tm
        first = k == 0
        gather(b + tm, 1)
        compute(0)
        scatter(jnp.where(first, pend_base, b - tm), 1, jnp.where(first, pend_scale, 1.0))
        gather(b + 2 * tm, 0)
        compute(1)
        scatter(b, 0, 1.0)
        return carry

    lax.fori_loop(0, n_pairs, pair, 0)

    @pl.when(n_pairs > 0)
    def _():
        last = 2 * n_pairs - 1
        sched_smem[0] = base0 + (last + 1) * tm
        sched_smem[1] = base0 + last * tm
        pend_smem[0] = jnp.where(last < nt, 1.0, 0.0)

    @pl.when(e == N_EXPERTS - 1)
    def _():
        scatter(sched_smem[1], 1, pend_smem[0])


def _dsp_expert_call(layer, h_slabs, pos, w, ntile, off, cnt, w_gate, w_up, w_down):
    n_sb = pos.shape[0]
    out_rows = (DSP_SB + DSP_CH) * SLAB

    def per_sb(s, e, *prefetch):
        return (s, 0, 0)

    def w_idx(s, e, *prefetch):
        return (layer, e, 0, 0)

    grid_spec = pltpu.PrefetchScalarGridSpec(
        num_scalar_prefetch=3,
        grid=(n_sb, N_EXPERTS),
        in_specs=[
            pl.BlockSpec((None, 1, 2 * DSP_SB), per_sb, memory_space=pltpu.SMEM),
            pl.BlockSpec((None, 1, 2 * DSP_SB), per_sb, memory_space=pltpu.SMEM),
            pl.BlockSpec((DSP_SB * SLAB, LANES), lambda s, e, *prefetch: (s, 0)),
            pl.BlockSpec((None, None, D_MODEL, EXPERT_FF), w_idx),
            pl.BlockSpec((None, None, D_MODEL, EXPERT_FF), w_idx),
            pl.BlockSpec((None, None, EXPERT_FF, D_MODEL), w_idx),
        ],
        out_specs=pl.BlockSpec((None, out_rows, LANES), per_sb),
        scratch_shapes=[pltpu.SMEM((DSP_SLOTS,), jnp.int32),
                        pltpu.SMEM((DSP_SLOTS,), F32),
                        pltpu.SMEM((2,), jnp.int32),
                        pltpu.SMEM((1,), F32),
                        pltpu.VMEM((DSP_TM * SLAB, LANES), F32),
                        pltpu.VMEM((DSP_TM * SLAB, LANES), F32),
                        pltpu.VMEM((DSP_TM * SLAB, LANES), F32),
                        pltpu.VMEM((DSP_TM * SLAB, LANES), F32),
                        pltpu.VMEM((D_MODEL, EXPERT_FF), BF16),
                        pltpu.VMEM((D_MODEL, EXPERT_FF), BF16),
                        pltpu.VMEM((EXPERT_FF, D_MODEL), BF16)],
    )
    return pl.pallas_call(
        _dsp_expert_kernel,
        grid_spec=grid_spec,
        out_shape=jax.ShapeDtypeStruct((n_sb, out_rows, LANES), F32),
        compiler_params=pltpu.CompilerParams(vmem_limit_bytes=VMEM_LIMIT),
        name="moe_experts",
    )(ntile, off, cnt, pos, w, h_slabs, w_gate, w_up, w_down)


def _dsp_final_kernel(x_ref, m_ref, mods_ref, fg_ref, o_ref, *, chunks_per_row):
    g2 = _dsp_mods(mods_ref, pl.program_id(0), chunks_per_row, 3)[2]
    y = x_ref[...] + g2 * _from_slabs(m_ref, DSP_CH)
    o_ref[...] = y * lax.rsqrt(jnp.mean(y * y, axis=-1, keepdims=True) + EPS) * fg_ref[...]


def _dsp_final_call(layer, x2d, m_slabs, mods, final_g, chunks_per_row):
    n = x2d.shape[0]
    ch = DSP_CH
    per_sb = DSP_SB // ch
    return pl.pallas_call(
        functools.partial(_dsp_final_kernel, chunks_per_row=chunks_per_row),
        grid=(n // ch,),
        in_specs=[
            pl.BlockSpec((ch, D_MODEL), lambda c: (c, 0)),
            pl.BlockSpec((None, ch * SLAB, LANES), lambda c: (c // per_sb, c % per_sb, 0)),
            pl.BlockSpec((None, MODS_ROWS, N_MOD * D_MODEL), lambda c: (layer, 0, 0)),
            pl.BlockSpec((1, D_MODEL), lambda c: (0, 0)),
        ],
        out_specs=pl.BlockSpec((ch, D_MODEL), lambda c: (c, 0)),
        out_shape=jax.ShapeDtypeStruct(x2d.shape, F32),
        compiler_params=pltpu.CompilerParams(vmem_limit_bytes=VMEM_LIMIT),
        name="final_norm",
    )(x2d, m_slabs, mods, final_g)


def _sparse_moe(layer, x2d, mods, ln2, w_router, b_router, w_gate, w_up, w_down, chunks_per_row):
    n_sb = x2d.shape[0] // DSP_SB
    h_slabs, route, counts = _dsp_route_call(layer, x2d, mods, ln2, w_router, b_router,
                                             chunks_per_row)
    pos, w, ntile, off, cnt = _dispatch_tables(route, counts, n_sb)
    return _dsp_expert_call(layer, h_slabs, pos, w, ntile, off, cnt, w_gate, w_up, w_down)


LAT_PROJ_ROWS = 512


def _lat_proj_kernel(x_ref, mods_ref, ln_ref, win_ref, wgate_ref, gb_ref, *rest, has_moe):
    zc_ref, zn_ref, zm_ref, zg_ref = rest[-4:]
    row = 1 + pl.program_id(0) // (DEC_SEQ // LAT_PROJ_ROWS)
    sh1, sc1, _ = _mod_rows(mods_ref, row, 0)
    x = x_ref[...]
    if has_moe:
        moe_ref, mods_prev_ref, xo_ref = rest[0], rest[1], rest[2]
        x = x + _mod_rows(mods_prev_ref, row, 3)[2] * _from_slabs(moe_ref, LAT_PROJ_ROWS)
        xo_ref[...] = x
    h = _rms_mod(x, ln_ref[...], sc1, sh1).astype(BF16)
    zc_ref[...] = _dot(h, win_ref[:, C_CONV:C_NA])
    zn_ref[...] = _dot(h, win_ref[:, C_NA:C_ML])
    zm_ref[...] = _dot(h, win_ref[:, C_ML:C_IG])
    zg_ref[...] = _dot(h, wgate_ref[...]) + gb_ref[...]


def _lat_proj_call(layer, x2d, moe, mods, ln1, w_in, gate_b):
    n = x2d.shape[0]
    tb = LAT_PROJ_ROWS
    widths = (C_NA - C_CONV, C_ML - C_NA, C_IG - C_ML, 2 * LANES)
    in_specs = [
        pl.BlockSpec((tb, D_MODEL), lambda i: (i, 0)),
        pl.BlockSpec((None, MODS_ROWS, N_MOD * D_MODEL), lambda i: (layer, 0, 0)),
        pl.BlockSpec((None, 1, D_MODEL), lambda i: (layer, 0, 0)),
        pl.BlockSpec((None, D_MODEL, C_IG), lambda i: (layer, 0, 0)),
        pl.BlockSpec((None, D_MODEL, 2 * LANES), lambda i: (layer, 0, 0)),
        pl.BlockSpec((None, 1, 2 * LANES), lambda i: (layer, 0, 0)),
    ]
    args = [x2d, mods, ln1, w_in[0], w_in[1], gate_b]
    out_specs = [pl.BlockSpec((tb, w), lambda i: (i, 0)) for w in widths]
    out_shape = [jax.ShapeDtypeStruct((n, w), F32) for w in widths]
    if moe is not None:
        per_sb = DSP_SB // tb
        in_specs += [pl.BlockSpec((None, tb * SLAB, LANES), lambda i: (i // per_sb, i % per_sb, 0)),
                     pl.BlockSpec((None, MODS_ROWS, N_MOD * D_MODEL), lambda i: (layer - 1, 0, 0))]
        args += [moe, mods]
        out_specs = [pl.BlockSpec((tb, D_MODEL), lambda i: (i, 0))] + out_specs
        out_shape = [jax.ShapeDtypeStruct((n, D_MODEL), F32)] + out_shape
    outs = pl.pallas_call(
        functools.partial(_lat_proj_kernel, has_moe=moe is not None),
        grid=(n // tb,),
        in_specs=in_specs,
        out_specs=out_specs,
        out_shape=out_shape,
        compiler_params=pltpu.CompilerParams(vmem_limit_bytes=VMEM_LIMIT),
        name="lat_proj",
    )(*args)
    return tuple(outs) if moe is not None else (x2d,) + tuple(outs)


def _na_row_start(r):
    rows = DEC_SEQ // GRID_W
    return min(max(r - NA_WIN_ROWS // 2, 0), rows - NA_WIN_ROWS)


def _lat_local_kernel(zc_ref, zn_ref, kc_ref, vc_ref, band_ref, convw_ref, o_ref):
    o_ref[:, 0:CONV_DIM] = _short_conv(zc_ref[...], convw_ref[...])
    rows = DEC_SEQ // GRID_W
    blk = NA_WIN_ROWS * GRID_W
    q_all = (zn_ref[:, 0:NA_DIM] * NA_HEAD_DIM ** -0.5).astype(BF16)
    k_all = zn_ref[:, NA_DIM:2 * NA_DIM].astype(BF16)
    v_all = zn_ref[:, 2 * NA_DIM:3 * NA_DIM].astype(BF16)
    kc_all = kc_ref[...].astype(BF16)
    vc_all = vc_ref[...].astype(BF16)
    heads = []
    for h in range(NA_HEADS):
        sl = slice(h * NA_HEAD_DIM, (h + 1) * NA_HEAD_DIM)
        q, k, v, kc, vc = q_all[:, sl], k_all[:, sl], v_all[:, sl], kc_all[:, sl], vc_all[:, sl]
        band = band_ref[h]
        s_ctx = _dot_nt(q, kc)
        m_ctx = jnp.max(s_ctx, axis=-1, keepdims=True)
        m_rows, l_rows, o_rows = [], [], []
        for r in range(rows):
            start = _na_row_start(r)
            off = (start - r + NA_WIN_ROWS - 1) * GRID_W
            q_r = q[r * GRID_W:(r + 1) * GRID_W]
            s_loc = _dot_nt(q_r, k[start * GRID_W:start * GRID_W + blk]) + band[:, off:off + blk]
            m = jnp.maximum(jnp.max(s_loc, axis=-1, keepdims=True), m_ctx[r * GRID_W:(r + 1) * GRID_W])
            p_loc = jnp.exp(s_loc - m)
            m_rows.append(m)
            l_rows.append(jnp.sum(p_loc, axis=-1, keepdims=True))
            o_rows.append(_dot(p_loc.astype(BF16), v[start * GRID_W:start * GRID_W + blk]))
        p_ctx = jnp.exp(s_ctx - jnp.concatenate(m_rows, axis=0))
        den = jnp.concatenate(l_rows, axis=0) + jnp.sum(p_ctx, axis=-1, keepdims=True)
        o = jnp.concatenate(o_rows, axis=0) + _dot(p_ctx.astype(BF16), vc)
        heads.append(o / den)
    o_ref[:, CONV_DIM:CONV_DIM + NA_DIM] = jnp.concatenate(heads, axis=-1)


def _lat_local_call(layer, zc, zn, cache_k, cache_v, band, conv_w):
    nb = zc.shape[0] // DEC_SEQ
    T = DEC_SEQ
    return pl.pallas_call(
        _lat_local_kernel,
        grid=(nb,),
        in_specs=[
            pl.BlockSpec((T, 3 * CONV_DIM), lambda b: (b, 0)),
            pl.BlockSpec((T, 3 * NA_DIM), lambda b: (b, 0)),
            pl.BlockSpec((None, None, PAST_LEN, NA_DIM), lambda b: (b, layer, 0, 0)),
            pl.BlockSpec((None, None, PAST_LEN, NA_DIM), lambda b: (b, layer, 0, 0)),
            pl.BlockSpec((None, NA_HEADS, GRID_W, band.shape[-1]), lambda b: (layer, 0, 0, 0)),
            pl.BlockSpec((None, 3, CONV_DIM), lambda b: (layer, 0, 0)),
        ],
        out_specs=pl.BlockSpec((T, CONV_DIM + NA_DIM), lambda b: (b, 0)),
        out_shape=jax.ShapeDtypeStruct((zc.shape[0], CONV_DIM + NA_DIM), F32),
        compiler_params=pltpu.CompilerParams(vmem_limit_bytes=VMEM_LIMIT),
        name="lat_conv_na",
    )(zc, zn, cache_k, cache_v, band, conv_w)


def _rope(x, cos, sin_signed):
    w = x.shape[-1]
    half = NA_HEAD_DIM // 2
    lane = lax.broadcasted_iota(jnp.int32, x.shape, 1)
    partner = jnp.where(lane % (2 * half) < half, pltpu.roll(x, w - half, axis=1), pltpu.roll(x, half, axis=1))
    return x * cos + partner * sin_signed


def _lat_mlstm_kernel(q_ref, k_ref, v_ref, o_ref, zg_ref, c0_ref, n0_ref, m0_ref, cos_ref, sin_ref,
                      mlg_ref, out_ref, bcol_scr, rowt_scr):
    hh = pl.program_id(1)

    @pl.when(hh == 0)
    def _():
        b_all, r_all = _gate_terms(zg_ref[:, 0:LANES], zg_ref[:, LANES:2 * LANES])
        bcol_scr[...] = b_all
        rowt_scr[...] = r_all

    b_col = bcol_scr[...]
    cos, sin_signed = cos_ref[...], sin_ref[...]
    q = _rope(q_ref[...], cos, sin_signed)
    k = _rope(k_ref[...], cos, sin_signed) * ML_HEAD_DIM ** -0.5
    v = v_ref[...]
    lane = lax.broadcasted_iota(jnp.int32, (DEC_SEQ, LANES), 1)
    hsum = None
    for d in range(N_DIRS):
        j = d * ML_HEADS + hh
        bc = jnp.sum(jnp.where(lane == j, b_col, 0.0), axis=-1, keepdims=True)
        lr = rowt_scr[pl.ds(j, 1), :]
        m0 = m0_ref[d, :, 0:1]
        ho = _mlstm_outputs(q, k, v, bc, lr, m0, c0_ref[d], n0_ref[d], d == 1, 256)
        hsum = ho if hsum is None else hsum + ho
    out_ref[...] = _head_readout(hsum, o_ref[...], mlg_ref[...])


def _lat_mlstm_call(layer, zm, zg, state_c, state_n, state_m, cos, sin_signed, ml_g):
    nb = zm.shape[0] // DEC_SEQ
    T = DEC_SEQ
    hd = ML_HEAD_DIM

    def col(part):
        return pl.BlockSpec((T, hd), lambda b, h: (b, part * ML_HEADS + h))

    return pl.pallas_call(
        _lat_mlstm_kernel,
        grid=(nb, ML_HEADS),
        in_specs=[
            col(0), col(1), col(2), col(3),
            pl.BlockSpec((T, 2 * LANES), lambda b, h: (b, 0)),
            pl.BlockSpec((None, None, N_DIRS, None, hd, hd), lambda b, h: (b, layer, 0, h, 0, 0)),
            pl.BlockSpec((None, None, N_DIRS, None, 1, hd), lambda b, h: (b, layer, 0, h, 0, 0)),
            pl.BlockSpec((None, None, N_DIRS, None, 1, LANES), lambda b, h: (b, layer, 0, h, 0, 0)),
            pl.BlockSpec((T, hd), lambda b, h: (0, 0)),
            pl.BlockSpec((T, hd), lambda b, h: (0, 0)),
            pl.BlockSpec((None, 1, hd), lambda b, h: (layer, 0, h)),
        ],
        out_specs=pl.BlockSpec((T, hd), lambda b, h: (b, h)),
        out_shape=jax.ShapeDtypeStruct((zm.shape[0], ML_DIM), F32),
        scratch_shapes=[pltpu.VMEM((T, LANES), F32), pltpu.VMEM((LANES, T), F32)],
        compiler_params=pltpu.CompilerParams(vmem_limit_bytes=VMEM_LIMIT),
        name="lat_mlstm",
    )(zm, zm, zm, zm, zg, state_c, state_n, state_m, cos, sin_signed, ml_g)


def _lat_merge_kernel(x_ref, loc_ref, ml_ref, mods_ref, wout_ref, o_ref):
    row = 1 + pl.program_id(0) // (DEC_SEQ // LAT_PROJ_ROWS)
    g1 = _mod_rows(mods_ref, row, 0)[2]
    split = CONV_DIM + NA_DIM
    y = _dot(loc_ref[...].astype(BF16), wout_ref[0:split, :]) + _dot(ml_ref[...].astype(BF16), wout_ref[split:, :])
    o_ref[...] = x_ref[...] + g1 * y


def _lat_merge_call(layer, x2d, loc, ml, mods, w_out):
    n = x2d.shape[0]
    tb = LAT_PROJ_ROWS
    return pl.pallas_call(
        _lat_merge_kernel,
        grid=(n // tb,),
        in_specs=[
            pl.BlockSpec((tb, D_MODEL), lambda i: (i, 0)),
            pl.BlockSpec((tb, CONV_DIM + NA_DIM), lambda i: (i, 0)),
            pl.BlockSpec((tb, ML_DIM), lambda i: (i, 0)),
            pl.BlockSpec((None, MODS_ROWS, N_MOD * D_MODEL), lambda i: (layer, 0, 0)),
            pl.BlockSpec((None, D_MODEL, D_MODEL), lambda i: (layer, 0, 0)),
        ],
        out_specs=pl.BlockSpec((tb, D_MODEL), lambda i: (i, 0)),
        out_shape=jax.ShapeDtypeStruct(x2d.shape, F32),
        compiler_params=pltpu.CompilerParams(vmem_limit_bytes=VMEM_LIMIT),
        name="lat_merge",
    )(x2d, loc, ml, mods, w_out)


def _pad_lanes(a, width):
    return jnp.pad(a, [(0, 0)] * (a.ndim - 1) + [(0, width - a.shape[-1])])


def _pack_w_in(w_in):
    gates = w_in[..., C_IG:].astype(BF16)
    gates = jnp.concatenate([_pad_lanes(gates[..., :N_GATE], LANES),
                             _pad_lanes(gates[..., N_GATE:], LANES)], axis=-1)
    return w_in[..., :C_IG].astype(BF16), gates


def _pack_gate_bias(ml_gate_b):
    gb = ml_gate_b.reshape(DEPTH, 2, N_GATE).astype(F32)
    return jnp.concatenate([_pad_lanes(gb[:, 0], LANES), _pad_lanes(gb[:, 1], LANES)], axis=-1)[:, None, :]


def _rpb_band(na_rpb):
    cols = np.arange(GRID_W)
    col_idx = np.clip(cols[None, :] - cols[:, None] + NA_WIN_COLS - 1, 0, 2 * NA_WIN_COLS - 2)
    col_start = np.clip(cols - NA_WIN_COLS // 2, 0, GRID_W - NA_WIN_COLS)
    col_mask = (cols[None, :] >= col_start[:, None]) & (cols[None, :] < col_start[:, None] + NA_WIN_COLS)
    pick = (col_idx[None] == np.arange(2 * NA_WIN_COLS - 1)[:, None, None]).astype(np.float32)
    t = jnp.einsum('lhri,iqk->lhqrk', na_rpb.astype(F32), jnp.asarray(pick), precision=HIGHEST)
    t = jnp.where(col_mask[None, None, :, None, :], t, NEG_INF)
    t = t.reshape(DEPTH, NA_HEADS, GRID_W, (2 * NA_WIN_ROWS - 1) * GRID_W)
    return _pad_lanes(t, 2 * NA_WIN_ROWS * GRID_W)


def _rope_tables():
    t = np.arange(DEC_SEQ)
    pos = np.stack([t // GRID_W, t % GRID_W], axis=-1).astype(np.float32)
    nf = ML_HEAD_DIM // 4
    inv = jnp.asarray(ROPE_BASE, F32) ** (-jnp.arange(nf, dtype=F32) / nf)
    ang = jnp.asarray(pos)[:, :, None] * inv
    cos = jnp.cos(ang)
    sin = jnp.sin(ang)
    cos_t = jnp.concatenate([cos, cos], axis=-1).reshape(DEC_SEQ, ML_HEAD_DIM)
    sin_t = jnp.concatenate([-sin, sin], axis=-1).reshape(DEC_SEQ, ML_HEAD_DIM)
    return cos_t, sin_t


def kernel(x_prompt, x_sample, cache_k, cache_v, state_C, state_n, state_m, c, c_ctx, w_mod, b_mod,
           ln1_g, w_in, conv_w, na_rpb, ml_gate_b, ml_norm_g, w_out, ln2_g, w_rg, b_rg, w_re, b_re,
           w_gate, w_up, w_down, final_g):
    nb_ctx = x_prompt.shape[0]
    nb_lat = x_sample.shape[0]
    assert 1 + nb_lat <= MODS_ROWS

    cvecs = jnp.concatenate([c_ctx[None, :], c,
                             jnp.zeros((MODS_ROWS - 1 - nb_lat, D_MODEL), F32)], axis=0)
    mods = _mods_call(cvecs, w_mod, b_mod)

    w_in_p = _pack_w_in(w_in)
    w_out_b = w_out.astype(BF16)
    gate_b = _pack_gate_bias(ml_gate_b)
    ln1 = ln1_g.reshape(DEPTH, 1, D_MODEL)
    ln2 = ln2_g.reshape(DEPTH, 1, D_MODEL)
    ml_g = ml_norm_g.reshape(DEPTH, 1, ML_DIM)
    w_router = _pad_lanes(jnp.concatenate([w_re, w_rg], axis=-1), LANES)
    b_router = _pad_lanes(jnp.concatenate([b_re, b_rg], axis=-1), LANES)[:, None, :]
    fg = final_g.reshape(1, D_MODEL)
    band = _rpb_band(na_rpb)
    cos_t, sin_t = _rope_tables()
    ck = cache_k.reshape(nb_lat, DEPTH, PAST_LEN, NA_DIM)
    cv = cache_v.reshape(nb_lat, DEPTH, PAST_LEN, NA_DIM)
    st_n = state_n.reshape(nb_lat, DEPTH, N_DIRS, ML_HEADS, 1, ML_HEAD_DIM)
    st_m = jnp.broadcast_to(state_m[..., None, None], (nb_lat, DEPTH, N_DIRS, ML_HEADS, 1, LANES))

    xp = x_prompt
    xs = x_sample.reshape(nb_lat * DEC_SEQ, D_MODEL)
    states = None
    lat_chunks = DEC_SEQ // DSP_CH
    moe_p = moe_s = None
    for l in range(DEPTH):
        xp, states = _ctx_mixer_call(l, xp, moe_p, mods, ln1, w_in_p, conv_w, gate_b, ml_g, w_out_b,
                                     states)
        moe_p = _sparse_moe(l, xp.reshape(nb_ctx * SEQ, D_MODEL), mods, ln2, w_router, b_router,
                            w_gate, w_up, w_down, None)

        xs, zc, zn, zm, zg = _lat_proj_call(l, xs, moe_s, mods, ln1, w_in_p, gate_b)
        loc = _lat_local_call(l, zc, zn, ck, cv, band, conv_w)
        ml = _lat_mlstm_call(l, zm, zg, state_C, st_n, st_m, cos_t, sin_t, ml_g)
        xs = _lat_merge_call(l, xs, loc, ml, mods, w_out_b)
        moe_s = _sparse_moe(l, xs, mods, ln2, w_router, b_router, w_gate, w_up, w_down, lat_chunks)

    xp = _dsp_final_call(DEPTH - 1, xp.reshape(nb_ctx * SEQ, D_MODEL), moe_p, mods, fg, None)
    xs = _dsp_final_call(DEPTH - 1, xs, moe_s, mods, fg, lat_chunks)
    xp = xp.reshape(nb_ctx, SEQ, D_MODEL)

    new_k, new_v, new_c, new_n, new_m = states
    return (xp, xs.reshape(nb_lat, DEC_SEQ, D_MODEL),
            new_k.reshape(nb_ctx, DEPTH, SEQ, NA_HEADS, NA_HEAD_DIM),
            new_v.reshape(nb_ctx, DEPTH, SEQ, NA_HEADS, NA_HEAD_DIM),
            new_c,
            new_n.reshape(nb_ctx, DEPTH, N_DIRS, ML_HEADS, ML_HEAD_DIM),
            new_m[..., 0].reshape(nb_ctx, DEPTH, N_DIRS, ML_HEADS))
```

```python
import functools

import numpy as np
import jax
import jax.numpy as jnp
from jax import lax
from jax.experimental import pallas as pl
from jax.experimental.pallas import tpu as pltpu

F32 = jnp.float32
BF16 = jnp.bfloat16

D_MODEL = 1024
SEQ = 256
DEPTH = 4
DEC_SEQ = 1024
PAST_LEN = 512
GRID_W = 64
CONV_DIM = 256
NA_HEADS = 4
NA_HEAD_DIM = 64
NA_DIM = NA_HEADS * NA_HEAD_DIM
NA_WIN_ROWS = 8
NA_WIN_COLS = 16
ML_HEADS = 4
ML_HEAD_DIM = 128
ML_DIM = ML_HEADS * ML_HEAD_DIM
N_DIRS = 2
N_GROUPS = 4
EXPERTS_PER_GROUP = 4
N_EXPERTS = N_GROUPS * EXPERTS_PER_GROUP
EXPERT_FF = 512
ROPE_BASE = 10000.0
EPS = 1e-6
N_MOD = 6

LANES = 128
N_GATE = N_DIRS * ML_HEADS
C_CONV = 0
C_NA = 3 * CONV_DIM
C_ML = C_NA + 3 * NA_DIM
C_IG = C_ML + 4 * ML_DIM
CUMSUM_BLOCK = 256
MODS_ROWS = 8
VMEM_LIMIT = 56 * 1024 * 1024

NEG_INF = float("-inf")
HIGHEST = lax.Precision.HIGHEST


def _dot(a, b):
    return jnp.dot(a, b, preferred_element_type=F32)


def _dot_nt(a, b):
    return lax.dot_general(a, b, (((1,), (1,)), ((), ())), preferred_element_type=F32)


def _dot_split(a, b):
    a_hi = a.astype(BF16)
    a_lo = (a - a_hi.astype(F32)).astype(BF16)
    b_hi = b.astype(BF16)
    b_lo = (b - b_hi.astype(F32)).astype(BF16)
    return _dot(jnp.concatenate([a_hi, a_hi, a_lo], axis=1), jnp.concatenate([b_hi, b_lo, b_hi], axis=0))


def _rms_mod(x, g, sc, sh):
    y = x * lax.rsqrt(jnp.mean(x * x, axis=-1, keepdims=True) + EPS)
    return (y * g) * (1.0 + sc) + sh


def _mod_rows(mods_ref, row, first):
    return [mods_ref[pl.ds(row, 1), pl.ds((first + j) * D_MODEL, D_MODEL)] for j in range(3)]


def _mods_kernel(cv_ref, w_ref, b_ref, o_ref):
    cv = cv_ref[...]
    s = cv * jax.nn.sigmoid(cv)
    o_ref[...] = _dot(s.astype(BF16), w_ref[...].astype(BF16)) + b_ref[...]


def _mods_call(cvecs, w_mod, b_mod):
    tn = 1536
    n = N_MOD * D_MODEL
    return pl.pallas_call(
        _mods_kernel,
        grid=(DEPTH, n // tn),
        in_specs=[pl.BlockSpec((MODS_ROWS, D_MODEL), lambda l, j: (0, 0)),
                  pl.BlockSpec((None, D_MODEL, tn), lambda l, j: (l, 0, j)),
                  pl.BlockSpec((None, 1, tn), lambda l, j: (l, 0, j))],
        out_specs=pl.BlockSpec((None, MODS_ROWS, tn), lambda l, j: (l, 0, j)),
        out_shape=jax.ShapeDtypeStruct((DEPTH, MODS_ROWS, n), F32),
        compiler_params=pltpu.CompilerParams(vmem_limit_bytes=VMEM_LIMIT),
        name="adaln_mods",
    )(cvecs, w_mod, b_mod.reshape(DEPTH, 1, n))


def _short_conv(zc, cw):
    T = zc.shape[0]
    cb = zc[:, 0:CONV_DIM]
    u = zc[:, CONV_DIM:2 * CONV_DIM] * zc[:, 2 * CONV_DIM:3 * CONV_DIM]
    t = lax.broadcasted_iota(jnp.int32, u.shape, 0)
    u_prev = jnp.where(t == 0, 0.0, pltpu.roll(u, 1, axis=0))
    u_next = jnp.where(t == T - 1, 0.0, pltpu.roll(u, T - 1, axis=0))
    return cb * (cw[0:1, :] * u_prev + cw[1:2, :] * u + cw[2:3, :] * u_next)


def _with_ones(v):
    return jnp.concatenate([v, jnp.ones_like(v)], axis=-1)


def _softmax_attention(nq, nk, nv):
    outs = []
    for h in range(NA_HEADS):
        sl = slice(h * NA_HEAD_DIM, (h + 1) * NA_HEAD_DIM)
        q = (nq[:, sl] * NA_HEAD_DIM ** -0.5).astype(BF16)
        s = _dot_nt(q, nk[:, sl].astype(BF16))
        p = jnp.exp(s - jnp.max(s, axis=-1, keepdims=True))
        o = _dot(p.astype(BF16), nv[:, sl].astype(BF16)) / jnp.sum(p, axis=-1, keepdims=True)
        outs.append(o)
    return jnp.concatenate(outs, axis=-1)


def _log_sigmoid(x):
    return jnp.minimum(x, 0.0) - jnp.log(1.0 + jnp.exp(-jnp.abs(x)))


def _gate_terms(zi, zf):
    T = zi.shape[0]
    lf = _log_sigmoid(zf)
    blk = min(T, CUMSUM_BLOCK)
    r = lax.broadcasted_iota(jnp.int32, (blk, blk), 0)
    c = lax.broadcasted_iota(jnp.int32, (blk, blk), 1)
    tril = jnp.where(c <= r, 1.0, 0.0).astype(F32)
    parts, carry = [], None
    for r0 in range(0, T, blk):
        b = jnp.dot(tril, lf[r0:r0 + blk], precision=HIGHEST, preferred_element_type=F32)
        if carry is not None:
            b = b + carry
        carry = b[blk - 1:blk, :]
        parts.append(b)
    b_fwd = parts[0] if len(parts) == 1 else jnp.concatenate(parts, axis=0)
    b_bwd = (b_fwd[T - 1:T, :] - b_fwd) + lf
    lane = lax.broadcasted_iota(jnp.int32, zi.shape, 1)
    b_col = jnp.where(lane < ML_HEADS, b_fwd, b_bwd)
    row_t = jnp.transpose(zi - b_col)
    return b_col, row_t


def _mlstm_outputs(q, k, v, b_col, li_row, m0, c0, n0, backward, q_block):
    T = q.shape[0]
    qb, kb, vb = q.astype(BF16), k.astype(BF16), v.astype(BF16)
    c0b = None if c0 is None else c0.astype(BF16)
    outs = []
    for r0 in range(0, T, q_block):
        ks, ke = (r0, T) if backward else (0, r0 + q_block)
        bq = b_col[r0:r0 + q_block]
        d = bq + li_row[:, ks:ke]
        t_idx = r0 + lax.broadcasted_iota(jnp.int32, d.shape, 0)
        s_idx = ks + lax.broadcasted_iota(jnp.int32, d.shape, 1)
        d = jnp.where((s_idx >= t_idx) if backward else (s_idx <= t_idx), d, NEG_INF)
        inter = bq + m0
        m_t = jnp.maximum(jnp.max(d, axis=-1, keepdims=True), inter)
        w = jnp.exp(d - m_t)
        s = _dot_nt(qb[r0:r0 + q_block], kb[ks:ke]) * w
        num = _dot(s.astype(BF16), vb[ks:ke])
        den = jnp.sum(s, axis=-1, keepdims=True)
        if c0 is not None:
            a = jnp.exp(inter - m_t)
            num = num + a * _dot(qb[r0:r0 + q_block], c0b)
            den = den + a * jnp.sum(q[r0:r0 + q_block] * n0, axis=-1, keepdims=True)
        outs.append(num / jnp.maximum(jnp.abs(den), jnp.exp(-m_t)))
    return outs[0] if len(outs) == 1 else jnp.concatenate(outs, axis=0)


def _mlstm_state(k, v, b_col, li_col, backward):
    T = k.shape[0]
    b_tot = b_col[0:1] if backward else b_col[T - 1:T]
    g = (b_tot - b_col) + li_col
    m_new = jnp.maximum(b_tot, jnp.max(g, axis=0, keepdims=True))
    kw = k * jnp.exp(g - m_new)
    c_new = _dot(jnp.transpose(kw).astype(BF16), v.astype(BF16))
    n_new = jnp.sum(kw, axis=0, keepdims=True)
    return c_new, n_new, m_new


def _head_readout(h, mo, g):
    hn = h * lax.rsqrt(jnp.mean(h * h, axis=-1, keepdims=True) + EPS)
    return hn * g * jax.nn.sigmoid(mo)


def _ctx_mixer_kernel(x_ref, mods_ref, ln_ref, win_ref, wgate_ref, convw_ref, gb_ref, mlg_ref, wout_ref,
                      *rest, has_moe):
    xo_ref, ko_ref, vo_ref, co_ref, no_ref, mo_ref = rest[-6:]
    T = SEQ
    sh1, sc1, g1 = _mod_rows(mods_ref, 0, 0)
    x = x_ref[...]
    if has_moe:
        moe_ref, mods_prev_ref = rest[0], rest[1]
        x = x + _mod_rows(mods_prev_ref, 0, 3)[2] * _from_slabs(moe_ref, T)
    h = _rms_mod(x, ln_ref[...], sc1, sh1).astype(BF16)

    conv_o = _short_conv(_dot(h, win_ref[:, C_CONV:C_NA]), convw_ref[...])

    zn = _dot(h, win_ref[:, C_NA:C_ML])
    nk, nv = zn[:, NA_DIM:2 * NA_DIM], zn[:, 2 * NA_DIM:3 * NA_DIM]
    ko_ref[...] = nk
    vo_ref[...] = nv
    na_o = _softmax_attention(zn[:, 0:NA_DIM], nk, nv)

    zg = _dot(h, wgate_ref[...])
    zi = zg[:, 0:LANES] + gb_ref[:, 0:LANES]
    zf = zg[:, LANES:2 * LANES] + gb_ref[:, LANES:2 * LANES]
    b_col, row_t = _gate_terms(zi, zf)
    m0 = jnp.zeros((1, 1), F32)

    zm = _dot(h, win_ref[:, C_ML:C_IG])
    ml_parts = []
    for hh in range(ML_HEADS):
        zq, zk, zv, zo = (zm[:, p * ML_DIM + hh * ML_HEAD_DIM:p * ML_DIM + (hh + 1) * ML_HEAD_DIM]
                          for p in range(4))
        zk = zk * ML_HEAD_DIM ** -0.5
        hsum = None
        for d in range(N_DIRS):
            j = d * ML_HEADS + hh
            bc = b_col[:, j:j + 1]
            ho = _mlstm_outputs(zq, zk, zv, bc, row_t[j:j + 1, :], m0, None, None, d == 1, T)
            hsum = ho if hsum is None else hsum + ho
            c_new, n_new, m_new = _mlstm_state(zk, zv, bc, zi[:, j:j + 1], d == 1)
            co_ref[d, hh] = c_new
            no_ref[j:j + 1, :] = n_new
            mo_ref[j:j + 1, :] = jnp.broadcast_to(m_new, (1, LANES))
        ml_parts.append(_head_readout(hsum, zo, mlg_ref[:, hh * ML_HEAD_DIM:(hh + 1) * ML_HEAD_DIM]))

    mix = jnp.concatenate([conv_o, na_o] + ml_parts, axis=-1).astype(BF16)
    xo_ref[...] = x + g1 * _dot(mix, wout_ref[...])


def _ctx_mixer_call(layer, x, moe, mods, ln1, w_in, conv_w, gate_b, ml_g, w_out, prev):
    B = x.shape[0]
    T = SEQ
    G = None
    state_shapes = [
        jax.ShapeDtypeStruct((B, DEPTH, T, NA_DIM), F32),
        jax.ShapeDtypeStruct((B, DEPTH, T, NA_DIM), F32),
        jax.ShapeDtypeStruct((B, DEPTH, N_DIRS, ML_HEADS, ML_HEAD_DIM, ML_HEAD_DIM), F32),
        jax.ShapeDtypeStruct((B, DEPTH, N_GATE, ML_HEAD_DIM), F32),
        jax.ShapeDtypeStruct((B, DEPTH, N_GATE, LANES), F32),
    ]
    in_specs = [
        pl.BlockSpec((G, T, D_MODEL), lambda b: (b, 0, 0)),
        pl.BlockSpec((None, MODS_ROWS, N_MOD * D_MODEL), lambda b: (layer, 0, 0)),
        pl.BlockSpec((None, 1, D_MODEL), lambda b: (layer, 0, 0)),
        pl.BlockSpec((None, D_MODEL, C_IG), lambda b: (layer, 0, 0)),
        pl.BlockSpec((None, D_MODEL, 2 * LANES), lambda b: (layer, 0, 0)),
        pl.BlockSpec((None, 3, CONV_DIM), lambda b: (layer, 0, 0)),
        pl.BlockSpec((None, 1, 2 * LANES), lambda b: (layer, 0, 0)),
        pl.BlockSpec((None, 1, ML_DIM), lambda b: (layer, 0, 0)),
        pl.BlockSpec((None, D_MODEL, D_MODEL), lambda b: (layer, 0, 0)),
    ]
    args = [x, mods, ln1, w_in[0], w_in[1], conv_w, gate_b, ml_g, w_out]
    if moe is not None:
        per_sb = DSP_SB // T
        in_specs += [pl.BlockSpec((None, T * SLAB, LANES), lambda b: (b // per_sb, b % per_sb, 0)),
                     pl.BlockSpec((None, MODS_ROWS, N_MOD * D_MODEL), lambda b: (layer - 1, 0, 0))]
        args += [moe, mods]
    aliases = {}
    if prev is not None:
        in_specs += [pl.BlockSpec(memory_space=pl.ANY)] * len(prev)
        aliases = {len(args) + i: 1 + i for i in range(len(prev))}
        args += list(prev)
    out_specs = [
        pl.BlockSpec((G, T, D_MODEL), lambda b: (b, 0, 0)),
        pl.BlockSpec((G, None, T, NA_DIM), lambda b: (b, layer, 0, 0)),
        pl.BlockSpec((G, None, T, NA_DIM), lambda b: (b, layer, 0, 0)),
        pl.BlockSpec((G, None, N_DIRS, ML_HEADS, ML_HEAD_DIM, ML_HEAD_DIM),
                     lambda b: (b, layer, 0, 0, 0, 0)),
        pl.BlockSpec((G, None, N_GATE, ML_HEAD_DIM), lambda b: (b, layer, 0, 0)),
        pl.BlockSpec((G, None, N_GATE, LANES), lambda b: (b, layer, 0, 0)),
    ]
    outs = pl.pallas_call(
        functools.partial(_ctx_mixer_kernel, has_moe=moe is not None),
        grid=(B,),
        in_specs=in_specs,
        out_specs=out_specs,
        out_shape=[jax.ShapeDtypeStruct(x.shape, F32)] + state_shapes,
        input_output_aliases=aliases,
        compiler_params=pltpu.CompilerParams(vmem_limit_bytes=VMEM_LIMIT),
        name="ctx_mixer",
    )(*args)
    return outs[0], tuple(outs[1:])


def _route(lg):
    lane = lax.broadcasted_iota(jnp.int32, lg.shape, 1)
    lane_f = lane.astype(F32)
    big = float(LANES)
    is_g = (lane >= N_EXPERTS) & (lane < N_EXPERTS + N_GROUPS)
    glm = jnp.where(is_g, lg, NEG_INF)
    gmax = jnp.max(glm, axis=-1, keepdims=True)
    g_top = jnp.min(jnp.where(glm == gmax, lane_f, big), axis=-1, keepdims=True) - N_EXPERTS
    gp = 1.0 / jnp.sum(jnp.where(is_g, jnp.exp(lg - gmax), 0.0), axis=-1, keepdims=True)
    grp = (lane >> (EXPERTS_PER_GROUP.bit_length() - 1)).astype(F32)
    in_grp = (lane < N_EXPERTS) & (grp == g_top)
    e1 = jnp.where(in_grp, lg, NEG_INF)
    v1 = jnp.max(e1, axis=-1, keepdims=True)
    i1 = jnp.min(jnp.where(e1 == v1, lane_f, big), axis=-1, keepdims=True)
    e2 = jnp.where(in_grp & (lane_f != i1), lg, NEG_INF)
    v2 = jnp.max(e2, axis=-1, keepdims=True)
    i2 = jnp.min(jnp.where(e2 == v2, lane_f, big), axis=-1, keepdims=True)
    t = jnp.exp(v2 - v1)
    w1 = 1.0 / (1.0 + t)
    w2 = t / (1.0 + t)
    return i1, i2, gp * w1, gp * w2


SLAB = D_MODEL // LANES
DSP_SB = 2048
DSP_TM = 160
DSP_CH = 512
DSP_PMAX = 2 * DSP_SB + N_EXPERTS * DSP_TM
DSP_NULL = DSP_PMAX + 2 * DSP_TM
DSP_SLOTS = DSP_NULL + 2 * DSP_TM
DSP_UNROLL = 8


def _to_slabs(ref, x):
    n = x.shape[0]
    for s in range(SLAB):
        ref[pl.ds(s, n, stride=SLAB), :] = x[:, s * LANES:(s + 1) * LANES]


def _from_slabs(ref, n):
    return jnp.concatenate([ref[pl.ds(s, n, stride=SLAB), :] for s in range(SLAB)], axis=-1)


def _dsp_mods(mods_ref, chunk, chunks_per_row, first):
    row = 0 if chunks_per_row is None else 1 + chunk // chunks_per_row
    return _mod_rows(mods_ref, row, first)


def _dsp_route_kernel(x_ref, mods_ref, ln_ref, wr_ref, br_ref, h_ref, r_ref, cnt_ref, run_scr, *,
                      chunks_per_row):
    c = pl.program_id(0)
    ch = DSP_CH

    @pl.when(c % (DSP_SB // ch) == 0)
    def _():
        run_scr[...] = jnp.zeros_like(run_scr)

    sh2, sc2, _ = _dsp_mods(mods_ref, c, chunks_per_row, 3)
    h2 = _rms_mod(x_ref[...], ln_ref[...], sc2, sh2)
    _to_slabs(h_ref, h2)
    lg = _dot_split(h2, wr_ref[...]) + br_ref[...]
    i1, i2, w1, w2 = _route(lg)
    lane = lax.broadcasted_iota(jnp.int32, lg.shape, 1)
    lane_f = lane.astype(F32)
    oh1 = jnp.where(lane_f == i1, 1.0, 0.0)
    oh2 = jnp.where(lane_f == i2, 1.0, 0.0)
    both = oh1 + oh2
    r = lax.broadcasted_iota(jnp.int32, (ch, ch), 0)
    s = lax.broadcasted_iota(jnp.int32, (ch, ch), 1)
    earlier = jnp.where(s < r, 1.0, 0.0).astype(BF16)
    run = run_scr[0:1, :]
    before = _dot(earlier, both.astype(BF16)) + run
    rank1 = jnp.sum(before * oh1, axis=-1, keepdims=True)
    rank2 = jnp.sum(before * oh2, axis=-1, keepdims=True)
    run = run + jnp.sum(both, axis=0, keepdims=True)
    run_scr[0:1, :] = run
    cnt_ref[...] = jnp.broadcast_to(run, cnt_ref.shape)
    cols = (i1, i2, w1, w2, rank1, rank2)
    out = jnp.zeros(lg.shape, F32)
    for j, col in enumerate(cols):
        out = jnp.where(lane == j, col, out)
    r_ref[...] = out


def _dsp_route_call(layer, x2d, mods, ln2, w_router, b_router, chunks_per_row):
    n = x2d.shape[0]
    ch = DSP_CH
    per_sb = DSP_SB // ch
    return pl.pallas_call(
        functools.partial(_dsp_route_kernel, chunks_per_row=chunks_per_row),
        grid=(n // ch,),
        in_specs=[
            pl.BlockSpec((ch, D_MODEL), lambda c: (c, 0)),
            pl.BlockSpec((None, MODS_ROWS, N_MOD * D_MODEL), lambda c: (layer, 0, 0)),
            pl.BlockSpec((None, 1, D_MODEL), lambda c: (layer, 0, 0)),
            pl.BlockSpec((None, D_MODEL, LANES), lambda c: (layer, 0, 0)),
            pl.BlockSpec((None, 1, LANES), lambda c: (layer, 0, 0)),
        ],
        out_specs=[pl.BlockSpec((ch * SLAB, LANES), lambda c: (c, 0)),
                   pl.BlockSpec((ch, LANES), lambda c: (c, 0)),
                   pl.BlockSpec((None, SLAB, LANES), lambda c: (c // per_sb, 0, 0))],
        out_shape=[jax.ShapeDtypeStruct((n * SLAB, LANES), F32),
                   jax.ShapeDtypeStruct((n, LANES), F32),
                   jax.ShapeDtypeStruct((n // DSP_SB, SLAB, LANES), F32)],
        scratch_shapes=[pltpu.VMEM((SLAB, LANES), F32)],
        compiler_params=pltpu.CompilerParams(vmem_limit_bytes=VMEM_LIMIT),
        name="moe_route",
    )(x2d, mods, ln2, w_router, b_router)


def _dispatch_tables(route, counts, n_sb):
    e = route[:, 0:2].astype(jnp.int32).reshape(n_sb, 2 * DSP_SB)
    rank = route[:, 4:6].astype(jnp.int32).reshape(n_sb, 2 * DSP_SB)
    w = route[:, 2:4].reshape(n_sb, 1, 2 * DSP_SB)
    cnt = counts[:, 0, :N_EXPERTS].astype(jnp.int32)
    ntile = (cnt + (DSP_TM - 1)) // DSP_TM
    off = (jnp.cumsum(ntile, axis=1) - ntile) * DSP_TM
    is_e = e[:, None, :] == jnp.arange(N_EXPERTS, dtype=jnp.int32)[None, :, None]
    pos = (jnp.sum(jnp.where(is_e, off[:, :, None], 0), axis=1) + rank).reshape(n_sb, 1, 2 * DSP_SB)
    return pos, w, ntile.reshape(-1), off.reshape(-1), cnt.reshape(-1)


def _dsp_expert_kernel(ntile_ref, off_ref, cnt_ref, pos_ref, w_ref, h_ref, wg_ref, wu_ref, wd_ref,
                       o_ref, row_smem, wgt_smem, sched_smem, pend_smem, xs0, xs1, ys0, ys1,
                       wg_scr, wu_scr, wd_scr):
    sb = pl.program_id(0)
    e = pl.program_id(1)
    tm, u = DSP_TM, DSP_UNROLL
    xs_scr, ys_scr = (xs0, xs1), (ys0, ys1)
    slab_shift = SLAB.bit_length() - 1
    spare_row = DSP_SB * SLAB
    last_row = (DSP_SB - 1) * SLAB

    @pl.when(e == 0)
    def _():
        for ee in range(N_EXPERTS):
            q = sb * N_EXPERTS + ee

            def pad(p, carry):
                row_smem[p] = spare_row
                wgt_smem[p] = 0.0
                return carry

            lax.fori_loop(off_ref[q] + cnt_ref[q], off_ref[q] + ntile_ref[q] * tm, pad, 0)

        def invert(j, carry):
            for k in range(u):
                a = j * u + k
                p = pos_ref[0, a]
                row_smem[p] = lax.shift_left(lax.shift_right_logical(a, 1), slab_shift)
                wgt_smem[p] = w_ref[0, a]
            return carry

        lax.fori_loop(0, 2 * DSP_SB // u, invert, 0)

        last_q = sb * N_EXPERTS + (N_EXPERTS - 1)
        used_end = off_ref[last_q] + ntile_ref[last_q] * tm

        def pad_end(p, carry):
            row_smem[used_end + p] = spare_row
            wgt_smem[used_end + p] = 0.0
            row_smem[DSP_NULL + p] = spare_row
            wgt_smem[DSP_NULL + p] = 0.0
            return carry

        lax.fori_loop(0, 2 * tm, pad_end, 0)
        sched_smem[0] = -1
        sched_smem[1] = DSP_NULL
        pend_smem[0] = 0.0
        o_ref[...] = jnp.zeros_like(o_ref)
        for ys in ys_scr:
            ys[...] = jnp.zeros_like(ys)

    q = sb * N_EXPERTS + e
    base0 = off_ref[q]
    nt = ntile_ref[q]
    wg_scr[...] = wg_ref[...].astype(BF16)
    wu_scr[...] = wu_ref[...].astype(BF16)
    wd_scr[...] = wd_ref[...].astype(BF16)

    def gather(base, buf):
        for r in range(tm):
            src = jnp.minimum(row_smem[base + r], last_row)
            xs_scr[buf][r * SLAB:(r + 1) * SLAB, :] = h_ref[pl.ds(pl.multiple_of(src, SLAB), SLAB), :]

    def compute(buf):
        x = _from_slabs(xs_scr[buf], tm).astype(BF16)
        a = jax.nn.silu(_dot(x, wg_scr[...])) * _dot(x, wu_scr[...])
        _to_slabs(ys_scr[buf], _dot(a.astype(BF16), wd_scr[...]))

    def scatter(base, buf, scale):
        for j in range(tm // u):
            upd = []
            for k in range(u):
                r = j * u + k
                dst = pl.multiple_of(row_smem[base + r], SLAB)
                y = ys_scr[buf][r * SLAB:(r + 1) * SLAB, :]
                upd.append((dst, o_ref[pl.ds(dst, SLAB), :] + (wgt_smem[base + r] * scale) * y))
            for dst, v in upd:
                o_ref[pl.ds(dst, SLAB), :] = v

    n_pairs = lax.shift_right_logical(nt + 1, 1)

    @pl.when((n_pairs > 0) & (sched_smem[0] != base0))
    def _():
        gather(base0, 0)

    pend_base = sched_smem[1]
    pend_scale = pend_smem[0]

    def pair(k, carry):
        b = base0 + 2 * k * tm
        first = k == 0
        gather(b + tm, 1)
        compute(0)
        scatter(jnp.where(first, pend_base, b - tm), 1, jnp.where(first, pend_scale, 1.0))
        gather(b + 2 * tm, 0)
        compute(1)
        scatter(b, 0, 1.0)
        return carry

    lax.fori_loop(0, n_pairs, pair, 0)

    @pl.when(n_pairs > 0)
    def _():
        last = 2 * n_pairs - 1
        sched_smem[0] = base0 + (last + 1) * tm
        sched_smem[1] = base0 + last * tm
        pend_smem[0] = jnp.where(last < nt, 1.0, 0.0)

    @pl.when(e == N_EXPERTS - 1)
    def _():
        scatter(sched_smem[1], 1, pend_smem[0])


def _dsp_expert_call(layer, h_slabs, pos, w, ntile, off, cnt, w_gate, w_up, w_down):
    n_sb = pos.shape[0]
    out_rows = (DSP_SB + DSP_CH) * SLAB

    def per_sb(s, e, *prefetch):
        return (s, 0, 0)

    def w_idx(s, e, *prefetch):
        return (layer, e, 0, 0)

    grid_spec = pltpu.PrefetchScalarGridSpec(
        num_scalar_prefetch=3,
        grid=(n_sb, N_EXPERTS),
        in_specs=[
            pl.BlockSpec((None, 1, 2 * DSP_SB), per_sb, memory_space=pltpu.SMEM),
            pl.BlockSpec((None, 1, 2 * DSP_SB), per_sb, memory_space=pltpu.SMEM),
            pl.BlockSpec((DSP_SB * SLAB, LANES), lambda s, e, *prefetch: (s, 0)),
            pl.BlockSpec((None, None, D_MODEL, EXPERT_FF), w_idx),
            pl.BlockSpec((None, None, D_MODEL, EXPERT_FF), w_idx),
            pl.BlockSpec((None, None, EXPERT_FF, D_MODEL), w_idx),
        ],
        out_specs=pl.BlockSpec((None, out_rows, LANES), per_sb),
        scratch_shapes=[pltpu.SMEM((DSP_SLOTS,), jnp.int32),
                        pltpu.SMEM((DSP_SLOTS,), F32),
                        pltpu.SMEM((2,), jnp.int32),
                        pltpu.SMEM((1,), F32),
                        pltpu.VMEM((DSP_TM * SLAB, LANES), F32),
                        pltpu.VMEM((DSP_TM * SLAB, LANES), F32),
                        pltpu.VMEM((DSP_TM * SLAB, LANES), F32),
                        pltpu.VMEM((DSP_TM * SLAB, LANES), F32),
                        pltpu.VMEM((D_MODEL, EXPERT_FF), BF16),
                        pltpu.VMEM((D_MODEL, EXPERT_FF), BF16),
                        pltpu.VMEM((EXPERT_FF, D_MODEL), BF16)],
    )
    return pl.pallas_call(
        _dsp_expert_kernel,
        grid_spec=grid_spec,
        out_shape=jax.ShapeDtypeStruct((n_sb, out_rows, LANES), F32),
        compiler_params=pltpu.CompilerParams(vmem_limit_bytes=VMEM_LIMIT),
        name="moe_experts",
    )(ntile, off, cnt, pos, w, h_slabs, w_gate, w_up, w_down)


def _dsp_final_kernel(x_ref, m_ref, mods_ref, fg_ref, o_ref, *, chunks_per_row):
    g2 = _dsp_mods(mods_ref, pl.program_id(0), chunks_per_row, 3)[2]
    y = x_ref[...] + g2 * _from_slabs(m_ref, DSP_CH)
    o_ref[...] = y * lax.rsqrt(jnp.mean(y * y, axis=-1, keepdims=True) + EPS) * fg_ref[...]


def _dsp_final_call(layer, x2d, m_slabs, mods, final_g, chunks_per_row):
    n = x2d.shape[0]
    ch = DSP_CH
    per_sb = DSP_SB // ch
    return pl.pallas_call(
        functools.partial(_dsp_final_kernel, chunks_per_row=chunks_per_row),
        grid=(n // ch,),
        in_specs=[
            pl.BlockSpec((ch, D_MODEL), lambda c: (c, 0)),
            pl.BlockSpec((None, ch * SLAB, LANES), lambda c: (c // per_sb, c % per_sb, 0)),
            pl.BlockSpec((None, MODS_ROWS, N_MOD * D_MODEL), lambda c: (layer, 0, 0)),
            pl.BlockSpec((1, D_MODEL), lambda c: (0, 0)),
        ],
        out_specs=pl.BlockSpec((ch, D_MODEL), lambda c: (c, 0)),
        out_shape=jax.ShapeDtypeStruct(x2d.shape, F32),
        compiler_params=pltpu.CompilerParams(vmem_limit_bytes=VMEM_LIMIT),
        name="final_norm",
    )(x2d, m_slabs, mods, final_g)


def _sparse_moe(layer, x2d, mods, ln2, w_router, b_router, w_gate, w_up, w_down, chunks_per_row):
    n_sb = x2d.shape[0] // DSP_SB
    h_slabs, route, counts = _dsp_route_call(layer, x2d, mods, ln2, w_router, b_router,
                                             chunks_per_row)
    pos, w, ntile, off, cnt = _dispatch_tables(route, counts, n_sb)
    return _dsp_expert_call(layer, h_slabs, pos, w, ntile, off, cnt, w_gate, w_up, w_down)


LAT_PROJ_ROWS = 512


def _lat_proj_kernel(x_ref, mods_ref, ln_ref, win_ref, wgate_ref, gb_ref, *rest, has_moe):
    zc_ref, zn_ref, zm_ref, zg_ref = rest[-4:]
    row = 1 + pl.program_id(0) // (DEC_SEQ // LAT_PROJ_ROWS)
    sh1, sc1, _ = _mod_rows(mods_ref, row, 0)
    x = x_ref[...]
    if has_moe:
        moe_ref, mods_prev_ref, xo_ref = rest[0], rest[1], rest[2]
        x = x + _mod_rows(mods_prev_ref, row, 3)[2] * _from_slabs(moe_ref, LAT_PROJ_ROWS)
        xo_ref[...] = x
    h = _rms_mod(x, ln_ref[...], sc1, sh1).astype(BF16)
    zc_ref[...] = _dot(h, win_ref[:, C_CONV:C_NA])
    zn_ref[...] = _dot(h, win_ref[:, C_NA:C_ML])
    zm_ref[...] = _dot(h, win_ref[:, C_ML:C_IG])
    zg_ref[...] = _dot(h, wgate_ref[...]) + gb_ref[...]


def _lat_proj_call(layer, x2d, moe, mods, ln1, w_in, gate_b):
    n = x2d.shape[0]
    tb = LAT_PROJ_ROWS
    widths = (C_NA - C_CONV, C_ML - C_NA, C_IG - C_ML, 2 * LANES)
    in_specs = [
        pl.BlockSpec((tb, D_MODEL), lambda i: (i, 0)),
        pl.BlockSpec((None, MODS_ROWS, N_MOD * D_MODEL), lambda i: (layer, 0, 0)),
        pl.BlockSpec((None, 1, D_MODEL), lambda i: (layer, 0, 0)),
        pl.BlockSpec((None, D_MODEL, C_IG), lambda i: (layer, 0, 0)),
        pl.BlockSpec((None, D_MODEL, 2 * LANES), lambda i: (layer, 0, 0)),
        pl.BlockSpec((None, 1, 2 * LANES), lambda i: (layer, 0, 0)),
    ]
    args = [x2d, mods, ln1, w_in[0], w_in[1], gate_b]
    out_specs = [pl.BlockSpec((tb, w), lambda i: (i, 0)) for w in widths]
    out_shape = [jax.ShapeDtypeStruct((n, w), F32) for w in widths]
    if moe is not None:
        per_sb = DSP_SB // tb
        in_specs += [pl.BlockSpec((None, tb * SLAB, LANES), lambda i: (i // per_sb, i % per_sb, 0)),
                     pl.BlockSpec((None, MODS_ROWS, N_MOD * D_MODEL), lambda i: (layer - 1, 0, 0))]
        args += [moe, mods]
        out_specs = [pl.BlockSpec((tb, D_MODEL), lambda i: (i, 0))] + out_specs
        out_shape = [jax.ShapeDtypeStruct((n, D_MODEL), F32)] + out_shape
    outs = pl.pallas_call(
        functools.partial(_lat_proj_kernel, has_moe=moe is not None),
        grid=(n // tb,),
        in_specs=in_specs,
        out_specs=out_specs,
        out_shape=out_shape,
        compiler_params=pltpu.CompilerParams(vmem_limit_bytes=VMEM_LIMIT),
        name="lat_proj",
    )(*args)
    return tuple(outs) if moe is not None else (x2d,) + tuple(outs)


def _na_row_start(r):
    rows = DEC_SEQ // GRID_W
    return min(max(r - NA_WIN_ROWS // 2, 0), rows - NA_WIN_ROWS)


def _lat_local_kernel(zc_ref, zn_ref, kc_ref, vc_ref, band_ref, convw_ref, o_ref):
    o_ref[:, 0:CONV_DIM] = _short_conv(zc_ref[...], convw_ref[...])
    rows = DEC_SEQ // GRID_W
    blk = NA_WIN_ROWS * GRID_W
    q_all = (zn_ref[:, 0:NA_DIM] * NA_HEAD_DIM ** -0.5).astype(BF16)
    k_all = zn_ref[:, NA_DIM:2 * NA_DIM].astype(BF16)
    v_all = zn_ref[:, 2 * NA_DIM:3 * NA_DIM].astype(BF16)
    kc_all = kc_ref[...].astype(BF16)
    vc_all = vc_ref[...].astype(BF16)
    heads = []
    for h in range(NA_HEADS):
        sl = slice(h * NA_HEAD_DIM, (h + 1) * NA_HEAD_DIM)
        q, k, v, kc, vc = q_all[:, sl], k_all[:, sl], v_all[:, sl], kc_all[:, sl], vc_all[:, sl]
        band = band_ref[h]
        s_ctx = _dot_nt(q, kc)
        m_ctx = jnp.max(s_ctx, axis=-1, keepdims=True)
        v1 = _with_ones(v)
        m_rows, o_rows = [], []
        for r in range(rows):
            start = _na_row_start(r)
            off = (start - r + NA_WIN_ROWS - 1) * GRID_W
            q_r = q[r * GRID_W:(r + 1) * GRID_W]
            s_loc = _dot_nt(q_r, k[start * GRID_W:start * GRID_W + blk]) + band[:, off:off + blk]
            m = jnp.maximum(jnp.max(s_loc, axis=-1, keepdims=True), m_ctx[r * GRID_W:(r + 1) * GRID_W])
            p_loc = jnp.exp(s_loc - m)
            m_rows.append(m)
            o_rows.append(_dot(p_loc.astype(BF16), v1[start * GRID_W:start * GRID_W + blk]))
        p_ctx = jnp.exp(s_ctx - jnp.concatenate(m_rows, axis=0))
        o = jnp.concatenate(o_rows, axis=0) + _dot(p_ctx.astype(BF16), _with_ones(vc))
        heads.append(o[:, 0:NA_HEAD_DIM] / o[:, NA_HEAD_DIM:NA_HEAD_DIM + 1])
    o_ref[:, CONV_DIM:CONV_DIM + NA_DIM] = jnp.concatenate(heads, axis=-1)


def _lat_local_call(layer, zc, zn, cache_k, cache_v, band, conv_w):
    nb = zc.shape[0] // DEC_SEQ
    T = DEC_SEQ
    return pl.pallas_call(
        _lat_local_kernel,
        grid=(nb,),
        in_specs=[
            pl.BlockSpec((T, 3 * CONV_DIM), lambda b: (b, 0)),
            pl.BlockSpec((T, 3 * NA_DIM), lambda b: (b, 0)),
            pl.BlockSpec((None, None, PAST_LEN, NA_DIM), lambda b: (b, layer, 0, 0)),
            pl.BlockSpec((None, None, PAST_LEN, NA_DIM), lambda b: (b, layer, 0, 0)),
            pl.BlockSpec((None, NA_HEADS, GRID_W, band.shape[-1]), lambda b: (layer, 0, 0, 0)),
            pl.BlockSpec((None, 3, CONV_DIM), lambda b: (layer, 0, 0)),
        ],
        out_specs=pl.BlockSpec((T, CONV_DIM + NA_DIM), lambda b: (b, 0)),
        out_shape=jax.ShapeDtypeStruct((zc.shape[0], CONV_DIM + NA_DIM), F32),
        compiler_params=pltpu.CompilerParams(vmem_limit_bytes=VMEM_LIMIT),
        name="lat_conv_na",
    )(zc, zn, cache_k, cache_v, band, conv_w)


def _rope(x, cos, sin_signed):
    w = x.shape[-1]
    half = NA_HEAD_DIM // 2
    lane = lax.broadcasted_iota(jnp.int32, x.shape, 1)
    partner = jnp.where(lane % (2 * half) < half, pltpu.roll(x, w - half, axis=1), pltpu.roll(x, half, axis=1))
    return x * cos + partner * sin_signed


def _lat_mlstm_kernel(q_ref, k_ref, v_ref, o_ref, zg_ref, c0_ref, n0_ref, m0_ref, cos_ref, sin_ref,
                      mlg_ref, out_ref, bcol_scr, rowt_scr):
    hh = pl.program_id(1)

    @pl.when(hh == 0)
    def _():
        b_all, r_all = _gate_terms(zg_ref[:, 0:LANES], zg_ref[:, LANES:2 * LANES])
        bcol_scr[...] = b_all
        rowt_scr[...] = r_all

    b_col = bcol_scr[...]
    cos, sin_signed = cos_ref[...], sin_ref[...]
    q = _rope(q_ref[...], cos, sin_signed)
    k = _rope(k_ref[...], cos, sin_signed) * ML_HEAD_DIM ** -0.5
    v = v_ref[...]
    lane = lax.broadcasted_iota(jnp.int32, (DEC_SEQ, LANES), 1)
    hsum = None
    for d in range(N_DIRS):
        j = d * ML_HEADS + hh
        bc = jnp.sum(jnp.where(lane == j, b_col, 0.0), axis=-1, keepdims=True)
        lr = rowt_scr[pl.ds(j, 1), :]
        m0 = m0_ref[d, :, 0:1]
        ho = _mlstm_outputs(q, k, v, bc, lr, m0, c0_ref[d], n0_ref[d], d == 1, 256)
        hsum = ho if hsum is None else hsum + ho
    out_ref[...] = _head_readout(hsum, o_ref[...], mlg_ref[...])


def _lat_mlstm_call(layer, zm, zg, state_c, state_n, state_m, cos, sin_signed, ml_g):
    nb = zm.shape[0] // DEC_SEQ
    T = DEC_SEQ
    hd = ML_HEAD_DIM

    def col(part):
        return pl.BlockSpec((T, hd), lambda b, h: (b, part * ML_HEADS + h))

    return pl.pallas_call(
        _lat_mlstm_kernel,
        grid=(nb, ML_HEADS),
        in_specs=[
            col(0), col(1), col(2), col(3),
            pl.BlockSpec((T, 2 * LANES), lambda b, h: (b, 0)),
            pl.BlockSpec((None, None, N_DIRS, None, hd, hd), lambda b, h: (b, layer, 0, h, 0, 0)),
            pl.BlockSpec((None, None, N_DIRS, None, 1, hd), lambda b, h: (b, layer, 0, h, 0, 0)),
            pl.BlockSpec((None, None, N_DIRS, None, 1, LANES), lambda b, h: (b, layer, 0, h, 0, 0)),
            pl.BlockSpec((T, hd), lambda b, h: (0, 0)),
            pl.BlockSpec((T, hd), lambda b, h: (0, 0)),
            pl.BlockSpec((None, 1, hd), lambda b, h: (layer, 0, h)),
        ],
        out_specs=pl.BlockSpec((T, hd), lambda b, h: (b, h)),
        out_shape=jax.ShapeDtypeStruct((zm.shape[0], ML_DIM), F32),
        scratch_shapes=[pltpu.VMEM((T, LANES), F32), pltpu.VMEM((LANES, T), F32)],
        compiler_params=pltpu.CompilerParams(vmem_limit_bytes=VMEM_LIMIT),
        name="lat_mlstm",
    )(zm, zm, zm, zm, zg, state_c, state_n, state_m, cos, sin_signed, ml_g)


def _lat_merge_kernel(x_ref, loc_ref, ml_ref, mods_ref, wout_ref, o_ref):
    row = 1 + pl.program_id(0) // (DEC_SEQ // LAT_PROJ_ROWS)
    g1 = _mod_rows(mods_ref, row, 0)[2]
    split = CONV_DIM + NA_DIM
    y = _dot(loc_ref[...].astype(BF16), wout_ref[0:split, :]) + _dot(ml_ref[...].astype(BF16), wout_ref[split:, :])
    o_ref[...] = x_ref[...] + g1 * y


def _lat_merge_call(layer, x2d, loc, ml, mods, w_out):
    n = x2d.shape[0]
    tb = LAT_PROJ_ROWS
    return pl.pallas_call(
        _lat_merge_kernel,
        grid=(n // tb,),
        in_specs=[
            pl.BlockSpec((tb, D_MODEL), lambda i: (i, 0)),
            pl.BlockSpec((tb, CONV_DIM + NA_DIM), lambda i: (i, 0)),
            pl.BlockSpec((tb, ML_DIM), lambda i: (i, 0)),
            pl.BlockSpec((None, MODS_ROWS, N_MOD * D_MODEL), lambda i: (layer, 0, 0)),
            pl.BlockSpec((None, D_MODEL, D_MODEL), lambda i: (layer, 0, 0)),
        ],
        out_specs=pl.BlockSpec((tb, D_MODEL), lambda i: (i, 0)),
        out_shape=jax.ShapeDtypeStruct(x2d.shape, F32),
        compiler_params=pltpu.CompilerParams(vmem_limit_bytes=VMEM_LIMIT),
        name="lat_merge",
    )(x2d, loc, ml, mods, w_out)


def _pad_lanes(a, width):
    return jnp.pad(a, [(0, 0)] * (a.ndim - 1) + [(0, width - a.shape[-1])])


def _pack_w_in(w_in):
    w_b = w_in.astype(BF16)
    gates = w_b[..., C_IG:]
    gates = jnp.concatenate([_pad_lanes(gates[..., :N_GATE], LANES),
                             _pad_lanes(gates[..., N_GATE:], LANES)], axis=-1)
    return w_b, gates


def _pack_gate_bias(ml_gate_b):
    gb = ml_gate_b.reshape(DEPTH, 2, N_GATE).astype(F32)
    return jnp.concatenate([_pad_lanes(gb[:, 0], LANES), _pad_lanes(gb[:, 1], LANES)], axis=-1)[:, None, :]


def _rpb_band(na_rpb):
    cols = np.arange(GRID_W)
    col_idx = np.clip(cols[None, :] - cols[:, None] + NA_WIN_COLS - 1, 0, 2 * NA_WIN_COLS - 2)
    col_start = np.clip(cols - NA_WIN_COLS // 2, 0, GRID_W - NA_WIN_COLS)
    col_mask = (cols[None, :] >= col_start[:, None]) & (cols[None, :] < col_start[:, None] + NA_WIN_COLS)
    pick = (col_idx[None] == np.arange(2 * NA_WIN_COLS - 1)[:, None, None]).astype(np.float32)
    t = jnp.einsum('lhri,iqk->lhqrk', na_rpb.astype(F32), jnp.asarray(pick), precision=HIGHEST)
    t = jnp.where(col_mask[None, None, :, None, :], t, NEG_INF)
    t = t.reshape(DEPTH, NA_HEADS, GRID_W, (2 * NA_WIN_ROWS - 1) * GRID_W)
    return _pad_lanes(t, 2 * NA_WIN_ROWS * GRID_W)


def _rope_tables():
    t = np.arange(DEC_SEQ)
    pos = np.stack([t // GRID_W, t % GRID_W], axis=-1).astype(np.float32)
    nf = ML_HEAD_DIM // 4
    inv = jnp.asarray(ROPE_BASE, F32) ** (-jnp.arange(nf, dtype=F32) / nf)
    ang = jnp.asarray(pos)[:, :, None] * inv
    cos = jnp.cos(ang)
    sin = jnp.sin(ang)
    cos_t = jnp.concatenate([cos, cos], axis=-1).reshape(DEC_SEQ, ML_HEAD_DIM)
    sin_t = jnp.concatenate([-sin, sin], axis=-1).reshape(DEC_SEQ, ML_HEAD_DIM)
    return cos_t, sin_t


def kernel(x_prompt, x_sample, cache_k, cache_v, state_C, state_n, state_m, c, c_ctx, w_mod, b_mod,
           ln1_g, w_in, conv_w, na_rpb, ml_gate_b, ml_norm_g, w_out, ln2_g, w_rg, b_rg, w_re, b_re,
           w_gate, w_up, w_down, final_g):
    nb_ctx = x_prompt.shape[0]
    nb_lat = x_sample.shape[0]
    assert 1 + nb_lat <= MODS_ROWS

    cvecs = jnp.concatenate([c_ctx[None, :], c,
                             jnp.zeros((MODS_ROWS - 1 - nb_lat, D_MODEL), F32)], axis=0)
    mods = _mods_call(cvecs, w_mod, b_mod)

    w_in_p = _pack_w_in(w_in)
    w_out_b = w_out.astype(BF16)
    gate_b = _pack_gate_bias(ml_gate_b)
    ln1 = ln1_g.reshape(DEPTH, 1, D_MODEL)
    ln2 = ln2_g.reshape(DEPTH, 1, D_MODEL)
    ml_g = ml_norm_g.reshape(DEPTH, 1, ML_DIM)
    w_router = _pad_lanes(jnp.concatenate([w_re, w_rg], axis=-1), LANES)
    b_router = _pad_lanes(jnp.concatenate([b_re, b_rg], axis=-1), LANES)[:, None, :]
    fg = final_g.reshape(1, D_MODEL)
    band = _rpb_band(na_rpb)
    cos_t, sin_t = _rope_tables()
    ck = cache_k.reshape(nb_lat, DEPTH, PAST_LEN, NA_DIM)
    cv = cache_v.reshape(nb_lat, DEPTH, PAST_LEN, NA_DIM)
    st_n = state_n.reshape(nb_lat, DEPTH, N_DIRS, ML_HEADS, 1, ML_HEAD_DIM)
    st_m = jnp.broadcast_to(state_m[..., None, None], (nb_lat, DEPTH, N_DIRS, ML_HEADS, 1, LANES))

    xp = x_prompt
    xs = x_sample.reshape(nb_lat * DEC_SEQ, D_MODEL)
    states = None
    lat_chunks = DEC_SEQ // DSP_CH
    moe_p = moe_s = None
    for l in range(DEPTH):
        xp, states = _ctx_mixer_call(l, xp, moe_p, mods, ln1, w_in_p, conv_w, gate_b, ml_g, w_out_b,
                                     states)
        moe_p = _sparse_moe(l, xp.reshape(nb_ctx * SEQ, D_MODEL), mods, ln2, w_router, b_router,
                            w_gate, w_up, w_down, None)

        xs, zc, zn, zm, zg = _lat_proj_call(l, xs, moe_s, mods, ln1, w_in_p, gate_b)
        loc = _lat_local_call(l, zc, zn, ck, cv, band, conv_w)
        ml = _lat_mlstm_call(l, zm, zg, state_C, st_n, st_m, cos_t, sin_t, ml_g)
        xs = _lat_merge_call(l, xs, loc, ml, mods, w_out_b)
        moe_s = _sparse_moe(l, xs, mods, ln2, w_router, b_router, w_gate, w_up, w_down, lat_chunks)

    xp = _dsp_final_call(DEPTH - 1, xp.reshape(nb_ctx * SEQ, D_MODEL), moe_p, mods, fg, None)
    xs = _dsp_final_call(DEPTH - 1, xs, moe_s, mods, fg, lat_chunks)
    xp = xp.reshape(nb_ctx, SEQ, D_MODEL)

    new_k, new_v, new_c, new_n, new_m = states
    return (xp, xs.reshape(nb_lat, DEC_SEQ, D_MODEL),
            new_k.reshape(nb_ctx, DEPTH, SEQ, NA_HEADS, NA_HEAD_DIM),
            new_v.reshape(nb_ctx, DEPTH, SEQ, NA_HEADS, NA_HEAD_DIM),
            new_c,
            new_n.reshape(nb_ctx, DEPTH, N_DIRS, ML_HEADS, ML_HEAD_DIM),
            new_m[..., 0].reshape(nb_ctx, DEPTH, N_DIRS, ML_HEADS))
```

```python
import functools

import numpy as np
import jax
import jax.numpy as jnp
from jax import lax
from jax.experimental import pallas as pl
from jax.experimental.pallas import tpu as pltpu

F32 = jnp.float32
BF16 = jnp.bfloat16

D_MODEL = 1024
SEQ = 256
DEPTH = 4
DEC_SEQ = 1024
PAST_LEN = 512
GRID_W = 64
CONV_DIM = 256
NA_HEADS = 4
NA_HEAD_DIM = 64
NA_DIM = NA_HEADS * NA_HEAD_DIM
NA_WIN_ROWS = 8
NA_WIN_COLS = 16
ML_HEADS = 4
ML_HEAD_DIM = 128
ML_DIM = ML_HEADS * ML_HEAD_DIM
N_DIRS = 2
N_GROUPS = 4
EXPERTS_PER_GROUP = 4
N_EXPERTS = N_GROUPS * EXPERTS_PER_GROUP
EXPERT_FF = 512
ROPE_BASE = 10000.0
EPS = 1e-6
N_MOD = 6

LANES = 128
N_GATE = N_DIRS * ML_HEADS
C_CONV = 0
C_NA = 3 * CONV_DIM
C_ML = C_NA + 3 * NA_DIM
C_IG = C_ML + 4 * ML_DIM
CUMSUM_BLOCK = 256
MODS_ROWS = 8
VMEM_LIMIT = 56 * 1024 * 1024

NEG_INF = float("-inf")
HIGHEST = lax.Precision.HIGHEST


def _dot(a, b):
    return jnp.dot(a, b, preferred_element_type=F32)


def _dot_nt(a, b):
    return lax.dot_general(a, b, (((1,), (1,)), ((), ())), preferred_element_type=F32)


def _dot_split(a, b):
    a_hi = a.astype(BF16)
    a_lo = (a - a_hi.astype(F32)).astype(BF16)
    b_hi = b.astype(BF16)
    b_lo = (b - b_hi.astype(F32)).astype(BF16)
    return _dot(jnp.concatenate([a_hi, a_hi, a_lo], axis=1), jnp.concatenate([b_hi, b_lo, b_hi], axis=0))


def _rms_mod(x, g, sc, sh):
    y = x * lax.rsqrt(jnp.mean(x * x, axis=-1, keepdims=True) + EPS)
    return (y * g) * (1.0 + sc) + sh


def _mod_rows(mods_ref, row, first):
    return [mods_ref[pl.ds(row, 1), pl.ds((first + j) * D_MODEL, D_MODEL)] for j in range(3)]


def _mods_kernel(cv_ref, w_ref, b_ref, o_ref):
    cv = cv_ref[...]
    s = cv * jax.nn.sigmoid(cv)
    o_ref[...] = _dot(s.astype(BF16), w_ref[...].astype(BF16)) + b_ref[...]


def _mods_call(cvecs, w_mod, b_mod):
    tn = 1536
    n = N_MOD * D_MODEL
    return pl.pallas_call(
        _mods_kernel,
        grid=(DEPTH, n // tn),
        in_specs=[pl.BlockSpec((MODS_ROWS, D_MODEL), lambda l, j: (0, 0)),
                  pl.BlockSpec((None, D_MODEL, tn), lambda l, j: (l, 0, j)),
                  pl.BlockSpec((None, 1, tn), lambda l, j: (l, 0, j))],
        out_specs=pl.BlockSpec((None, MODS_ROWS, tn), lambda l, j: (l, 0, j)),
        out_shape=jax.ShapeDtypeStruct((DEPTH, MODS_ROWS, n), F32),
        compiler_params=pltpu.CompilerParams(vmem_limit_bytes=VMEM_LIMIT),
        name="adaln_mods",
    )(cvecs, w_mod, b_mod.reshape(DEPTH, 1, n))


def _short_conv(zc, cw):
    T = zc.shape[0]
    cb = zc[:, 0:CONV_DIM]
    u = zc[:, CONV_DIM:2 * CONV_DIM] * zc[:, 2 * CONV_DIM:3 * CONV_DIM]
    t = lax.broadcasted_iota(jnp.int32, u.shape, 0)
    u_prev = jnp.where(t == 0, 0.0, pltpu.roll(u, 1, axis=0))
    u_next = jnp.where(t == T - 1, 0.0, pltpu.roll(u, T - 1, axis=0))
    return cb * (cw[0:1, :] * u_prev + cw[1:2, :] * u + cw[2:3, :] * u_next)


def _with_ones(v):
    return jnp.concatenate([v, jnp.ones_like(v)], axis=-1)


def _softmax_attention(nq, nk, nv):
    outs = []
    for h in range(NA_HEADS):
        sl = slice(h * NA_HEAD_DIM, (h + 1) * NA_HEAD_DIM)
        q = (nq[:, sl] * NA_HEAD_DIM ** -0.5).astype(BF16)
        s = _dot_nt(q, nk[:, sl].astype(BF16))
        p = jnp.exp(s - jnp.max(s, axis=-1, keepdims=True))
        o = _dot(p.astype(BF16), nv[:, sl].astype(BF16)) / jnp.sum(p, axis=-1, keepdims=True)
        outs.append(o)
    return jnp.concatenate(outs, axis=-1)


def _log_sigmoid(x):
    return jnp.minimum(x, 0.0) - jnp.log(1.0 + jnp.exp(-jnp.abs(x)))


def _gate_terms(zi, zf):
    T = zi.shape[0]
    lf = _log_sigmoid(zf)
    blk = min(T, CUMSUM_BLOCK)
    r = lax.broadcasted_iota(jnp.int32, (blk, blk), 0)
    c = lax.broadcasted_iota(jnp.int32, (blk, blk), 1)
    tril = jnp.where(c <= r, 1.0, 0.0).astype(F32)
    parts, carry = [], None
    for r0 in range(0, T, blk):
        b = jnp.dot(tril, lf[r0:r0 + blk], precision=HIGHEST, preferred_element_type=F32)
        if carry is not None:
            b = b + carry
        carry = b[blk - 1:blk, :]
        parts.append(b)
    b_fwd = parts[0] if len(parts) == 1 else jnp.concatenate(parts, axis=0)
    b_bwd = (b_fwd[T - 1:T, :] - b_fwd) + lf
    lane = lax.broadcasted_iota(jnp.int32, zi.shape, 1)
    b_col = jnp.where(lane < ML_HEADS, b_fwd, b_bwd)
    row_t = jnp.transpose(zi - b_col)
    return b_col, row_t


def _mlstm_outputs(q, k, v, b_col, li_row, m0, c0, n0, backward, q_block):
    T = q.shape[0]
    qb, kb, vb = q.astype(BF16), k.astype(BF16), v.astype(BF16)
    c0b = None if c0 is None else c0.astype(BF16)
    outs = []
    for r0 in range(0, T, q_block):
        ks, ke = (r0, T) if backward else (0, r0 + q_block)
        bq = b_col[r0:r0 + q_block]
        d = bq + li_row[:, ks:ke]
        t_idx = r0 + lax.broadcasted_iota(jnp.int32, d.shape, 0)
        s_idx = ks + lax.broadcasted_iota(jnp.int32, d.shape, 1)
        d = jnp.where((s_idx >= t_idx) if backward else (s_idx <= t_idx), d, NEG_INF)
        inter = bq + m0
        m_t = jnp.maximum(jnp.max(d, axis=-1, keepdims=True), inter)
        w = jnp.exp(d - m_t)
        s = _dot_nt(qb[r0:r0 + q_block], kb[ks:ke]) * w
        num = _dot(s.astype(BF16), vb[ks:ke])
        den = jnp.sum(s, axis=-1, keepdims=True)
        if c0 is not None:
            a = jnp.exp(inter - m_t)
            num = num + a * _dot(qb[r0:r0 + q_block], c0b)
            den = den + a * jnp.sum(q[r0:r0 + q_block] * n0, axis=-1, keepdims=True)
        outs.append(num / jnp.maximum(jnp.abs(den), jnp.exp(-m_t)))
    return outs[0] if len(outs) == 1 else jnp.concatenate(outs, axis=0)


def _mlstm_state(k, v, b_col, li_col, backward):
    T = k.shape[0]
    b_tot = b_col[0:1] if backward else b_col[T - 1:T]
    g = (b_tot - b_col) + li_col
    m_new = jnp.maximum(b_tot, jnp.max(g, axis=0, keepdims=True))
    kw = k * jnp.exp(g - m_new)
    c_new = _dot(jnp.transpose(kw).astype(BF16), v.astype(BF16))
    n_new = jnp.sum(kw, axis=0, keepdims=True)
    return c_new, n_new, m_new


def _head_readout(h, mo, g):
    hn = h * lax.rsqrt(jnp.mean(h * h, axis=-1, keepdims=True) + EPS)
    return hn * g * jax.nn.sigmoid(mo)


def _ctx_mixer_kernel(x_ref, mods_ref, ln_ref, win_ref, wgate_ref, convw_ref, gb_ref, mlg_ref, wout_ref,
                      *rest, has_moe):
    xo_ref, ko_ref, vo_ref, co_ref, no_ref, mo_ref = rest[-6:]
    T = SEQ
    sh1, sc1, g1 = _mod_rows(mods_ref, 0, 0)
    x = x_ref[...]
    if has_moe:
        moe_ref, mods_prev_ref = rest[0], rest[1]
        x = x + _mod_rows(mods_prev_ref, 0, 3)[2] * _from_slabs(moe_ref, T)
    h = _rms_mod(x, ln_ref[...], sc1, sh1).astype(BF16)

    conv_o = _short_conv(_dot(h, win_ref[:, C_CONV:C_NA]), convw_ref[...])

    zn = _dot(h, win_ref[:, C_NA:C_ML])
    nk, nv = zn[:, NA_DIM:2 * NA_DIM], zn[:, 2 * NA_DIM:3 * NA_DIM]
    ko_ref[...] = nk
    vo_ref[...] = nv
    na_o = _softmax_attention(zn[:, 0:NA_DIM], nk, nv)

    zg = _dot(h, wgate_ref[...])
    zi = zg[:, 0:LANES] + gb_ref[:, 0:LANES]
    zf = zg[:, LANES:2 * LANES] + gb_ref[:, LANES:2 * LANES]
    b_col, row_t = _gate_terms(zi, zf)
    m0 = jnp.zeros((1, 1), F32)

    zm = _dot(h, win_ref[:, C_ML:C_IG])
    ml_parts = []
    for hh in range(ML_HEADS):
        zq, zk, zv, zo = (zm[:, p * ML_DIM + hh * ML_HEAD_DIM:p * ML_DIM + (hh + 1) * ML_HEAD_DIM]
                          for p in range(4))
        zk = zk * ML_HEAD_DIM ** -0.5
        hsum = None
        for d in range(N_DIRS):
            j = d * ML_HEADS + hh
            bc = b_col[:, j:j + 1]
            ho = _mlstm_outputs(zq, zk, zv, bc, row_t[j:j + 1, :], m0, None, None, d == 1, T)
            hsum = ho if hsum is None else hsum + ho
            c_new, n_new, m_new = _mlstm_state(zk, zv, bc, zi[:, j:j + 1], d == 1)
            co_ref[d, hh] = c_new
            no_ref[j:j + 1, :] = n_new
            mo_ref[j:j + 1, :] = jnp.broadcast_to(m_new, (1, LANES))
        ml_parts.append(_head_readout(hsum, zo, mlg_ref[:, hh * ML_HEAD_DIM:(hh + 1) * ML_HEAD_DIM]))

    mix = jnp.concatenate([conv_o, na_o] + ml_parts, axis=-1).astype(BF16)
    xo_ref[...] = x + g1 * _dot(mix, wout_ref[...])


def _ctx_mixer_call(layer, x, moe, mods, ln1, w_in, conv_w, gate_b, ml_g, w_out, prev):
    B = x.shape[0]
    T = SEQ
    G = None
    state_shapes = [
        jax.ShapeDtypeStruct((B, DEPTH, T, NA_DIM), F32),
        jax.ShapeDtypeStruct((B, DEPTH, T, NA_DIM), F32),
        jax.ShapeDtypeStruct((B, DEPTH, N_DIRS, ML_HEADS, ML_HEAD_DIM, ML_HEAD_DIM), F32),
        jax.ShapeDtypeStruct((B, DEPTH, N_GATE, ML_HEAD_DIM), F32),
        jax.ShapeDtypeStruct((B, DEPTH, N_GATE, LANES), F32),
    ]
    in_specs = [
        pl.BlockSpec((G, T, D_MODEL), lambda b: (b, 0, 0)),
        pl.BlockSpec((None, MODS_ROWS, N_MOD * D_MODEL), lambda b: (layer, 0, 0)),
        pl.BlockSpec((None, 1, D_MODEL), lambda b: (layer, 0, 0)),
        pl.BlockSpec((None, D_MODEL, C_IG), lambda b: (layer, 0, 0)),
        pl.BlockSpec((None, D_MODEL, 2 * LANES), lambda b: (layer, 0, 0)),
        pl.BlockSpec((None, 3, CONV_DIM), lambda b: (layer, 0, 0)),
        pl.BlockSpec((None, 1, 2 * LANES), lambda b: (layer, 0, 0)),
        pl.BlockSpec((None, 1, ML_DIM), lambda b: (layer, 0, 0)),
        pl.BlockSpec((None, D_MODEL, D_MODEL), lambda b: (layer, 0, 0)),
    ]
    args = [x, mods, ln1, w_in[0], w_in[1], conv_w, gate_b, ml_g, w_out]
    if moe is not None:
        per_sb = DSP_SB // T
        in_specs += [pl.BlockSpec((None, T * SLAB, LANES), lambda b: (b // per_sb, b % per_sb, 0)),
                     pl.BlockSpec((None, MODS_ROWS, N_MOD * D_MODEL), lambda b: (layer - 1, 0, 0))]
        args += [moe, mods]
    aliases = {}
    if prev is not None:
        in_specs += [pl.BlockSpec(memory_space=pl.ANY)] * len(prev)
        aliases = {len(args) + i: 1 + i for i in range(len(prev))}
        args += list(prev)
    out_specs = [
        pl.BlockSpec((G, T, D_MODEL), lambda b: (b, 0, 0)),
        pl.BlockSpec((G, None, T, NA_DIM), lambda b: (b, layer, 0, 0)),
        pl.BlockSpec((G, None, T, NA_DIM), lambda b: (b, layer, 0, 0)),
        pl.BlockSpec((G, None, N_DIRS, ML_HEADS, ML_HEAD_DIM, ML_HEAD_DIM),
                     lambda b: (b, layer, 0, 0, 0, 0)),
        pl.BlockSpec((G, None, N_GATE, ML_HEAD_DIM), lambda b: (b, layer, 0, 0)),
        pl.BlockSpec((G, None, N_GATE, LANES), lambda b: (b, layer, 0, 0)),
    ]
    outs = pl.pallas_call(
        functools.partial(_ctx_mixer_kernel, has_moe=moe is not None),
        grid=(B,),
        in_specs=in_specs,
        out_specs=out_specs,
        out_shape=[jax.ShapeDtypeStruct(x.shape, F32)] + state_shapes,
        input_output_aliases=aliases,
        compiler_params=pltpu.CompilerParams(vmem_limit_bytes=VMEM_LIMIT),
        name="ctx_mixer",
    )(*args)
    return outs[0], tuple(outs[1:])


def _route(lg):
    lane = lax.broadcasted_iota(jnp.int32, lg.shape, 1)
    lane_f = lane.astype(F32)
    big = float(LANES)
    is_g = (lane >= N_EXPERTS) & (lane < N_EXPERTS + N_GROUPS)
    glm = jnp.where(is_g, lg, NEG_INF)
    gmax = jnp.max(glm, axis=-1, keepdims=True)
    g_top = jnp.min(jnp.where(glm == gmax, lane_f, big), axis=-1, keepdims=True) - N_EXPERTS
    gp = 1.0 / jnp.sum(jnp.where(is_g, jnp.exp(lg - gmax), 0.0), axis=-1, keepdims=True)
    grp = (lane >> (EXPERTS_PER_GROUP.bit_length() - 1)).astype(F32)
    in_grp = (lane < N_EXPERTS) & (grp == g_top)
    e1 = jnp.where(in_grp, lg, NEG_INF)
    v1 = jnp.max(e1, axis=-1, keepdims=True)
    i1 = jnp.min(jnp.where(e1 == v1, lane_f, big), axis=-1, keepdims=True)
    e2 = jnp.where(in_grp & (lane_f != i1), lg, NEG_INF)
    v2 = jnp.max(e2, axis=-1, keepdims=True)
    i2 = jnp.min(jnp.where(e2 == v2, lane_f, big), axis=-1, keepdims=True)
    t = jnp.exp(v2 - v1)
    w1 = 1.0 / (1.0 + t)
    w2 = t / (1.0 + t)
    return i1, i2, gp * w1, gp * w2


SLAB = D_MODEL // LANES
DSP_SB = 2048
DSP_TM = 160
DSP_CH = 512
DSP_PMAX = 2 * DSP_SB + N_EXPERTS * DSP_TM
DSP_NULL = DSP_PMAX + 2 * DSP_TM
DSP_SLOTS = DSP_NULL + 2 * DSP_TM
DSP_PAD_PAIR = 2 * DSP_SB
DSP_W_LEN = 2 * DSP_SB + LANES
DSP_UNROLL = 8


def _to_slabs(ref, x):
    n = x.shape[0]
    for s in range(SLAB):
        ref[pl.ds(s, n, stride=SLAB), :] = x[:, s * LANES:(s + 1) * LANES]


def _from_slabs(ref, n):
    return jnp.concatenate([ref[pl.ds(s, n, stride=SLAB), :] for s in range(SLAB)], axis=-1)


def _dsp_mods(mods_ref, chunk, chunks_per_row, first):
    row = 0 if chunks_per_row is None else 1 + chunk // chunks_per_row
    return _mod_rows(mods_ref, row, first)


def _dsp_route_kernel(x_ref, mods_ref, ln_ref, wr_ref, br_ref, h_ref, r_ref, cnt_ref, run_scr, *,
                      chunks_per_row):
    c = pl.program_id(0)
    ch = DSP_CH

    @pl.when(c % (DSP_SB // ch) == 0)
    def _():
        run_scr[...] = jnp.zeros_like(run_scr)

    sh2, sc2, _ = _dsp_mods(mods_ref, c, chunks_per_row, 3)
    h2 = _rms_mod(x_ref[...], ln_ref[...], sc2, sh2)
    _to_slabs(h_ref, h2)
    lg = _dot_split(h2, wr_ref[...]) + br_ref[...]
    i1, i2, w1, w2 = _route(lg)
    lane = lax.broadcasted_iota(jnp.int32, lg.shape, 1)
    lane_f = lane.astype(F32)
    oh1 = jnp.where(lane_f == i1, 1.0, 0.0)
    oh2 = jnp.where(lane_f == i2, 1.0, 0.0)
    both = oh1 + oh2
    r = lax.broadcasted_iota(jnp.int32, (ch, ch), 0)
    s = lax.broadcasted_iota(jnp.int32, (ch, ch), 1)
    earlier = jnp.where(s < r, 1.0, 0.0).astype(BF16)
    run = run_scr[0:1, :]
    before = _dot(earlier, both.astype(BF16)) + run
    rank1 = jnp.sum(before * oh1, axis=-1, keepdims=True)
    rank2 = jnp.sum(before * oh2, axis=-1, keepdims=True)
    run = run + jnp.sum(both, axis=0, keepdims=True)
    run_scr[0:1, :] = run
    cnt_ref[...] = jnp.broadcast_to(run, cnt_ref.shape)
    cols = (i1, i2, w1, w2, rank1, rank2)
    out = jnp.zeros(lg.shape, F32)
    for j, col in enumerate(cols):
        out = jnp.where(lane == j, col, out)
    r_ref[...] = out


def _dsp_route_call(layer, x2d, mods, ln2, w_router, b_router, chunks_per_row):
    n = x2d.shape[0]
    ch = DSP_CH
    per_sb = DSP_SB // ch
    return pl.pallas_call(
        functools.partial(_dsp_route_kernel, chunks_per_row=chunks_per_row),
        grid=(n // ch,),
        in_specs=[
            pl.BlockSpec((ch, D_MODEL), lambda c: (c, 0)),
            pl.BlockSpec((None, MODS_ROWS, N_MOD * D_MODEL), lambda c: (layer, 0, 0)),
            pl.BlockSpec((None, 1, D_MODEL), lambda c: (layer, 0, 0)),
            pl.BlockSpec((None, D_MODEL, LANES), lambda c: (layer, 0, 0)),
            pl.BlockSpec((None, 1, LANES), lambda c: (layer, 0, 0)),
        ],
        out_specs=[pl.BlockSpec((ch * SLAB, LANES), lambda c: (c, 0)),
                   pl.BlockSpec((ch, LANES), lambda c: (c, 0)),
                   pl.BlockSpec((None, SLAB, LANES), lambda c: (c // per_sb, 0, 0))],
        out_shape=[jax.ShapeDtypeStruct((n * SLAB, LANES), F32),
                   jax.ShapeDtypeStruct((n, LANES), F32),
                   jax.ShapeDtypeStruct((n // DSP_SB, SLAB, LANES), F32)],
        scratch_shapes=[pltpu.VMEM((SLAB, LANES), F32)],
        compiler_params=pltpu.CompilerParams(vmem_limit_bytes=VMEM_LIMIT),
        name="moe_route",
    )(x2d, mods, ln2, w_router, b_router)


def _dispatch_tables(route, counts, n_sb):
    e = route[:, 0:2].astype(jnp.int32).reshape(n_sb, 2 * DSP_SB)
    rank = route[:, 4:6].astype(jnp.int32).reshape(n_sb, 2 * DSP_SB)
    w = _pad_lanes(route[:, 2:4].reshape(n_sb, 1, 2 * DSP_SB), DSP_W_LEN)
    cnt = counts[:, 0, :N_EXPERTS].astype(jnp.int32)
    ntile = (cnt + (DSP_TM - 1)) // DSP_TM
    off = (jnp.cumsum(ntile, axis=1) - ntile) * DSP_TM
    is_e = e[:, None, :] == jnp.arange(N_EXPERTS, dtype=jnp.int32)[None, :, None]
    pos = (jnp.sum(jnp.where(is_e, off[:, :, None], 0), axis=1) + rank).reshape(n_sb, 1, 2 * DSP_SB)
    return pos, w, ntile.reshape(-1), off.reshape(-1), cnt.reshape(-1)


def _dsp_expert_kernel(ntile_ref, off_ref, cnt_ref, pos_ref, w_ref, h_ref, wg_ref, wu_ref, wd_ref,
                       o_ref, pair_smem, sched_smem, pend_smem, xs0, xs1, ys0, ys1,
                       wg_scr, wu_scr, wd_scr):
    sb = pl.program_id(0)
    e = pl.program_id(1)
    tm, u = DSP_TM, DSP_UNROLL
    xs_scr, ys_scr = (xs0, xs1), (ys0, ys1)
    slab_shift = SLAB.bit_length() - 1
    last_row = (DSP_SB - 1) * SLAB

    def slab_row(pair):
        return lax.shift_left(lax.shift_right_logical(pair, 1), slab_shift)

    @pl.when(e == 0)
    def _():
        for ee in range(N_EXPERTS):
            q = sb * N_EXPERTS + ee

            def pad(p, carry):
                pair_smem[p] = DSP_PAD_PAIR
                return carry

            lax.fori_loop(off_ref[q] + cnt_ref[q], off_ref[q] + ntile_ref[q] * tm, pad, 0)

        def invert(j, carry):
            for k in range(u):
                a = j * u + k
                pair_smem[pos_ref[0, a]] = a
            return carry

        lax.fori_loop(0, 2 * DSP_SB // u, invert, 0)

        last_q = sb * N_EXPERTS + (N_EXPERTS - 1)
        used_end = off_ref[last_q] + ntile_ref[last_q] * tm

        def pad_end(p, carry):
            pair_smem[used_end + p] = DSP_PAD_PAIR
            pair_smem[DSP_NULL + p] = DSP_PAD_PAIR
            return carry

        lax.fori_loop(0, 2 * tm, pad_end, 0)
        sched_smem[0] = -1
        sched_smem[1] = DSP_NULL
        pend_smem[0] = 0.0
        o_ref[...] = jnp.zeros_like(o_ref)
        for ys in ys_scr:
            ys[...] = jnp.zeros_like(ys)

    q = sb * N_EXPERTS + e
    base0 = off_ref[q]
    nt = ntile_ref[q]
    wg_scr[...] = wg_ref[...].astype(BF16)
    wu_scr[...] = wu_ref[...].astype(BF16)
    wd_scr[...] = wd_ref[...].astype(BF16)

    def gather(base, buf):
        for r in range(tm):
            src = jnp.minimum(slab_row(pair_smem[base + r]), last_row)
            xs_scr[buf][r * SLAB:(r + 1) * SLAB, :] = h_ref[pl.ds(pl.multiple_of(src, SLAB), SLAB), :]

    def compute(buf):
        x = _from_slabs(xs_scr[buf], tm).astype(BF16)
        a = jax.nn.silu(_dot(x, wg_scr[...])) * _dot(x, wu_scr[...])
        _to_slabs(ys_scr[buf], _dot(a.astype(BF16), wd_scr[...]))

    def scatter(base, buf, scale):
        for j in range(tm // u):
            upd = []
            for k in range(u):
                r = j * u + k
                pair = pair_smem[base + r]
                dst = pl.multiple_of(slab_row(pair), SLAB)
                y = ys_scr[buf][r * SLAB:(r + 1) * SLAB, :]
                upd.append((dst, o_ref[pl.ds(dst, SLAB), :] + (w_ref[0, pair] * scale) * y))
            for dst, v in upd:
                o_ref[pl.ds(dst, SLAB), :] = v

    n_pairs = lax.shift_right_logical(nt + 1, 1)

    @pl.when((n_pairs > 0) & (sched_smem[0] != base0))
    def _():
        gather(base0, 0)

    pend_base = sched_smem[1]
    pend_scale = pend_smem[0]

    def pair(k, carry):
        b = base0 + 2 * k * tm
        first = k == 0
        gather(b + tm, 1)
        compute(0)
        scatter(jnp.where(first, pend_base, b - tm), 1, jnp.where(first, pend_scale, 1.0))
        gather(b + 2 * tm, 0)
        compute(1)
        scatter(b, 0, 1.0)
        return carry

    lax.fori_loop(0, n_pairs, pair, 0)

    @pl.when(n_pairs > 0)
    def _():
        last = 2 * n_pairs - 1
        sched_smem[0] = base0 + (last + 1) * tm
        sched_smem[1] = base0 + last * tm
        pend_smem[0] = jnp.where(last < nt, 1.0, 0.0)

    @pl.when(e == N_EXPERTS - 1)
    def _():
        scatter(sched_smem[1], 1, pend_smem[0])


def _dsp_expert_call(layer, h_slabs, pos, w, ntile, off, cnt, w_gate, w_up, w_down):
    n_sb = pos.shape[0]
    out_rows = (DSP_SB + DSP_CH) * SLAB

    def per_sb(s, e, *prefetch):
        return (s, 0, 0)

    def w_idx(s, e, *prefetch):
        return (layer, e, 0, 0)

    grid_spec = pltpu.PrefetchScalarGridSpec(
        num_scalar_prefetch=3,
        grid=(n_sb, N_EXPERTS),
        in_specs=[
            pl.BlockSpec((None, 1, 2 * DSP_SB), per_sb, memory_space=pltpu.SMEM),
            pl.BlockSpec((None, 1, DSP_W_LEN), per_sb, memory_space=pltpu.SMEM),
            pl.BlockSpec((DSP_SB * SLAB, LANES), lambda s, e, *prefetch: (s, 0)),
            pl.BlockSpec((None, None, D_MODEL, EXPERT_FF), w_idx),
            pl.BlockSpec((None, None, D_MODEL, EXPERT_FF), w_idx),
            pl.BlockSpec((None, None, EXPERT_FF, D_MODEL), w_idx),
        ],
        out_specs=pl.BlockSpec((None, out_rows, LANES), per_sb),
        scratch_shapes=[pltpu.SMEM((DSP_SLOTS,), jnp.int32),
                        pltpu.SMEM((2,), jnp.int32),
                        pltpu.SMEM((1,), F32),
                        pltpu.VMEM((DSP_TM * SLAB, LANES), F32),
                        pltpu.VMEM((DSP_TM * SLAB, LANES), F32),
                        pltpu.VMEM((DSP_TM * SLAB, LANES), F32),
                        pltpu.VMEM((DSP_TM * SLAB, LANES), F32),
                        pltpu.VMEM((D_MODEL, EXPERT_FF), BF16),
                        pltpu.VMEM((D_MODEL, EXPERT_FF), BF16),
                        pltpu.VMEM((EXPERT_FF, D_MODEL), BF16)],
    )
    return pl.pallas_call(
        _dsp_expert_kernel,
        grid_spec=grid_spec,
        out_shape=jax.ShapeDtypeStruct((n_sb, out_rows, LANES), F32),
        compiler_params=pltpu.CompilerParams(vmem_limit_bytes=VMEM_LIMIT),
        name="moe_experts",
    )(ntile, off, cnt, pos, w, h_slabs, w_gate, w_up, w_down)


def _dsp_final_kernel(x_ref, m_ref, mods_ref, fg_ref, o_ref, *, chunks_per_row):
    g2 = _dsp_mods(mods_ref, pl.program_id(0), chunks_per_row, 3)[2]
    y = x_ref[...] + g2 * _from_slabs(m_ref, DSP_CH)
    o_ref[...] = y * lax.rsqrt(jnp.mean(y * y, axis=-1, keepdims=True) + EPS) * fg_ref[...]


def _dsp_final_call(layer, x2d, m_slabs, mods, final_g, chunks_per_row):
    n = x2d.shape[0]
    ch = DSP_CH
    per_sb = DSP_SB // ch
    return pl.pallas_call(
        functools.partial(_dsp_final_kernel, chunks_per_row=chunks_per_row),
        grid=(n // ch,),
        in_specs=[
            pl.BlockSpec((ch, D_MODEL), lambda c: (c, 0)),
            pl.BlockSpec((None, ch * SLAB, LANES), lambda c: (c // per_sb, c % per_sb, 0)),
            pl.BlockSpec((None, MODS_ROWS, N_MOD * D_MODEL), lambda c: (layer, 0, 0)),
            pl.BlockSpec((1, D_MODEL), lambda c: (0, 0)),
        ],
        out_specs=pl.BlockSpec((ch, D_MODEL), lambda c: (c, 0)),
        out_shape=jax.ShapeDtypeStruct(x2d.shape, F32),
        compiler_params=pltpu.CompilerParams(vmem_limit_bytes=VMEM_LIMIT),
        name="final_norm",
    )(x2d, m_slabs, mods, final_g)


def _sparse_moe(layer, x2d, mods, ln2, w_router, b_router, w_gate, w_up, w_down, chunks_per_row):
    n_sb = x2d.shape[0] // DSP_SB
    h_slabs, route, counts = _dsp_route_call(layer, x2d, mods, ln2, w_router, b_router,
                                             chunks_per_row)
    pos, w, ntile, off, cnt = _dispatch_tables(route, counts, n_sb)
    return _dsp_expert_call(layer, h_slabs, pos, w, ntile, off, cnt, w_gate, w_up, w_down)


LAT_PROJ_ROWS = 512


def _lat_proj_kernel(x_ref, mods_ref, ln_ref, win_ref, wgate_ref, gb_ref, *rest, has_moe):
    zc_ref, zn_ref, zm_ref, zg_ref = rest[-4:]
    row = 1 + pl.program_id(0) // (DEC_SEQ // LAT_PROJ_ROWS)
    sh1, sc1, _ = _mod_rows(mods_ref, row, 0)
    x = x_ref[...]
    if has_moe:
        moe_ref, mods_prev_ref, xo_ref = rest[0], rest[1], rest[2]
        x = x + _mod_rows(mods_prev_ref, row, 3)[2] * _from_slabs(moe_ref, LAT_PROJ_ROWS)
        xo_ref[...] = x
    h = _rms_mod(x, ln_ref[...], sc1, sh1).astype(BF16)
    zc_ref[...] = _dot(h, win_ref[:, C_CONV:C_NA])
    zn_ref[...] = _dot(h, win_ref[:, C_NA:C_ML])
    zm_ref[...] = _dot(h, win_ref[:, C_ML:C_IG])
    zg_ref[...] = _dot(h, wgate_ref[...]) + gb_ref[...]


def _lat_proj_call(layer, x2d, moe, mods, ln1, w_in, gate_b):
    n = x2d.shape[0]
    tb = LAT_PROJ_ROWS
    widths = (C_NA - C_CONV, C_ML - C_NA, C_IG - C_ML, 2 * LANES)
    in_specs = [
        pl.BlockSpec((tb, D_MODEL), lambda i: (i, 0)),
        pl.BlockSpec((None, MODS_ROWS, N_MOD * D_MODEL), lambda i: (layer, 0, 0)),
        pl.BlockSpec((None, 1, D_MODEL), lambda i: (layer, 0, 0)),
        pl.BlockSpec((None, D_MODEL, C_IG), lambda i: (layer, 0, 0)),
        pl.BlockSpec((None, D_MODEL, 2 * LANES), lambda i: (layer, 0, 0)),
        pl.BlockSpec((None, 1, 2 * LANES), lambda i: (layer, 0, 0)),
    ]
    args = [x2d, mods, ln1, w_in[0], w_in[1], gate_b]
    out_specs = [pl.BlockSpec((tb, w), lambda i: (i, 0)) for w in widths]
    out_shape = [jax.ShapeDtypeStruct((n, w), F32) for w in widths]
    if moe is not None:
        per_sb = DSP_SB // tb
        in_specs += [pl.BlockSpec((None, tb * SLAB, LANES), lambda i: (i // per_sb, i % per_sb, 0)),
                     pl.BlockSpec((None, MODS_ROWS, N_MOD * D_MODEL), lambda i: (layer - 1, 0, 0))]
        args += [moe, mods]
        out_specs = [pl.BlockSpec((tb, D_MODEL), lambda i: (i, 0))] + out_specs
        out_shape = [jax.ShapeDtypeStruct((n, D_MODEL), F32)] + out_shape
    outs = pl.pallas_call(
        functools.partial(_lat_proj_kernel, has_moe=moe is not None),
        grid=(n // tb,),
        in_specs=in_specs,
        out_specs=out_specs,
        out_shape=out_shape,
        compiler_params=pltpu.CompilerParams(vmem_limit_bytes=VMEM_LIMIT),
        name="lat_proj",
    )(*args)
    return tuple(outs) if moe is not None else (x2d,) + tuple(outs)


def _na_row_start(r):
    rows = DEC_SEQ // GRID_W
    return min(max(r - NA_WIN_ROWS // 2, 0), rows - NA_WIN_ROWS)


def _lat_local_kernel(zc_ref, zn_ref, kc_ref, vc_ref, band_ref, convw_ref, o_ref):
    o_ref[:, 0:CONV_DIM] = _short_conv(zc_ref[...], convw_ref[...])
    rows = DEC_SEQ // GRID_W
    blk = NA_WIN_ROWS * GRID_W
    q_all = (zn_ref[:, 0:NA_DIM] * NA_HEAD_DIM ** -0.5).astype(BF16)
    k_all = zn_ref[:, NA_DIM:2 * NA_DIM].astype(BF16)
    v_all = zn_ref[:, 2 * NA_DIM:3 * NA_DIM].astype(BF16)
    kc_all = kc_ref[...].astype(BF16)
    vc_all = vc_ref[...].astype(BF16)
    heads = []
    for h in range(NA_HEADS):
        sl = slice(h * NA_HEAD_DIM, (h + 1) * NA_HEAD_DIM)
        q, k, v, kc, vc = q_all[:, sl], k_all[:, sl], v_all[:, sl], kc_all[:, sl], vc_all[:, sl]
        band = band_ref[h]
        s_ctx = _dot_nt(q, kc)
        m_ctx = jnp.max(s_ctx, axis=-1, keepdims=True)
        v1 = _with_ones(v)
        m_rows, o_rows = [], []
        for r in range(rows):
            start = _na_row_start(r)
            off = (start - r + NA_WIN_ROWS - 1) * GRID_W
            q_r = q[r * GRID_W:(r + 1) * GRID_W]
            s_loc = _dot_nt(q_r, k[start * GRID_W:start * GRID_W + blk]) + band[:, off:off + blk]
            m = jnp.maximum(jnp.max(s_loc, axis=-1, keepdims=True), m_ctx[r * GRID_W:(r + 1) * GRID_W])
            p_loc = jnp.exp(s_loc - m)
            m_rows.append(m)
            o_rows.append(_dot(p_loc.astype(BF16), v1[start * GRID_W:start * GRID_W + blk]))
        p_ctx = jnp.exp(s_ctx - jnp.concatenate(m_rows, axis=0))
        o = jnp.concatenate(o_rows, axis=0) + _dot(p_ctx.astype(BF16), _with_ones(vc))
        heads.append(o[:, 0:NA_HEAD_DIM] / o[:, NA_HEAD_DIM:NA_HEAD_DIM + 1])
    o_ref[:, CONV_DIM:CONV_DIM + NA_DIM] = jnp.concatenate(heads, axis=-1)


def _lat_local_call(layer, zc, zn, cache_k, cache_v, band, conv_w):
    nb = zc.shape[0] // DEC_SEQ
    T = DEC_SEQ
    return pl.pallas_call(
        _lat_local_kernel,
        grid=(nb,),
        in_specs=[
            pl.BlockSpec((T, 3 * CONV_DIM), lambda b: (b, 0)),
            pl.BlockSpec((T, 3 * NA_DIM), lambda b: (b, 0)),
            pl.BlockSpec((None, None, PAST_LEN, NA_DIM), lambda b: (b, layer, 0, 0)),
            pl.BlockSpec((None, None, PAST_LEN, NA_DIM), lambda b: (b, layer, 0, 0)),
            pl.BlockSpec((None, NA_HEADS, GRID_W, band.shape[-1]), lambda b: (layer, 0, 0, 0)),
            pl.BlockSpec((None, 3, CONV_DIM), lambda b: (layer, 0, 0)),
        ],
        out_specs=pl.BlockSpec((T, CONV_DIM + NA_DIM), lambda b: (b, 0)),
        out_shape=jax.ShapeDtypeStruct((zc.shape[0], CONV_DIM + NA_DIM), F32),
        compiler_params=pltpu.CompilerParams(vmem_limit_bytes=VMEM_LIMIT),
        name="lat_conv_na",
    )(zc, zn, cache_k, cache_v, band, conv_w)


def _rope(x, cos, sin_signed):
    w = x.shape[-1]
    half = NA_HEAD_DIM // 2
    lane = lax.broadcasted_iota(jnp.int32, x.shape, 1)
    partner = jnp.where(lane % (2 * half) < half, pltpu.roll(x, w - half, axis=1), pltpu.roll(x, half, axis=1))
    return x * cos + partner * sin_signed


def _lat_mlstm_kernel(q_ref, k_ref, v_ref, o_ref, zg_ref, c0_ref, n0_ref, m0_ref, cos_ref, sin_ref,
                      mlg_ref, out_ref, bcol_scr, rowt_scr):
    hh = pl.program_id(1)

    @pl.when(hh == 0)
    def _():
        b_all, r_all = _gate_terms(zg_ref[:, 0:LANES], zg_ref[:, LANES:2 * LANES])
        bcol_scr[...] = b_all
        rowt_scr[...] = r_all

    b_col = bcol_scr[...]
    cos, sin_signed = cos_ref[...], sin_ref[...]
    q = _rope(q_ref[...], cos, sin_signed)
    k = _rope(k_ref[...], cos, sin_signed) * ML_HEAD_DIM ** -0.5
    v = v_ref[...]
    lane = lax.broadcasted_iota(jnp.int32, (DEC_SEQ, LANES), 1)
    hsum = None
    for d in range(N_DIRS):
        j = d * ML_HEADS + hh
        bc = jnp.sum(jnp.where(lane == j, b_col, 0.0), axis=-1, keepdims=True)
        lr = rowt_scr[pl.ds(j, 1), :]
        m0 = m0_ref[d, :, 0:1]
        ho = _mlstm_outputs(q, k, v, bc, lr, m0, c0_ref[d], n0_ref[d], d == 1, 256)
        hsum = ho if hsum is None else hsum + ho
    out_ref[...] = _head_readout(hsum, o_ref[...], mlg_ref[...])


def _lat_mlstm_call(layer, zm, zg, state_c, state_n, state_m, cos, sin_signed, ml_g):
    nb = zm.shape[0] // DEC_SEQ
    T = DEC_SEQ
    hd = ML_HEAD_DIM

    def col(part):
        return pl.BlockSpec((T, hd), lambda b, h: (b, part * ML_HEADS + h))

    return pl.pallas_call(
        _lat_mlstm_kernel,
        grid=(nb, ML_HEADS),
        in_specs=[
            col(0), col(1), col(2), col(3),
            pl.BlockSpec((T, 2 * LANES), lambda b, h: (b, 0)),
            pl.BlockSpec((None, None, N_DIRS, None, hd, hd), lambda b, h: (b, layer, 0, h, 0, 0)),
            pl.BlockSpec((None, None, N_DIRS, None, 1, hd), lambda b, h: (b, layer, 0, h, 0, 0)),
            pl.BlockSpec((None, None, N_DIRS, None, 1, LANES), lambda b, h: (b, layer, 0, h, 0, 0)),
            pl.BlockSpec((T, hd), lambda b, h: (0, 0)),
            pl.BlockSpec((T, hd), lambda b, h: (0, 0)),
            pl.BlockSpec((None, 1, hd), lambda b, h: (layer, 0, h)),
        ],
        out_specs=pl.BlockSpec((T, hd), lambda b, h: (b, h)),
        out_shape=jax.ShapeDtypeStruct((zm.shape[0], ML_DIM), F32),
        scratch_shapes=[pltpu.VMEM((T, LANES), F32), pltpu.VMEM((LANES, T), F32)],
        compiler_params=pltpu.CompilerParams(vmem_limit_bytes=VMEM_LIMIT),
        name="lat_mlstm",
    )(zm, zm, zm, zm, zg, state_c, state_n, state_m, cos, sin_signed, ml_g)


def _lat_merge_kernel(x_ref, loc_ref, ml_ref, mods_ref, wout_ref, o_ref):
    row = 1 + pl.program_id(0) // (DEC_SEQ // LAT_PROJ_ROWS)
    g1 = _mod_rows(mods_ref, row, 0)[2]
    split = CONV_DIM + NA_DIM
    y = _dot(loc_ref[...].astype(BF16), wout_ref[0:split, :]) + _dot(ml_ref[...].astype(BF16), wout_ref[split:, :])
    o_ref[...] = x_ref[...] + g1 * y


def _lat_merge_call(layer, x2d, loc, ml, mods, w_out):
    n = x2d.shape[0]
    tb = LAT_PROJ_ROWS
    return pl.pallas_call(
        _lat_merge_kernel,
        grid=(n // tb,),
        in_specs=[
            pl.BlockSpec((tb, D_MODEL), lambda i: (i, 0)),
            pl.BlockSpec((tb, CONV_DIM + NA_DIM), lambda i: (i, 0)),
            pl.BlockSpec((tb, ML_DIM), lambda i: (i, 0)),
            pl.BlockSpec((None, MODS_ROWS, N_MOD * D_MODEL), lambda i: (layer, 0, 0)),
            pl.BlockSpec((None, D_MODEL, D_MODEL), lambda i: (layer, 0, 0)),
        ],
        out_specs=pl.BlockSpec((tb, D_MODEL), lambda i: (i, 0)),
        out_shape=jax.ShapeDtypeStruct(x2d.shape, F32),
        compiler_params=pltpu.CompilerParams(vmem_limit_bytes=VMEM_LIMIT),
        name="lat_merge",
    )(x2d, loc, ml, mods, w_out)


def _pad_lanes(a, width):
    return jnp.pad(a, [(0, 0)] * (a.ndim - 1) + [(0, width - a.shape[-1])])


def _pack_w_in(w_in):
    w_b = w_in.astype(BF16)
    gates = w_b[..., C_IG:]
    gates = jnp.concatenate([_pad_lanes(gates[..., :N_GATE], LANES),
                             _pad_lanes(gates[..., N_GATE:], LANES)], axis=-1)
    return w_b, gates


def _pack_gate_bias(ml_gate_b):
    gb = ml_gate_b.reshape(DEPTH, 2, N_GATE).astype(F32)
    return jnp.concatenate([_pad_lanes(gb[:, 0], LANES), _pad_lanes(gb[:, 1], LANES)], axis=-1)[:, None, :]


def _rpb_band(na_rpb):
    cols = np.arange(GRID_W)
    col_idx = np.clip(cols[None, :] - cols[:, None] + NA_WIN_COLS - 1, 0, 2 * NA_WIN_COLS - 2)
    col_start = np.clip(cols - NA_WIN_COLS // 2, 0, GRID_W - NA_WIN_COLS)
    col_mask = (cols[None, :] >= col_start[:, None]) & (cols[None, :] < col_start[:, None] + NA_WIN_COLS)
    pick = (col_idx[None] == np.arange(2 * NA_WIN_COLS - 1)[:, None, None]).astype(np.float32)
    t = jnp.einsum('lhri,iqk->lhqrk', na_rpb.astype(F32), jnp.asarray(pick), precision=HIGHEST)
    t = jnp.where(col_mask[None, None, :, None, :], t, NEG_INF)
    t = t.reshape(DEPTH, NA_HEADS, GRID_W, (2 * NA_WIN_ROWS - 1) * GRID_W)
    return _pad_lanes(t, 2 * NA_WIN_ROWS * GRID_W)


def _rope_tables():
    t = np.arange(DEC_SEQ)
    pos = np.stack([t // GRID_W, t % GRID_W], axis=-1).astype(np.float32)
    nf = ML_HEAD_DIM // 4
    inv = jnp.asarray(ROPE_BASE, F32) ** (-jnp.arange(nf, dtype=F32) / nf)
    ang = jnp.asarray(pos)[:, :, None] * inv
    cos = jnp.cos(ang)
    sin = jnp.sin(ang)
    cos_t = jnp.concatenate([cos, cos], axis=-1).reshape(DEC_SEQ, ML_HEAD_DIM)
    sin_t = jnp.concatenate([-sin, sin], axis=-1).reshape(DEC_SEQ, ML_HEAD_DIM)
    return cos_t, sin_t


def kernel(x_prompt, x_sample, cache_k, cache_v, state_C, state_n, state_m, c, c_ctx, w_mod, b_mod,
           ln1_g, w_in, conv_w, na_rpb, ml_gate_b, ml_norm_g, w_out, ln2_g, w_rg, b_rg, w_re, b_re,
           w_gate, w_up, w_down, final_g):
    nb_ctx = x_prompt.shape[0]
    nb_lat = x_sample.shape[0]
    assert 1 + nb_lat <= MODS_ROWS

    cvecs = jnp.concatenate([c_ctx[None, :], c,
                             jnp.zeros((MODS_ROWS - 1 - nb_lat, D_MODEL), F32)], axis=0)
    mods = _mods_call(cvecs, w_mod, b_mod)

    w_in_p = _pack_w_in(w_in)
    w_out_b = w_out.astype(BF16)
    gate_b = _pack_gate_bias(ml_gate_b)
    ln1 = ln1_g.reshape(DEPTH, 1, D_MODEL)
    ln2 = ln2_g.reshape(DEPTH, 1, D_MODEL)
    ml_g = ml_norm_g.reshape(DEPTH, 1, ML_DIM)
    w_router = _pad_lanes(jnp.concatenate([w_re, w_rg], axis=-1), LANES)
    b_router = _pad_lanes(jnp.concatenate([b_re, b_rg], axis=-1), LANES)[:, None, :]
    fg = final_g.reshape(1, D_MODEL)
    band = _rpb_band(na_rpb)
    cos_t, sin_t = _rope_tables()
    ck = cache_k.reshape(nb_lat, DEPTH, PAST_LEN, NA_DIM)
    cv = cache_v.reshape(nb_lat, DEPTH, PAST_LEN, NA_DIM)
    st_n = state_n.reshape(nb_lat, DEPTH, N_DIRS, ML_HEADS, 1, ML_HEAD_DIM)
    st_m = jnp.broadcast_to(state_m[..., None, None], (nb_lat, DEPTH, N_DIRS, ML_HEADS, 1, LANES))

    xp = x_prompt
    xs = x_sample.reshape(nb_lat * DEC_SEQ, D_MODEL)
    states = None
    lat_chunks = DEC_SEQ // DSP_CH
    moe_p = moe_s = None
    for l in range(DEPTH):
        xp, states = _ctx_mixer_call(l, xp, moe_p, mods, ln1, w_in_p, conv_w, gate_b, ml_g, w_out_b,
                                     states)
        moe_p = _sparse_moe(l, xp.reshape(nb_ctx * SEQ, D_MODEL), mods, ln2, w_router, b_router,
                            w_gate, w_up, w_down, None)

        xs, zc, zn, zm, zg = _lat_proj_call(l, xs, moe_s, mods, ln1, w_in_p, gate_b)
        loc = _lat_local_call(l, zc, zn, ck, cv, band, conv_w)
        ml = _lat_mlstm_call(l, zm, zg, state_C, st_n, st_m, cos_t, sin_t, ml_g)
        xs = _lat_merge_call(l, xs, loc, ml, mods, w_out_b)
        moe_s = _sparse_moe(l, xs, mods, ln2, w_router, b_router, w_gate, w_up, w_down, lat_chunks)

    xp = _dsp_final_call(DEPTH - 1, xp.reshape(nb_ctx * SEQ, D_MODEL), moe_p, mods, fg, None)
    xs = _dsp_final_call(DEPTH - 1, xs, moe_s, mods, fg, lat_chunks)
    xp = xp.reshape(nb_ctx, SEQ, D_MODEL)

    new_k, new_v, new_c, new_n, new_m = states
    return (xp, xs.reshape(nb_lat, DEC_SEQ, D_MODEL),
            new_k.reshape(nb_ctx, DEPTH, SEQ, NA_HEADS, NA_HEAD_DIM),
            new_v.reshape(nb_ctx, DEPTH, SEQ, NA_HEADS, NA_HEAD_DIM),
            new_c,
            new_n.reshape(nb_ctx, DEPTH, N_DIRS, ML_HEADS, ML_HEAD_DIM),
            new_m[..., 0].reshape(nb_ctx, DEPTH, N_DIRS, ML_HEADS))
```

```python
import functools

import numpy as np
import jax
import jax.numpy as jnp
from jax import lax
from jax.experimental import pallas as pl
from jax.experimental.pallas import tpu as pltpu

F32 = jnp.float32
BF16 = jnp.bfloat16

D_MODEL = 1024
SEQ = 256
DEPTH = 4
DEC_SEQ = 1024
PAST_LEN = 512
GRID_W = 64
CONV_DIM = 256
NA_HEADS = 4
NA_HEAD_DIM = 64
NA_DIM = NA_HEADS * NA_HEAD_DIM
NA_WIN_ROWS = 8
NA_WIN_COLS = 16
ML_HEADS = 4
ML_HEAD_DIM = 128
ML_DIM = ML_HEADS * ML_HEAD_DIM
N_DIRS = 2
N_GROUPS = 4
EXPERTS_PER_GROUP = 4
N_EXPERTS = N_GROUPS * EXPERTS_PER_GROUP
EXPERT_FF = 512
ROPE_BASE = 10000.0
EPS = 1e-6
N_MOD = 6

LANES = 128
N_GATE = N_DIRS * ML_HEADS
C_CONV = 0
C_NA = 3 * CONV_DIM
C_ML = C_NA + 3 * NA_DIM
C_IG = C_ML + 4 * ML_DIM
CUMSUM_BLOCK = 256
MODS_ROWS = 8
VMEM_LIMIT = 56 * 1024 * 1024

NEG_INF = float("-inf")
HIGHEST = lax.Precision.HIGHEST


def _dot(a, b):
    return jnp.dot(a, b, preferred_element_type=F32)


def _dot_nt(a, b):
    return lax.dot_general(a, b, (((1,), (1,)), ((), ())), preferred_element_type=F32)


def _dot_split(a, b):
    a_hi = a.astype(BF16)
    a_lo = (a - a_hi.astype(F32)).astype(BF16)
    b_hi = b.astype(BF16)
    b_lo = (b - b_hi.astype(F32)).astype(BF16)
    return _dot(jnp.concatenate([a_hi, a_hi, a_lo], axis=1), jnp.concatenate([b_hi, b_lo, b_hi], axis=0))


def _rms_mod(x, g, sc, sh):
    y = x * lax.rsqrt(jnp.mean(x * x, axis=-1, keepdims=True) + EPS)
    return (y * g) * (1.0 + sc) + sh


def _mod_rows(mods_ref, row, first):
    return [mods_ref[pl.ds(row, 1), pl.ds((first + j) * D_MODEL, D_MODEL)] for j in range(3)]


def _mods_kernel(cv_ref, w_ref, b_ref, o_ref):
    cv = cv_ref[...]
    s = cv * jax.nn.sigmoid(cv)
    o_ref[...] = _dot(s.astype(BF16), w_ref[...].astype(BF16)) + b_ref[...]


def _mods_call(cvecs, w_mod, b_mod):
    tn = 1536
    n = N_MOD * D_MODEL
    return pl.pallas_call(
        _mods_kernel,
        grid=(DEPTH, n // tn),
        in_specs=[pl.BlockSpec((MODS_ROWS, D_MODEL), lambda l, j: (0, 0)),
                  pl.BlockSpec((None, D_MODEL, tn), lambda l, j: (l, 0, j)),
                  pl.BlockSpec((None, 1, tn), lambda l, j: (l, 0, j))],
        out_specs=pl.BlockSpec((None, MODS_ROWS, tn), lambda l, j: (l, 0, j)),
        out_shape=jax.ShapeDtypeStruct((DEPTH, MODS_ROWS, n), F32),
        compiler_params=pltpu.CompilerParams(vmem_limit_bytes=VMEM_LIMIT),
        name="adaln_mods",
    )(cvecs, w_mod, b_mod.reshape(DEPTH, 1, n))


def _short_conv(zc, cw):
    T = zc.shape[0]
    cb = zc[:, 0:CONV_DIM]
    u = zc[:, CONV_DIM:2 * CONV_DIM] * zc[:, 2 * CONV_DIM:3 * CONV_DIM]
    t = lax.broadcasted_iota(jnp.int32, u.shape, 0)
    u_prev = jnp.where(t == 0, 0.0, pltpu.roll(u, 1, axis=0))
    u_next = jnp.where(t == T - 1, 0.0, pltpu.roll(u, T - 1, axis=0))
    return cb * (cw[0:1, :] * u_prev + cw[1:2, :] * u + cw[2:3, :] * u_next)


def _with_ones(v):
    return jnp.concatenate([v, jnp.ones_like(v)], axis=-1)


def _softmax_attention(nq, nk, nv):
    outs = []
    for h in range(NA_HEADS):
        sl = slice(h * NA_HEAD_DIM, (h + 1) * NA_HEAD_DIM)
        q = (nq[:, sl] * NA_HEAD_DIM ** -0.5).astype(BF16)
        s = _dot_nt(q, nk[:, sl].astype(BF16))
        p = jnp.exp(s - jnp.max(s, axis=-1, keepdims=True))
        o = _dot(p.astype(BF16), nv[:, sl].astype(BF16)) / jnp.sum(p, axis=-1, keepdims=True)
        outs.append(o)
    return jnp.concatenate(outs, axis=-1)


def _log_sigmoid(x):
    return jnp.minimum(x, 0.0) - jnp.log(1.0 + jnp.exp(-jnp.abs(x)))


def _gate_terms(zi, zf):
    T = zi.shape[0]
    lf = _log_sigmoid(zf)
    blk = min(T, CUMSUM_BLOCK)
    r = lax.broadcasted_iota(jnp.int32, (blk, blk), 0)
    c = lax.broadcasted_iota(jnp.int32, (blk, blk), 1)
    tril = jnp.where(c <= r, 1.0, 0.0).astype(F32)
    parts, carry = [], None
    for r0 in range(0, T, blk):
        b = jnp.dot(tril, lf[r0:r0 + blk], precision=HIGHEST, preferred_element_type=F32)
        if carry is not None:
            b = b + carry
        carry = b[blk - 1:blk, :]
        parts.append(b)
    b_fwd = parts[0] if len(parts) == 1 else jnp.concatenate(parts, axis=0)
    b_bwd = (b_fwd[T - 1:T, :] - b_fwd) + lf
    lane = lax.broadcasted_iota(jnp.int32, zi.shape, 1)
    b_col = jnp.where(lane < ML_HEADS, b_fwd, b_bwd)
    row_t = jnp.transpose(zi - b_col)
    return b_col, row_t


def _mlstm_outputs(q, k, v, b_col, li_row, m0, c0, n0, backward, q_block):
    T = q.shape[0]
    qb, kb, vb = q.astype(BF16), k.astype(BF16), v.astype(BF16)
    c0b = None if c0 is None else c0.astype(BF16)
    outs = []
    for r0 in range(0, T, q_block):
        ks, ke = (r0, T) if backward else (0, r0 + q_block)
        bq = b_col[r0:r0 + q_block]
        d = bq + li_row[:, ks:ke]
        t_idx = r0 + lax.broadcasted_iota(jnp.int32, d.shape, 0)
        s_idx = ks + lax.broadcasted_iota(jnp.int32, d.shape, 1)
        d = jnp.where((s_idx >= t_idx) if backward else (s_idx <= t_idx), d, NEG_INF)
        inter = bq + m0
        m_t = jnp.maximum(jnp.max(d, axis=-1, keepdims=True), inter)
        w = jnp.exp(d - m_t)
        s = _dot_nt(qb[r0:r0 + q_block], kb[ks:ke]) * w
        num = _dot(s.astype(BF16), vb[ks:ke])
        den = jnp.sum(s, axis=-1, keepdims=True)
        if c0 is not None:
            a = jnp.exp(inter - m_t)
            num = num + a * _dot(qb[r0:r0 + q_block], c0b)
            den = den + a * jnp.sum(q[r0:r0 + q_block] * n0, axis=-1, keepdims=True)
        outs.append(num / jnp.maximum(jnp.abs(den), jnp.exp(-m_t)))
    return outs[0] if len(outs) == 1 else jnp.concatenate(outs, axis=0)


def _mlstm_state(k, v, b_col, li_col, backward):
    T = k.shape[0]
    b_tot = b_col[0:1] if backward else b_col[T - 1:T]
    g = (b_tot - b_col) + li_col
    m_new = jnp.maximum(b_tot, jnp.max(g, axis=0, keepdims=True))
    kw = k * jnp.exp(g - m_new)
    c_new = _dot(jnp.transpose(kw).astype(BF16), v.astype(BF16))
    n_new = jnp.sum(kw, axis=0, keepdims=True)
    return c_new, n_new, m_new


def _head_readout(h, mo, g):
    hn = h * lax.rsqrt(jnp.mean(h * h, axis=-1, keepdims=True) + EPS)
    return hn * g * jax.nn.sigmoid(mo)


def _ctx_mixer_kernel(x_ref, mods_ref, ln_ref, win_ref, wgate_ref, convw_ref, gb_ref, mlg_ref, wout_ref,
                      *rest, has_moe):
    xo_ref, ko_ref, vo_ref, co_ref, no_ref, mo_ref = rest[-6:]
    T = SEQ
    sh1, sc1, g1 = _mod_rows(mods_ref, 0, 0)
    x = x_ref[...]
    if has_moe:
        moe_ref, mods_prev_ref = rest[0], rest[1]
        x = x + _mod_rows(mods_prev_ref, 0, 3)[2] * _from_slabs(moe_ref, T)
    h = _rms_mod(x, ln_ref[...], sc1, sh1).astype(BF16)

    conv_o = _short_conv(_dot(h, win_ref[:, C_CONV:C_NA]), convw_ref[...])

    zn = _dot(h, win_ref[:, C_NA:C_ML])
    nk, nv = zn[:, NA_DIM:2 * NA_DIM], zn[:, 2 * NA_DIM:3 * NA_DIM]
    ko_ref[...] = nk
    vo_ref[...] = nv
    na_o = _softmax_attention(zn[:, 0:NA_DIM], nk, nv)

    zg = _dot(h, wgate_ref[...])
    zi = zg[:, 0:LANES] + gb_ref[:, 0:LANES]
    zf = zg[:, LANES:2 * LANES] + gb_ref[:, LANES:2 * LANES]
    b_col, row_t = _gate_terms(zi, zf)
    m0 = jnp.zeros((1, 1), F32)

    zm = _dot(h, win_ref[:, C_ML:C_IG])
    ml_parts = []
    for hh in range(ML_HEADS):
        zq, zk, zv, zo = (zm[:, p * ML_DIM + hh * ML_HEAD_DIM:p * ML_DIM + (hh + 1) * ML_HEAD_DIM]
                          for p in range(4))
        zk = zk * ML_HEAD_DIM ** -0.5
        hsum = None
        for d in range(N_DIRS):
            j = d * ML_HEADS + hh
            bc = b_col[:, j:j + 1]
            ho = _mlstm_outputs(zq, zk, zv, bc, row_t[j:j + 1, :], m0, None, None, d == 1, T)
            hsum = ho if hsum is None else hsum + ho
            c_new, n_new, m_new = _mlstm_state(zk, zv, bc, zi[:, j:j + 1], d == 1)
            co_ref[d, hh] = c_new
            no_ref[j:j + 1, :] = n_new
            mo_ref[j:j + 1, :] = jnp.broadcast_to(m_new, (1, LANES))
        ml_parts.append(_head_readout(hsum, zo, mlg_ref[:, hh * ML_HEAD_DIM:(hh + 1) * ML_HEAD_DIM]))

    mix = jnp.concatenate([conv_o, na_o] + ml_parts, axis=-1).astype(BF16)
    xo_ref[...] = x + g1 * _dot(mix, wout_ref[...])


def _ctx_mixer_call(layer, x, moe, mods, ln1, w_in, conv_w, gate_b, ml_g, w_out, prev):
    B = x.shape[0]
    T = SEQ
    G = None
    state_shapes = [
        jax.ShapeDtypeStruct((B, DEPTH, T, NA_DIM), F32),
        jax.ShapeDtypeStruct((B, DEPTH, T, NA_DIM), F32),
        jax.ShapeDtypeStruct((B, DEPTH, N_DIRS, ML_HEADS, ML_HEAD_DIM, ML_HEAD_DIM), F32),
        jax.ShapeDtypeStruct((B, DEPTH, N_GATE, ML_HEAD_DIM), F32),
        jax.ShapeDtypeStruct((B, DEPTH, N_GATE, LANES), F32),
    ]
    in_specs = [
        pl.BlockSpec((G, T, D_MODEL), lambda b: (b, 0, 0)),
        pl.BlockSpec((None, MODS_ROWS, N_MOD * D_MODEL), lambda b: (layer, 0, 0)),
        pl.BlockSpec((None, 1, D_MODEL), lambda b: (layer, 0, 0)),
        pl.BlockSpec((None, D_MODEL, C_IG), lambda b: (layer, 0, 0)),
        pl.BlockSpec((None, D_MODEL, 2 * LANES), lambda b: (layer, 0, 0)),
        pl.BlockSpec((None, 3, CONV_DIM), lambda b: (layer, 0, 0)),
        pl.BlockSpec((None, 1, 2 * LANES), lambda b: (layer, 0, 0)),
        pl.BlockSpec((None, 1, ML_DIM), lambda b: (layer, 0, 0)),
        pl.BlockSpec((None, D_MODEL, D_MODEL), lambda b: (layer, 0, 0)),
    ]
    args = [x, mods, ln1, w_in[0], w_in[1], conv_w, gate_b, ml_g, w_out]
    if moe is not None:
        per_sb = DSP_SB // T
        in_specs += [pl.BlockSpec((None, T * SLAB, LANES), lambda b: (b // per_sb, b % per_sb, 0)),
                     pl.BlockSpec((None, MODS_ROWS, N_MOD * D_MODEL), lambda b: (layer - 1, 0, 0))]
        args += [moe, mods]
    aliases = {}
    if prev is not None:
        in_specs += [pl.BlockSpec(memory_space=pl.ANY)] * len(prev)
        aliases = {len(args) + i: 1 + i for i in range(len(prev))}
        args += list(prev)
    out_specs = [
        pl.BlockSpec((G, T, D_MODEL), lambda b: (b, 0, 0)),
        pl.BlockSpec((G, None, T, NA_DIM), lambda b: (b, layer, 0, 0)),
        pl.BlockSpec((G, None, T, NA_DIM), lambda b: (b, layer, 0, 0)),
        pl.BlockSpec((G, None, N_DIRS, ML_HEADS, ML_HEAD_DIM, ML_HEAD_DIM),
                     lambda b: (b, layer, 0, 0, 0, 0)),
        pl.BlockSpec((G, None, N_GATE, ML_HEAD_DIM), lambda b: (b, layer, 0, 0)),
        pl.BlockSpec((G, None, N_GATE, LANES), lambda b: (b, layer, 0, 0)),
    ]
    outs = pl.pallas_call(
        functools.partial(_ctx_mixer_kernel, has_moe=moe is not None),
        grid=(B,),
        in_specs=in_specs,
        out_specs=out_specs,
        out_shape=[jax.ShapeDtypeStruct(x.shape, F32)] + state_shapes,
        input_output_aliases=aliases,
        compiler_params=pltpu.CompilerParams(vmem_limit_bytes=VMEM_LIMIT),
        name="ctx_mixer",
    )(*args)
    return outs[0], tuple(outs[1:])


def _route(lg):
    lane = lax.broadcasted_iota(jnp.int32, lg.shape, 1)
    lane_f = lane.astype(F32)
    big = float(LANES)
    is_g = (lane >= N_EXPERTS) & (lane < N_EXPERTS + N_GROUPS)
    glm = jnp.where(is_g, lg, NEG_INF)
    gmax = jnp.max(glm, axis=-1, keepdims=True)
    g_top = jnp.min(jnp.where(glm == gmax, lane_f, big), axis=-1, keepdims=True) - N_EXPERTS
    gp = 1.0 / jnp.sum(jnp.where(is_g, jnp.exp(lg - gmax), 0.0), axis=-1, keepdims=True)
    grp = (lane >> (EXPERTS_PER_GROUP.bit_length() - 1)).astype(F32)
    in_grp = (lane < N_EXPERTS) & (grp == g_top)
    e1 = jnp.where(in_grp, lg, NEG_INF)
    v1 = jnp.max(e1, axis=-1, keepdims=True)
    i1 = jnp.min(jnp.where(e1 == v1, lane_f, big), axis=-1, keepdims=True)
    e2 = jnp.where(in_grp & (lane_f != i1), lg, NEG_INF)
    v2 = jnp.max(e2, axis=-1, keepdims=True)
    i2 = jnp.min(jnp.where(e2 == v2, lane_f, big), axis=-1, keepdims=True)
    t = jnp.exp(v2 - v1)
    w1 = 1.0 / (1.0 + t)
    w2 = t / (1.0 + t)
    return i1, i2, gp * w1, gp * w2


SLAB = D_MODEL // LANES
DSP_SB = 2048
DSP_TM = 160
DSP_CH = 512
DSP_PMAX = 2 * DSP_SB + N_EXPERTS * DSP_TM
DSP_NULL = DSP_PMAX + 2 * DSP_TM
DSP_SLOTS = DSP_NULL + 2 * DSP_TM
DSP_PAD_PAIR = 2 * DSP_SB
DSP_W_LEN = 2 * DSP_SB + LANES
DSP_UNROLL = 8


def _to_slabs(ref, x):
    n = x.shape[0]
    for s in range(SLAB):
        ref[pl.ds(s, n, stride=SLAB), :] = x[:, s * LANES:(s + 1) * LANES]


def _from_slabs(ref, n):
    return jnp.concatenate([ref[pl.ds(s, n, stride=SLAB), :] for s in range(SLAB)], axis=-1)


def _dsp_mods(mods_ref, chunk, chunks_per_row, first):
    row = 0 if chunks_per_row is None else 1 + chunk // chunks_per_row
    return _mod_rows(mods_ref, row, first)


def _dsp_route_kernel(x_ref, mods_ref, ln_ref, wr_ref, br_ref, h_ref, r_ref, cnt_ref, run_scr, *,
                      chunks_per_row):
    c = pl.program_id(0)
    ch = DSP_CH

    @pl.when(c % (DSP_SB // ch) == 0)
    def _():
        run_scr[...] = jnp.zeros_like(run_scr)

    sh2, sc2, _ = _dsp_mods(mods_ref, c, chunks_per_row, 3)
    h2 = _rms_mod(x_ref[...], ln_ref[...], sc2, sh2)
    _to_slabs(h_ref, h2)
    lg = _dot_split(h2, wr_ref[...]) + br_ref[...]
    i1, i2, w1, w2 = _route(lg)
    lane = lax.broadcasted_iota(jnp.int32, lg.shape, 1)
    lane_f = lane.astype(F32)
    oh1 = jnp.where(lane_f == i1, 1.0, 0.0)
    oh2 = jnp.where(lane_f == i2, 1.0, 0.0)
    both = oh1 + oh2
    r = lax.broadcasted_iota(jnp.int32, (ch, ch), 0)
    s = lax.broadcasted_iota(jnp.int32, (ch, ch), 1)
    earlier = jnp.where(s < r, 1.0, 0.0).astype(BF16)
    run = run_scr[0:1, :]
    before = _dot(earlier, both.astype(BF16)) + run
    rank1 = jnp.sum(before * oh1, axis=-1, keepdims=True)
    rank2 = jnp.sum(before * oh2, axis=-1, keepdims=True)
    run = run + jnp.sum(both, axis=0, keepdims=True)
    run_scr[0:1, :] = run
    cnt_ref[...] = jnp.broadcast_to(run, cnt_ref.shape)
    cols = (i1, i2, w1, w2, rank1, rank2)
    out = jnp.zeros(lg.shape, F32)
    for j, col in enumerate(cols):
        out = jnp.where(lane == j, col, out)
    r_ref[...] = out


def _dsp_route_call(layer, x2d, mods, ln2, w_router, b_router, chunks_per_row):
    n = x2d.shape[0]
    ch = DSP_CH
    per_sb = DSP_SB // ch
    return pl.pallas_call(
        functools.partial(_dsp_route_kernel, chunks_per_row=chunks_per_row),
        grid=(n // ch,),
        in_specs=[
            pl.BlockSpec((ch, D_MODEL), lambda c: (c, 0)),
            pl.BlockSpec((None, MODS_ROWS, N_MOD * D_MODEL), lambda c: (layer, 0, 0)),
            pl.BlockSpec((None, 1, D_MODEL), lambda c: (layer, 0, 0)),
            pl.BlockSpec((None, D_MODEL, LANES), lambda c: (layer, 0, 0)),
            pl.BlockSpec((None, 1, LANES), lambda c: (layer, 0, 0)),
        ],
        out_specs=[pl.BlockSpec((ch * SLAB, LANES), lambda c: (c, 0)),
                   pl.BlockSpec((ch, LANES), lambda c: (c, 0)),
                   pl.BlockSpec((None, SLAB, LANES), lambda c: (c // per_sb, 0, 0))],
        out_shape=[jax.ShapeDtypeStruct((n * SLAB, LANES), F32),
                   jax.ShapeDtypeStruct((n, LANES), F32),
                   jax.ShapeDtypeStruct((n // DSP_SB, SLAB, LANES), F32)],
        scratch_shapes=[pltpu.VMEM((SLAB, LANES), F32)],
        compiler_params=pltpu.CompilerParams(vmem_limit_bytes=VMEM_LIMIT),
        name="moe_route",
    )(x2d, mods, ln2, w_router, b_router)


def _dispatch_tables(route, counts, n_sb):
    e = route[:, 0:2].astype(jnp.int32).reshape(n_sb, 2 * DSP_SB)
    rank = route[:, 4:6].astype(jnp.int32).reshape(n_sb, 2 * DSP_SB)
    w = _pad_lanes(route[:, 2:4].reshape(n_sb, 1, 2 * DSP_SB), DSP_W_LEN)
    cnt = counts[:, 0, :N_EXPERTS].astype(jnp.int32)
    ntile = (cnt + (DSP_TM - 1)) // DSP_TM
    off = (jnp.cumsum(ntile, axis=1) - ntile) * DSP_TM
    is_e = e[:, None, :] == jnp.arange(N_EXPERTS, dtype=jnp.int32)[None, :, None]
    pos = (jnp.sum(jnp.where(is_e, off[:, :, None], 0), axis=1) + rank).reshape(n_sb, 1, 2 * DSP_SB)
    return pos, w, ntile.reshape(-1), off.reshape(-1), cnt.reshape(-1)


def _dsp_expert_kernel(ntile_ref, off_ref, cnt_ref, pos_ref, w_ref, h_ref, wg_ref, wu_ref, wd_ref,
                       o_ref, pair_smem, sched_smem, pend_smem, xs0, xs1, ys0, ys1,
                       wg_scr, wu_scr, wd_scr):
    sb = pl.program_id(0)
    e = pl.program_id(1)
    tm, u = DSP_TM, DSP_UNROLL
    xs_scr, ys_scr = (xs0, xs1), (ys0, ys1)
    slab_shift = SLAB.bit_length() - 1
    last_row = (DSP_SB - 1) * SLAB

    def slab_row(pair):
        return lax.shift_left(lax.shift_right_logical(pair, 1), slab_shift)

    @pl.when(e == 0)
    def _():
        for ee in range(N_EXPERTS):
            q = sb * N_EXPERTS + ee

            def pad(p, carry):
                pair_smem[p] = DSP_PAD_PAIR
                return carry

            lax.fori_loop(off_ref[q] + cnt_ref[q], off_ref[q] + ntile_ref[q] * tm, pad, 0)

        def invert(j, carry):
            for k in range(u):
                a = j * u + k
                pair_smem[pos_ref[0, a]] = a
            return carry

        lax.fori_loop(0, 2 * DSP_SB // u, invert, 0)

        last_q = sb * N_EXPERTS + (N_EXPERTS - 1)
        used_end = off_ref[last_q] + ntile_ref[last_q] * tm

        def pad_end(p, carry):
            pair_smem[used_end + p] = DSP_PAD_PAIR
            pair_smem[DSP_NULL + p] = DSP_PAD_PAIR
            return carry

        lax.fori_loop(0, 2 * tm, pad_end, 0)
        sched_smem[0] = -1
        sched_smem[1] = DSP_NULL
        pend_smem[0] = 0.0
        o_ref[...] = jnp.zeros_like(o_ref)
        for ys in ys_scr:
            ys[...] = jnp.zeros_like(ys)

    q = sb * N_EXPERTS + e
    base0 = off_ref[q]
    nt = ntile_ref[q]
    wg_scr[...] = wg_ref[...].astype(BF16)
    wu_scr[...] = wu_ref[...].astype(BF16)
    wd_scr[...] = wd_ref[...].astype(BF16)

    def gather(base, buf):
        for r in range(tm):
            src = jnp.minimum(slab_row(pair_smem[base + r]), last_row)
            xs_scr[buf][r * SLAB:(r + 1) * SLAB, :] = h_ref[pl.ds(pl.multiple_of(src, SLAB), SLAB), :]

    def compute(buf):
        x = _from_slabs(xs_scr[buf], tm).astype(BF16)
        a = jax.nn.silu(_dot(x, wg_scr[...])) * _dot(x, wu_scr[...])
        _to_slabs(ys_scr[buf], _dot(a.astype(BF16), wd_scr[...]))

    def scatter(base, buf, scale):
        for j in range(tm // u):
            upd = []
            for k in range(u):
                r = j * u + k
                pair = pair_smem[base + r]
                dst = pl.multiple_of(slab_row(pair), SLAB)
                y = ys_scr[buf][r * SLAB:(r + 1) * SLAB, :]
                upd.append((dst, o_ref[pl.ds(dst, SLAB), :] + (w_ref[0, pair] * scale) * y))
            for dst, v in upd:
                o_ref[pl.ds(dst, SLAB), :] = v

    n_pairs = lax.shift_right_logical(nt + 1, 1)

    @pl.when((n_pairs > 0) & (sched_smem[0] != base0))
    def _():
        gather(base0, 0)

    pend_base = sched_smem[1]
    pend_scale = pend_smem[0]

    def pair(k, carry):
        b = base0 + 2 * k * tm
        first = k == 0
        gather(b + tm, 1)
        compute(0)
        scatter(jnp.where(first, pend_base, b - tm), 1, jnp.where(first, pend_scale, 1.0))
        gather(b + 2 * tm, 0)
        compute(1)
        scatter(b, 0, 1.0)
        return carry

    lax.fori_loop(0, n_pairs, pair, 0)

    @pl.when(n_pairs > 0)
    def _():
        last = 2 * n_pairs - 1
        sched_smem[0] = base0 + (last + 1) * tm
        sched_smem[1] = base0 + last * tm
        pend_smem[0] = jnp.where(last < nt, 1.0, 0.0)

    @pl.when(e == N_EXPERTS - 1)
    def _():
        scatter(sched_smem[1], 1, pend_smem[0])


def _dsp_expert_call(layer, h_slabs, pos, w, ntile, off, cnt, w_gate, w_up, w_down):
    n_sb = pos.shape[0]
    out_rows = (DSP_SB + DSP_CH) * SLAB

    def per_sb(s, e, *prefetch):
        return (s, 0, 0)

    def w_idx(s, e, *prefetch):
        return (layer, e, 0, 0)

    grid_spec = pltpu.PrefetchScalarGridSpec(
        num_scalar_prefetch=3,
        grid=(n_sb, N_EXPERTS),
        in_specs=[
            pl.BlockSpec((None, 1, 2 * DSP_SB), per_sb, memory_space=pltpu.SMEM),
            pl.BlockSpec((None, 1, DSP_W_LEN), per_sb, memory_space=pltpu.SMEM),
            pl.BlockSpec((DSP_SB * SLAB, LANES), lambda s, e, *prefetch: (s, 0)),
            pl.BlockSpec((None, None, D_MODEL, EXPERT_FF), w_idx),
            pl.BlockSpec((None, None, D_MODEL, EXPERT_FF), w_idx),
            pl.BlockSpec((None, None, EXPERT_FF, D_MODEL), w_idx),
        ],
        out_specs=pl.BlockSpec((None, out_rows, LANES), per_sb),
        scratch_shapes=[pltpu.SMEM((DSP_SLOTS,), jnp.int32),
                        pltpu.SMEM((2,), jnp.int32),
                        pltpu.SMEM((1,), F32),
                        pltpu.VMEM((DSP_TM * SLAB, LANES), F32),
                        pltpu.VMEM((DSP_TM * SLAB, LANES), F32),
                        pltpu.VMEM((DSP_TM * SLAB, LANES), F32),
                        pltpu.VMEM((DSP_TM * SLAB, LANES), F32),
                        pltpu.VMEM((D_MODEL, EXPERT_FF), BF16),
                        pltpu.VMEM((D_MODEL, EXPERT_FF), BF16),
                        pltpu.VMEM((EXPERT_FF, D_MODEL), BF16)],
    )
    return pl.pallas_call(
        _dsp_expert_kernel,
        grid_spec=grid_spec,
        out_shape=jax.ShapeDtypeStruct((n_sb, out_rows, LANES), F32),
        compiler_params=pltpu.CompilerParams(vmem_limit_bytes=VMEM_LIMIT),
        name="moe_experts",
    )(ntile, off, cnt, pos, w, h_slabs, w_gate, w_up, w_down)


def _dsp_final_kernel(x_ref, m_ref, mods_ref, fg_ref, o_ref, *, chunks_per_row):
    g2 = _dsp_mods(mods_ref, pl.program_id(0), chunks_per_row, 3)[2]
    y = x_ref[...] + g2 * _from_slabs(m_ref, DSP_CH)
    o_ref[...] = y * lax.rsqrt(jnp.mean(y * y, axis=-1, keepdims=True) + EPS) * fg_ref[...]


def _dsp_final_call(layer, x2d, m_slabs, mods, final_g, chunks_per_row):
    n = x2d.shape[0]
    ch = DSP_CH
    per_sb = DSP_SB // ch
    return pl.pallas_call(
        functools.partial(_dsp_final_kernel, chunks_per_row=chunks_per_row),
        grid=(n // ch,),
        in_specs=[
            pl.BlockSpec((ch, D_MODEL), lambda c: (c, 0)),
            pl.BlockSpec((None, ch * SLAB, LANES), lambda c: (c // per_sb, c % per_sb, 0)),
            pl.BlockSpec((None, MODS_ROWS, N_MOD * D_MODEL), lambda c: (layer, 0, 0)),
            pl.BlockSpec((1, D_MODEL), lambda c: (0, 0)),
        ],
        out_specs=pl.BlockSpec((ch, D_MODEL), lambda c: (c, 0)),
        out_shape=jax.ShapeDtypeStruct(x2d.shape, F32),
        compiler_params=pltpu.CompilerParams(vmem_limit_bytes=VMEM_LIMIT),
        name="final_norm",
    )(x2d, m_slabs, mods, final_g)


def _sparse_moe(layer, x2d, mods, ln2, w_router, b_router, w_gate, w_up, w_down, chunks_per_row):
    n_sb = x2d.shape[0] // DSP_SB
    h_slabs, route, counts = _dsp_route_call(layer, x2d, mods, ln2, w_router, b_router,
                                             chunks_per_row)
    pos, w, ntile, off, cnt = _dispatch_tables(route, counts, n_sb)
    return _dsp_expert_call(layer, h_slabs, pos, w, ntile, off, cnt, w_gate, w_up, w_down)


LAT_PROJ_ROWS = 512


def _lat_proj_kernel(x_ref, mods_ref, ln_ref, win_ref, wgate_ref, gb_ref, *rest, has_moe):
    zc_ref, zn_ref, zm_ref, zg_ref = rest[-4:]
    row = 1 + pl.program_id(0) // (DEC_SEQ // LAT_PROJ_ROWS)
    sh1, sc1, _ = _mod_rows(mods_ref, row, 0)
    x = x_ref[...]
    if has_moe:
        moe_ref, mods_prev_ref, xo_ref = rest[0], rest[1], rest[2]
        x = x + _mod_rows(mods_prev_ref, row, 3)[2] * _from_slabs(moe_ref, LAT_PROJ_ROWS)
        xo_ref[...] = x
    h = _rms_mod(x, ln_ref[...], sc1, sh1).astype(BF16)
    zc_ref[...] = _dot(h, win_ref[:, C_CONV:C_NA])
    zn_ref[...] = _dot(h, win_ref[:, C_NA:C_ML])
    zm_ref[...] = _dot(h, win_ref[:, C_ML:C_IG])
    zg_ref[...] = _dot(h, wgate_ref[...]) + gb_ref[...]


def _lat_proj_call(layer, x2d, moe, mods, ln1, w_in, gate_b):
    n = x2d.shape[0]
    tb = LAT_PROJ_ROWS
    widths = (C_NA - C_CONV, C_ML - C_NA, C_IG - C_ML, 2 * LANES)
    in_specs = [
        pl.BlockSpec((tb, D_MODEL), lambda i: (i, 0)),
        pl.BlockSpec((None, MODS_ROWS, N_MOD * D_MODEL), lambda i: (layer, 0, 0)),
        pl.BlockSpec((None, 1, D_MODEL), lambda i: (layer, 0, 0)),
        pl.BlockSpec((None, D_MODEL, C_IG), lambda i: (layer, 0, 0)),
        pl.BlockSpec((None, D_MODEL, 2 * LANES), lambda i: (layer, 0, 0)),
        pl.BlockSpec((None, 1, 2 * LANES), lambda i: (layer, 0, 0)),
    ]
    args = [x2d, mods, ln1, w_in[0], w_in[1], gate_b]
    out_specs = [pl.BlockSpec((tb, w), lambda i: (i, 0)) for w in widths]
    out_shape = [jax.ShapeDtypeStruct((n, w), F32) for w in widths]
    if moe is not None:
        per_sb = DSP_SB // tb
        in_specs += [pl.BlockSpec((None, tb * SLAB, LANES), lambda i: (i // per_sb, i % per_sb, 0)),
                     pl.BlockSpec((None, MODS_ROWS, N_MOD * D_MODEL), lambda i: (layer - 1, 0, 0))]
        args += [moe, mods]
        out_specs = [pl.BlockSpec((tb, D_MODEL), lambda i: (i, 0))] + out_specs
        out_shape = [jax.ShapeDtypeStruct((n, D_MODEL), F32)] + out_shape
    outs = pl.pallas_call(
        functools.partial(_lat_proj_kernel, has_moe=moe is not None),
        grid=(n // tb,),
        in_specs=in_specs,
        out_specs=out_specs,
        out_shape=out_shape,
        compiler_params=pltpu.CompilerParams(vmem_limit_bytes=VMEM_LIMIT),
        name="lat_proj",
    )(*args)
    return tuple(outs) if moe is not None else (x2d,) + tuple(outs)


def _na_row_start(r):
    rows = DEC_SEQ // GRID_W
    return min(max(r - NA_WIN_ROWS // 2, 0), rows - NA_WIN_ROWS)


def _na_row_groups():
    rows = DEC_SEQ // GRID_W
    groups, lo = [], 0
    for r in range(1, rows + 1):
        if r == rows or _na_row_start(r) != _na_row_start(lo):
            groups.append((lo, r))
            lo = r
    return tuple(groups)


_NA_ROW_GROUPS = _na_row_groups()


def _lat_local_kernel(zc_ref, zn_ref, kc_ref, vc_ref, band_ref, convw_ref, o_ref):
    o_ref[:, 0:CONV_DIM] = _short_conv(zc_ref[...], convw_ref[...])
    blk = NA_WIN_ROWS * GRID_W
    q_all = (zn_ref[:, 0:NA_DIM] * NA_HEAD_DIM ** -0.5).astype(BF16)
    k_all = zn_ref[:, NA_DIM:2 * NA_DIM].astype(BF16)
    v_all = zn_ref[:, 2 * NA_DIM:3 * NA_DIM].astype(BF16)
    kc_all = kc_ref[...].astype(BF16)
    vc_all = vc_ref[...].astype(BF16)
    heads = []
    for h in range(NA_HEADS):
        sl = slice(h * NA_HEAD_DIM, (h + 1) * NA_HEAD_DIM)
        q, k, v, kc, vc = q_all[:, sl], k_all[:, sl], v_all[:, sl], kc_all[:, sl], vc_all[:, sl]
        band = band_ref[h]
        s_ctx = _dot_nt(q, kc)
        m_ctx = jnp.max(s_ctx, axis=-1, keepdims=True)
        v1 = _with_ones(v)
        m_rows, o_rows = [], []
        for r_lo, r_hi in _NA_ROW_GROUPS:
            start = _na_row_start(r_lo)
            bias = [band[:, (start - r + NA_WIN_ROWS - 1) * GRID_W:(start - r + NA_WIN_ROWS - 1) * GRID_W + blk]
                    for r in range(r_lo, r_hi)]
            bias = bias[0] if len(bias) == 1 else jnp.concatenate(bias, axis=0)
            q_g = q[r_lo * GRID_W:r_hi * GRID_W]
            s_loc = _dot_nt(q_g, k[start * GRID_W:start * GRID_W + blk]) + bias
            m = jnp.maximum(jnp.max(s_loc, axis=-1, keepdims=True), m_ctx[r_lo * GRID_W:r_hi * GRID_W])
            p_loc = jnp.exp(s_loc - m)
            m_rows.append(m)
            o_rows.append(_dot(p_loc.astype(BF16), v1[start * GRID_W:start * GRID_W + blk]))
        p_ctx = jnp.exp(s_ctx - jnp.concatenate(m_rows, axis=0))
        o = jnp.concatenate(o_rows, axis=0) + _dot(p_ctx.astype(BF16), _with_ones(vc))
        heads.append(o[:, 0:NA_HEAD_DIM] / o[:, NA_HEAD_DIM:NA_HEAD_DIM + 1])
    o_ref[:, CONV_DIM:CONV_DIM + NA_DIM] = jnp.concatenate(heads, axis=-1)


def _lat_local_call(layer, zc, zn, cache_k, cache_v, band, conv_w):
    nb = zc.shape[0] // DEC_SEQ
    T = DEC_SEQ
    return pl.pallas_call(
        _lat_local_kernel,
        grid=(nb,),
        in_specs=[
            pl.BlockSpec((T, 3 * CONV_DIM), lambda b: (b, 0)),
            pl.BlockSpec((T, 3 * NA_DIM), lambda b: (b, 0)),
            pl.BlockSpec((None, None, PAST_LEN, NA_DIM), lambda b: (b, layer, 0, 0)),
            pl.BlockSpec((None, None, PAST_LEN, NA_DIM), lambda b: (b, layer, 0, 0)),
            pl.BlockSpec((None, NA_HEADS, GRID_W, band.shape[-1]), lambda b: (layer, 0, 0, 0)),
            pl.BlockSpec((None, 3, CONV_DIM), lambda b: (layer, 0, 0)),
        ],
        out_specs=pl.BlockSpec((T, CONV_DIM + NA_DIM), lambda b: (b, 0)),
        out_shape=jax.ShapeDtypeStruct((zc.shape[0], CONV_DIM + NA_DIM), F32),
        compiler_params=pltpu.CompilerParams(vmem_limit_bytes=VMEM_LIMIT),
        name="lat_conv_na",
    )(zc, zn, cache_k, cache_v, band, conv_w)


def _rope(x, cos, sin_signed):
    w = x.shape[-1]
    half = NA_HEAD_DIM // 2
    lane = lax.broadcasted_iota(jnp.int32, x.shape, 1)
    partner = jnp.where(lane % (2 * half) < half, pltpu.roll(x, w - half, axis=1), pltpu.roll(x, half, axis=1))
    return x * cos + partner * sin_signed


def _lat_mlstm_kernel(q_ref, k_ref, v_ref, o_ref, zg_ref, c0_ref, n0_ref, m0_ref, cos_ref, sin_ref,
                      mlg_ref, out_ref, bcol_scr, rowt_scr):
    hh = pl.program_id(1)

    @pl.when(hh == 0)
    def _():
        b_all, r_all = _gate_terms(zg_ref[:, 0:LANES], zg_ref[:, LANES:2 * LANES])
        bcol_scr[...] = b_all
        rowt_scr[...] = r_all

    b_col = bcol_scr[...]
    cos, sin_signed = cos_ref[...], sin_ref[...]
    q = _rope(q_ref[...], cos, sin_signed)
    k = _rope(k_ref[...], cos, sin_signed) * ML_HEAD_DIM ** -0.5
    v = v_ref[...]
    lane = lax.broadcasted_iota(jnp.int32, (DEC_SEQ, LANES), 1)
    hsum = None
    for d in range(N_DIRS):
        j = d * ML_HEADS + hh
        bc = jnp.sum(jnp.where(lane == j, b_col, 0.0), axis=-1, keepdims=True)
        lr = rowt_scr[pl.ds(j, 1), :]
        m0 = m0_ref[d, :, 0:1]
        ho = _mlstm_outputs(q, k, v, bc, lr, m0, c0_ref[d], n0_ref[d], d == 1, 256)
        hsum = ho if hsum is None else hsum + ho
    out_ref[...] = _head_readout(hsum, o_ref[...], mlg_ref[...])


def _lat_mlstm_call(layer, zm, zg, state_c, state_n, state_m, cos, sin_signed, ml_g):
    nb = zm.shape[0] // DEC_SEQ
    T = DEC_SEQ
    hd = ML_HEAD_DIM

    def col(part):
        return pl.BlockSpec((T, hd), lambda b, h: (b, part * ML_HEADS + h))

    return pl.pallas_call(
        _lat_mlstm_kernel,
        grid=(nb, ML_HEADS),
        in_specs=[
            col(0), col(1), col(2), col(3),
            pl.BlockSpec((T, 2 * LANES), lambda b, h: (b, 0)),
            pl.BlockSpec((None, None, N_DIRS, None, hd, hd), lambda b, h: (b, layer, 0, h, 0, 0)),
            pl.BlockSpec((None, None, N_DIRS, None, 1, hd), lambda b, h: (b, layer, 0, h, 0, 0)),
            pl.BlockSpec((None, None, N_DIRS, None, 1, LANES), lambda b, h: (b, layer, 0, h, 0, 0)),
            pl.BlockSpec((T, hd), lambda b, h: (0, 0)),
            pl.BlockSpec((T, hd), lambda b, h: (0, 0)),
            pl.BlockSpec((None, 1, hd), lambda b, h: (layer, 0, h)),
        ],
        out_specs=pl.BlockSpec((T, hd), lambda b, h: (b, h)),
        out_shape=jax.ShapeDtypeStruct((zm.shape[0], ML_DIM), F32),
        scratch_shapes=[pltpu.VMEM((T, LANES), F32), pltpu.VMEM((LANES, T), F32)],
        compiler_params=pltpu.CompilerParams(vmem_limit_bytes=VMEM_LIMIT),
        name="lat_mlstm",
    )(zm, zm, zm, zm, zg, state_c, state_n, state_m, cos, sin_signed, ml_g)


def _lat_merge_kernel(x_ref, loc_ref, ml_ref, mods_ref, wout_ref, o_ref):
    row = 1 + pl.program_id(0) // (DEC_SEQ // LAT_PROJ_ROWS)
    g1 = _mod_rows(mods_ref, row, 0)[2]
    split = CONV_DIM + NA_DIM
    y = _dot(loc_ref[...].astype(BF16), wout_ref[0:split, :]) + _dot(ml_ref[...].astype(BF16), wout_ref[split:, :])
    o_ref[...] = x_ref[...] + g1 * y


def _lat_merge_call(layer, x2d, loc, ml, mods, w_out):
    n = x2d.shape[0]
    tb = LAT_PROJ_ROWS
    return pl.pallas_call(
        _lat_merge_kernel,
        grid=(n // tb,),
        in_specs=[
            pl.BlockSpec((tb, D_MODEL), lambda i: (i, 0)),
            pl.BlockSpec((tb, CONV_DIM + NA_DIM), lambda i: (i, 0)),
            pl.BlockSpec((tb, ML_DIM), lambda i: (i, 0)),
            pl.BlockSpec((None, MODS_ROWS, N_MOD * D_MODEL), lambda i: (layer, 0, 0)),
            pl.BlockSpec((None, D_MODEL, D_MODEL), lambda i: (layer, 0, 0)),
        ],
        out_specs=pl.BlockSpec((tb, D_MODEL), lambda i: (i, 0)),
        out_shape=jax.ShapeDtypeStruct(x2d.shape, F32),
        compiler_params=pltpu.CompilerParams(vmem_limit_bytes=VMEM_LIMIT),
        name="lat_merge",
    )(x2d, loc, ml, mods, w_out)


def _pad_lanes(a, width):
    return jnp.pad(a, [(0, 0)] * (a.ndim - 1) + [(0, width - a.shape[-1])])


def _pack_w_in(w_in):
    w_b = w_in.astype(BF16)
    gates = w_b[..., C_IG:]
    gates = jnp.concatenate([_pad_lanes(gates[..., :N_GATE], LANES),
                             _pad_lanes(gates[..., N_GATE:], LANES)], axis=-1)
    return w_b, gates


def _pack_gate_bias(ml_gate_b):
    gb = ml_gate_b.reshape(DEPTH, 2, N_GATE).astype(F32)
    return jnp.concatenate([_pad_lanes(gb[:, 0], LANES), _pad_lanes(gb[:, 1], LANES)], axis=-1)[:, None, :]


def _rpb_band(na_rpb):
    cols = np.arange(GRID_W)
    col_idx = np.clip(cols[None, :] - cols[:, None] + NA_WIN_COLS - 1, 0, 2 * NA_WIN_COLS - 2)
    col_start = np.clip(cols - NA_WIN_COLS // 2, 0, GRID_W - NA_WIN_COLS)
    col_mask = (cols[None, :] >= col_start[:, None]) & (cols[None, :] < col_start[:, None] + NA_WIN_COLS)
    pick = (col_idx[None] == np.arange(2 * NA_WIN_COLS - 1)[:, None, None]).astype(np.float32)
    t = jnp.einsum('lhri,iqk->lhqrk', na_rpb.astype(F32), jnp.asarray(pick), precision=HIGHEST)
    t = jnp.where(col_mask[None, None, :, None, :], t, NEG_INF)
    t = t.reshape(DEPTH, NA_HEADS, GRID_W, (2 * NA_WIN_ROWS - 1) * GRID_W)
    return _pad_lanes(t, 2 * NA_WIN_ROWS * GRID_W)


def _rope_tables():
    t = np.arange(DEC_SEQ)
    pos = np.stack([t // GRID_W, t % GRID_W], axis=-1).astype(np.float32)
    nf = ML_HEAD_DIM // 4
    inv = jnp.asarray(ROPE_BASE, F32) ** (-jnp.arange(nf, dtype=F32) / nf)
    ang = jnp.asarray(pos)[:, :, None] * inv
    cos = jnp.cos(ang)
    sin = jnp.sin(ang)
    cos_t = jnp.concatenate([cos, cos], axis=-1).reshape(DEC_SEQ, ML_HEAD_DIM)
    sin_t = jnp.concatenate([-sin, sin], axis=-1).reshape(DEC_SEQ, ML_HEAD_DIM)
    return cos_t, sin_t


def kernel(x_prompt, x_sample, cache_k, cache_v, state_C, state_n, state_m, c, c_ctx, w_mod, b_mod,
           ln1_g, w_in, conv_w, na_rpb, ml_gate_b, ml_norm_g, w_out, ln2_g, w_rg, b_rg, w_re, b_re,
           w_gate, w_up, w_down, final_g):
    nb_ctx = x_prompt.shape[0]
    nb_lat = x_sample.shape[0]
    assert 1 + nb_lat <= MODS_ROWS

    cvecs = jnp.concatenate([c_ctx[None, :], c,
                             jnp.zeros((MODS_ROWS - 1 - nb_lat, D_MODEL), F32)], axis=0)
    mods = _mods_call(cvecs, w_mod, b_mod)

    w_in_p = _pack_w_in(w_in)
    w_out_b = w_out.astype(BF16)
    gate_b = _pack_gate_bias(ml_gate_b)
    ln1 = ln1_g.reshape(DEPTH, 1, D_MODEL)
    ln2 = ln2_g.reshape(DEPTH, 1, D_MODEL)
    ml_g = ml_norm_g.reshape(DEPTH, 1, ML_DIM)
    w_router = _pad_lanes(jnp.concatenate([w_re, w_rg], axis=-1), LANES)
    b_router = _pad_lanes(jnp.concatenate([b_re, b_rg], axis=-1), LANES)[:, None, :]
    fg = final_g.reshape(1, D_MODEL)
    band = _rpb_band(na_rpb)
    cos_t, sin_t = _rope_tables()
    ck = cache_k.reshape(nb_lat, DEPTH, PAST_LEN, NA_DIM)
    cv = cache_v.reshape(nb_lat, DEPTH, PAST_LEN, NA_DIM)
    st_n = state_n.reshape(nb_lat, DEPTH, N_DIRS, ML_HEADS, 1, ML_HEAD_DIM)
    st_m = jnp.broadcast_to(state_m[..., None, None], (nb_lat, DEPTH, N_DIRS, ML_HEADS, 1, LANES))

    xp = x_prompt
    xs = x_sample.reshape(nb_lat * DEC_SEQ, D_MODEL)
    states = None
    lat_chunks = DEC_SEQ // DSP_CH
    moe_p = moe_s = None
    for l in range(DEPTH):
        xp, states = _ctx_mixer_call(l, xp, moe_p, mods, ln1, w_in_p, conv_w, gate_b, ml_g, w_out_b,
                                     states)
        moe_p = _sparse_moe(l, xp.reshape(nb_ctx * SEQ, D_MODEL), mods, ln2, w_router, b_router,
                            w_gate, w_up, w_down, None)

        xs, zc, zn, zm, zg = _lat_proj_call(l, xs, moe_s, mods, ln1, w_in_p, gate_b)
        loc = _lat_local_call(l, zc, zn, ck, cv, band, conv_w)
        ml = _lat_mlstm_call(l, zm, zg, state_C, st_n, st_m, cos_t, sin_t, ml_g)
        xs = _lat_merge_call(l, xs, loc, ml, mods, w_out_b)
        moe_s = _sparse_moe(l, xs, mods, ln2, w_router, b_router, w_gate, w_up, w_down, lat_chunks)

    xp = _dsp_final_call(DEPTH - 1, xp.reshape(nb_ctx * SEQ, D_MODEL), moe_p, mods, fg, None)
    xs = _dsp_final_call(DEPTH - 1, xs, moe_s, mods, fg, lat_chunks)
    xp = xp.reshape(nb_ctx, SEQ, D_MODEL)

    new_k, new_v, new_c, new_n, new_m = states
    return (xp, xs.reshape(nb_lat, DEC_SEQ, D_MODEL),
            new_k.reshape(nb_ctx, DEPTH, SEQ, NA_HEADS, NA_HEAD_DIM),
            new_v.reshape(nb_ctx, DEPTH, SEQ, NA_HEADS, NA_HEAD_DIM),
            new_c,
            new_n.reshape(nb_ctx, DEPTH, N_DIRS, ML_HEADS, ML_HEAD_DIM),
            new_m[..., 0].reshape(nb_ctx, DEPTH, N_DIRS, ML_HEADS))
```

```python
import functools

import numpy as np
import jax
import jax.numpy as jnp
from jax import lax
from jax.experimental import pallas as pl
from jax.experimental.pallas import tpu as pltpu

F32 = jnp.float32
BF16 = jnp.bfloat16

D_MODEL = 1024
SEQ = 256
DEPTH = 4
DEC_SEQ = 1024
PAST_LEN = 512
GRID_W = 64
CONV_DIM = 256
NA_HEADS = 4
NA_HEAD_DIM = 64
NA_DIM = NA_HEADS * NA_HEAD_DIM
NA_WIN_ROWS = 8
NA_WIN_COLS = 16
ML_HEADS = 4
ML_HEAD_DIM = 128
ML_DIM = ML_HEADS * ML_HEAD_DIM
N_DIRS = 2
N_GROUPS = 4
EXPERTS_PER_GROUP = 4
N_EXPERTS = N_GROUPS * EXPERTS_PER_GROUP
EXPERT_FF = 512
ROPE_BASE = 10000.0
EPS = 1e-6
N_MOD = 6

LANES = 128
N_GATE = N_DIRS * ML_HEADS
C_CONV = 0
C_NA = 3 * CONV_DIM
C_ML = C_NA + 3 * NA_DIM
C_IG = C_ML + 4 * ML_DIM
CUMSUM_BLOCK = 256
MODS_ROWS = 8
VMEM_LIMIT = 56 * 1024 * 1024

NEG_INF = float("-inf")
HIGHEST = lax.Precision.HIGHEST


def _dot(a, b):
    return jnp.dot(a, b, preferred_element_type=F32)


def _dot_nt(a, b):
    return lax.dot_general(a, b, (((1,), (1,)), ((), ())), preferred_element_type=F32)


def _dot_split(a, b):
    a_hi = a.astype(BF16)
    a_lo = (a - a_hi.astype(F32)).astype(BF16)
    b_hi = b.astype(BF16)
    b_lo = (b - b_hi.astype(F32)).astype(BF16)
    return _dot(jnp.concatenate([a_hi, a_hi, a_lo], axis=1), jnp.concatenate([b_hi, b_lo, b_hi], axis=0))


def _rms_mod(x, g, sc, sh):
    y = x * lax.rsqrt(jnp.mean(x * x, axis=-1, keepdims=True) + EPS)
    return (y * g) * (1.0 + sc) + sh


def _mod_rows(mods_ref, row, first):
    return [mods_ref[pl.ds(row, 1), pl.ds((first + j) * D_MODEL, D_MODEL)] for j in range(3)]


def _mods_kernel(cv_ref, w_ref, b_ref, o_ref):
    cv = cv_ref[...]
    s = cv * jax.nn.sigmoid(cv)
    o_ref[...] = _dot(s.astype(BF16), w_ref[...].astype(BF16)) + b_ref[...]


def _mods_call(cvecs, w_mod, b_mod):
    tn = 1536
    n = N_MOD * D_MODEL
    return pl.pallas_call(
        _mods_kernel,
        grid=(DEPTH, n // tn),
        in_specs=[pl.BlockSpec((MODS_ROWS, D_MODEL), lambda l, j: (0, 0)),
                  pl.BlockSpec((None, D_MODEL, tn), lambda l, j: (l, 0, j)),
                  pl.BlockSpec((None, 1, tn), lambda l, j: (l, 0, j))],
        out_specs=pl.BlockSpec((None, MODS_ROWS, tn), lambda l, j: (l, 0, j)),
        out_shape=jax.ShapeDtypeStruct((DEPTH, MODS_ROWS, n), F32),
        compiler_params=pltpu.CompilerParams(vmem_limit_bytes=VMEM_LIMIT),
        name="adaln_mods",
    )(cvecs, w_mod, b_mod.reshape(DEPTH, 1, n))


def _short_conv(zc, cw):
    T = zc.shape[0]
    cb = zc[:, 0:CONV_DIM]
    u = zc[:, CONV_DIM:2 * CONV_DIM] * zc[:, 2 * CONV_DIM:3 * CONV_DIM]
    t = lax.broadcasted_iota(jnp.int32, u.shape, 0)
    u_prev = jnp.where(t == 0, 0.0, pltpu.roll(u, 1, axis=0))
    u_next = jnp.where(t == T - 1, 0.0, pltpu.roll(u, T - 1, axis=0))
    return cb * (cw[0:1, :] * u_prev + cw[1:2, :] * u + cw[2:3, :] * u_next)


def _with_ones(v):
    return jnp.concatenate([v, jnp.ones_like(v)], axis=-1)


def _softmax_attention(nq, nk, nv):
    outs = []
    for h in range(NA_HEADS):
        sl = slice(h * NA_HEAD_DIM, (h + 1) * NA_HEAD_DIM)
        q = (nq[:, sl] * NA_HEAD_DIM ** -0.5).astype(BF16)
        s = _dot_nt(q, nk[:, sl].astype(BF16))
        p = jnp.exp(s - jnp.max(s, axis=-1, keepdims=True))
        o = _dot(p.astype(BF16), nv[:, sl].astype(BF16)) / jnp.sum(p, axis=-1, keepdims=True)
        outs.append(o)
    return jnp.concatenate(outs, axis=-1)


def _log_sigmoid(x):
    return jnp.minimum(x, 0.0) - jnp.log(1.0 + jnp.exp(-jnp.abs(x)))


def _gate_terms(zi, zf):
    T = zi.shape[0]
    lf = _log_sigmoid(zf)
    blk = min(T, CUMSUM_BLOCK)
    r = lax.broadcasted_iota(jnp.int32, (blk, blk), 0)
    c = lax.broadcasted_iota(jnp.int32, (blk, blk), 1)
    tril = jnp.where(c <= r, 1.0, 0.0).astype(F32)
    parts, carry = [], None
    for r0 in range(0, T, blk):
        b = jnp.dot(tril, lf[r0:r0 + blk], precision=HIGHEST, preferred_element_type=F32)
        if carry is not None:
            b = b + carry
        carry = b[blk - 1:blk, :]
        parts.append(b)
    b_fwd = parts[0] if len(parts) == 1 else jnp.concatenate(parts, axis=0)
    b_bwd = (b_fwd[T - 1:T, :] - b_fwd) + lf
    lane = lax.broadcasted_iota(jnp.int32, zi.shape, 1)
    b_col = jnp.where(lane < ML_HEADS, b_fwd, b_bwd)
    row_t = jnp.transpose(zi - b_col)
    return b_col, row_t


def _mlstm_outputs(q, k, v, b_col, li_row, m0, c0, n0, backward, q_block):
    T = q.shape[0]
    qb, kb, vb = q.astype(BF16), k.astype(BF16), v.astype(BF16)
    c0b = None if c0 is None else c0.astype(BF16)
    outs = []
    for r0 in range(0, T, q_block):
        ks, ke = (r0, T) if backward else (0, r0 + q_block)
        bq = b_col[r0:r0 + q_block]
        d = bq + li_row[:, ks:ke]
        t_idx = r0 + lax.broadcasted_iota(jnp.int32, d.shape, 0)
        s_idx = ks + lax.broadcasted_iota(jnp.int32, d.shape, 1)
        d = jnp.where((s_idx >= t_idx) if backward else (s_idx <= t_idx), d, NEG_INF)
        inter = bq + m0
        m_t = jnp.maximum(jnp.max(d, axis=-1, keepdims=True), inter)
        w = jnp.exp(d - m_t)
        s = _dot_nt(qb[r0:r0 + q_block], kb[ks:ke]) * w
        num = _dot(s.astype(BF16), vb[ks:ke])
        den = jnp.sum(s, axis=-1, keepdims=True)
        if c0 is not None:
            a = jnp.exp(inter - m_t)
            num = num + a * _dot(qb[r0:r0 + q_block], c0b)
            den = den + a * jnp.sum(q[r0:r0 + q_block] * n0, axis=-1, keepdims=True)
        outs.append(num / jnp.maximum(jnp.abs(den), jnp.exp(-m_t)))
    return outs[0] if len(outs) == 1 else jnp.concatenate(outs, axis=0)


def _mlstm_state(k, v, b_col, li_col, backward):
    T = k.shape[0]
    b_tot = b_col[0:1] if backward else b_col[T - 1:T]
    g = (b_tot - b_col) + li_col
    m_new = jnp.maximum(b_tot, jnp.max(g, axis=0, keepdims=True))
    kw = k * jnp.exp(g - m_new)
    c_new = lax.dot_general(kw.astype(BF16), v.astype(BF16), (((0,), (0,)), ((), ())),
                            preferred_element_type=F32)
    n_new = jnp.sum(kw, axis=0, keepdims=True)
    return c_new, n_new, m_new


def _head_readout(h, mo, g):
    hn = h * lax.rsqrt(jnp.mean(h * h, axis=-1, keepdims=True) + EPS)
    return hn * g * jax.nn.sigmoid(mo)


def _ctx_mixer_kernel(x_ref, mods_ref, ln_ref, win_ref, wgate_ref, convw_ref, gb_ref, mlg_ref, wout_ref,
                      *rest, has_moe):
    xo_ref, ko_ref, vo_ref, co_ref, no_ref, mo_ref = rest[-6:]
    T = SEQ
    sh1, sc1, g1 = _mod_rows(mods_ref, 0, 0)
    x = x_ref[...]
    if has_moe:
        moe_ref, mods_prev_ref = rest[0], rest[1]
        x = x + _mod_rows(mods_prev_ref, 0, 3)[2] * _from_slabs(moe_ref, T)
    h = _rms_mod(x, ln_ref[...], sc1, sh1).astype(BF16)

    conv_o = _short_conv(_dot(h, win_ref[:, C_CONV:C_NA]), convw_ref[...])

    zn = _dot(h, win_ref[:, C_NA:C_ML])
    nk, nv = zn[:, NA_DIM:2 * NA_DIM], zn[:, 2 * NA_DIM:3 * NA_DIM]
    ko_ref[...] = nk
    vo_ref[...] = nv
    na_o = _softmax_attention(zn[:, 0:NA_DIM], nk, nv)

    zg = _dot(h, wgate_ref[...])
    zi = zg[:, 0:LANES] + gb_ref[:, 0:LANES]
    zf = zg[:, LANES:2 * LANES] + gb_ref[:, LANES:2 * LANES]
    b_col, row_t = _gate_terms(zi, zf)
    m0 = jnp.zeros((1, 1), F32)

    zm = _dot(h, win_ref[:, C_ML:C_IG])
    ml_parts = []
    for hh in range(ML_HEADS):
        zq, zk, zv, zo = (zm[:, p * ML_DIM + hh * ML_HEAD_DIM:p * ML_DIM + (hh + 1) * ML_HEAD_DIM]
                          for p in range(4))
        zk = zk * ML_HEAD_DIM ** -0.5
        hsum = None
        for d in range(N_DIRS):
            j = d * ML_HEADS + hh
            bc = b_col[:, j:j + 1]
            ho = _mlstm_outputs(zq, zk, zv, bc, row_t[j:j + 1, :], m0, None, None, d == 1, T)
            hsum = ho if hsum is None else hsum + ho
            c_new, n_new, m_new = _mlstm_state(zk, zv, bc, zi[:, j:j + 1], d == 1)
            co_ref[d, hh] = c_new
            no_ref[j:j + 1, :] = n_new
            mo_ref[j:j + 1, :] = jnp.broadcast_to(m_new, (1, LANES))
        ml_parts.append(_head_readout(hsum, zo, mlg_ref[:, hh * ML_HEAD_DIM:(hh + 1) * ML_HEAD_DIM]))

    mix = jnp.concatenate([conv_o, na_o] + ml_parts, axis=-1).astype(BF16)
    xo_ref[...] = x + g1 * _dot(mix, wout_ref[...])


def _ctx_mixer_call(layer, x, moe, mods, ln1, w_in, conv_w, gate_b, ml_g, w_out, prev):
    B = x.shape[0]
    T = SEQ
    G = None
    state_shapes = [
        jax.ShapeDtypeStruct((B, DEPTH, T, NA_DIM), F32),
        jax.ShapeDtypeStruct((B, DEPTH, T, NA_DIM), F32),
        jax.ShapeDtypeStruct((B, DEPTH, N_DIRS, ML_HEADS, ML_HEAD_DIM, ML_HEAD_DIM), F32),
        jax.ShapeDtypeStruct((B, DEPTH, N_GATE, ML_HEAD_DIM), F32),
        jax.ShapeDtypeStruct((B, DEPTH, N_GATE, LANES), F32),
    ]
    in_specs = [
        pl.BlockSpec((G, T, D_MODEL), lambda b: (b, 0, 0)),
        pl.BlockSpec((None, MODS_ROWS, N_MOD * D_MODEL), lambda b: (layer, 0, 0)),
        pl.BlockSpec((None, 1, D_MODEL), lambda b: (layer, 0, 0)),
        pl.BlockSpec((None, D_MODEL, C_IG), lambda b: (layer, 0, 0)),
        pl.BlockSpec((None, D_MODEL, 2 * LANES), lambda b: (layer, 0, 0)),
        pl.BlockSpec((None, 3, CONV_DIM), lambda b: (layer, 0, 0)),
        pl.BlockSpec((None, 1, 2 * LANES), lambda b: (layer, 0, 0)),
        pl.BlockSpec((None, 1, ML_DIM), lambda b: (layer, 0, 0)),
        pl.BlockSpec((None, D_MODEL, D_MODEL), lambda b: (layer, 0, 0)),
    ]
    args = [x, mods, ln1, w_in[0], w_in[1], conv_w, gate_b, ml_g, w_out]
    if moe is not None:
        per_sb = DSP_SB // T
        in_specs += [pl.BlockSpec((None, T * SLAB, LANES), lambda b: (b // per_sb, b % per_sb, 0)),
                     pl.BlockSpec((None, MODS_ROWS, N_MOD * D_MODEL), lambda b: (layer - 1, 0, 0))]
        args += [moe, mods]
    aliases = {}
    if prev is not None:
        in_specs += [pl.BlockSpec(memory_space=pl.ANY)] * len(prev)
        aliases = {len(args) + i: 1 + i for i in range(len(prev))}
        args += list(prev)
    out_specs = [
        pl.BlockSpec((G, T, D_MODEL), lambda b: (b, 0, 0)),
        pl.BlockSpec((G, None, T, NA_DIM), lambda b: (b, layer, 0, 0)),
        pl.BlockSpec((G, None, T, NA_DIM), lambda b: (b, layer, 0, 0)),
        pl.BlockSpec((G, None, N_DIRS, ML_HEADS, ML_HEAD_DIM, ML_HEAD_DIM),
                     lambda b: (b, layer, 0, 0, 0, 0)),
        pl.BlockSpec((G, None, N_GATE, ML_HEAD_DIM), lambda b: (b, layer, 0, 0)),
        pl.BlockSpec((G, None, N_GATE, LANES), lambda b: (b, layer, 0, 0)),
    ]
    outs = pl.pallas_call(
        functools.partial(_ctx_mixer_kernel, has_moe=moe is not None),
        grid=(B,),
        in_specs=in_specs,
        out_specs=out_specs,
        out_shape=[jax.ShapeDtypeStruct(x.shape, F32)] + state_shapes,
        input_output_aliases=aliases,
        compiler_params=pltpu.CompilerParams(vmem_limit_bytes=VMEM_LIMIT),
        name="ctx_mixer",
    )(*args)
    return outs[0], tuple(outs[1:])


def _route(lg):
    lane = lax.broadcasted_iota(jnp.int32, lg.shape, 1)
    lane_f = lane.astype(F32)
    big = float(LANES)
    is_g = (lane >= N_EXPERTS) & (lane < N_EXPERTS + N_GROUPS)
    glm = jnp.where(is_g, lg, NEG_INF)
    gmax = jnp.max(glm, axis=-1, keepdims=True)
    g_top = jnp.min(jnp.where(glm == gmax, lane_f, big), axis=-1, keepdims=True) - N_EXPERTS
    gp = 1.0 / jnp.sum(jnp.where(is_g, jnp.exp(lg - gmax), 0.0), axis=-1, keepdims=True)
    grp = (lane >> (EXPERTS_PER_GROUP.bit_length() - 1)).astype(F32)
    in_grp = (lane < N_EXPERTS) & (grp == g_top)
    e1 = jnp.where(in_grp, lg, NEG_INF)
    v1 = jnp.max(e1, axis=-1, keepdims=True)
    i1 = jnp.min(jnp.where(e1 == v1, lane_f, big), axis=-1, keepdims=True)
    e2 = jnp.where(in_grp & (lane_f != i1), lg, NEG_INF)
    v2 = jnp.max(e2, axis=-1, keepdims=True)
    i2 = jnp.min(jnp.where(e2 == v2, lane_f, big), axis=-1, keepdims=True)
    t = jnp.exp(v2 - v1)
    w1 = 1.0 / (1.0 + t)
    w2 = t / (1.0 + t)
    return i1, i2, gp * w1, gp * w2


SLAB = D_MODEL // LANES
DSP_SB = 2048
DSP_TM = 160
DSP_CH = 512
DSP_PMAX = 2 * DSP_SB + N_EXPERTS * DSP_TM
DSP_NULL = DSP_PMAX + 2 * DSP_TM
DSP_SLOTS = DSP_NULL + 2 * DSP_TM
DSP_PAD_PAIR = 2 * DSP_SB
DSP_W_LEN = 2 * DSP_SB + LANES
DSP_UNROLL = 8


def _to_slabs(ref, x):
    n = x.shape[0]
    for s in range(SLAB):
        ref[pl.ds(s, n, stride=SLAB), :] = x[:, s * LANES:(s + 1) * LANES]


def _from_slabs(ref, n):
    return jnp.concatenate([ref[pl.ds(s, n, stride=SLAB), :] for s in range(SLAB)], axis=-1)


def _dsp_mods(mods_ref, chunk, chunks_per_row, first):
    row = 0 if chunks_per_row is None else 1 + chunk // chunks_per_row
    return _mod_rows(mods_ref, row, first)


def _dsp_route_kernel(x_ref, mods_ref, ln_ref, wr_ref, br_ref, h_ref, r_ref, cnt_ref, run_scr, *,
                      chunks_per_row):
    c = pl.program_id(0)
    ch = DSP_CH

    @pl.when(c % (DSP_SB // ch) == 0)
    def _():
        run_scr[...] = jnp.zeros_like(run_scr)

    sh2, sc2, _ = _dsp_mods(mods_ref, c, chunks_per_row, 3)
    h2 = _rms_mod(x_ref[...], ln_ref[...], sc2, sh2)
    _to_slabs(h_ref, h2)
    lg = _dot_split(h2, wr_ref[...]) + br_ref[...]
    i1, i2, w1, w2 = _route(lg)
    lane = lax.broadcasted_iota(jnp.int32, lg.shape, 1)
    lane_f = lane.astype(F32)
    oh1 = jnp.where(lane_f == i1, 1.0, 0.0)
    oh2 = jnp.where(lane_f == i2, 1.0, 0.0)
    both = oh1 + oh2
    r = lax.broadcasted_iota(jnp.int32, (ch, ch), 0)
    s = lax.broadcasted_iota(jnp.int32, (ch, ch), 1)
    earlier = jnp.where(s < r, 1.0, 0.0).astype(BF16)
    run = run_scr[0:1, :]
    before = _dot(earlier, both.astype(BF16)) + run
    rank1 = jnp.sum(before * oh1, axis=-1, keepdims=True)
    rank2 = jnp.sum(before * oh2, axis=-1, keepdims=True)
    run = run + jnp.sum(both, axis=0, keepdims=True)
    run_scr[0:1, :] = run
    cnt_ref[...] = jnp.broadcast_to(run, cnt_ref.shape)
    cols = (i1, i2, w1, w2, rank1, rank2)
    out = jnp.zeros(lg.shape, F32)
    for j, col in enumerate(cols):
        out = jnp.where(lane == j, col, out)
    r_ref[...] = out


def _dsp_route_call(layer, x2d, mods, ln2, w_router, b_router, chunks_per_row):
    n = x2d.shape[0]
    ch = DSP_CH
    per_sb = DSP_SB // ch
    return pl.pallas_call(
        functools.partial(_dsp_route_kernel, chunks_per_row=chunks_per_row),
        grid=(n // ch,),
        in_specs=[
            pl.BlockSpec((ch, D_MODEL), lambda c: (c, 0)),
            pl.BlockSpec((None, MODS_ROWS, N_MOD * D_MODEL), lambda c: (layer, 0, 0)),
            pl.BlockSpec((None, 1, D_MODEL), lambda c: (layer, 0, 0)),
            pl.BlockSpec((None, D_MODEL, LANES), lambda c: (layer, 0, 0)),
            pl.BlockSpec((None, 1, LANES), lambda c: (layer, 0, 0)),
        ],
        out_specs=[pl.BlockSpec((ch * SLAB, LANES), lambda c: (c, 0)),
                   pl.BlockSpec((ch, LANES), lambda c: (c, 0)),
                   pl.BlockSpec((None, SLAB, LANES), lambda c: (c // per_sb, 0, 0))],
        out_shape=[jax.ShapeDtypeStruct((n * SLAB, LANES), F32),
                   jax.ShapeDtypeStruct((n, LANES), F32),
                   jax.ShapeDtypeStruct((n // DSP_SB, SLAB, LANES), F32)],
        scratch_shapes=[pltpu.VMEM((SLAB, LANES), F32)],
        compiler_params=pltpu.CompilerParams(vmem_limit_bytes=VMEM_LIMIT),
        name="moe_route",
    )(x2d, mods, ln2, w_router, b_router)


def _dispatch_tables(route, counts, n_sb):
    e = route[:, 0:2].astype(jnp.int32).reshape(n_sb, 2 * DSP_SB)
    rank = route[:, 4:6].astype(jnp.int32).reshape(n_sb, 2 * DSP_SB)
    w = _pad_lanes(route[:, 2:4].reshape(n_sb, 1, 2 * DSP_SB), DSP_W_LEN)
    cnt = counts[:, 0, :N_EXPERTS].astype(jnp.int32)
    ntile = (cnt + (DSP_TM - 1)) // DSP_TM
    off = (jnp.cumsum(ntile, axis=1) - ntile) * DSP_TM
    is_e = e[:, None, :] == jnp.arange(N_EXPERTS, dtype=jnp.int32)[None, :, None]
    pos = (jnp.sum(jnp.where(is_e, off[:, :, None], 0), axis=1) + rank).reshape(n_sb, 1, 2 * DSP_SB)
    return pos, w, ntile.reshape(-1), off.reshape(-1), cnt.reshape(-1)


def _dsp_expert_kernel(ntile_ref, off_ref, cnt_ref, pos_ref, w_ref, h_ref, wg_ref, wu_ref, wd_ref,
                       o_ref, pair_smem, sched_smem, pend_smem, xs0, xs1, ys0, ys1,
                       wg_scr, wu_scr, wd_scr):
    sb = pl.program_id(0)
    e = pl.program_id(1)
    tm, u = DSP_TM, DSP_UNROLL
    xs_scr, ys_scr = (xs0, xs1), (ys0, ys1)
    slab_shift = SLAB.bit_length() - 1
    last_row = (DSP_SB - 1) * SLAB

    def slab_row(pair):
        return lax.shift_left(lax.shift_right_logical(pair, 1), slab_shift)

    @pl.when(e == 0)
    def _():
        for ee in range(N_EXPERTS):
            q = sb * N_EXPERTS + ee

            def pad(p, carry):
                pair_smem[p] = DSP_PAD_PAIR
                return carry

            lax.fori_loop(off_ref[q] + cnt_ref[q], off_ref[q] + ntile_ref[q] * tm, pad, 0)

        def invert(j, carry):
            for k in range(u):
                a = j * u + k
                pair_smem[pos_ref[0, a]] = a
            return carry

        lax.fori_loop(0, 2 * DSP_SB // u, invert, 0)

        last_q = sb * N_EXPERTS + (N_EXPERTS - 1)
        used_end = off_ref[last_q] + ntile_ref[last_q] * tm

        def pad_end(p, carry):
            pair_smem[used_end + p] = DSP_PAD_PAIR
            pair_smem[DSP_NULL + p] = DSP_PAD_PAIR
            return carry

        lax.fori_loop(0, 2 * tm, pad_end, 0)
        sched_smem[0] = -1
        sched_smem[1] = DSP_NULL
        pend_smem[0] = 0.0
        o_ref[...] = jnp.zeros_like(o_ref)
        for ys in ys_scr:
            ys[...] = jnp.zeros_like(ys)

    q = sb * N_EXPERTS + e
    base0 = off_ref[q]
    nt = ntile_ref[q]
    wg_scr[...] = wg_ref[...].astype(BF16)
    wu_scr[...] = wu_ref[...].astype(BF16)
    wd_scr[...] = wd_ref[...].astype(BF16)

    def gather(base, buf):
        for r in range(tm):
            src = jnp.minimum(slab_row(pair_smem[base + r]), last_row)
            xs_scr[buf][r * SLAB:(r + 1) * SLAB, :] = h_ref[pl.ds(pl.multiple_of(src, SLAB), SLAB), :]

    def compute(buf):
        x = _from_slabs(xs_scr[buf], tm).astype(BF16)
        a = jax.nn.silu(_dot(x, wg_scr[...])) * _dot(x, wu_scr[...])
        _to_slabs(ys_scr[buf], _dot(a.astype(BF16), wd_scr[...]))

    def scatter(base, buf, scale):
        for j in range(tm // u):
            upd = []
            for k in range(u):
                r = j * u + k
                pair = pair_smem[base + r]
                dst = pl.multiple_of(slab_row(pair), SLAB)
                y = ys_scr[buf][r * SLAB:(r + 1) * SLAB, :]
                upd.append((dst, o_ref[pl.ds(dst, SLAB), :] + (w_ref[0, pair] * scale) * y))
            for dst, v in upd:
                o_ref[pl.ds(dst, SLAB), :] = v

    n_pairs = lax.shift_right_logical(nt + 1, 1)

    @pl.when((n_pairs > 0) & (sched_smem[0] != base0))
    def _():
        gather(base0, 0)

    pend_base = sched_smem[1]
    pend_scale = pend_smem[0]

    def pair(k, carry):
        b = base0 + 2 * k * tm
        first = k == 0
        gather(b + tm, 1)
        compute(0)
        scatter(jnp.where(first, pend_base, b - tm), 1, jnp.where(first, pend_scale, 1.0))
        gather(b + 2 * tm, 0)
        compute(1)
        scatter(b, 0, 1.0)
        return carry

    lax.fori_loop(0, n_pairs, pair, 0)

    @pl.when(n_pairs > 0)
    def _():
        last = 2 * n_pairs - 1
        sched_smem[0] = base0 + (last + 1) * tm
        sched_smem[1] = base0 + last * tm
        pend_smem[0] = jnp.where(last < nt, 1.0, 0.0)

    @pl.when(e == N_EXPERTS - 1)
    def _():
        scatter(sched_smem[1], 1, pend_smem[0])


def _dsp_expert_call(layer, h_slabs, pos, w, ntile, off, cnt, w_gate, w_up, w_down):
    n_sb = pos.shape[0]
    out_rows = (DSP_SB + DSP_CH) * SLAB

    def per_sb(s, e, *prefetch):
        return (s, 0, 0)

    def w_idx(s, e, *prefetch):
        return (layer, e, 0, 0)

    grid_spec = pltpu.PrefetchScalarGridSpec(
        num_scalar_prefetch=3,
        grid=(n_sb, N_EXPERTS),
        in_specs=[
            pl.BlockSpec((None, 1, 2 * DSP_SB), per_sb, memory_space=pltpu.SMEM),
            pl.BlockSpec((None, 1, DSP_W_LEN), per_sb, memory_space=pltpu.SMEM),
            pl.BlockSpec((DSP_SB * SLAB, LANES), lambda s, e, *prefetch: (s, 0)),
            pl.BlockSpec((None, None, D_MODEL, EXPERT_FF), w_idx),
            pl.BlockSpec((None, None, D_MODEL, EXPERT_FF), w_idx),
            pl.BlockSpec((None, None, EXPERT_FF, D_MODEL), w_idx),
        ],
        out_specs=pl.BlockSpec((None, out_rows, LANES), per_sb),
        scratch_shapes=[pltpu.SMEM((DSP_SLOTS,), jnp.int32),
                        pltpu.SMEM((2,), jnp.int32),
                        pltpu.SMEM((1,), F32),
                        pltpu.VMEM((DSP_TM * SLAB, LANES), F32),
                        pltpu.VMEM((DSP_TM * SLAB, LANES), F32),
                        pltpu.VMEM((DSP_TM * SLAB, LANES), F32),
                        pltpu.VMEM((DSP_TM * SLAB, LANES), F32),
                        pltpu.VMEM((D_MODEL, EXPERT_FF), BF16),
                        pltpu.VMEM((D_MODEL, EXPERT_FF), BF16),
                        pltpu.VMEM((EXPERT_FF, D_MODEL), BF16)],
    )
    return pl.pallas_call(
        _dsp_expert_kernel,
        grid_spec=grid_spec,
        out_shape=jax.ShapeDtypeStruct((n_sb, out_rows, LANES), F32),
        compiler_params=pltpu.CompilerParams(vmem_limit_bytes=VMEM_LIMIT),
        name="moe_experts",
    )(ntile, off, cnt, pos, w, h_slabs, w_gate, w_up, w_down)


def _dsp_final_kernel(x_ref, m_ref, mods_ref, fg_ref, o_ref, *, chunks_per_row):
    g2 = _dsp_mods(mods_ref, pl.program_id(0), chunks_per_row, 3)[2]
    y = x_ref[...] + g2 * _from_slabs(m_ref, DSP_CH)
    o_ref[...] = y * lax.rsqrt(jnp.mean(y * y, axis=-1, keepdims=True) + EPS) * fg_ref[...]


def _dsp_final_call(layer, x2d, m_slabs, mods, final_g, chunks_per_row):
    n = x2d.shape[0]
    ch = DSP_CH
    per_sb = DSP_SB // ch
    return pl.pallas_call(
        functools.partial(_dsp_final_kernel, chunks_per_row=chunks_per_row),
        grid=(n // ch,),
        in_specs=[
            pl.BlockSpec((ch, D_MODEL), lambda c: (c, 0)),
            pl.BlockSpec((None, ch * SLAB, LANES), lambda c: (c // per_sb, c % per_sb, 0)),
            pl.BlockSpec((None, MODS_ROWS, N_MOD * D_MODEL), lambda c: (layer, 0, 0)),
            pl.BlockSpec((1, D_MODEL), lambda c: (0, 0)),
        ],
        out_specs=pl.BlockSpec((ch, D_MODEL), lambda c: (c, 0)),
        out_shape=jax.ShapeDtypeStruct(x2d.shape, F32),
        compiler_params=pltpu.CompilerParams(vmem_limit_bytes=VMEM_LIMIT),
        name="final_norm",
    )(x2d, m_slabs, mods, final_g)


def _sparse_moe(layer, x2d, mods, ln2, w_router, b_router, w_gate, w_up, w_down, chunks_per_row):
    n_sb = x2d.shape[0] // DSP_SB
    h_slabs, route, counts = _dsp_route_call(layer, x2d, mods, ln2, w_router, b_router,
                                             chunks_per_row)
    pos, w, ntile, off, cnt = _dispatch_tables(route, counts, n_sb)
    return _dsp_expert_call(layer, h_slabs, pos, w, ntile, off, cnt, w_gate, w_up, w_down)


LAT_PROJ_ROWS = 512


def _lat_proj_kernel(x_ref, mods_ref, ln_ref, win_ref, wgate_ref, gb_ref, *rest, has_moe):
    zc_ref, zn_ref, zm_ref, zg_ref = rest[-4:]
    row = 1 + pl.program_id(0) // (DEC_SEQ // LAT_PROJ_ROWS)
    sh1, sc1, _ = _mod_rows(mods_ref, row, 0)
    x = x_ref[...]
    if has_moe:
        moe_ref, mods_prev_ref, xo_ref = rest[0], rest[1], rest[2]
        x = x + _mod_rows(mods_prev_ref, row, 3)[2] * _from_slabs(moe_ref, LAT_PROJ_ROWS)
        xo_ref[...] = x
    h = _rms_mod(x, ln_ref[...], sc1, sh1).astype(BF16)
    zc_ref[...] = _dot(h, win_ref[:, C_CONV:C_NA])
    zn_ref[...] = _dot(h, win_ref[:, C_NA:C_ML])
    zm_ref[...] = _dot(h, win_ref[:, C_ML:C_IG])
    zg_ref[...] = _dot(h, wgate_ref[...]) + gb_ref[...]


def _lat_proj_call(layer, x2d, moe, mods, ln1, w_in, gate_b):
    n = x2d.shape[0]
    tb = LAT_PROJ_ROWS
    widths = (C_NA - C_CONV, C_ML - C_NA, C_IG - C_ML, 2 * LANES)
    in_specs = [
        pl.BlockSpec((tb, D_MODEL), lambda i: (i, 0)),
        pl.BlockSpec((None, MODS_ROWS, N_MOD * D_MODEL), lambda i: (layer, 0, 0)),
        pl.BlockSpec((None, 1, D_MODEL), lambda i: (layer, 0, 0)),
        pl.BlockSpec((None, D_MODEL, C_IG), lambda i: (layer, 0, 0)),
        pl.BlockSpec((None, D_MODEL, 2 * LANES), lambda i: (layer, 0, 0)),
        pl.BlockSpec((None, 1, 2 * LANES), lambda i: (layer, 0, 0)),
    ]
    args = [x2d, mods, ln1, w_in[0], w_in[1], gate_b]
    out_specs = [pl.BlockSpec((tb, w), lambda i: (i, 0)) for w in widths]
    out_shape = [jax.ShapeDtypeStruct((n, w), F32) for w in widths]
    if moe is not None:
        per_sb = DSP_SB // tb
        in_specs += [pl.BlockSpec((None, tb * SLAB, LANES), lambda i: (i // per_sb, i % per_sb, 0)),
                     pl.BlockSpec((None, MODS_ROWS, N_MOD * D_MODEL), lambda i: (layer - 1, 0, 0))]
        args += [moe, mods]
        out_specs = [pl.BlockSpec((tb, D_MODEL), lambda i: (i, 0))] + out_specs
        out_shape = [jax.ShapeDtypeStruct((n, D_MODEL), F32)] + out_shape
    outs = pl.pallas_call(
        functools.partial(_lat_proj_kernel, has_moe=moe is not None),
        grid=(n // tb,),
        in_specs=in_specs,
        out_specs=out_specs,
        out_shape=out_shape,
        compiler_params=pltpu.CompilerParams(vmem_limit_bytes=VMEM_LIMIT),
        name="lat_proj",
    )(*args)
    return tuple(outs) if moe is not None else (x2d,) + tuple(outs)


def _na_row_start(r):
    rows = DEC_SEQ // GRID_W
    return min(max(r - NA_WIN_ROWS // 2, 0), rows - NA_WIN_ROWS)


def _na_row_groups():
    rows = DEC_SEQ // GRID_W
    groups, lo = [], 0
    for r in range(1, rows + 1):
        if r == rows or _na_row_start(r) != _na_row_start(lo):
            groups.append((lo, r))
            lo = r
    return tuple(groups)


_NA_ROW_GROUPS = _na_row_groups()


def _lat_local_kernel(zc_ref, zn_ref, kc_ref, vc_ref, band_ref, convw_ref, o_ref):
    o_ref[:, 0:CONV_DIM] = _short_conv(zc_ref[...], convw_ref[...])
    blk = NA_WIN_ROWS * GRID_W
    q_all = (zn_ref[:, 0:NA_DIM] * NA_HEAD_DIM ** -0.5).astype(BF16)
    k_all = zn_ref[:, NA_DIM:2 * NA_DIM].astype(BF16)
    v_all = zn_ref[:, 2 * NA_DIM:3 * NA_DIM].astype(BF16)
    kc_all = kc_ref[...].astype(BF16)
    vc_all = vc_ref[...].astype(BF16)
    heads = []
    for h in range(NA_HEADS):
        sl = slice(h * NA_HEAD_DIM, (h + 1) * NA_HEAD_DIM)
        q, k, v, kc, vc = q_all[:, sl], k_all[:, sl], v_all[:, sl], kc_all[:, sl], vc_all[:, sl]
        band = band_ref[h]
        s_ctx = _dot_nt(q, kc)
        m_ctx = jnp.max(s_ctx, axis=-1, keepdims=True)
        v1 = _with_ones(v)
        m_rows, o_rows = [], []
        for r_lo, r_hi in _NA_ROW_GROUPS:
            start = _na_row_start(r_lo)
            bias = [band[:, (start - r + NA_WIN_ROWS - 1) * GRID_W:(start - r + NA_WIN_ROWS - 1) * GRID_W + blk]
                    for r in range(r_lo, r_hi)]
            bias = bias[0] if len(bias) == 1 else jnp.concatenate(bias, axis=0)
            q_g = q[r_lo * GRID_W:r_hi * GRID_W]
            s_loc = _dot_nt(q_g, k[start * GRID_W:start * GRID_W + blk]) + bias
            m = jnp.maximum(jnp.max(s_loc, axis=-1, keepdims=True), m_ctx[r_lo * GRID_W:r_hi * GRID_W])
            p_loc = jnp.exp(s_loc - m)
            m_rows.append(m)
            o_rows.append(_dot(p_loc.astype(BF16), v1[start * GRID_W:start * GRID_W + blk]))
        p_ctx = jnp.exp(s_ctx - jnp.concatenate(m_rows, axis=0))
        o = jnp.concatenate(o_rows, axis=0) + _dot(p_ctx.astype(BF16), _with_ones(vc))
        heads.append(o[:, 0:NA_HEAD_DIM] / o[:, NA_HEAD_DIM:NA_HEAD_DIM + 1])
    o_ref[:, CONV_DIM:CONV_DIM + NA_DIM] = jnp.concatenate(heads, axis=-1)


def _lat_local_call(layer, zc, zn, cache_k, cache_v, band, conv_w):
    nb = zc.shape[0] // DEC_SEQ
    T = DEC_SEQ
    return pl.pallas_call(
        _lat_local_kernel,
        grid=(nb,),
        in_specs=[
            pl.BlockSpec((T, 3 * CONV_DIM), lambda b: (b, 0)),
            pl.BlockSpec((T, 3 * NA_DIM), lambda b: (b, 0)),
            pl.BlockSpec((None, None, PAST_LEN, NA_DIM), lambda b: (b, layer, 0, 0)),
            pl.BlockSpec((None, None, PAST_LEN, NA_DIM), lambda b: (b, layer, 0, 0)),
            pl.BlockSpec((None, NA_HEADS, GRID_W, band.shape[-1]), lambda b: (layer, 0, 0, 0)),
            pl.BlockSpec((None, 3, CONV_DIM), lambda b: (layer, 0, 0)),
        ],
        out_specs=pl.BlockSpec((T, CONV_DIM + NA_DIM), lambda b: (b, 0)),
        out_shape=jax.ShapeDtypeStruct((zc.shape[0], CONV_DIM + NA_DIM), F32),
        compiler_params=pltpu.CompilerParams(vmem_limit_bytes=VMEM_LIMIT),
        name="lat_conv_na",
    )(zc, zn, cache_k, cache_v, band, conv_w)


def _rope(x, cos, sin_signed):
    w = x.shape[-1]
    half = NA_HEAD_DIM // 2
    lane = lax.broadcasted_iota(jnp.int32, x.shape, 1)
    partner = jnp.where(lane % (2 * half) < half, pltpu.roll(x, w - half, axis=1), pltpu.roll(x, half, axis=1))
    return x * cos + partner * sin_signed


def _lat_mlstm_kernel(q_ref, k_ref, v_ref, o_ref, zg_ref, c0_ref, n0_ref, m0_ref, cos_ref, sin_ref,
                      mlg_ref, out_ref, bcol_scr, rowt_scr):
    hh = pl.program_id(1)

    @pl.when(hh == 0)
    def _():
        b_all, r_all = _gate_terms(zg_ref[:, 0:LANES], zg_ref[:, LANES:2 * LANES])
        bcol_scr[...] = b_all
        rowt_scr[...] = r_all

    b_col = bcol_scr[...]
    cos, sin_signed = cos_ref[...], sin_ref[...]
    q = _rope(q_ref[...], cos, sin_signed)
    k = _rope(k_ref[...], cos, sin_signed) * ML_HEAD_DIM ** -0.5
    v = v_ref[...]
    lane = lax.broadcasted_iota(jnp.int32, (DEC_SEQ, LANES), 1)
    hsum = None
    for d in range(N_DIRS):
        j = d * ML_HEADS + hh
        bc = jnp.sum(jnp.where(lane == j, b_col, 0.0), axis=-1, keepdims=True)
        lr = rowt_scr[pl.ds(j, 1), :]
        m0 = m0_ref[d, :, 0:1]
        ho = _mlstm_outputs(q, k, v, bc, lr, m0, c0_ref[d], n0_ref[d], d == 1, 256)
        hsum = ho if hsum is None else hsum + ho
    out_ref[...] = _head_readout(hsum, o_ref[...], mlg_ref[...])


def _lat_mlstm_call(layer, zm, zg, state_c, state_n, state_m, cos, sin_signed, ml_g):
    nb = zm.shape[0] // DEC_SEQ
    T = DEC_SEQ
    hd = ML_HEAD_DIM

    def col(part):
        return pl.BlockSpec((T, hd), lambda b, h: (b, part * ML_HEADS + h))

    return pl.pallas_call(
        _lat_mlstm_kernel,
        grid=(nb, ML_HEADS),
        in_specs=[
            col(0), col(1), col(2), col(3),
            pl.BlockSpec((T, 2 * LANES), lambda b, h: (b, 0)),
            pl.BlockSpec((None, None, N_DIRS, None, hd, hd), lambda b, h: (b, layer, 0, h, 0, 0)),
            pl.BlockSpec((None, None, N_DIRS, None, 1, hd), lambda b, h: (b, layer, 0, h, 0, 0)),
            pl.BlockSpec((None, None, N_DIRS, None, 1, LANES), lambda b, h: (b, layer, 0, h, 0, 0)),
            pl.BlockSpec((T, hd), lambda b, h: (0, 0)),
            pl.BlockSpec((T, hd), lambda b, h: (0, 0)),
            pl.BlockSpec((None, 1, hd), lambda b, h: (layer, 0, h)),
        ],
        out_specs=pl.BlockSpec((T, hd), lambda b, h: (b, h)),
        out_shape=jax.ShapeDtypeStruct((zm.shape[0], ML_DIM), F32),
        scratch_shapes=[pltpu.VMEM((T, LANES), F32), pltpu.VMEM((LANES, T), F32)],
        compiler_params=pltpu.CompilerParams(vmem_limit_bytes=VMEM_LIMIT),
        name="lat_mlstm",
    )(zm, zm, zm, zm, zg, state_c, state_n, state_m, cos, sin_signed, ml_g)


def _lat_merge_kernel(x_ref, loc_ref, ml_ref, mods_ref, wout_ref, o_ref):
    row = 1 + pl.program_id(0) // (DEC_SEQ // LAT_PROJ_ROWS)
    g1 = _mod_rows(mods_ref, row, 0)[2]
    split = CONV_DIM + NA_DIM
    y = _dot(loc_ref[...].astype(BF16), wout_ref[0:split, :]) + _dot(ml_ref[...].astype(BF16), wout_ref[split:, :])
    o_ref[...] = x_ref[...] + g1 * y


def _lat_merge_call(layer, x2d, loc, ml, mods, w_out):
    n = x2d.shape[0]
    tb = LAT_PROJ_ROWS
    return pl.pallas_call(
        _lat_merge_kernel,
        grid=(n // tb,),
        in_specs=[
            pl.BlockSpec((tb, D_MODEL), lambda i: (i, 0)),
            pl.BlockSpec((tb, CONV_DIM + NA_DIM), lambda i: (i, 0)),
            pl.BlockSpec((tb, ML_DIM), lambda i: (i, 0)),
            pl.BlockSpec((None, MODS_ROWS, N_MOD * D_MODEL), lambda i: (layer, 0, 0)),
            pl.BlockSpec((None, D_MODEL, D_MODEL), lambda i: (layer, 0, 0)),
        ],
        out_specs=pl.BlockSpec((tb, D_MODEL), lambda i: (i, 0)),
        out_shape=jax.ShapeDtypeStruct(x2d.shape, F32),
        compiler_params=pltpu.CompilerParams(vmem_limit_bytes=VMEM_LIMIT),
        name="lat_merge",
    )(x2d, loc, ml, mods, w_out)


def _pad_lanes(a, width):
    return jnp.pad(a, [(0, 0)] * (a.ndim - 1) + [(0, width - a.shape[-1])])


def _pack_w_in(w_in):
    w_b = w_in.astype(BF16)
    gates = w_b[..., C_IG:]
    gates = jnp.concatenate([_pad_lanes(gates[..., :N_GATE], LANES),
                             _pad_lanes(gates[..., N_GATE:], LANES)], axis=-1)
    return w_b, gates


def _pack_gate_bias(ml_gate_b):
    gb = ml_gate_b.reshape(DEPTH, 2, N_GATE).astype(F32)
    return jnp.concatenate([_pad_lanes(gb[:, 0], LANES), _pad_lanes(gb[:, 1], LANES)], axis=-1)[:, None, :]


def _rpb_band(na_rpb):
    cols = np.arange(GRID_W)
    col_idx = np.clip(cols[None, :] - cols[:, None] + NA_WIN_COLS - 1, 0, 2 * NA_WIN_COLS - 2)
    col_start = np.clip(cols - NA_WIN_COLS // 2, 0, GRID_W - NA_WIN_COLS)
    col_mask = (cols[None, :] >= col_start[:, None]) & (cols[None, :] < col_start[:, None] + NA_WIN_COLS)
    pick = (col_idx[None] == np.arange(2 * NA_WIN_COLS - 1)[:, None, None]).astype(np.float32)
    t = jnp.einsum('lhri,iqk->lhqrk', na_rpb.astype(F32), jnp.asarray(pick), precision=HIGHEST)
    t = jnp.where(col_mask[None, None, :, None, :], t, NEG_INF)
    t = t.reshape(DEPTH, NA_HEADS, GRID_W, (2 * NA_WIN_ROWS - 1) * GRID_W)
    return _pad_lanes(t, 2 * NA_WIN_ROWS * GRID_W)


def _rope_tables():
    t = np.arange(DEC_SEQ)
    pos = np.stack([t // GRID_W, t % GRID_W], axis=-1).astype(np.float32)
    nf = ML_HEAD_DIM // 4
    inv = jnp.asarray(ROPE_BASE, F32) ** (-jnp.arange(nf, dtype=F32) / nf)
    ang = jnp.asarray(pos)[:, :, None] * inv
    cos = jnp.cos(ang)
    sin = jnp.sin(ang)
    cos_t = jnp.concatenate([cos, cos], axis=-1).reshape(DEC_SEQ, ML_HEAD_DIM)
    sin_t = jnp.concatenate([-sin, sin], axis=-1).reshape(DEC_SEQ, ML_HEAD_DIM)
    return cos_t, sin_t


def kernel(x_prompt, x_sample, cache_k, cache_v, state_C, state_n, state_m, c, c_ctx, w_mod, b_mod,
           ln1_g, w_in, conv_w, na_rpb, ml_gate_b, ml_norm_g, w_out, ln2_g, w_rg, b_rg, w_re, b_re,
           w_gate, w_up, w_down, final_g):
    nb_ctx = x_prompt.shape[0]
    nb_lat = x_sample.shape[0]
    assert 1 + nb_lat <= MODS_ROWS

    cvecs = jnp.concatenate([c_ctx[None, :], c,
                             jnp.zeros((MODS_ROWS - 1 - nb_lat, D_MODEL), F32)], axis=0)
    mods = _mods_call(cvecs, w_mod, b_mod)

    w_in_p = _pack_w_in(w_in)
    w_out_b = w_out.astype(BF16)
    gate_b = _pack_gate_bias(ml_gate_b)
    ln1 = ln1_g.reshape(DEPTH, 1, D_MODEL)
    ln2 = ln2_g.reshape(DEPTH, 1, D_MODEL)
    ml_g = ml_norm_g.reshape(DEPTH, 1, ML_DIM)
    w_router = _pad_lanes(jnp.concatenate([w_re, w_rg], axis=-1), LANES)
    b_router = _pad_lanes(jnp.concatenate([b_re, b_rg], axis=-1), LANES)[:, None, :]
    fg = final_g.reshape(1, D_MODEL)
    band = _rpb_band(na_rpb)
    cos_t, sin_t = _rope_tables()
    ck = cache_k.reshape(nb_lat, DEPTH, PAST_LEN, NA_DIM)
    cv = cache_v.reshape(nb_lat, DEPTH, PAST_LEN, NA_DIM)
    st_n = state_n.reshape(nb_lat, DEPTH, N_DIRS, ML_HEADS, 1, ML_HEAD_DIM)
    st_m = jnp.broadcast_to(state_m[..., None, None], (nb_lat, DEPTH, N_DIRS, ML_HEADS, 1, LANES))

    xp = x_prompt
    xs = x_sample.reshape(nb_lat * DEC_SEQ, D_MODEL)
    states = None
    lat_chunks = DEC_SEQ // DSP_CH
    moe_p = moe_s = None
    for l in range(DEPTH):
        xp, states = _ctx_mixer_call(l, xp, moe_p, mods, ln1, w_in_p, conv_w, gate_b, ml_g, w_out_b,
                                     states)
        moe_p = _sparse_moe(l, xp.reshape(nb_ctx * SEQ, D_MODEL), mods, ln2, w_router, b_router,
                            w_gate, w_up, w_down, None)

        xs, zc, zn, zm, zg = _lat_proj_call(l, xs, moe_s, mods, ln1, w_in_p, gate_b)
        loc = _lat_local_call(l, zc, zn, ck, cv, band, conv_w)
        ml = _lat_mlstm_call(l, zm, zg, state_C, st_n, st_m, cos_t, sin_t, ml_g)
        xs = _lat_merge_call(l, xs, loc, ml, mods, w_out_b)
        moe_s = _sparse_moe(l, xs, mods, ln2, w_router, b_router, w_gate, w_up, w_down, lat_chunks)

    xp = _dsp_final_call(DEPTH - 1, xp.reshape(nb_ctx * SEQ, D_MODEL), moe_p, mods, fg, None)
    xs = _dsp_final_call(DEPTH - 1, xs, moe_s, mods, fg, lat_chunks)
    xp = xp.reshape(nb_ctx, SEQ, D_MODEL)

    new_k, new_v, new_c, new_n, new_m = states
    return (xp, xs.reshape(nb_lat, DEC_SEQ, D_MODEL),
            new_k.reshape(nb_ctx, DEPTH, SEQ, NA_HEADS, NA_HEAD_DIM),
            new_v.reshape(nb_ctx, DEPTH, SEQ, NA_HEADS, NA_HEAD_DIM),
            new_c,
            new_n.reshape(nb_ctx, DEPTH, N_DIRS, ML_HEADS, ML_HEAD_DIM),
            new_m[..., 0].reshape(nb_ctx, DEPTH, N_DIRS, ML_HEADS))
```

```python
import functools

import numpy as np
import jax
import jax.numpy as jnp
from jax import lax
from jax.experimental import pallas as pl
from jax.experimental.pallas import tpu as pltpu

F32 = jnp.float32
BF16 = jnp.bfloat16

D_MODEL = 1024
SEQ = 256
DEPTH = 4
DEC_SEQ = 1024
PAST_LEN = 512
GRID_W = 64
CONV_DIM = 256
NA_HEADS = 4
NA_HEAD_DIM = 64
NA_DIM = NA_HEADS * NA_HEAD_DIM
NA_WIN_ROWS = 8
NA_WIN_COLS = 16
ML_HEADS = 4
ML_HEAD_DIM = 128
ML_DIM = ML_HEADS * ML_HEAD_DIM
N_DIRS = 2
N_GROUPS = 4
EXPERTS_PER_GROUP = 4
N_EXPERTS = N_GROUPS * EXPERTS_PER_GROUP
EXPERT_FF = 512
ROPE_BASE = 10000.0
EPS = 1e-6
N_MOD = 6

LANES = 128
N_GATE = N_DIRS * ML_HEADS
C_CONV = 0
C_NA = 3 * CONV_DIM
C_ML = C_NA + 3 * NA_DIM
C_IG = C_ML + 4 * ML_DIM
CUMSUM_BLOCK = 256
MODS_ROWS = 8
VMEM_LIMIT = 56 * 1024 * 1024

NEG_INF = float("-inf")
HIGHEST = lax.Precision.HIGHEST


def _dot(a, b):
    return jnp.dot(a, b, preferred_element_type=F32)


def _dot_nt(a, b):
    return lax.dot_general(a, b, (((1,), (1,)), ((), ())), preferred_element_type=F32)


def _dot_split(a, b):
    a_hi = a.astype(BF16)
    a_lo = (a - a_hi.astype(F32)).astype(BF16)
    b_hi = b.astype(BF16)
    b_lo = (b - b_hi.astype(F32)).astype(BF16)
    return _dot(jnp.concatenate([a_hi, a_hi, a_lo], axis=1), jnp.concatenate([b_hi, b_lo, b_hi], axis=0))


def _rms_mod(x, g, sc, sh):
    y = x * lax.rsqrt(jnp.mean(x * x, axis=-1, keepdims=True) + EPS)
    return (y * g) * (1.0 + sc) + sh


def _mod_rows(mods_ref, row, first):
    return [mods_ref[pl.ds(row, 1), pl.ds((first + j) * D_MODEL, D_MODEL)] for j in range(3)]


def _mods_kernel(cv_ref, w_ref, b_ref, o_ref):
    cv = cv_ref[...]
    s = cv * jax.nn.sigmoid(cv)
    o_ref[...] = _dot(s.astype(BF16), w_ref[...].astype(BF16)) + b_ref[...]


def _mods_call(cvecs, w_mod, b_mod):
    tn = 1536
    n = N_MOD * D_MODEL
    return pl.pallas_call(
        _mods_kernel,
        grid=(DEPTH, n // tn),
        in_specs=[pl.BlockSpec((MODS_ROWS, D_MODEL), lambda l, j: (0, 0)),
                  pl.BlockSpec((None, D_MODEL, tn), lambda l, j: (l, 0, j)),
                  pl.BlockSpec((None, 1, tn), lambda l, j: (l, 0, j))],
        out_specs=pl.BlockSpec((None, MODS_ROWS, tn), lambda l, j: (l, 0, j)),
        out_shape=jax.ShapeDtypeStruct((DEPTH, MODS_ROWS, n), F32),
        compiler_params=pltpu.CompilerParams(vmem_limit_bytes=VMEM_LIMIT),
        name="adaln_mods",
    )(cvecs, w_mod, b_mod.reshape(DEPTH, 1, n))


def _short_conv(zc, cw):
    T = zc.shape[0]
    cb = zc[:, 0:CONV_DIM]
    u = zc[:, CONV_DIM:2 * CONV_DIM] * zc[:, 2 * CONV_DIM:3 * CONV_DIM]
    t = lax.broadcasted_iota(jnp.int32, u.shape, 0)
    u_prev = jnp.where(t == 0, 0.0, pltpu.roll(u, 1, axis=0))
    u_next = jnp.where(t == T - 1, 0.0, pltpu.roll(u, T - 1, axis=0))
    return cb * (cw[0:1, :] * u_prev + cw[1:2, :] * u + cw[2:3, :] * u_next)


def _with_ones(v):
    return jnp.concatenate([v, jnp.ones_like(v)], axis=-1)


def _softmax_attention(nq, nk, nv):
    outs = []
    for h in range(NA_HEADS):
        sl = slice(h * NA_HEAD_DIM, (h + 1) * NA_HEAD_DIM)
        q = (nq[:, sl] * NA_HEAD_DIM ** -0.5).astype(BF16)
        s = _dot_nt(q, nk[:, sl].astype(BF16))
        p = jnp.exp(s - jnp.max(s, axis=-1, keepdims=True))
        o = _dot(p.astype(BF16), nv[:, sl].astype(BF16)) / jnp.sum(p, axis=-1, keepdims=True)
        outs.append(o)
    return jnp.concatenate(outs, axis=-1)


def _log_sigmoid(x):
    return jnp.minimum(x, 0.0) - jnp.log(1.0 + jnp.exp(-jnp.abs(x)))


def _gate_terms(zi, zf):
    T = zi.shape[0]
    lf = _log_sigmoid(zf)
    blk = min(T, CUMSUM_BLOCK)
    r = lax.broadcasted_iota(jnp.int32, (blk, blk), 0)
    c = lax.broadcasted_iota(jnp.int32, (blk, blk), 1)
    tril = jnp.where(c <= r, 1.0, 0.0).astype(F32)
    parts, carry = [], None
    for r0 in range(0, T, blk):
        b = jnp.dot(tril, lf[r0:r0 + blk], precision=HIGHEST, preferred_element_type=F32)
        if carry is not None:
            b = b + carry
        carry = b[blk - 1:blk, :]
        parts.append(b)
    b_fwd = parts[0] if len(parts) == 1 else jnp.concatenate(parts, axis=0)
    b_bwd = (b_fwd[T - 1:T, :] - b_fwd) + lf
    lane = lax.broadcasted_iota(jnp.int32, zi.shape, 1)
    b_col = jnp.where(lane < ML_HEADS, b_fwd, b_bwd)
    row_t = jnp.transpose(zi - b_col)
    return b_col, row_t


def _mlstm_outputs(q, k, v, b_col, li_row, m0, c0, n0, backward, q_block):
    T = q.shape[0]
    qb, kb, vb = q.astype(BF16), k.astype(BF16), v.astype(BF16)
    c0b = None if c0 is None else c0.astype(BF16)
    outs = []
    for r0 in range(0, T, q_block):
        ks, ke = (r0, T) if backward else (0, r0 + q_block)
        bq = b_col[r0:r0 + q_block]
        d = bq + li_row[:, ks:ke]
        t_idx = r0 + lax.broadcasted_iota(jnp.int32, d.shape, 0)
        s_idx = ks + lax.broadcasted_iota(jnp.int32, d.shape, 1)
        d = jnp.where((s_idx >= t_idx) if backward else (s_idx <= t_idx), d, NEG_INF)
        inter = bq + m0
        m_t = jnp.maximum(jnp.max(d, axis=-1, keepdims=True), inter)
        w = jnp.exp(d - m_t)
        s = _dot_nt(qb[r0:r0 + q_block], kb[ks:ke]) * w
        num = _dot(s.astype(BF16), vb[ks:ke])
        den = jnp.sum(s, axis=-1, keepdims=True)
        if c0 is not None:
            a = jnp.exp(inter - m_t)
            num = num + a * _dot(qb[r0:r0 + q_block], c0b)
            den = den + a * jnp.sum(q[r0:r0 + q_block] * n0, axis=-1, keepdims=True)
        outs.append(num / jnp.maximum(jnp.abs(den), jnp.exp(-m_t)))
    return outs[0] if len(outs) == 1 else jnp.concatenate(outs, axis=0)


def _mlstm_state(k, v, b_col, li_col, backward):
    T = k.shape[0]
    b_tot = b_col[0:1] if backward else b_col[T - 1:T]
    g = (b_tot - b_col) + li_col
    m_new = jnp.maximum(b_tot, jnp.max(g, axis=0, keepdims=True))
    kw = k * jnp.exp(g - m_new)
    c_new = lax.dot_general(kw.astype(BF16), v.astype(BF16), (((0,), (0,)), ((), ())),
                            preferred_element_type=F32)
    n_new = jnp.sum(kw, axis=0, keepdims=True)
    return c_new, n_new, m_new


def _head_readout(h, mo, g):
    hn = h * lax.rsqrt(jnp.mean(h * h, axis=-1, keepdims=True) + EPS)
    return hn * g * jax.nn.sigmoid(mo)


def _ctx_mixer_kernel(x_ref, mods_ref, ln_ref, win_ref, wgate_ref, convw_ref, gb_ref, mlg_ref, wout_ref,
                      *rest, has_moe):
    xo_ref, ko_ref, vo_ref, co_ref, no_ref, mo_ref = rest[-6:]
    T = SEQ
    sh1, sc1, g1 = _mod_rows(mods_ref, 0, 0)
    x = x_ref[...]
    if has_moe:
        moe_ref, mods_prev_ref = rest[0], rest[1]
        x = x + _mod_rows(mods_prev_ref, 0, 3)[2] * _from_slabs(moe_ref, T)
    h = _rms_mod(x, ln_ref[...], sc1, sh1).astype(BF16)

    conv_o = _short_conv(_dot(h, win_ref[:, C_CONV:C_NA]), convw_ref[...])

    zn = _dot(h, win_ref[:, C_NA:C_ML])
    nk, nv = zn[:, NA_DIM:2 * NA_DIM], zn[:, 2 * NA_DIM:3 * NA_DIM]
    ko_ref[...] = nk
    vo_ref[...] = nv
    na_o = _softmax_attention(zn[:, 0:NA_DIM], nk, nv)

    zg = _dot(h, wgate_ref[...])
    zi = zg[:, 0:LANES] + gb_ref[:, 0:LANES]
    zf = zg[:, LANES:2 * LANES] + gb_ref[:, LANES:2 * LANES]
    b_col, row_t = _gate_terms(zi, zf)
    m0 = jnp.zeros((1, 1), F32)

    zm = _dot(h, win_ref[:, C_ML:C_IG])
    ml_parts = []
    for hh in range(ML_HEADS):
        zq, zk, zv, zo = (zm[:, p * ML_DIM + hh * ML_HEAD_DIM:p * ML_DIM + (hh + 1) * ML_HEAD_DIM]
                          for p in range(4))
        zk = zk * ML_HEAD_DIM ** -0.5
        hsum = None
        for d in range(N_DIRS):
            j = d * ML_HEADS + hh
            bc = b_col[:, j:j + 1]
            ho = _mlstm_outputs(zq, zk, zv, bc, row_t[j:j + 1, :], m0, None, None, d == 1, T)
            hsum = ho if hsum is None else hsum + ho
            c_new, n_new, m_new = _mlstm_state(zk, zv, bc, zi[:, j:j + 1], d == 1)
            co_ref[d, hh] = c_new
            no_ref[j:j + 1, :] = n_new
            mo_ref[j:j + 1, :] = jnp.broadcast_to(m_new, (1, LANES))
        ml_parts.append(_head_readout(hsum, zo, mlg_ref[:, hh * ML_HEAD_DIM:(hh + 1) * ML_HEAD_DIM]))

    mix = jnp.concatenate([conv_o, na_o] + ml_parts, axis=-1).astype(BF16)
    xo_ref[...] = x + g1 * _dot(mix, wout_ref[...])


def _ctx_mixer_call(layer, x, moe, mods, ln1, w_in, conv_w, gate_b, ml_g, w_out, prev):
    B = x.shape[0]
    T = SEQ
    state_shapes = [
        jax.ShapeDtypeStruct((B, DEPTH, T, NA_DIM), F32),
        jax.ShapeDtypeStruct((B, DEPTH, T, NA_DIM), F32),
        jax.ShapeDtypeStruct((B, DEPTH, N_DIRS, ML_HEADS, ML_HEAD_DIM, ML_HEAD_DIM), F32),
        jax.ShapeDtypeStruct((B, DEPTH, N_GATE, ML_HEAD_DIM), F32),
        jax.ShapeDtypeStruct((B, DEPTH, N_GATE, LANES), F32),
    ]
    in_specs = [
        pl.BlockSpec((None, T, D_MODEL), lambda b: (b, 0, 0)),
        pl.BlockSpec((None, MODS_ROWS, N_MOD * D_MODEL), lambda b: (layer, 0, 0)),
        pl.BlockSpec((None, 1, D_MODEL), lambda b: (layer, 0, 0)),
        pl.BlockSpec((None, D_MODEL, C_IG), lambda b: (layer, 0, 0)),
        pl.BlockSpec((None, D_MODEL, 2 * LANES), lambda b: (layer, 0, 0)),
        pl.BlockSpec((None, 3, CONV_DIM), lambda b: (layer, 0, 0)),
        pl.BlockSpec((None, 1, 2 * LANES), lambda b: (layer, 0, 0)),
        pl.BlockSpec((None, 1, ML_DIM), lambda b: (layer, 0, 0)),
        pl.BlockSpec((None, D_MODEL, D_MODEL), lambda b: (layer, 0, 0)),
    ]
    args = [x, mods, ln1, w_in[0], w_in[1], conv_w, gate_b, ml_g, w_out]
    if moe is not None:
        per_sb = DSP_SB // T
        in_specs += [pl.BlockSpec((None, T * SLAB, LANES), lambda b: (b // per_sb, b % per_sb, 0)),
                     pl.BlockSpec((None, MODS_ROWS, N_MOD * D_MODEL), lambda b: (layer - 1, 0, 0))]
        args += [moe, mods]
    aliases = {}
    if prev is not None:
        in_specs += [pl.BlockSpec(memory_space=pl.ANY)] * len(prev)
        aliases = {len(args) + i: 1 + i for i in range(len(prev))}
        args += list(prev)
    out_specs = [
        pl.BlockSpec((None, T, D_MODEL), lambda b: (b, 0, 0)),
        pl.BlockSpec((None, None, T, NA_DIM), lambda b: (b, layer, 0, 0)),
        pl.BlockSpec((None, None, T, NA_DIM), lambda b: (b, layer, 0, 0)),
        pl.BlockSpec((None, None, N_DIRS, ML_HEADS, ML_HEAD_DIM, ML_HEAD_DIM),
                     lambda b: (b, layer, 0, 0, 0, 0)),
        pl.BlockSpec((None, None, N_GATE, ML_HEAD_DIM), lambda b: (b, layer, 0, 0)),
        pl.BlockSpec((None, None, N_GATE, LANES), lambda b: (b, layer, 0, 0)),
    ]
    outs = pl.pallas_call(
        functools.partial(_ctx_mixer_kernel, has_moe=moe is not None),
        grid=(B,),
        in_specs=in_specs,
        out_specs=out_specs,
        out_shape=[jax.ShapeDtypeStruct(x.shape, F32)] + state_shapes,
        input_output_aliases=aliases,
        compiler_params=pltpu.CompilerParams(vmem_limit_bytes=VMEM_LIMIT),
        name="ctx_mixer",
    )(*args)
    return outs[0], tuple(outs[1:])


def _route(lg):
    lane = lax.broadcasted_iota(jnp.int32, lg.shape, 1)
    lane_f = lane.astype(F32)
    big = float(LANES)
    is_g = (lane >= N_EXPERTS) & (lane < N_EXPERTS + N_GROUPS)
    glm = jnp.where(is_g, lg, NEG_INF)
    gmax = jnp.max(glm, axis=-1, keepdims=True)
    g_top = jnp.min(jnp.where(glm == gmax, lane_f, big), axis=-1, keepdims=True) - N_EXPERTS
    gp = 1.0 / jnp.sum(jnp.where(is_g, jnp.exp(lg - gmax), 0.0), axis=-1, keepdims=True)
    grp = (lane >> (EXPERTS_PER_GROUP.bit_length() - 1)).astype(F32)
    in_grp = (lane < N_EXPERTS) & (grp == g_top)
    e1 = jnp.where(in_grp, lg, NEG_INF)
    v1 = jnp.max(e1, axis=-1, keepdims=True)
    i1 = jnp.min(jnp.where(e1 == v1, lane_f, big), axis=-1, keepdims=True)
    e2 = jnp.where(in_grp & (lane_f != i1), lg, NEG_INF)
    v2 = jnp.max(e2, axis=-1, keepdims=True)
    i2 = jnp.min(jnp.where(e2 == v2, lane_f, big), axis=-1, keepdims=True)
    t = jnp.exp(v2 - v1)
    w1 = 1.0 / (1.0 + t)
    w2 = t / (1.0 + t)
    return i1, i2, gp * w1, gp * w2


SLAB = D_MODEL // LANES
DSP_SB = 2048
DSP_TM = 160
DSP_CH = 512
DSP_PMAX = 2 * DSP_SB + N_EXPERTS * DSP_TM
DSP_NULL = DSP_PMAX + 2 * DSP_TM
DSP_SLOTS = DSP_NULL + 2 * DSP_TM
DSP_PAD_PAIR = 2 * DSP_SB
DSP_W_LEN = 2 * DSP_SB + LANES
DSP_UNROLL = 8


def _to_slabs(ref, x):
    n = x.shape[0]
    for s in range(SLAB):
        ref[pl.ds(s, n, stride=SLAB), :] = x[:, s * LANES:(s + 1) * LANES]


def _from_slabs(ref, n):
    return jnp.concatenate([ref[pl.ds(s, n, stride=SLAB), :] for s in range(SLAB)], axis=-1)


def _dsp_mods(mods_ref, chunk, chunks_per_row, first):
    row = 0 if chunks_per_row is None else 1 + chunk // chunks_per_row
    return _mod_rows(mods_ref, row, first)


def _dsp_route_kernel(x_ref, mods_ref, ln_ref, wr_ref, br_ref, h_ref, r_ref, cnt_ref, run_scr, *,
                      chunks_per_row):
    c = pl.program_id(0)
    ch = DSP_CH

    @pl.when(c % (DSP_SB // ch) == 0)
    def _():
        run_scr[...] = jnp.zeros_like(run_scr)

    sh2, sc2, _ = _dsp_mods(mods_ref, c, chunks_per_row, 3)
    h2 = _rms_mod(x_ref[...], ln_ref[...], sc2, sh2)
    _to_slabs(h_ref, h2)
    lg = _dot_split(h2, wr_ref[...]) + br_ref[...]
    i1, i2, w1, w2 = _route(lg)
    lane = lax.broadcasted_iota(jnp.int32, lg.shape, 1)
    lane_f = lane.astype(F32)
    oh1 = jnp.where(lane_f == i1, 1.0, 0.0)
    oh2 = jnp.where(lane_f == i2, 1.0, 0.0)
    both = oh1 + oh2
    r = lax.broadcasted_iota(jnp.int32, (ch, ch), 0)
    s = lax.broadcasted_iota(jnp.int32, (ch, ch), 1)
    earlier = jnp.where(s < r, 1.0, 0.0).astype(BF16)
    run = run_scr[0:1, :]
    before = _dot(earlier, both.astype(BF16)) + run
    rank1 = jnp.sum(before * oh1, axis=-1, keepdims=True)
    rank2 = jnp.sum(before * oh2, axis=-1, keepdims=True)
    run = run + jnp.sum(both, axis=0, keepdims=True)
    run_scr[0:1, :] = run
    cnt_ref[...] = jnp.broadcast_to(run, cnt_ref.shape)
    cols = (i1, i2, w1, w2, rank1, rank2)
    out = jnp.zeros(lg.shape, F32)
    for j, col in enumerate(cols):
        out = jnp.where(lane == j, col, out)
    r_ref[...] = out


def _dsp_route_call(layer, x2d, mods, ln2, w_router, b_router, chunks_per_row):
    n = x2d.shape[0]
    ch = DSP_CH
    per_sb = DSP_SB // ch
    return pl.pallas_call(
        functools.partial(_dsp_route_kernel, chunks_per_row=chunks_per_row),
        grid=(n // ch,),
        in_specs=[
            pl.BlockSpec((ch, D_MODEL), lambda c: (c, 0)),
            pl.BlockSpec((None, MODS_ROWS, N_MOD * D_MODEL), lambda c: (layer, 0, 0)),
            pl.BlockSpec((None, 1, D_MODEL), lambda c: (layer, 0, 0)),
            pl.BlockSpec((None, D_MODEL, LANES), lambda c: (layer, 0, 0)),
            pl.BlockSpec((None, 1, LANES), lambda c: (layer, 0, 0)),
        ],
        out_specs=[pl.BlockSpec((ch * SLAB, LANES), lambda c: (c, 0)),
                   pl.BlockSpec((ch, LANES), lambda c: (c, 0)),
                   pl.BlockSpec((None, SLAB, LANES), lambda c: (c // per_sb, 0, 0))],
        out_shape=[jax.ShapeDtypeStruct((n * SLAB, LANES), F32),
                   jax.ShapeDtypeStruct((n, LANES), F32),
                   jax.ShapeDtypeStruct((n // DSP_SB, SLAB, LANES), F32)],
        scratch_shapes=[pltpu.VMEM((SLAB, LANES), F32)],
        compiler_params=pltpu.CompilerParams(vmem_limit_bytes=VMEM_LIMIT),
        name="moe_route",
    )(x2d, mods, ln2, w_router, b_router)


def _dispatch_tables(route, counts, n_sb):
    e = route[:, 0:2].astype(jnp.int32).reshape(n_sb, 2 * DSP_SB)
    rank = route[:, 4:6].astype(jnp.int32).reshape(n_sb, 2 * DSP_SB)
    w = _pad_lanes(route[:, 2:4].reshape(n_sb, 1, 2 * DSP_SB), DSP_W_LEN)
    cnt = counts[:, 0, :N_EXPERTS].astype(jnp.int32)
    ntile = (cnt + (DSP_TM - 1)) // DSP_TM
    off = (jnp.cumsum(ntile, axis=1) - ntile) * DSP_TM
    is_e = e[:, None, :] == jnp.arange(N_EXPERTS, dtype=jnp.int32)[None, :, None]
    pos = (jnp.sum(jnp.where(is_e, off[:, :, None], 0), axis=1) + rank).reshape(n_sb, 1, 2 * DSP_SB)
    return pos, w, ntile.reshape(-1), off.reshape(-1), cnt.reshape(-1)


def _dsp_expert_kernel(ntile_ref, off_ref, cnt_ref, pos_ref, w_ref, h_ref, wg_ref, wu_ref, wd_ref,
                       o_ref, pair_smem, sched_smem, pend_smem, xs0, xs1, ys0, ys1,
                       wg_scr, wu_scr, wd_scr):
    sb = pl.program_id(0)
    e = pl.program_id(1)
    tm, u = DSP_TM, DSP_UNROLL
    xs_scr, ys_scr = (xs0, xs1), (ys0, ys1)
    slab_shift = SLAB.bit_length() - 1
    last_row = (DSP_SB - 1) * SLAB

    def slab_row(pair):
        return lax.shift_left(lax.shift_right_logical(pair, 1), slab_shift)

    @pl.when(e == 0)
    def _():
        for ee in range(N_EXPERTS):
            q = sb * N_EXPERTS + ee

            def pad(p, carry):
                pair_smem[p] = DSP_PAD_PAIR
                return carry

            lax.fori_loop(off_ref[q] + cnt_ref[q], off_ref[q] + ntile_ref[q] * tm, pad, 0)

        def invert(j, carry):
            for k in range(u):
                a = j * u + k
                pair_smem[pos_ref[0, a]] = a
            return carry

        lax.fori_loop(0, 2 * DSP_SB // u, invert, 0)

        last_q = sb * N_EXPERTS + (N_EXPERTS - 1)
        used_end = off_ref[last_q] + ntile_ref[last_q] * tm

        def pad_end(p, carry):
            pair_smem[used_end + p] = DSP_PAD_PAIR
            pair_smem[DSP_NULL + p] = DSP_PAD_PAIR
            return carry

        lax.fori_loop(0, 2 * tm, pad_end, 0)
        sched_smem[0] = -1
        sched_smem[1] = DSP_NULL
        pend_smem[0] = 0.0
        o_ref[...] = jnp.zeros_like(o_ref)
        for ys in ys_scr:
            ys[...] = jnp.zeros_like(ys)

    q = sb * N_EXPERTS + e
    base0 = off_ref[q]
    nt = ntile_ref[q]
    wg_scr[...] = wg_ref[...].astype(BF16)
    wu_scr[...] = wu_ref[...].astype(BF16)
    wd_scr[...] = wd_ref[...].astype(BF16)

    def gather(base, buf):
        for r in range(tm):
            src = jnp.minimum(slab_row(pair_smem[base + r]), last_row)
            xs_scr[buf][r * SLAB:(r + 1) * SLAB, :] = h_ref[pl.ds(pl.multiple_of(src, SLAB), SLAB), :]

    def compute(buf):
        x = _from_slabs(xs_scr[buf], tm).astype(BF16)
        a = jax.nn.silu(_dot(x, wg_scr[...])) * _dot(x, wu_scr[...])
        _to_slabs(ys_scr[buf], _dot(a.astype(BF16), wd_scr[...]))

    def scatter(base, buf, scale):
        for j in range(tm // u):
            upd = []
            for k in range(u):
                r = j * u + k
                pair = pair_smem[base + r]
                dst = pl.multiple_of(slab_row(pair), SLAB)
                y = ys_scr[buf][r * SLAB:(r + 1) * SLAB, :]
                upd.append((dst, o_ref[pl.ds(dst, SLAB), :] + (w_ref[0, pair] * scale) * y))
            for dst, v in upd:
                o_ref[pl.ds(dst, SLAB), :] = v

    n_pairs = lax.shift_right_logical(nt + 1, 1)

    @pl.when((n_pairs > 0) & (sched_smem[0] != base0))
    def _():
        gather(base0, 0)

    pend_base = sched_smem[1]
    pend_scale = pend_smem[0]

    def pair(k, carry):
        b = base0 + 2 * k * tm
        first = k == 0
        gather(b + tm, 1)
        compute(0)
        scatter(jnp.where(first, pend_base, b - tm), 1, jnp.where(first, pend_scale, 1.0))
        gather(b + 2 * tm, 0)
        compute(1)
        scatter(b, 0, 1.0)
        return carry

    lax.fori_loop(0, n_pairs, pair, 0)

    @pl.when(n_pairs > 0)
    def _():
        last = 2 * n_pairs - 1
        sched_smem[0] = base0 + (last + 1) * tm
        sched_smem[1] = base0 + last * tm
        pend_smem[0] = jnp.where(last < nt, 1.0, 0.0)

    @pl.when(e == N_EXPERTS - 1)
    def _():
        scatter(sched_smem[1], 1, pend_smem[0])


def _dsp_expert_call(layer, h_slabs, pos, w, ntile, off, cnt, w_gate, w_up, w_down):
    n_sb = pos.shape[0]
    out_rows = (DSP_SB + DSP_CH) * SLAB

    def per_sb(s, e, *prefetch):
        return (s, 0, 0)

    def w_idx(s, e, *prefetch):
        return (layer, e, 0, 0)

    grid_spec = pltpu.PrefetchScalarGridSpec(
        num_scalar_prefetch=3,
        grid=(n_sb, N_EXPERTS),
        in_specs=[
            pl.BlockSpec((None, 1, 2 * DSP_SB), per_sb, memory_space=pltpu.SMEM),
            pl.BlockSpec((None, 1, DSP_W_LEN), per_sb, memory_space=pltpu.SMEM),
            pl.BlockSpec((DSP_SB * SLAB, LANES), lambda s, e, *prefetch: (s, 0)),
            pl.BlockSpec((None, None, D_MODEL, EXPERT_FF), w_idx),
            pl.BlockSpec((None, None, D_MODEL, EXPERT_FF), w_idx),
            pl.BlockSpec((None, None, EXPERT_FF, D_MODEL), w_idx),
        ],
        out_specs=pl.BlockSpec((None, out_rows, LANES), per_sb),
        scratch_shapes=[pltpu.SMEM((DSP_SLOTS,), jnp.int32),
                        pltpu.SMEM((2,), jnp.int32),
                        pltpu.SMEM((1,), F32),
                        pltpu.VMEM((DSP_TM * SLAB, LANES), F32),
                        pltpu.VMEM((DSP_TM * SLAB, LANES), F32),
                        pltpu.VMEM((DSP_TM * SLAB, LANES), F32),
                        pltpu.VMEM((DSP_TM * SLAB, LANES), F32),
                        pltpu.VMEM((D_MODEL, EXPERT_FF), BF16),
                        pltpu.VMEM((D_MODEL, EXPERT_FF), BF16),
                        pltpu.VMEM((EXPERT_FF, D_MODEL), BF16)],
    )
    return pl.pallas_call(
        _dsp_expert_kernel,
        grid_spec=grid_spec,
        out_shape=jax.ShapeDtypeStruct((n_sb, out_rows, LANES), F32),
        compiler_params=pltpu.CompilerParams(vmem_limit_bytes=VMEM_LIMIT),
        name="moe_experts",
    )(ntile, off, cnt, pos, w, h_slabs, w_gate, w_up, w_down)


def _dsp_final_kernel(x_ref, m_ref, mods_ref, fg_ref, o_ref, *, chunks_per_row):
    g2 = _dsp_mods(mods_ref, pl.program_id(0), chunks_per_row, 3)[2]
    y = x_ref[...] + g2 * _from_slabs(m_ref, DSP_CH)
    o_ref[...] = y * lax.rsqrt(jnp.mean(y * y, axis=-1, keepdims=True) + EPS) * fg_ref[...]


def _dsp_final_call(layer, x2d, m_slabs, mods, final_g, chunks_per_row):
    n = x2d.shape[0]
    ch = DSP_CH
    per_sb = DSP_SB // ch
    return pl.pallas_call(
        functools.partial(_dsp_final_kernel, chunks_per_row=chunks_per_row),
        grid=(n // ch,),
        in_specs=[
            pl.BlockSpec((ch, D_MODEL), lambda c: (c, 0)),
            pl.BlockSpec((None, ch * SLAB, LANES), lambda c: (c // per_sb, c % per_sb, 0)),
            pl.BlockSpec((None, MODS_ROWS, N_MOD * D_MODEL), lambda c: (layer, 0, 0)),
            pl.BlockSpec((1, D_MODEL), lambda c: (0, 0)),
        ],
        out_specs=pl.BlockSpec((ch, D_MODEL), lambda c: (c, 0)),
        out_shape=jax.ShapeDtypeStruct(x2d.shape, F32),
        compiler_params=pltpu.CompilerParams(vmem_limit_bytes=VMEM_LIMIT),
        name="final_norm",
    )(x2d, m_slabs, mods, final_g)


def _sparse_moe(layer, x2d, mods, ln2, w_router, b_router, w_gate, w_up, w_down, chunks_per_row):
    n_sb = x2d.shape[0] // DSP_SB
    h_slabs, route, counts = _dsp_route_call(layer, x2d, mods, ln2, w_router, b_router,
                                             chunks_per_row)
    pos, w, ntile, off, cnt = _dispatch_tables(route, counts, n_sb)
    return _dsp_expert_call(layer, h_slabs, pos, w, ntile, off, cnt, w_gate, w_up, w_down)


LAT_PROJ_ROWS = 512


def _lat_proj_kernel(x_ref, mods_ref, ln_ref, win_ref, wgate_ref, gb_ref, *rest, has_moe):
    zc_ref, zn_ref, zm_ref, zg_ref = rest[-4:]
    row = 1 + pl.program_id(0) // (DEC_SEQ // LAT_PROJ_ROWS)
    sh1, sc1, _ = _mod_rows(mods_ref, row, 0)
    x = x_ref[...]
    if has_moe:
        moe_ref, mods_prev_ref, xo_ref = rest[0], rest[1], rest[2]
        x = x + _mod_rows(mods_prev_ref, row, 3)[2] * _from_slabs(moe_ref, LAT_PROJ_ROWS)
        xo_ref[...] = x
    h = _rms_mod(x, ln_ref[...], sc1, sh1).astype(BF16)
    zc_ref[...] = _dot(h, win_ref[:, C_CONV:C_NA])
    zn_ref[...] = _dot(h, win_ref[:, C_NA:C_ML])
    zm_ref[...] = _dot(h, win_ref[:, C_ML:C_IG])
    zg_ref[...] = _dot(h, wgate_ref[...]) + gb_ref[...]


def _lat_proj_call(layer, x2d, moe, mods, ln1, w_in, gate_b):
    n = x2d.shape[0]
    tb = LAT_PROJ_ROWS
    widths = (C_NA - C_CONV, C_ML - C_NA, C_IG - C_ML, 2 * LANES)
    in_specs = [
        pl.BlockSpec((tb, D_MODEL), lambda i: (i, 0)),
        pl.BlockSpec((None, MODS_ROWS, N_MOD * D_MODEL), lambda i: (layer, 0, 0)),
        pl.BlockSpec((None, 1, D_MODEL), lambda i: (layer, 0, 0)),
        pl.BlockSpec((None, D_MODEL, C_IG), lambda i: (layer, 0, 0)),
        pl.BlockSpec((None, D_MODEL, 2 * LANES), lambda i: (layer, 0, 0)),
        pl.BlockSpec((None, 1, 2 * LANES), lambda i: (layer, 0, 0)),
    ]
    args = [x2d, mods, ln1, w_in[0], w_in[1], gate_b]
    out_specs = [pl.BlockSpec((tb, w), lambda i: (i, 0)) for w in widths]
    out_shape = [jax.ShapeDtypeStruct((n, w), F32) for w in widths]
    if moe is not None:
        per_sb = DSP_SB // tb
        in_specs += [pl.BlockSpec((None, tb * SLAB, LANES), lambda i: (i // per_sb, i % per_sb, 0)),
                     pl.BlockSpec((None, MODS_ROWS, N_MOD * D_MODEL), lambda i: (layer - 1, 0, 0))]
        args += [moe, mods]
        out_specs = [pl.BlockSpec((tb, D_MODEL), lambda i: (i, 0))] + out_specs
        out_shape = [jax.ShapeDtypeStruct((n, D_MODEL), F32)] + out_shape
    outs = pl.pallas_call(
        functools.partial(_lat_proj_kernel, has_moe=moe is not None),
        grid=(n // tb,),
        in_specs=in_specs,
        out_specs=out_specs,
        out_shape=out_shape,
        compiler_params=pltpu.CompilerParams(vmem_limit_bytes=VMEM_LIMIT),
        name="lat_proj",
    )(*args)
    return tuple(outs) if moe is not None else (x2d,) + tuple(outs)


def _na_row_start(r):
    rows = DEC_SEQ // GRID_W
    return min(max(r - NA_WIN_ROWS // 2, 0), rows - NA_WIN_ROWS)


def _na_row_groups():
    rows = DEC_SEQ // GRID_W
    groups, lo = [], 0
    for r in range(1, rows + 1):
        if r == rows or _na_row_start(r) != _na_row_start(lo):
            groups.append((lo, r))
            lo = r
    return tuple(groups)


_NA_ROW_GROUPS = _na_row_groups()


def _lat_local_kernel(zc_ref, zn_ref, kc_ref, vc_ref, band_ref, convw_ref, o_ref):
    o_ref[:, 0:CONV_DIM] = _short_conv(zc_ref[...], convw_ref[...])
    blk = NA_WIN_ROWS * GRID_W
    q_all = (zn_ref[:, 0:NA_DIM] * NA_HEAD_DIM ** -0.5).astype(BF16)
    k_all = zn_ref[:, NA_DIM:2 * NA_DIM].astype(BF16)
    v_all = zn_ref[:, 2 * NA_DIM:3 * NA_DIM].astype(BF16)
    kc_all = kc_ref[...].astype(BF16)
    vc_all = vc_ref[...].astype(BF16)
    heads = []
    for h in range(NA_HEADS):
        sl = slice(h * NA_HEAD_DIM, (h + 1) * NA_HEAD_DIM)
        q, k, v, kc, vc = q_all[:, sl], k_all[:, sl], v_all[:, sl], kc_all[:, sl], vc_all[:, sl]
        band = band_ref[h]
        s_ctx = _dot_nt(q, kc)
        m_ctx = jnp.max(s_ctx, axis=-1, keepdims=True)
        v1 = _with_ones(v)
        m_rows, o_rows = [], []
        for r_lo, r_hi in _NA_ROW_GROUPS:
            start = _na_row_start(r_lo)
            bias = [band[:, (start - r + NA_WIN_ROWS - 1) * GRID_W:(start - r + NA_WIN_ROWS - 1) * GRID_W + blk]
                    for r in range(r_lo, r_hi)]
            bias = bias[0] if len(bias) == 1 else jnp.concatenate(bias, axis=0)
            q_g = q[r_lo * GRID_W:r_hi * GRID_W]
            s_loc = _dot_nt(q_g, k[start * GRID_W:start * GRID_W + blk]) + bias
            m = jnp.maximum(jnp.max(s_loc, axis=-1, keepdims=True), m_ctx[r_lo * GRID_W:r_hi * GRID_W])
            p_loc = jnp.exp(s_loc - m)
            m_rows.append(m)
            o_rows.append(_dot(p_loc.astype(BF16), v1[start * GRID_W:start * GRID_W + blk]))
        p_ctx = jnp.exp(s_ctx - jnp.concatenate(m_rows, axis=0))
        o = jnp.concatenate(o_rows, axis=0) + _dot(p_ctx.astype(BF16), _with_ones(vc))
        heads.append(o[:, 0:NA_HEAD_DIM] / o[:, NA_HEAD_DIM:NA_HEAD_DIM + 1])
    o_ref[:, CONV_DIM:CONV_DIM + NA_DIM] = jnp.concatenate(heads, axis=-1)


def _lat_local_call(layer, zc, zn, cache_k, cache_v, band, conv_w):
    nb = zc.shape[0] // DEC_SEQ
    T = DEC_SEQ
    return pl.pallas_call(
        _lat_local_kernel,
        grid=(nb,),
        in_specs=[
            pl.BlockSpec((T, 3 * CONV_DIM), lambda b: (b, 0)),
            pl.BlockSpec((T, 3 * NA_DIM), lambda b: (b, 0)),
            pl.BlockSpec((None, None, PAST_LEN, NA_DIM), lambda b: (b, layer, 0, 0)),
            pl.BlockSpec((None, None, PAST_LEN, NA_DIM), lambda b: (b, layer, 0, 0)),
            pl.BlockSpec((None, NA_HEADS, GRID_W, band.shape[-1]), lambda b: (layer, 0, 0, 0)),
            pl.BlockSpec((None, 3, CONV_DIM), lambda b: (layer, 0, 0)),
        ],
        out_specs=pl.BlockSpec((T, CONV_DIM + NA_DIM), lambda b: (b, 0)),
        out_shape=jax.ShapeDtypeStruct((zc.shape[0], CONV_DIM + NA_DIM), F32),
        compiler_params=pltpu.CompilerParams(vmem_limit_bytes=VMEM_LIMIT),
        name="lat_conv_na",
    )(zc, zn, cache_k, cache_v, band, conv_w)


def _rope(x, cos, sin_signed):
    w = x.shape[-1]
    half = NA_HEAD_DIM // 2
    lane = lax.broadcasted_iota(jnp.int32, x.shape, 1)
    partner = jnp.where(lane % (2 * half) < half, pltpu.roll(x, w - half, axis=1), pltpu.roll(x, half, axis=1))
    return x * cos + partner * sin_signed


def _lat_mlstm_kernel(q_ref, k_ref, v_ref, o_ref, zg_ref, c0_ref, n0_ref, m0_ref, cos_ref, sin_ref,
                      mlg_ref, out_ref, bcol_scr, rowt_scr):
    hh = pl.program_id(1)

    @pl.when(hh == 0)
    def _():
        b_all, r_all = _gate_terms(zg_ref[:, 0:LANES], zg_ref[:, LANES:2 * LANES])
        bcol_scr[...] = b_all
        rowt_scr[...] = r_all

    b_col = bcol_scr[...]
    cos, sin_signed = cos_ref[...], sin_ref[...]
    q = _rope(q_ref[...], cos, sin_signed)
    k = _rope(k_ref[...], cos, sin_signed) * ML_HEAD_DIM ** -0.5
    v = v_ref[...]
    lane = lax.broadcasted_iota(jnp.int32, (DEC_SEQ, LANES), 1)
    hsum = None
    for d in range(N_DIRS):
        j = d * ML_HEADS + hh
        bc = jnp.sum(jnp.where(lane == j, b_col, 0.0), axis=-1, keepdims=True)
        lr = rowt_scr[pl.ds(j, 1), :]
        m0 = m0_ref[d, :, 0:1]
        ho = _mlstm_outputs(q, k, v, bc, lr, m0, c0_ref[d], n0_ref[d], d == 1, 256)
        hsum = ho if hsum is None else hsum + ho
    out_ref[...] = _head_readout(hsum, o_ref[...], mlg_ref[...])


def _lat_mlstm_call(layer, zm, zg, state_c, state_n, state_m, cos, sin_signed, ml_g):
    nb = zm.shape[0] // DEC_SEQ
    T = DEC_SEQ
    hd = ML_HEAD_DIM

    def col(part):
        return pl.BlockSpec((T, hd), lambda b, h: (b, part * ML_HEADS + h))

    return pl.pallas_call(
        _lat_mlstm_kernel,
        grid=(nb, ML_HEADS),
        in_specs=[
            col(0), col(1), col(2), col(3),
            pl.BlockSpec((T, 2 * LANES), lambda b, h: (b, 0)),
            pl.BlockSpec((None, None, N_DIRS, None, hd, hd), lambda b, h: (b, layer, 0, h, 0, 0)),
            pl.BlockSpec((None, None, N_DIRS, None, 1, hd), lambda b, h: (b, layer, 0, h, 0, 0)),
            pl.BlockSpec((None, None, N_DIRS, None, 1, LANES), lambda b, h: (b, layer, 0, h, 0, 0)),
            pl.BlockSpec((T, hd), lambda b, h: (0, 0)),
            pl.BlockSpec((T, hd), lambda b, h: (0, 0)),
            pl.BlockSpec((None, 1, hd), lambda b, h: (layer, 0, h)),
        ],
        out_specs=pl.BlockSpec((T, hd), lambda b, h: (b, h)),
        out_shape=jax.ShapeDtypeStruct((zm.shape[0], ML_DIM), F32),
        scratch_shapes=[pltpu.VMEM((T, LANES), F32), pltpu.VMEM((LANES, T), F32)],
        compiler_params=pltpu.CompilerParams(vmem_limit_bytes=VMEM_LIMIT),
        name="lat_mlstm",
    )(zm, zm, zm, zm, zg, state_c, state_n, state_m, cos, sin_signed, ml_g)


def _lat_merge_kernel(x_ref, loc_ref, ml_ref, mods_ref, wout_ref, o_ref):
    row = 1 + pl.program_id(0) // (DEC_SEQ // LAT_PROJ_ROWS)
    g1 = _mod_rows(mods_ref, row, 0)[2]
    split = CONV_DIM + NA_DIM
    y = _dot(loc_ref[...].astype(BF16), wout_ref[0:split, :]) + _dot(ml_ref[...].astype(BF16), wout_ref[split:, :])
    o_ref[...] = x_ref[...] + g1 * y


def _lat_merge_call(layer, x2d, loc, ml, mods, w_out):
    n = x2d.shape[0]
    tb = LAT_PROJ_ROWS
    return pl.pallas_call(
        _lat_merge_kernel,
        grid=(n // tb,),
        in_specs=[
            pl.BlockSpec((tb, D_MODEL), lambda i: (i, 0)),
            pl.BlockSpec((tb, CONV_DIM + NA_DIM), lambda i: (i, 0)),
            pl.BlockSpec((tb, ML_DIM), lambda i: (i, 0)),
            pl.BlockSpec((None, MODS_ROWS, N_MOD * D_MODEL), lambda i: (layer, 0, 0)),
            pl.BlockSpec((None, D_MODEL, D_MODEL), lambda i: (layer, 0, 0)),
        ],
        out_specs=pl.BlockSpec((tb, D_MODEL), lambda i: (i, 0)),
        out_shape=jax.ShapeDtypeStruct(x2d.shape, F32),
        compiler_params=pltpu.CompilerParams(vmem_limit_bytes=VMEM_LIMIT),
        name="lat_merge",
    )(x2d, loc, ml, mods, w_out)


def _pad_lanes(a, width):
    return jnp.pad(a, [(0, 0)] * (a.ndim - 1) + [(0, width - a.shape[-1])])


def _pack_w_in(w_in):
    w_b = w_in.astype(BF16)
    gates = w_b[..., C_IG:]
    gates = jnp.concatenate([_pad_lanes(gates[..., :N_GATE], LANES),
                             _pad_lanes(gates[..., N_GATE:], LANES)], axis=-1)
    return w_b, gates


def _pack_gate_bias(ml_gate_b):
    gb = ml_gate_b.reshape(DEPTH, 2, N_GATE).astype(F32)
    return jnp.concatenate([_pad_lanes(gb[:, 0], LANES), _pad_lanes(gb[:, 1], LANES)], axis=-1)[:, None, :]


def _rpb_band(na_rpb):
    cols = np.arange(GRID_W)
    col_idx = np.clip(cols[None, :] - cols[:, None] + NA_WIN_COLS - 1, 0, 2 * NA_WIN_COLS - 2)
    col_start = np.clip(cols - NA_WIN_COLS // 2, 0, GRID_W - NA_WIN_COLS)
    col_mask = (cols[None, :] >= col_start[:, None]) & (cols[None, :] < col_start[:, None] + NA_WIN_COLS)
    pick = (col_idx[None] == np.arange(2 * NA_WIN_COLS - 1)[:, None, None]).astype(np.float32)
    t = jnp.einsum('lhri,iqk->lhqrk', na_rpb.astype(F32), jnp.asarray(pick), precision=HIGHEST)
    t = jnp.where(col_mask[None, None, :, None, :], t, NEG_INF)
    t = t.reshape(DEPTH, NA_HEADS, GRID_W, (2 * NA_WIN_ROWS - 1) * GRID_W)
    return _pad_lanes(t, 2 * NA_WIN_ROWS * GRID_W)


def _rope_tables():
    t = np.arange(DEC_SEQ)
    pos = np.stack([t // GRID_W, t % GRID_W], axis=-1).astype(np.float32)
    nf = ML_HEAD_DIM // 4
    inv = jnp.asarray(ROPE_BASE, F32) ** (-jnp.arange(nf, dtype=F32) / nf)
    ang = jnp.asarray(pos)[:, :, None] * inv
    cos = jnp.cos(ang)
    sin = jnp.sin(ang)
    cos_t = jnp.concatenate([cos, cos], axis=-1).reshape(DEC_SEQ, ML_HEAD_DIM)
    sin_t = jnp.concatenate([-sin, sin], axis=-1).reshape(DEC_SEQ, ML_HEAD_DIM)
    return cos_t, sin_t


def kernel(x_prompt, x_sample, cache_k, cache_v, state_C, state_n, state_m, c, c_ctx, w_mod, b_mod,
           ln1_g, w_in, conv_w, na_rpb, ml_gate_b, ml_norm_g, w_out, ln2_g, w_rg, b_rg, w_re, b_re,
           w_gate, w_up, w_down, final_g):
    nb_ctx = x_prompt.shape[0]
    nb_lat = x_sample.shape[0]
    assert 1 + nb_lat <= MODS_ROWS

    cvecs = jnp.concatenate([c_ctx[None, :], c,
                             jnp.zeros((MODS_ROWS - 1 - nb_lat, D_MODEL), F32)], axis=0)
    mods = _mods_call(cvecs, w_mod, b_mod)

    w_in_p = _pack_w_in(w_in)
    w_out_b = w_out.astype(BF16)
    gate_b = _pack_gate_bias(ml_gate_b)
    ln1 = ln1_g.reshape(DEPTH, 1, D_MODEL)
    ln2 = ln2_g.reshape(DEPTH, 1, D_MODEL)
    ml_g = ml_norm_g.reshape(DEPTH, 1, ML_DIM)
    w_router = _pad_lanes(jnp.concatenate([w_re, w_rg], axis=-1), LANES)
    b_router = _pad_lanes(jnp.concatenate([b_re, b_rg], axis=-1), LANES)[:, None, :]
    fg = final_g.reshape(1, D_MODEL)
    band = _rpb_band(na_rpb)
    cos_t, sin_t = _rope_tables()
    ck = cache_k.reshape(nb_lat, DEPTH, PAST_LEN, NA_DIM)
    cv = cache_v.reshape(nb_lat, DEPTH, PAST_LEN, NA_DIM)
    st_n = state_n.reshape(nb_lat, DEPTH, N_DIRS, ML_HEADS, 1, ML_HEAD_DIM)
    st_m = jnp.broadcast_to(state_m[..., None, None], (nb_lat, DEPTH, N_DIRS, ML_HEADS, 1, LANES))

    xp = x_prompt
    xs = x_sample.reshape(nb_lat * DEC_SEQ, D_MODEL)
    states = None
    lat_chunks = DEC_SEQ // DSP_CH
    moe_p = moe_s = None
    for l in range(DEPTH):
        xp, states = _ctx_mixer_call(l, xp, moe_p, mods, ln1, w_in_p, conv_w, gate_b, ml_g, w_out_b,
                                     states)
        moe_p = _sparse_moe(l, xp.reshape(nb_ctx * SEQ, D_MODEL), mods, ln2, w_router, b_router,
                            w_gate, w_up, w_down, None)

        xs, zc, zn, zm, zg = _lat_proj_call(l, xs, moe_s, mods, ln1, w_in_p, gate_b)
        loc = _lat_local_call(l, zc, zn, ck, cv, band, conv_w)
        ml = _lat_mlstm_call(l, zm, zg, state_C, st_n, st_m, cos_t, sin_t, ml_g)
        xs = _lat_merge_call(l, xs, loc, ml, mods, w_out_b)
        moe_s = _sparse_moe(l, xs, mods, ln2, w_router, b_router, w_gate, w_up, w_down, lat_chunks)

    xp = _dsp_final_call(DEPTH - 1, xp.reshape(nb_ctx * SEQ, D_MODEL), moe_p, mods, fg, None)
    xs = _dsp_final_call(DEPTH - 1, xs, moe_s, mods, fg, lat_chunks)
    xp = xp.reshape(nb_ctx, SEQ, D_MODEL)

    new_k, new_v, new_c, new_n, new_m = states
    return (xp, xs.reshape(nb_lat, DEC_SEQ, D_MODEL),
            new_k.reshape(nb_ctx, DEPTH, SEQ, NA_HEADS, NA_HEAD_DIM),
            new_v.reshape(nb_ctx, DEPTH, SEQ, NA_HEADS, NA_HEAD_DIM),
            new_c,
            new_n.reshape(nb_ctx, DEPTH, N_DIRS, ML_HEADS, ML_HEAD_DIM),
            new_m[..., 0].reshape(nb_ctx, DEPTH, N_DIRS, ML_HEADS))
```

```python
import functools

import numpy as np
import jax
import jax.numpy as jnp
from jax import lax
from jax.experimental import pallas as pl
from jax.experimental.pallas import tpu as pltpu

F32 = jnp.float32
BF16 = jnp.bfloat16

D_MODEL = 1024
SEQ = 256
DEPTH = 4
DEC_SEQ = 1024
PAST_LEN = 512
GRID_W = 64
CONV_DIM = 256
NA_HEADS = 4
NA_HEAD_DIM = 64
NA_DIM = NA_HEADS * NA_HEAD_DIM
NA_WIN_ROWS = 8
NA_WIN_COLS = 16
ML_HEADS = 4
ML_HEAD_DIM = 128
ML_DIM = ML_HEADS * ML_HEAD_DIM
N_DIRS = 2
N_GROUPS = 4
EXPERTS_PER_GROUP = 4
N_EXPERTS = N_GROUPS * EXPERTS_PER_GROUP
EXPERT_FF = 512
ROPE_BASE = 10000.0
EPS = 1e-6
N_MOD = 6

LANES = 128
N_GATE = N_DIRS * ML_HEADS
C_CONV = 0
C_NA = 3 * CONV_DIM
C_ML = C_NA + 3 * NA_DIM
C_IG = C_ML + 4 * ML_DIM
CUMSUM_BLOCK = 256
MODS_ROWS = 8
VMEM_LIMIT = 56 * 1024 * 1024

NEG_INF = float("-inf")
HIGHEST = lax.Precision.HIGHEST


def _dot(a, b):
    return jnp.dot(a, b, preferred_element_type=F32)


def _dot_nt(a, b):
    return lax.dot_general(a, b, (((1,), (1,)), ((), ())), preferred_element_type=F32)


def _dot_split(a, b):
    a_hi = a.astype(BF16)
    a_lo = (a - a_hi.astype(F32)).astype(BF16)
    b_hi = b.astype(BF16)
    b_lo = (b - b_hi.astype(F32)).astype(BF16)
    return _dot(jnp.concatenate([a_hi, a_hi, a_lo], axis=1), jnp.concatenate([b_hi, b_lo, b_hi], axis=0))


def _rms_mod(x, g, sc, sh):
    y = x * lax.rsqrt(jnp.mean(x * x, axis=-1, keepdims=True) + EPS)
    return (y * g) * (1.0 + sc) + sh


def _mod_rows(mods_ref, row, first):
    return [mods_ref[pl.ds(row, 1), pl.ds((first + j) * D_MODEL, D_MODEL)] for j in range(3)]


def _mods_kernel(cv_ref, w_ref, b_ref, o_ref):
    cv = cv_ref[...]
    s = cv * jax.nn.sigmoid(cv)
    o_ref[...] = _dot(s.astype(BF16), w_ref[...].astype(BF16)) + b_ref[...]


def _mods_call(cvecs, w_mod, b_mod):
    tn = 1536
    n = N_MOD * D_MODEL
    return pl.pallas_call(
        _mods_kernel,
        grid=(DEPTH, n // tn),
        in_specs=[pl.BlockSpec((MODS_ROWS, D_MODEL), lambda l, j: (0, 0)),
                  pl.BlockSpec((None, D_MODEL, tn), lambda l, j: (l, 0, j)),
                  pl.BlockSpec((None, 1, tn), lambda l, j: (l, 0, j))],
        out_specs=pl.BlockSpec((None, MODS_ROWS, tn), lambda l, j: (l, 0, j)),
        out_shape=jax.ShapeDtypeStruct((DEPTH, MODS_ROWS, n), F32),
        compiler_params=pltpu.CompilerParams(vmem_limit_bytes=VMEM_LIMIT),
        name="adaln_mods",
    )(cvecs, w_mod, b_mod.reshape(DEPTH, 1, n))


def _short_conv(zc, cw):
    T = zc.shape[0]
    cb = zc[:, 0:CONV_DIM]
    u = zc[:, CONV_DIM:2 * CONV_DIM] * zc[:, 2 * CONV_DIM:3 * CONV_DIM]
    t = lax.broadcasted_iota(jnp.int32, u.shape, 0)
    u_prev = jnp.where(t == 0, 0.0, pltpu.roll(u, 1, axis=0))
    u_next = jnp.where(t == T - 1, 0.0, pltpu.roll(u, T - 1, axis=0))
    return cb * (cw[0:1, :] * u_prev + cw[1:2, :] * u + cw[2:3, :] * u_next)


def _with_ones(v):
    return jnp.concatenate([v, jnp.ones_like(v)], axis=-1)


def _softmax_attention(nq, nk, nv):
    outs = []
    for h in range(NA_HEADS):
        sl = slice(h * NA_HEAD_DIM, (h + 1) * NA_HEAD_DIM)
        q = (nq[:, sl] * NA_HEAD_DIM ** -0.5).astype(BF16)
        s = _dot_nt(q, nk[:, sl].astype(BF16))
        p = jnp.exp(s - jnp.max(s, axis=-1, keepdims=True))
        o = _dot(p.astype(BF16), nv[:, sl].astype(BF16)) / jnp.sum(p, axis=-1, keepdims=True)
        outs.append(o)
    return jnp.concatenate(outs, axis=-1)


def _log_sigmoid(x):
    return jnp.minimum(x, 0.0) - jnp.log(1.0 + jnp.exp(-jnp.abs(x)))


def _gate_terms(zi, zf):
    T = zi.shape[0]
    lf = _log_sigmoid(zf)
    blk = min(T, CUMSUM_BLOCK)
    r = lax.broadcasted_iota(jnp.int32, (blk, blk), 0)
    c = lax.broadcasted_iota(jnp.int32, (blk, blk), 1)
    tril = jnp.where(c <= r, 1.0, 0.0).astype(F32)
    parts, carry = [], None
    for r0 in range(0, T, blk):
        b = jnp.dot(tril, lf[r0:r0 + blk], precision=HIGHEST, preferred_element_type=F32)
        if carry is not None:
            b = b + carry
        carry = b[blk - 1:blk, :]
        parts.append(b)
    b_fwd = parts[0] if len(parts) == 1 else jnp.concatenate(parts, axis=0)
    b_bwd = (b_fwd[T - 1:T, :] - b_fwd) + lf
    lane = lax.broadcasted_iota(jnp.int32, zi.shape, 1)
    b_col = jnp.where(lane < ML_HEADS, b_fwd, b_bwd)
    row_t = jnp.transpose(zi - b_col)
    return b_col, row_t


def _mlstm_outputs(q, k, v, b_col, li_row, m0, c0, n0, backward, q_block):
    T = q.shape[0]
    qb, kb, vb = q.astype(BF16), k.astype(BF16), v.astype(BF16)
    c0b = None if c0 is None else c0.astype(BF16)
    outs = []
    for r0 in range(0, T, q_block):
        ks, ke = (r0, T) if backward else (0, r0 + q_block)
        bq = b_col[r0:r0 + q_block]
        d = bq + li_row[:, ks:ke]
        t_idx = r0 + lax.broadcasted_iota(jnp.int32, d.shape, 0)
        s_idx = ks + lax.broadcasted_iota(jnp.int32, d.shape, 1)
        d = jnp.where((s_idx >= t_idx) if backward else (s_idx <= t_idx), d, NEG_INF)
        inter = bq + m0
        m_t = jnp.maximum(jnp.max(d, axis=-1, keepdims=True), inter)
        w = jnp.exp(d - m_t)
        s = _dot_nt(qb[r0:r0 + q_block], kb[ks:ke]) * w
        num = _dot(s.astype(BF16), vb[ks:ke])
        den = jnp.sum(s, axis=-1, keepdims=True)
        if c0 is not None:
            a = jnp.exp(inter - m_t)
            num = num + a * _dot(qb[r0:r0 + q_block], c0b)
            den = den + a * jnp.sum(q[r0:r0 + q_block] * n0, axis=-1, keepdims=True)
        outs.append(num / jnp.maximum(jnp.abs(den), jnp.exp(-m_t)))
    return outs[0] if len(outs) == 1 else jnp.concatenate(outs, axis=0)


def _mlstm_state(k, v, b_col, li_col, backward):
    T = k.shape[0]
    b_tot = b_col[0:1] if backward else b_col[T - 1:T]
    g = (b_tot - b_col) + li_col
    m_new = jnp.maximum(b_tot, jnp.max(g, axis=0, keepdims=True))
    kw = k * jnp.exp(g - m_new)
    c_new = lax.dot_general(kw.astype(BF16), v.astype(BF16), (((0,), (0,)), ((), ())),
                            preferred_element_type=F32)
    n_new = jnp.sum(kw, axis=0, keepdims=True)
    return c_new, n_new, m_new


def _head_readout(h, mo, g):
    hn = h * lax.rsqrt(jnp.mean(h * h, axis=-1, keepdims=True) + EPS)
    return hn * g * jax.nn.sigmoid(mo)


def _ctx_mixer_kernel(x_ref, mods_ref, ln_ref, win_ref, wgate_ref, convw_ref, gb_ref, mlg_ref, wout_ref,
                      *rest, has_moe):
    xo_ref, ko_ref, vo_ref, co_ref, no_ref, mo_ref = rest[-6:]
    T = SEQ
    sh1, sc1, g1 = _mod_rows(mods_ref, 0, 0)
    x = x_ref[...]
    if has_moe:
        moe_ref, mods_prev_ref = rest[0], rest[1]
        x = x + _mod_rows(mods_prev_ref, 0, 3)[2] * _from_slabs(moe_ref, T)
    h = _rms_mod(x, ln_ref[...], sc1, sh1).astype(BF16)

    conv_o = _short_conv(_dot(h, win_ref[:, C_CONV:C_NA]), convw_ref[...])

    zn = _dot(h, win_ref[:, C_NA:C_ML])
    nk, nv = zn[:, NA_DIM:2 * NA_DIM], zn[:, 2 * NA_DIM:3 * NA_DIM]
    ko_ref[...] = nk
    vo_ref[...] = nv
    na_o = _softmax_attention(zn[:, 0:NA_DIM], nk, nv)

    zg = _dot(h, wgate_ref[...])
    zi = zg[:, 0:LANES] + gb_ref[:, 0:LANES]
    zf = zg[:, LANES:2 * LANES] + gb_ref[:, LANES:2 * LANES]
    b_col, row_t = _gate_terms(zi, zf)
    m0 = jnp.zeros((1, 1), F32)

    zm = _dot(h, win_ref[:, C_ML:C_IG])
    ml_parts = []
    for hh in range(ML_HEADS):
        zq, zk, zv, zo = (zm[:, p * ML_DIM + hh * ML_HEAD_DIM:p * ML_DIM + (hh + 1) * ML_HEAD_DIM]
                          for p in range(4))
        zk = zk * ML_HEAD_DIM ** -0.5
        hsum = None
        for d in range(N_DIRS):
            j = d * ML_HEADS + hh
            bc = b_col[:, j:j + 1]
            ho = _mlstm_outputs(zq, zk, zv, bc, row_t[j:j + 1, :], m0, None, None, d == 1, T)
            hsum = ho if hsum is None else hsum + ho
            c_new, n_new, m_new = _mlstm_state(zk, zv, bc, zi[:, j:j + 1], d == 1)
            co_ref[d, hh] = c_new
            no_ref[j:j + 1, :] = n_new
            mo_ref[j:j + 1, :] = jnp.broadcast_to(m_new, (1, LANES))
        ml_parts.append(_head_readout(hsum, zo, mlg_ref[:, hh * ML_HEAD_DIM:(hh + 1) * ML_HEAD_DIM]))

    mix = jnp.concatenate([conv_o, na_o] + ml_parts, axis=-1).astype(BF16)
    xo_ref[...] = x + g1 * _dot(mix, wout_ref[...])


def _ctx_mixer_call(layer, x, moe, mods, ln1, w_in, conv_w, gate_b, ml_g, w_out, prev):
    B = x.shape[0]
    T = SEQ
    state_shapes = [
        jax.ShapeDtypeStruct((B, DEPTH, T, NA_DIM), F32),
        jax.ShapeDtypeStruct((B, DEPTH, T, NA_DIM), F32),
        jax.ShapeDtypeStruct((B, DEPTH, N_DIRS, ML_HEADS, ML_HEAD_DIM, ML_HEAD_DIM), F32),
        jax.ShapeDtypeStruct((B, DEPTH, N_GATE, ML_HEAD_DIM), F32),
        jax.ShapeDtypeStruct((B, DEPTH, N_GATE, LANES), F32),
    ]
    in_specs = [
        pl.BlockSpec((None, T, D_MODEL), lambda b: (b, 0, 0)),
        pl.BlockSpec((None, MODS_ROWS, N_MOD * D_MODEL), lambda b: (layer, 0, 0)),
        pl.BlockSpec((None, 1, D_MODEL), lambda b: (layer, 0, 0)),
        pl.BlockSpec((None, D_MODEL, C_IG), lambda b: (layer, 0, 0)),
        pl.BlockSpec((None, D_MODEL, 2 * LANES), lambda b: (layer, 0, 0)),
        pl.BlockSpec((None, 3, CONV_DIM), lambda b: (layer, 0, 0)),
        pl.BlockSpec((None, 1, 2 * LANES), lambda b: (layer, 0, 0)),
        pl.BlockSpec((None, 1, ML_DIM), lambda b: (layer, 0, 0)),
        pl.BlockSpec((None, D_MODEL, D_MODEL), lambda b: (layer, 0, 0)),
    ]
    args = [x, mods, ln1, w_in[0], w_in[1], conv_w, gate_b, ml_g, w_out]
    if moe is not None:
        per_sb = DSP_SB // T
        in_specs += [pl.BlockSpec((None, T * SLAB, LANES), lambda b: (b // per_sb, b % per_sb, 0)),
                     pl.BlockSpec((None, MODS_ROWS, N_MOD * D_MODEL), lambda b: (layer - 1, 0, 0))]
        args += [moe, mods]
    aliases = {}
    if prev is not None:
        in_specs += [pl.BlockSpec(memory_space=pl.ANY)] * len(prev)
        aliases = {len(args) + i: 1 + i for i in range(len(prev))}
        args += list(prev)
    out_specs = [
        pl.BlockSpec((None, T, D_MODEL), lambda b: (b, 0, 0)),
        pl.BlockSpec((None, None, T, NA_DIM), lambda b: (b, layer, 0, 0)),
        pl.BlockSpec((None, None, T, NA_DIM), lambda b: (b, layer, 0, 0)),
        pl.BlockSpec((None, None, N_DIRS, ML_HEADS, ML_HEAD_DIM, ML_HEAD_DIM),
                     lambda b: (b, layer, 0, 0, 0, 0)),
        pl.BlockSpec((None, None, N_GATE, ML_HEAD_DIM), lambda b: (b, layer, 0, 0)),
        pl.BlockSpec((None, None, N_GATE, LANES), lambda b: (b, layer, 0, 0)),
    ]
    outs = pl.pallas_call(
        functools.partial(_ctx_mixer_kernel, has_moe=moe is not None),
        grid=(B,),
        in_specs=in_specs,
        out_specs=out_specs,
        out_shape=[jax.ShapeDtypeStruct(x.shape, F32)] + state_shapes,
        input_output_aliases=aliases,
        compiler_params=pltpu.CompilerParams(vmem_limit_bytes=VMEM_LIMIT),
        name="ctx_mixer",
    )(*args)
    return outs[0], tuple(outs[1:])


def _route(lg):
    lane = lax.broadcasted_iota(jnp.int32, lg.shape, 1)
    lane_f = lane.astype(F32)
    big = float(LANES)
    is_g = (lane >= N_EXPERTS) & (lane < N_EXPERTS + N_GROUPS)
    glm = jnp.where(is_g, lg, NEG_INF)
    gmax = jnp.max(glm, axis=-1, keepdims=True)
    g_top = jnp.min(jnp.where(glm == gmax, lane_f, big), axis=-1, keepdims=True) - N_EXPERTS
    gp = 1.0 / jnp.sum(jnp.where(is_g, jnp.exp(lg - gmax), 0.0), axis=-1, keepdims=True)
    grp = (lane >> (EXPERTS_PER_GROUP.bit_length() - 1)).astype(F32)
    in_grp = (lane < N_EXPERTS) & (grp == g_top)
    e1 = jnp.where(in_grp, lg, NEG_INF)
    v1 = jnp.max(e1, axis=-1, keepdims=True)
    i1 = jnp.min(jnp.where(e1 == v1, lane_f, big), axis=-1, keepdims=True)
    e2 = jnp.where(in_grp & (lane_f != i1), lg, NEG_INF)
    v2 = jnp.max(e2, axis=-1, keepdims=True)
    i2 = jnp.min(jnp.where(e2 == v2, lane_f, big), axis=-1, keepdims=True)
    t = jnp.exp(v2 - v1)
    w1 = 1.0 / (1.0 + t)
    w2 = t / (1.0 + t)
    return i1, i2, gp * w1, gp * w2


SLAB = D_MODEL // LANES
DSP_SB = 2048
DSP_TM = 160
DSP_CH = 512
DSP_PMAX = 2 * DSP_SB + N_EXPERTS * DSP_TM
DSP_NULL = DSP_PMAX + 2 * DSP_TM
DSP_SLOTS = DSP_NULL + 2 * DSP_TM
DSP_PAD_PAIR = 2 * DSP_SB
DSP_W_LEN = 2 * DSP_SB + LANES
DSP_UNROLL = 8
DSP_RING = 3


def _to_slabs(ref, x):
    n = x.shape[0]
    for s in range(SLAB):
        ref[pl.ds(s, n, stride=SLAB), :] = x[:, s * LANES:(s + 1) * LANES]


def _from_slabs(ref, n):
    return jnp.concatenate([ref[pl.ds(s, n, stride=SLAB), :] for s in range(SLAB)], axis=-1)


def _dsp_mods(mods_ref, chunk, chunks_per_row, first):
    row = 0 if chunks_per_row is None else 1 + chunk // chunks_per_row
    return _mod_rows(mods_ref, row, first)


def _dsp_route_kernel(x_ref, mods_ref, ln_ref, wr_ref, br_ref, h_ref, r_ref, cnt_ref, run_scr, *,
                      chunks_per_row):
    c = pl.program_id(0)
    ch = DSP_CH

    @pl.when(c % (DSP_SB // ch) == 0)
    def _():
        run_scr[...] = jnp.zeros_like(run_scr)

    sh2, sc2, _ = _dsp_mods(mods_ref, c, chunks_per_row, 3)
    h2 = _rms_mod(x_ref[...], ln_ref[...], sc2, sh2)
    _to_slabs(h_ref, h2)
    lg = _dot_split(h2, wr_ref[...]) + br_ref[...]
    i1, i2, w1, w2 = _route(lg)
    lane = lax.broadcasted_iota(jnp.int32, lg.shape, 1)
    lane_f = lane.astype(F32)
    oh1 = jnp.where(lane_f == i1, 1.0, 0.0)
    oh2 = jnp.where(lane_f == i2, 1.0, 0.0)
    both = oh1 + oh2
    r = lax.broadcasted_iota(jnp.int32, (ch, ch), 0)
    s = lax.broadcasted_iota(jnp.int32, (ch, ch), 1)
    earlier = jnp.where(s < r, 1.0, 0.0).astype(BF16)
    run = run_scr[0:1, :]
    before = _dot(earlier, both.astype(BF16)) + run
    rank1 = jnp.sum(before * oh1, axis=-1, keepdims=True)
    rank2 = jnp.sum(before * oh2, axis=-1, keepdims=True)
    run = run + jnp.sum(both, axis=0, keepdims=True)
    run_scr[0:1, :] = run
    cnt_ref[...] = jnp.broadcast_to(run, cnt_ref.shape)
    cols = (i1, i2, w1, w2, rank1, rank2)
    out = jnp.zeros(lg.shape, F32)
    for j, col in enumerate(cols):
        out = jnp.where(lane == j, col, out)
    r_ref[...] = out


def _dsp_route_call(layer, x2d, mods, ln2, w_router, b_router, chunks_per_row):
    n = x2d.shape[0]
    ch = DSP_CH
    per_sb = DSP_SB // ch
    return pl.pallas_call(
        functools.partial(_dsp_route_kernel, chunks_per_row=chunks_per_row),
        grid=(n // ch,),
        in_specs=[
            pl.BlockSpec((ch, D_MODEL), lambda c: (c, 0)),
            pl.BlockSpec((None, MODS_ROWS, N_MOD * D_MODEL), lambda c: (layer, 0, 0)),
            pl.BlockSpec((None, 1, D_MODEL), lambda c: (layer, 0, 0)),
            pl.BlockSpec((None, D_MODEL, LANES), lambda c: (layer, 0, 0)),
            pl.BlockSpec((None, 1, LANES), lambda c: (layer, 0, 0)),
        ],
        out_specs=[pl.BlockSpec((ch * SLAB, LANES), lambda c: (c, 0)),
                   pl.BlockSpec((ch, LANES), lambda c: (c, 0)),
                   pl.BlockSpec((None, SLAB, LANES), lambda c: (c // per_sb, 0, 0))],
        out_shape=[jax.ShapeDtypeStruct((n * SLAB, LANES), F32),
                   jax.ShapeDtypeStruct((n, LANES), F32),
                   jax.ShapeDtypeStruct((n // DSP_SB, SLAB, LANES), F32)],
        scratch_shapes=[pltpu.VMEM((SLAB, LANES), F32)],
        compiler_params=pltpu.CompilerParams(vmem_limit_bytes=VMEM_LIMIT),
        name="moe_route",
    )(x2d, mods, ln2, w_router, b_router)


def _dispatch_tables(route, counts, n_sb):
    e = route[:, 0:2].astype(jnp.int32).reshape(n_sb, 2 * DSP_SB)
    rank = route[:, 4:6].astype(jnp.int32).reshape(n_sb, 2 * DSP_SB)
    w = _pad_lanes(route[:, 2:4].reshape(n_sb, 1, 2 * DSP_SB), DSP_W_LEN)
    cnt = counts[:, 0, :N_EXPERTS].astype(jnp.int32)
    ntile = (cnt + (DSP_TM - 1)) // DSP_TM
    off = (jnp.cumsum(ntile, axis=1) - ntile) * DSP_TM
    is_e = e[:, None, :] == jnp.arange(N_EXPERTS, dtype=jnp.int32)[None, :, None]
    pos = (jnp.sum(jnp.where(is_e, off[:, :, None], 0), axis=1) + rank).reshape(n_sb, 1, 2 * DSP_SB)
    return pos, w, ntile.reshape(-1), off.reshape(-1), cnt.reshape(-1)


def _dsp_expert_kernel(ntile_ref, off_ref, cnt_ref, pos_ref, w_ref, h_ref, wexp_hbm,
                       o_ref, pair_smem, sched_smem, pend_smem, xs0, xs1, ys0, ys1,
                       wexp_scr, wexp_sem, *, layer):
    sb = pl.program_id(0)
    e = pl.program_id(1)
    tm, u = DSP_TM, DSP_UNROLL

    step = sb * N_EXPERTS + e
    n_steps = pl.num_programs(0) * N_EXPERTS

    def weight_copy(g):
        slot = g % DSP_RING
        return pltpu.make_async_copy(wexp_hbm.at[layer, g % N_EXPERTS], wexp_scr.at[slot],
                                     wexp_sem.at[slot])

    @pl.when(step == 0)
    def _():
        for g in range(DSP_RING - 1):
            weight_copy(g).start()

    @pl.when(step + (DSP_RING - 1) < n_steps)
    def _():
        weight_copy(step + (DSP_RING - 1)).start()
    xs_scr, ys_scr = (xs0, xs1), (ys0, ys1)
    slab_shift = SLAB.bit_length() - 1
    last_row = (DSP_SB - 1) * SLAB

    def slab_row(pair):
        return lax.shift_left(lax.shift_right_logical(pair, 1), slab_shift)

    @pl.when(e == 0)
    def _():
        for ee in range(N_EXPERTS):
            q = sb * N_EXPERTS + ee

            def pad(p, carry):
                pair_smem[p] = DSP_PAD_PAIR
                return carry

            lax.fori_loop(off_ref[q] + cnt_ref[q], off_ref[q] + ntile_ref[q] * tm, pad, 0)

        def invert(j, carry):
            for k in range(u):
                a = j * u + k
                pair_smem[pos_ref[0, a]] = a
            return carry

        lax.fori_loop(0, 2 * DSP_SB // u, invert, 0)

        last_q = sb * N_EXPERTS + (N_EXPERTS - 1)
        used_end = off_ref[last_q] + ntile_ref[last_q] * tm

        def pad_end(p, carry):
            pair_smem[used_end + p] = DSP_PAD_PAIR
            pair_smem[DSP_NULL + p] = DSP_PAD_PAIR
            return carry

        lax.fori_loop(0, 2 * tm, pad_end, 0)
        sched_smem[0] = -1
        sched_smem[1] = DSP_NULL
        pend_smem[0] = 0.0
        o_ref[...] = jnp.zeros_like(o_ref)
        for ys in ys_scr:
            ys[...] = jnp.zeros_like(ys)

    q = sb * N_EXPERTS + e
    base0 = off_ref[q]
    nt = ntile_ref[q]
    weight_copy(step).wait()
    w_slot = wexp_scr.at[step % DSP_RING]

    def gather(base, buf):
        for r in range(tm):
            src = jnp.minimum(slab_row(pair_smem[base + r]), last_row)
            xs_scr[buf][r * SLAB:(r + 1) * SLAB, :] = h_ref[pl.ds(pl.multiple_of(src, SLAB), SLAB), :]

    def compute(buf):
        x = _from_slabs(xs_scr[buf], tm).astype(BF16)
        gu = _dot(x, w_slot[0:D_MODEL, :])
        a = jax.nn.silu(gu[:, 0:EXPERT_FF]) * gu[:, EXPERT_FF:2 * EXPERT_FF]
        _to_slabs(ys_scr[buf], _dot(a.astype(BF16), w_slot[D_MODEL:D_MODEL + EXPERT_FF, :]))

    def scatter(base, buf, scale):
        for j in range(tm // u):
            upd = []
            for k in range(u):
                r = j * u + k
                pair = pair_smem[base + r]
                dst = pl.multiple_of(slab_row(pair), SLAB)
                y = ys_scr[buf][r * SLAB:(r + 1) * SLAB, :]
                upd.append((dst, o_ref[pl.ds(dst, SLAB), :] + (w_ref[0, pair] * scale) * y))
            for dst, v in upd:
                o_ref[pl.ds(dst, SLAB), :] = v

    n_pairs = lax.shift_right_logical(nt + 1, 1)

    @pl.when((n_pairs > 0) & (sched_smem[0] != base0))
    def _():
        gather(base0, 0)

    pend_base = sched_smem[1]
    pend_scale = pend_smem[0]

    def pair(k, carry):
        b = base0 + 2 * k * tm
        first = k == 0
        gather(b + tm, 1)
        compute(0)
        scatter(jnp.where(first, pend_base, b - tm), 1, jnp.where(first, pend_scale, 1.0))
        gather(b + 2 * tm, 0)
        compute(1)
        scatter(b, 0, 1.0)
        return carry

    lax.fori_loop(0, n_pairs, pair, 0)

    @pl.when(n_pairs > 0)
    def _():
        last = 2 * n_pairs - 1
        sched_smem[0] = base0 + (last + 1) * tm
        sched_smem[1] = base0 + last * tm
        pend_smem[0] = jnp.where(last < nt, 1.0, 0.0)

    @pl.when(e == N_EXPERTS - 1)
    def _():
        scatter(sched_smem[1], 1, pend_smem[0])


def _dsp_expert_call(layer, h_slabs, pos, w, ntile, off, cnt, w_experts):
    n_sb = pos.shape[0]
    out_rows = (DSP_SB + DSP_CH) * SLAB

    def per_sb(s, e, *prefetch):
        return (s, 0, 0)

    grid_spec = pltpu.PrefetchScalarGridSpec(
        num_scalar_prefetch=3,
        grid=(n_sb, N_EXPERTS),
        in_specs=[
            pl.BlockSpec((None, 1, 2 * DSP_SB), per_sb, memory_space=pltpu.SMEM),
            pl.BlockSpec((None, 1, DSP_W_LEN), per_sb, memory_space=pltpu.SMEM),
            pl.BlockSpec((DSP_SB * SLAB, LANES), lambda s, e, *prefetch: (s, 0)),
            pl.BlockSpec(memory_space=pl.ANY),
        ],
        out_specs=pl.BlockSpec((None, out_rows, LANES), per_sb),
        scratch_shapes=[pltpu.SMEM((DSP_SLOTS,), jnp.int32),
                        pltpu.SMEM((2,), jnp.int32),
                        pltpu.SMEM((1,), F32),
                        pltpu.VMEM((DSP_TM * SLAB, LANES), F32),
                        pltpu.VMEM((DSP_TM * SLAB, LANES), F32),
                        pltpu.VMEM((DSP_TM * SLAB, LANES), F32),
                        pltpu.VMEM((DSP_TM * SLAB, LANES), F32),
                        pltpu.VMEM((DSP_RING, D_MODEL + EXPERT_FF, 2 * EXPERT_FF), BF16),
                        pltpu.SemaphoreType.DMA((DSP_RING,))],
    )
    return pl.pallas_call(
        functools.partial(_dsp_expert_kernel, layer=layer),
        grid_spec=grid_spec,
        out_shape=jax.ShapeDtypeStruct((n_sb, out_rows, LANES), F32),
        compiler_params=pltpu.CompilerParams(vmem_limit_bytes=VMEM_LIMIT,
                                             dimension_semantics=("arbitrary", "arbitrary")),
        name="moe_experts",
    )(ntile, off, cnt, pos, w, h_slabs, w_experts)


def _dsp_final_kernel(x_ref, m_ref, mods_ref, fg_ref, o_ref, *, chunks_per_row):
    g2 = _dsp_mods(mods_ref, pl.program_id(0), chunks_per_row, 3)[2]
    y = x_ref[...] + g2 * _from_slabs(m_ref, DSP_CH)
    o_ref[...] = y * lax.rsqrt(jnp.mean(y * y, axis=-1, keepdims=True) + EPS) * fg_ref[...]


def _dsp_final_call(layer, x2d, m_slabs, mods, final_g, chunks_per_row):
    n = x2d.shape[0]
    ch = DSP_CH
    per_sb = DSP_SB // ch
    return pl.pallas_call(
        functools.partial(_dsp_final_kernel, chunks_per_row=chunks_per_row),
        grid=(n // ch,),
        in_specs=[
            pl.BlockSpec((ch, D_MODEL), lambda c: (c, 0)),
            pl.BlockSpec((None, ch * SLAB, LANES), lambda c: (c // per_sb, c % per_sb, 0)),
            pl.BlockSpec((None, MODS_ROWS, N_MOD * D_MODEL), lambda c: (layer, 0, 0)),
            pl.BlockSpec((1, D_MODEL), lambda c: (0, 0)),
        ],
        out_specs=pl.BlockSpec((ch, D_MODEL), lambda c: (c, 0)),
        out_shape=jax.ShapeDtypeStruct(x2d.shape, F32),
        compiler_params=pltpu.CompilerParams(vmem_limit_bytes=VMEM_LIMIT),
        name="final_norm",
    )(x2d, m_slabs, mods, final_g)


def _sparse_moe(layer, x2d, mods, ln2, w_router, b_router, w_experts, chunks_per_row):
    n_sb = x2d.shape[0] // DSP_SB
    h_slabs, route, counts = _dsp_route_call(layer, x2d, mods, ln2, w_router, b_router,
                                             chunks_per_row)
    pos, w, ntile, off, cnt = _dispatch_tables(route, counts, n_sb)
    return _dsp_expert_call(layer, h_slabs, pos, w, ntile, off, cnt, w_experts)


LAT_PROJ_ROWS = 512


def _lat_proj_kernel(x_ref, mods_ref, ln_ref, win_ref, wgate_ref, gb_ref, *rest, has_moe):
    zc_ref, zn_ref, zm_ref, zg_ref = rest[-4:]
    row = 1 + pl.program_id(0) // (DEC_SEQ // LAT_PROJ_ROWS)
    sh1, sc1, _ = _mod_rows(mods_ref, row, 0)
    x = x_ref[...]
    if has_moe:
        moe_ref, mods_prev_ref, xo_ref = rest[0], rest[1], rest[2]
        x = x + _mod_rows(mods_prev_ref, row, 3)[2] * _from_slabs(moe_ref, LAT_PROJ_ROWS)
        xo_ref[...] = x
    h = _rms_mod(x, ln_ref[...], sc1, sh1).astype(BF16)
    zc_ref[...] = _dot(h, win_ref[:, C_CONV:C_NA])
    zn_ref[...] = _dot(h, win_ref[:, C_NA:C_ML])
    zm_ref[...] = _dot(h, win_ref[:, C_ML:C_IG])
    zg_ref[...] = _dot(h, wgate_ref[...]) + gb_ref[...]


def _lat_proj_call(layer, x2d, moe, mods, ln1, w_in, gate_b):
    n = x2d.shape[0]
    tb = LAT_PROJ_ROWS
    widths = (C_NA - C_CONV, C_ML - C_NA, C_IG - C_ML, 2 * LANES)
    in_specs = [
        pl.BlockSpec((tb, D_MODEL), lambda i: (i, 0)),
        pl.BlockSpec((None, MODS_ROWS, N_MOD * D_MODEL), lambda i: (layer, 0, 0)),
        pl.BlockSpec((None, 1, D_MODEL), lambda i: (layer, 0, 0)),
        pl.BlockSpec((None, D_MODEL, C_IG), lambda i: (layer, 0, 0)),
        pl.BlockSpec((None, D_MODEL, 2 * LANES), lambda i: (layer, 0, 0)),
        pl.BlockSpec((None, 1, 2 * LANES), lambda i: (layer, 0, 0)),
    ]
    args = [x2d, mods, ln1, w_in[0], w_in[1], gate_b]
    out_specs = [pl.BlockSpec((tb, w), lambda i: (i, 0)) for w in widths]
    out_shape = [jax.ShapeDtypeStruct((n, w), F32) for w in widths]
    if moe is not None:
        per_sb = DSP_SB // tb
        in_specs += [pl.BlockSpec((None, tb * SLAB, LANES), lambda i: (i // per_sb, i % per_sb, 0)),
                     pl.BlockSpec((None, MODS_ROWS, N_MOD * D_MODEL), lambda i: (layer - 1, 0, 0))]
        args += [moe, mods]
        out_specs = [pl.BlockSpec((tb, D_MODEL), lambda i: (i, 0))] + out_specs
        out_shape = [jax.ShapeDtypeStruct((n, D_MODEL), F32)] + out_shape
    outs = pl.pallas_call(
        functools.partial(_lat_proj_kernel, has_moe=moe is not None),
        grid=(n // tb,),
        in_specs=in_specs,
        out_specs=out_specs,
        out_shape=out_shape,
        compiler_params=pltpu.CompilerParams(vmem_limit_bytes=VMEM_LIMIT),
        name="lat_proj",
    )(*args)
    return tuple(outs) if moe is not None else (x2d,) + tuple(outs)


def _na_row_start(r):
    rows = DEC_SEQ // GRID_W
    return min(max(r - NA_WIN_ROWS // 2, 0), rows - NA_WIN_ROWS)


def _na_row_groups():
    rows = DEC_SEQ // GRID_W
    groups, lo = [], 0
    for r in range(1, rows + 1):
        if r == rows or _na_row_start(r) != _na_row_start(lo):
            groups.append((lo, r))
            lo = r
    return tuple(groups)


_NA_ROW_GROUPS = _na_row_groups()


def _lat_local_kernel(zc_ref, zn_ref, kc_ref, vc_ref, band_ref, convw_ref, o_ref):
    o_ref[:, 0:CONV_DIM] = _short_conv(zc_ref[...], convw_ref[...])
    blk = NA_WIN_ROWS * GRID_W
    q_all = (zn_ref[:, 0:NA_DIM] * NA_HEAD_DIM ** -0.5).astype(BF16)
    k_all = zn_ref[:, NA_DIM:2 * NA_DIM].astype(BF16)
    v_all = zn_ref[:, 2 * NA_DIM:3 * NA_DIM].astype(BF16)
    kc_all = kc_ref[...].astype(BF16)
    vc_all = vc_ref[...].astype(BF16)
    heads = []
    for h in range(NA_HEADS):
        sl = slice(h * NA_HEAD_DIM, (h + 1) * NA_HEAD_DIM)
        q, k, v, kc, vc = q_all[:, sl], k_all[:, sl], v_all[:, sl], kc_all[:, sl], vc_all[:, sl]
        band = band_ref[h]
        s_ctx = _dot_nt(q, kc)
        m_ctx = jnp.max(s_ctx, axis=-1, keepdims=True)
        v1 = _with_ones(v)
        m_rows, o_rows = [], []
        for r_lo, r_hi in _NA_ROW_GROUPS:
            start = _na_row_start(r_lo)
            bias = [band[:, (start - r + NA_WIN_ROWS - 1) * GRID_W:(start - r + NA_WIN_ROWS - 1) * GRID_W + blk]
                    for r in range(r_lo, r_hi)]
            bias = bias[0] if len(bias) == 1 else jnp.concatenate(bias, axis=0)
            q_g = q[r_lo * GRID_W:r_hi * GRID_W]
            s_loc = _dot_nt(q_g, k[start * GRID_W:start * GRID_W + blk]) + bias
            m = jnp.maximum(jnp.max(s_loc, axis=-1, keepdims=True), m_ctx[r_lo * GRID_W:r_hi * GRID_W])
            p_loc = jnp.exp(s_loc - m)
            m_rows.append(m)
            o_rows.append(_dot(p_loc.astype(BF16), v1[start * GRID_W:start * GRID_W + blk]))
        p_ctx = jnp.exp(s_ctx - jnp.concatenate(m_rows, axis=0))
        o = jnp.concatenate(o_rows, axis=0) + _dot(p_ctx.astype(BF16), _with_ones(vc))
        heads.append(o[:, 0:NA_HEAD_DIM] / o[:, NA_HEAD_DIM:NA_HEAD_DIM + 1])
    o_ref[:, CONV_DIM:CONV_DIM + NA_DIM] = jnp.concatenate(heads, axis=-1)


def _lat_local_call(layer, zc, zn, cache_k, cache_v, band, conv_w):
    nb = zc.shape[0] // DEC_SEQ
    T = DEC_SEQ
    return pl.pallas_call(
        _lat_local_kernel,
        grid=(nb,),
        in_specs=[
            pl.BlockSpec((T, 3 * CONV_DIM), lambda b: (b, 0)),
            pl.BlockSpec((T, 3 * NA_DIM), lambda b: (b, 0)),
            pl.BlockSpec((None, None, PAST_LEN, NA_DIM), lambda b: (b, layer, 0, 0)),
            pl.BlockSpec((None, None, PAST_LEN, NA_DIM), lambda b: (b, layer, 0, 0)),
            pl.BlockSpec((None, NA_HEADS, GRID_W, band.shape[-1]), lambda b: (layer, 0, 0, 0)),
            pl.BlockSpec((None, 3, CONV_DIM), lambda b: (layer, 0, 0)),
        ],
        out_specs=pl.BlockSpec((T, CONV_DIM + NA_DIM), lambda b: (b, 0)),
        out_shape=jax.ShapeDtypeStruct((zc.shape[0], CONV_DIM + NA_DIM), F32),
        compiler_params=pltpu.CompilerParams(vmem_limit_bytes=VMEM_LIMIT),
        name="lat_conv_na",
    )(zc, zn, cache_k, cache_v, band, conv_w)


def _rope(x, cos, sin_signed):
    w = x.shape[-1]
    half = NA_HEAD_DIM // 2
    lane = lax.broadcasted_iota(jnp.int32, x.shape, 1)
    partner = jnp.where(lane % (2 * half) < half, pltpu.roll(x, w - half, axis=1), pltpu.roll(x, half, axis=1))
    return x * cos + partner * sin_signed


def _lat_mlstm_kernel(q_ref, k_ref, v_ref, o_ref, zg_ref, c0_ref, n0_ref, m0_ref, cos_ref, sin_ref,
                      mlg_ref, out_ref, bcol_scr, rowt_scr):
    hh = pl.program_id(1)

    @pl.when(hh == 0)
    def _():
        b_all, r_all = _gate_terms(zg_ref[:, 0:LANES], zg_ref[:, LANES:2 * LANES])
        bcol_scr[...] = b_all
        rowt_scr[...] = r_all

    b_col = bcol_scr[...]
    cos, sin_signed = cos_ref[...], sin_ref[...]
    q = _rope(q_ref[...], cos, sin_signed)
    k = _rope(k_ref[...], cos, sin_signed) * ML_HEAD_DIM ** -0.5
    v = v_ref[...]
    lane = lax.broadcasted_iota(jnp.int32, (DEC_SEQ, LANES), 1)
    hsum = None
    for d in range(N_DIRS):
        j = d * ML_HEADS + hh
        bc = jnp.sum(jnp.where(lane == j, b_col, 0.0), axis=-1, keepdims=True)
        lr = rowt_scr[pl.ds(j, 1), :]
        m0 = m0_ref[d, :, 0:1]
        ho = _mlstm_outputs(q, k, v, bc, lr, m0, c0_ref[d], n0_ref[d], d == 1, 256)
        hsum = ho if hsum is None else hsum + ho
    out_ref[...] = _head_readout(hsum, o_ref[...], mlg_ref[...])


def _lat_mlstm_call(layer, zm, zg, state_c, state_n, state_m, cos, sin_signed, ml_g):
    nb = zm.shape[0] // DEC_SEQ
    T = DEC_SEQ
    hd = ML_HEAD_DIM

    def col(part):
        return pl.BlockSpec((T, hd), lambda b, h: (b, part * ML_HEADS + h))

    return pl.pallas_call(
        _lat_mlstm_kernel,
        grid=(nb, ML_HEADS),
        in_specs=[
            col(0), col(1), col(2), col(3),
            pl.BlockSpec((T, 2 * LANES), lambda b, h: (b, 0)),
            pl.BlockSpec((None, None, N_DIRS, None, hd, hd), lambda b, h: (b, layer, 0, h, 0, 0)),
            pl.BlockSpec((None, None, N_DIRS, None, 1, hd), lambda b, h: (b, layer, 0, h, 0, 0)),
            pl.BlockSpec((None, None, N_DIRS, None, 1, LANES), lambda b, h: (b, layer, 0, h, 0, 0)),
            pl.BlockSpec((T, hd), lambda b, h: (0, 0)),
            pl.BlockSpec((T, hd), lambda b, h: (0, 0)),
            pl.BlockSpec((None, 1, hd), lambda b, h: (layer, 0, h)),
        ],
        out_specs=pl.BlockSpec((T, hd), lambda b, h: (b, h)),
        out_shape=jax.ShapeDtypeStruct((zm.shape[0], ML_DIM), F32),
        scratch_shapes=[pltpu.VMEM((T, LANES), F32), pltpu.VMEM((LANES, T), F32)],
        compiler_params=pltpu.CompilerParams(vmem_limit_bytes=VMEM_LIMIT),
        name="lat_mlstm",
    )(zm, zm, zm, zm, zg, state_c, state_n, state_m, cos, sin_signed, ml_g)


def _lat_merge_kernel(x_ref, loc_ref, ml_ref, mods_ref, wout_ref, o_ref):
    row = 1 + pl.program_id(0) // (DEC_SEQ // LAT_PROJ_ROWS)
    g1 = _mod_rows(mods_ref, row, 0)[2]
    split = CONV_DIM + NA_DIM
    y = _dot(loc_ref[...].astype(BF16), wout_ref[0:split, :]) + _dot(ml_ref[...].astype(BF16), wout_ref[split:, :])
    o_ref[...] = x_ref[...] + g1 * y


def _lat_merge_call(layer, x2d, loc, ml, mods, w_out):
    n = x2d.shape[0]
    tb = LAT_PROJ_ROWS
    return pl.pallas_call(
        _lat_merge_kernel,
        grid=(n // tb,),
        in_specs=[
            pl.BlockSpec((tb, D_MODEL), lambda i: (i, 0)),
            pl.BlockSpec((tb, CONV_DIM + NA_DIM), lambda i: (i, 0)),
            pl.BlockSpec((tb, ML_DIM), lambda i: (i, 0)),
            pl.BlockSpec((None, MODS_ROWS, N_MOD * D_MODEL), lambda i: (layer, 0, 0)),
            pl.BlockSpec((None, D_MODEL, D_MODEL), lambda i: (layer, 0, 0)),
        ],
        out_specs=pl.BlockSpec((tb, D_MODEL), lambda i: (i, 0)),
        out_shape=jax.ShapeDtypeStruct(x2d.shape, F32),
        compiler_params=pltpu.CompilerParams(vmem_limit_bytes=VMEM_LIMIT),
        name="lat_merge",
    )(x2d, loc, ml, mods, w_out)


def _pad_lanes(a, width):
    return jnp.pad(a, [(0, 0)] * (a.ndim - 1) + [(0, width - a.shape[-1])])


def _pack_w_in(w_in):
    w_b = w_in.astype(BF16)
    gates = w_b[..., C_IG:]
    gates = jnp.concatenate([_pad_lanes(gates[..., :N_GATE], LANES),
                             _pad_lanes(gates[..., N_GATE:], LANES)], axis=-1)
    return w_b, gates


def _pack_gate_bias(ml_gate_b):
    gb = ml_gate_b.reshape(DEPTH, 2, N_GATE).astype(F32)
    return jnp.concatenate([_pad_lanes(gb[:, 0], LANES), _pad_lanes(gb[:, 1], LANES)], axis=-1)[:, None, :]


def _rpb_band(na_rpb):
    cols = np.arange(GRID_W)
    col_idx = np.clip(cols[None, :] - cols[:, None] + NA_WIN_COLS - 1, 0, 2 * NA_WIN_COLS - 2)
    col_start = np.clip(cols - NA_WIN_COLS // 2, 0, GRID_W - NA_WIN_COLS)
    col_mask = (cols[None, :] >= col_start[:, None]) & (cols[None, :] < col_start[:, None] + NA_WIN_COLS)
    pick = (col_idx[None] == np.arange(2 * NA_WIN_COLS - 1)[:, None, None]).astype(np.float32)
    t = jnp.einsum('lhri,iqk->lhqrk', na_rpb.astype(F32), jnp.asarray(pick), precision=HIGHEST)
    t = jnp.where(col_mask[None, None, :, None, :], t, NEG_INF)
    t = t.reshape(DEPTH, NA_HEADS, GRID_W, (2 * NA_WIN_ROWS - 1) * GRID_W)
    return _pad_lanes(t, 2 * NA_WIN_ROWS * GRID_W)


def _rope_tables():
    t = np.arange(DEC_SEQ)
    pos = np.stack([t // GRID_W, t % GRID_W], axis=-1).astype(np.float32)
    nf = ML_HEAD_DIM // 4
    inv = jnp.asarray(ROPE_BASE, F32) ** (-jnp.arange(nf, dtype=F32) / nf)
    ang = jnp.asarray(pos)[:, :, None] * inv
    cos = jnp.cos(ang)
    sin = jnp.sin(ang)
    cos_t = jnp.concatenate([cos, cos], axis=-1).reshape(DEC_SEQ, ML_HEAD_DIM)
    sin_t = jnp.concatenate([-sin, sin], axis=-1).reshape(DEC_SEQ, ML_HEAD_DIM)
    return cos_t, sin_t


def kernel(x_prompt, x_sample, cache_k, cache_v, state_C, state_n, state_m, c, c_ctx, w_mod, b_mod,
           ln1_g, w_in, conv_w, na_rpb, ml_gate_b, ml_norm_g, w_out, ln2_g, w_rg, b_rg, w_re, b_re,
           w_gate, w_up, w_down, final_g):
    nb_ctx = x_prompt.shape[0]
    nb_lat = x_sample.shape[0]
    assert 1 + nb_lat <= MODS_ROWS

    cvecs = jnp.concatenate([c_ctx[None, :], c,
                             jnp.zeros((MODS_ROWS - 1 - nb_lat, D_MODEL), F32)], axis=0)
    mods = _mods_call(cvecs, w_mod, b_mod)

    w_in_p = _pack_w_in(w_in)
    w_out_b = w_out.astype(BF16)
    w_experts = jnp.concatenate([jnp.concatenate([w_gate.astype(BF16), w_up.astype(BF16)], axis=-1),
                                 w_down.astype(BF16)], axis=-2)
    gate_b = _pack_gate_bias(ml_gate_b)
    ln1 = ln1_g.reshape(DEPTH, 1, D_MODEL)
    ln2 = ln2_g.reshape(DEPTH, 1, D_MODEL)
    ml_g = ml_norm_g.reshape(DEPTH, 1, ML_DIM)
    w_router = _pad_lanes(jnp.concatenate([w_re, w_rg], axis=-1), LANES)
    b_router = _pad_lanes(jnp.concatenate([b_re, b_rg], axis=-1), LANES)[:, None, :]
    fg = final_g.reshape(1, D_MODEL)
    band = _rpb_band(na_rpb)
    cos_t, sin_t = _rope_tables()
    ck = cache_k.reshape(nb_lat, DEPTH, PAST_LEN, NA_DIM)
    cv = cache_v.reshape(nb_lat, DEPTH, PAST_LEN, NA_DIM)
    st_n = state_n.reshape(nb_lat, DEPTH, N_DIRS, ML_HEADS, 1, ML_HEAD_DIM)
    st_m = jnp.broadcast_to(state_m[..., None, None], (nb_lat, DEPTH, N_DIRS, ML_HEADS, 1, LANES))

    xp = x_prompt
    xs = x_sample.reshape(nb_lat * DEC_SEQ, D_MODEL)
    states = None
    lat_chunks = DEC_SEQ // DSP_CH
    moe_p = moe_s = None
    for l in range(DEPTH):
        xp, states = _ctx_mixer_call(l, xp, moe_p, mods, ln1, w_in_p, conv_w, gate_b, ml_g, w_out_b,
                                     states)
        moe_p = _sparse_moe(l, xp.reshape(nb_ctx * SEQ, D_MODEL), mods, ln2, w_router, b_router,
                            w_experts, None)

        xs, zc, zn, zm, zg = _lat_proj_call(l, xs, moe_s, mods, ln1, w_in_p, gate_b)
        loc = _lat_local_call(l, zc, zn, ck, cv, band, conv_w)
        ml = _lat_mlstm_call(l, zm, zg, state_C, st_n, st_m, cos_t, sin_t, ml_g)
        xs = _lat_merge_call(l, xs, loc, ml, mods, w_out_b)
        moe_s = _sparse_moe(l, xs, mods, ln2, w_router, b_router, w_experts, lat_chunks)

    xp = _dsp_final_call(DEPTH - 1, xp.reshape(nb_ctx * SEQ, D_MODEL), moe_p, mods, fg, None)
    xs = _dsp_final_call(DEPTH - 1, xs, moe_s, mods, fg, lat_chunks)
    xp = xp.reshape(nb_ctx, SEQ, D_MODEL)

    new_k, new_v, new_c, new_n, new_m = states
    return (xp, xs.reshape(nb_lat, DEC_SEQ, D_MODEL),
            new_k.reshape(nb_ctx, DEPTH, SEQ, NA_HEADS, NA_HEAD_DIM),
            new_v.reshape(nb_ctx, DEPTH, SEQ, NA_HEADS, NA_HEAD_DIM),
            new_c,
            new_n.reshape(nb_ctx, DEPTH, N_DIRS, ML_HEADS, ML_HEAD_DIM),
            new_m[..., 0].reshape(nb_ctx, DEPTH, N_DIRS, ML_HEADS))
```
